```python
import math
import jax, jax.numpy as jnp
from jax import lax
import numpy as np

D_MODEL = 2048
BATCH = 2
SEQ = 4096
DEPTH = 2

MIX_WIDTH = D_MODEL // 2
N_BRANCH = 4
EPS = 1e-6

LRU_WIDTH = MIX_WIDTH
LRU_BLOCKS = 8
LRU_BLOCK = LRU_WIDTH // LRU_BLOCKS
CONV_WIDTH = 4
LRU_C = 8.0

RET_HEADS = 8
RET_QK_DIM = MIX_WIDTH // (2 * RET_HEADS)
RET_V_DIM = MIX_WIDTH // RET_HEADS
RET_CHUNK = 128
ROPE_THETA = 10000.0

SGU_WIDTH = MIX_WIDTH
SGU_GROUPS = 8
SGU_CHUNK = 128

RWKV_HEAD_DIM = 64
RWKV_HEADS = MIX_WIDTH // RWKV_HEAD_DIM
DECAY_LORA = 64
ICLR_LORA = 64
GATE_LORA = 128
RWKV_FEAT = 3 * MIX_WIDTH + 2 * DECAY_LORA + 2 * ICLR_LORA + GATE_LORA
DECAY_SCALE = math.exp(-0.5)

N_GROUPS = 4
EXPERTS_PER_GROUP = 8
N_EXPERTS = N_GROUPS * EXPERTS_PER_GROUP
TOP_K = 2
EXPERT_HIDDEN = D_MODEL // 4
MOE_BLOCK = 256

IN_SPLITS = (LRU_WIDTH, LRU_WIDTH,
             RET_HEADS * RET_QK_DIM, RET_HEADS * RET_QK_DIM, RET_HEADS * RET_V_DIM, RET_HEADS * RET_V_DIM,
             SGU_WIDTH, SGU_WIDTH,
             RWKV_FEAT,
             N_BRANCH * D_MODEL)
IN_WIDTH = sum(IN_SPLITS)

kernel_name = "hybrid_bidir_lru_ret_sgu_rwkv7_hmoe"

F32 = jnp.float32


def rms_norm(x, g):
    xf = x.astype(F32)
    y = xf * lax.rsqrt(jnp.mean(xf * xf, axis=-1, keepdims=True) + EPS)
    return (y * g.astype(F32)).astype(x.dtype)


def head_norm(x):
    xf = x.astype(F32)
    mu = jnp.mean(xf, axis=-1, keepdims=True)
    var = jnp.mean(jnp.square(xf - mu), axis=-1, keepdims=True)
    return ((xf - mu) * lax.rsqrt(var + EPS)).astype(x.dtype)


def rope(x, cos, sin):
    x1, x2 = jnp.split(x, 2, axis=-1)
    return jnp.concatenate([x1 * cos - x2 * sin, x2 * cos + x1 * sin], axis=-1)


def linear_scan_combine(left, right):
    a_l, b_l = left
    a_r, b_r = right
    return a_l * a_r, a_r * b_l + b_r


def rglru_branch(x_in, gate_in, conv_w, conv_b, w_r, b_r, w_i, b_i, lam):
    B, S, W = x_in.shape
    xc = lax.conv_general_dilated(
        x_in, conv_w[:, None, :], window_strides=(1,),
        padding=[(CONV_WIDTH // 2, CONV_WIDTH - 1 - CONV_WIDTH // 2)],
        dimension_numbers=("NWC", "WIO", "NWC"), feature_group_count=W) + conv_b
    xf = xc.astype(F32)
    xb = xf.reshape(B, S, LRU_BLOCKS, LRU_BLOCK)
    r = jax.nn.sigmoid(jnp.einsum("bsne,znef->zbsnf", xb, w_r.astype(F32)).reshape(2, B, S, W) + b_r[:, None, None, :])
    i = jax.nn.sigmoid(jnp.einsum("bsne,znef->zbsnf", xb, w_i.astype(F32)).reshape(2, B, S, W) + b_i[:, None, None, :])
    log_a = -LRU_C * r * jax.nn.softplus(-lam.astype(F32))[:, None, None, :]
    a = jnp.exp(log_a)
    u = jnp.sqrt(-jnp.expm1(2.0 * log_a)) * i * xf[None]
    _, h_fwd = lax.associative_scan(linear_scan_combine, (a[0], u[0]), axis=1)
    _, h_bwd = lax.associative_scan(linear_scan_combine, (a[1], u[1]), axis=1, reverse=True)
    return (h_fwd + h_bwd).astype(x_in.dtype) * jax.nn.gelu(gate_in)


def retention_branch(q, k, v, g, cos, sin):
    B, S, _ = q.shape
    H, dk, dv, C = RET_HEADS, RET_QK_DIM, RET_V_DIM, RET_CHUNK
    N = S // C
    q = rope(q.reshape(B, S, H, dk), cos, sin) * (dk ** -0.5)
    k = rope(k.reshape(B, S, H, dk), cos, sin)
    qc = q.reshape(B, N, C, H, dk).astype(F32)
    kc = k.reshape(B, N, C, H, dk).astype(F32)
    vc = v.reshape(B, N, C, H, dv).astype(F32)
    log_gamma = jnp.log1p(-jnp.exp2(-5.0 - jnp.arange(H, dtype=F32)))
    pos = jnp.arange(C, dtype=F32)

    def dec(p):
        return jnp.exp(p[:, None] * log_gamma[None, :])[..., None]

    intra = jnp.exp(jnp.abs(pos[:, None] - pos[None, :])[None] * log_gamma[:, None, None])
    scores = jnp.einsum("bnihd,bnjhd->bnhij", qc, kc) * intra
    o = jnp.einsum("bnhij,bnjhe->bnihe", scores, vc)
    kv_fwd = jnp.einsum("bnjhd,bnjhe->nbhde", kc * dec(C - 1 - pos), vc)
    kv_bwd = jnp.einsum("bnjhd,bnjhe->nbhde", kc * dec(pos), vc)
    chunk_decay = jnp.exp(C * log_gamma)[None, :, None, None]

    def carry_step(state, kv):
        return chunk_decay * state + kv, state

    zero = jnp.zeros((B, H, dk, dv), F32)
    _, st_fwd = lax.scan(carry_step, zero, kv_fwd)
    _, st_bwd = lax.scan(carry_step, zero, kv_bwd, reverse=True)
    o = (o + jnp.einsum("bnihd,nbhde->bnihe", qc * dec(pos + 1.0), st_fwd)
           + jnp.einsum("bnihd,nbhde->bnihe", qc * dec(C - pos), st_bwd))
    o = head_norm(o.reshape(B, S, H, dv)).reshape(B, S, H * dv)
    return jax.nn.silu(g) * o.astype(g.dtype)


def spatial_gating_branch(u, v, norm_g, w_s, b_s):
    B, S, W = u.shape
    N = S // SGU_CHUNK
    u = jax.nn.gelu(u)
    v = head_norm(jax.nn.gelu(v)) * norm_g
    vc = v.reshape(B, N, SGU_CHUNK, SGU_GROUPS, W // SGU_GROUPS)
    mixed = jnp.einsum("gij,bnjgd->bnigd", w_s, vc) + b_s.T[:, :, None]
    return u * mixed.reshape(B, S, W)


def rwkv7_branch(feat, mu_prev, mu_next, w0, w_up, a0, a_up, g_up, k_k, k_a, r_k):
    B, S, _ = feat.shape
    H, hd, W = RWKV_HEADS, RWKV_HEAD_DIM, MIX_WIDTH
    zero = jnp.zeros_like(feat[:, :1])
    prev = jnp.concatenate([zero, feat[:, :-1]], axis=1)
    nxt = jnp.concatenate([feat[:, 1:], zero], axis=1)
    feat = feat + mu_prev * (prev - feat) + mu_next * (nxt - feat)
    r, k, v, wd, ad, gd = jnp.split(
        feat, [W, 2 * W, 3 * W, 3 * W + 2 * DECAY_LORA, 3 * W + 2 * DECAY_LORA + 2 * ICLR_LORA], axis=-1)
    w_raw = jnp.einsum("bszr,zrw->zbsw", jnp.tanh(wd.reshape(B, S, 2, DECAY_LORA)), w_up) + w0[:, None, None, :]
    decay = jnp.exp(-DECAY_SCALE * jax.nn.sigmoid(w_raw.astype(F32)))
    a = jax.nn.sigmoid((jnp.einsum("bszr,zrw->zbsw", ad.reshape(B, S, 2, ICLR_LORA), a_up)
                        + a0[:, None, None, :]).astype(F32))
    g = jax.nn.sigmoid(gd) @ g_up
    kk = (k * k_k).reshape(B, S, H, hd).astype(F32)
    kk = kk * lax.rsqrt(jnp.sum(kk * kk, axis=-1, keepdims=True) + EPS)
    k_t = k.astype(F32)[None] * (1.0 + (a - 1.0) * k_a.astype(F32))
    rf = r.astype(F32).reshape(B, S, H, hd)
    vf = v.astype(F32).reshape(B, S, H, hd)

    def shared(t):
        return jnp.stack([t, jnp.flip(t, 1)])

    def own(t):
        return jnp.stack([t[0], jnp.flip(t[1], 1)]).reshape(2, B, S, H, hd)

    seq_in = (shared(rf), own(decay), shared(kk), own(a), own(k_t), shared(vf))
    seq_in = tuple(jnp.moveaxis(t, 2, 0) for t in seq_in)

    def step(state, inp):
        r_t, w_t, kk_t, a_t, k_tt, v_t = inp
        removal = jnp.einsum("zbhvk,zbhk->zbhv", state, kk_t)
        state = (state * w_t[..., None, :] - removal[..., None] * (a_t * kk_t)[..., None, :]
                 + v_t[..., None] * k_tt[..., None, :])
        return state, jnp.einsum("zbhvk,zbhk->zbhv", state, r_t)

    _, y = lax.scan(step, jnp.zeros((2, B, H, hd, hd), F32), seq_in)
    y = jnp.moveaxis(y, 0, 2)
    y = jnp.stack([y[0], jnp.flip(y[1], 1)])
    bonus = jnp.sum(rf[None] * k_t.reshape(2, B, S, H, hd) * r_k, axis=-1, keepdims=True) * vf[None]
    o = jnp.sum(head_norm(y) + bonus, axis=0).reshape(B, S, W)
    return o.astype(feat.dtype) * g


def hierarchical_moe(h, w_grp, b_grp, w_exp, b_exp, w_gu, w_down):
    B, S, D = h.shape
    T = B * S
    hf = h.reshape(T, D)
    grp_prob = jax.nn.softmax((hf @ w_grp + b_grp).astype(F32), axis=-1)
    grp_p, grp_i = lax.top_k(grp_prob, 1)
    exp_logits = (hf @ w_exp + b_exp).astype(F32).reshape(T, N_GROUPS, EXPERTS_PER_GROUP)
    in_group = exp_logits[jnp.arange(T), grp_i[:, 0]]
    top_p, top_i = lax.top_k(jax.nn.softmax(in_group, axis=-1), TOP_K)
    weights = grp_p * top_p / jnp.sum(top_p, axis=-1, keepdims=True)
    flat_e = (grp_i * EXPERTS_PER_GROUP + top_i).reshape(T * TOP_K)

    A = T * TOP_K
    order = jnp.argsort(flat_e)
    sorted_e = flat_e[order]
    counts = jnp.zeros((N_EXPERTS,), jnp.int32).at[flat_e].add(1)
    padded = (counts + MOE_BLOCK - 1) // MOE_BLOCK * MOE_BLOCK
    pad_end = jnp.cumsum(padded)
    pad_start = pad_end - padded
    start = jnp.cumsum(counts) - counts
    dest_sorted = pad_start[sorted_e] + jnp.arange(A, dtype=jnp.int32) - start[sorted_e]
    n_blocks = -(-A // MOE_BLOCK) + N_EXPERTS
    cap = n_blocks * MOE_BLOCK
    slot_token = jnp.full((cap,), T, jnp.int32).at[dest_sorted].set((order // TOP_K).astype(jnp.int32))
    block_expert = jnp.minimum(
        jnp.searchsorted(pad_end, jnp.arange(n_blocks, dtype=jnp.int32) * MOE_BLOCK, side="right"), N_EXPERTS - 1)
    x_pad = jnp.concatenate([hf, jnp.zeros((1, D), hf.dtype)], axis=0)
    xs = x_pad[slot_token].reshape(n_blocks, MOE_BLOCK, D)

    def expert_block(args):
        xb, e = args
        gate, up = jnp.split(xb @ w_gu[e], 2, axis=-1)
        return (jax.nn.silu(gate) * up) @ w_down[e]

    ys = lax.map(expert_block, (xs, block_expert)).reshape(cap, D)
    dest = jnp.zeros((A,), jnp.int32).at[order].set(dest_sorted)
    y_assign = ys[dest].reshape(T, TOP_K, D)
    return jnp.einsum("tk,tkd->td", weights.astype(hf.dtype), y_assign).reshape(B, S, D)


def setup_inputs(seed: int = 0) -> dict:
    key = jax.random.key(seed)
    ks = iter(jax.random.split(key, 48))
    L, D, W = DEPTH, D_MODEL, MIX_WIDTH

    def nrm(shape, scale):
        return scale * jax.random.normal(next(ks), shape, F32)

    def unif(shape, lo, hi):
        return jax.random.uniform(next(ks), shape, F32, lo, hi)

    lru_p = unif((L, 2, W), 0.9, 0.999) ** (1.0 / LRU_C)
    return {
        "x": nrm((BATCH, SEQ, D), 1.0),
        "c": nrm((BATCH, D), 1.0),
        "positions": jnp.broadcast_to(jnp.arange(SEQ, dtype=jnp.int32), (BATCH, SEQ)),
        "norm1_g": 1.0 + nrm((L, D), 0.02),
        "norm2_g": 1.0 + nrm((L, D), 0.02),
        "ada_w": nrm((L, D, 6 * D), 0.5 * D ** -0.5),
        "ada_b": nrm((L, 6 * D), 0.02),
        "w_in": nrm((L, D, IN_WIDTH), D ** -0.5),
        "lru_conv_w": nrm((L, CONV_WIDTH, W), CONV_WIDTH ** -0.5),
        "lru_conv_b": nrm((L, W), 0.02),
        "lru_w_r": nrm((L, 2, LRU_BLOCKS, LRU_BLOCK, LRU_BLOCK), LRU_BLOCK ** -0.5),
        "lru_b_r": nrm((L, 2, W), 0.1),
        "lru_w_i": nrm((L, 2, LRU_BLOCKS, LRU_BLOCK, LRU_BLOCK), LRU_BLOCK ** -0.5),
        "lru_b_i": nrm((L, 2, W), 0.1),
        "lru_lambda": jnp.log(lru_p) - jnp.log1p(-lru_p),
        "sgu_norm_g": 1.0 + nrm((L, W), 0.02),
        "sgu_w": nrm((L, SGU_GROUPS, SGU_CHUNK, SGU_CHUNK), 0.5 * SGU_CHUNK ** -0.5),
        "sgu_b": 1.0 + nrm((L, SGU_GROUPS, SGU_CHUNK), 0.1),
        "rwkv_mu_prev": unif((L, RWKV_FEAT), 0.0, 0.5),
        "rwkv_mu_next": unif((L, RWKV_FEAT), 0.0, 0.5),
        "rwkv_w0": -1.0 + nrm((L, 2, W), 0.5),
        "rwkv_w_up": nrm((L, 2, DECAY_LORA, W), 0.3 * DECAY_LORA ** -0.5),
        "rwkv_a0": nrm((L, 2, W), 0.5),
        "rwkv_a_up": nrm((L, 2, ICLR_LORA, W), 0.3 * ICLR_LORA ** -0.5),
        "rwkv_g_up": nrm((L, GATE_LORA, W), GATE_LORA ** -0.5),
        "rwkv_k_k": 0.85 + nrm((L, W), 0.05),
        "rwkv_k_a": 1.0 + nrm((L, W), 0.05),
        "rwkv_r_k": nrm((L, RWKV_HEADS, RWKV_HEAD_DIM), 0.1),
        "w_branch": nrm((L, N_BRANCH, W, D), W ** -0.5),
        "w_out": nrm((L, D, D), D ** -0.5),
        "router_grp_w": nrm((L, D, N_GROUPS), D ** -0.5),
        "router_grp_b": nrm((L, N_GROUPS), 0.01),
        "router_exp_w": nrm((L, D, N_EXPERTS), D ** -0.5),
        "router_exp_b": nrm((L, N_EXPERTS), 0.01),
        "expert_w_gu": nrm((L, N_EXPERTS, D, 2 * EXPERT_HIDDEN), D ** -0.5),
        "expert_w_down": nrm((L, N_EXPERTS, EXPERT_HIDDEN, D), EXPERT_HIDDEN ** -0.5),
        "final_norm_g": 1.0 + nrm((D,), 0.02),
    }


def reference(x, c, positions, norm1_g, norm2_g, ada_w, ada_b, w_in,
              lru_conv_w, lru_conv_b, lru_w_r, lru_b_r, lru_w_i, lru_b_i, lru_lambda,
              sgu_norm_g, sgu_w, sgu_b,
              rwkv_mu_prev, rwkv_mu_next, rwkv_w0, rwkv_w_up, rwkv_a0, rwkv_a_up, rwkv_g_up,
              rwkv_k_k, rwkv_k_a, rwkv_r_k,
              w_branch, w_out,
              router_grp_w, router_grp_b, router_exp_w, router_exp_b, expert_w_gu, expert_w_down,
              final_norm_g):
    B, S, D = x.shape
    inv_freq = ROPE_THETA ** (-jnp.arange(0, RET_QK_DIM, 2, dtype=F32) / RET_QK_DIM)
    ang = positions.astype(F32)[..., None] * inv_freq
    cos = jnp.cos(ang)[:, :, None, :].astype(x.dtype)
    sin = jnp.sin(ang)[:, :, None, :].astype(x.dtype)
    cond = jax.nn.silu(c)
    split_points = np.cumsum(IN_SPLITS)[:-1].tolist()
    for l in range(DEPTH):
        mod = cond @ ada_w[l] + ada_b[l]
        sh1, sc1, g1, sh2, sc2, g2 = [m[:, None, :] for m in jnp.split(mod, 6, axis=-1)]

        h = rms_norm(x, norm1_g[l]) * (1.0 + sc1) + sh1
        (lru_x, lru_g, ret_q, ret_k, ret_v, ret_g, sgu_u, sgu_v, rwkv_feat,
         merge_logits) = jnp.split(h @ w_in[l], split_points, axis=-1)
        y_a = rglru_branch(lru_x, lru_g, lru_conv_w[l], lru_conv_b[l], lru_w_r[l], lru_b_r[l],
                           lru_w_i[l], lru_b_i[l], lru_lambda[l])
        y_b = retention_branch(ret_q, ret_k, ret_v, ret_g, cos, sin)
        y_c = spatial_gating_branch(sgu_u, sgu_v, sgu_norm_g[l], sgu_w[l], sgu_b[l])
        y_d = rwkv7_branch(rwkv_feat, rwkv_mu_prev[l], rwkv_mu_next[l], rwkv_w0[l], rwkv_w_up[l],
                           rwkv_a0[l], rwkv_a_up[l], rwkv_g_up[l], rwkv_k_k[l], rwkv_k_a[l], rwkv_r_k[l])
        ys = jnp.stack([y_a, y_b, y_c, y_d], axis=2)
        branches = jnp.einsum("bsnw,nwd->bsnd", ys, w_branch[l])
        gates = jax.nn.sigmoid(merge_logits.reshape(B, S, N_BRANCH, D))
        merged = jnp.einsum("bsnd,bsnd->bsd", gates, branches)
        x = x + g1 * (merged @ w_out[l])

        h = rms_norm(x, norm2_g[l]) * (1.0 + sc2) + sh2
        x = x + g2 * hierarchical_moe(h, router_grp_w[l], router_grp_b[l], router_exp_w[l],
                                      router_exp_b[l], expert_w_gu[l], expert_w_down[l])
    return rms_norm(x, final_norm_g)
```

```python
import functools
import math

import jax
import jax.numpy as jnp
import numpy as np
from jax import lax
from jax.experimental import pallas as pl
from jax.experimental.pallas import tpu as pltpu

F32 = jnp.float32
BF16 = jnp.bfloat16
EPS = 1e-6

LANES = 128
SUBLANES = 8
VMEM_LIMIT_BYTES = 56 * 1024 * 1024

LRU_BLOCKS = 8
CONV_WIDTH = 4
LRU_C = 8.0
RET_HEADS = 8
RET_CHUNK = 128
ROPE_THETA = 10000.0
SGU_GROUPS = 8
SGU_CHUNK = 128
RWKV_HEAD_DIM = 64
DECAY_LORA = 64
ICLR_LORA = 64
GATE_LORA = 128
DECAY_SCALE = math.exp(-0.5)
N_GROUPS = 4
EXPERTS_PER_GROUP = 8
N_EXPERTS = N_GROUPS * EXPERTS_PER_GROUP
TOP_K = 2
MOE_BLOCK = 256
N_BRANCH = 4

RWKV_CHUNK = 64
HI = lax.Precision.HIGHEST


def _cparams(*sem):
    return pltpu.CompilerParams(dimension_semantics=sem, vmem_limit_bytes=VMEM_LIMIT_BYTES)


def _bdot(a, b):
    return jnp.dot(a.astype(BF16), b.astype(BF16), preferred_element_type=F32)


def _bdot_nt(a, b):
    return lax.dot_general(a.astype(BF16), b.astype(BF16), (((1,), (1,)), ((), ())),
                           preferred_element_type=F32)


def _gelu(x):
    return 0.5 * x * (1.0 + jnp.tanh(math.sqrt(2.0 / math.pi) * (x + 0.044715 * (x * x * x))))


def _sigmoid(x):
    return 1.0 / (1.0 + jnp.exp(-x))


def _silu(x):
    return x * _sigmoid(x)


def _iota(shape, dim):
    return lax.broadcasted_iota(jnp.int32, shape, dim)


def _ada_kernel(c_ref, w_ref, b_ref, o_ref):
    cond = _silu(c_ref[...])
    o_ref[0] = _bdot(cond, w_ref[0]) + b_ref[0]


def _ada(c_pad, ada_w, ada_b):
    L, D, N = ada_w.shape
    R = c_pad.shape[0]
    tn = 1024
    return pl.pallas_call(
        _ada_kernel,
        out_shape=jax.ShapeDtypeStruct((L, R, N), F32),
        grid=(L, N // tn),
        in_specs=[pl.BlockSpec((R, D), lambda l, j: (0, 0)),
                  pl.BlockSpec((1, D, tn), lambda l, j: (l, 0, j)),
                  pl.BlockSpec((1, 1, tn), lambda l, j: (l, 0, j))],
        out_specs=pl.BlockSpec((1, R, tn), lambda l, j: (l, 0, j)),
        compiler_params=_cparams("parallel", "parallel"),
        name="ada_mod",
    )(c_pad, ada_w, ada_b.reshape(L, 1, N))


def _normmod_kernel(x_ref, g_ref, sc_ref, sh_ref, o_ref):
    x = x_ref[0]
    y = x * lax.rsqrt(jnp.mean(x * x, axis=-1, keepdims=True) + EPS) * g_ref[...]
    o_ref[0] = (y * (1.0 + sc_ref[0]) + sh_ref[0]).astype(o_ref.dtype)


def _normmod(x, g, sc, sh, out_dtype):
    B, S, D = x.shape
    ts = min(512, S)
    return pl.pallas_call(
        _normmod_kernel,
        out_shape=jax.ShapeDtypeStruct((B, S, D), out_dtype),
        grid=(B, S // ts),
        in_specs=[pl.BlockSpec((1, ts, D), lambda b, i: (b, i, 0)),
                  pl.BlockSpec((1, D), lambda b, i: (0, 0)),
                  pl.BlockSpec((1, 1, D), lambda b, i: (b, 0, 0)),
                  pl.BlockSpec((1, 1, D), lambda b, i: (b, 0, 0))],
        out_specs=pl.BlockSpec((1, ts, D), lambda b, i: (b, i, 0)),
        compiler_params=_cparams("parallel", "parallel"),
        name="norm_mod",
    )(x, g.reshape(1, D), sc, sh)


def _mm_kernel(a_ref, w_ref, o_ref):
    o_ref[...] = jnp.dot(a_ref[...], w_ref[...], preferred_element_type=F32).astype(o_ref.dtype)


def _mm(a, w, tm, tn, out_dtype=F32, name="proj"):
    M, K = a.shape
    N = w.shape[1]
    return pl.pallas_call(
        _mm_kernel,
        out_shape=jax.ShapeDtypeStruct((M, N), out_dtype),
        grid=(M // tm, N // tn),
        in_specs=[pl.BlockSpec((tm, K), lambda i, j: (i, 0)),
                  pl.BlockSpec((K, tn), lambda i, j: (0, j))],
        out_specs=pl.BlockSpec((tm, tn), lambda i, j: (i, j)),
        compiler_params=_cparams("parallel", "parallel"),
        name=name,
    )(a, w)


def _mm_res_kernel(a_ref, w_ref, x_ref, g_ref, o_ref):
    y = jnp.dot(a_ref[...], w_ref[...], preferred_element_type=F32)
    o_ref[...] = x_ref[...] + g_ref[0] * y


def _mm_residual(a, w, x2d, gate, rows_per_batch, tm, tn, name="out_proj"):
    M, K = a.shape
    N = w.shape[1]
    tiles_per_batch = rows_per_batch // tm
    return pl.pallas_call(
        _mm_res_kernel,
        out_shape=jax.ShapeDtypeStruct((M, N), F32),
        grid=(M // tm, N // tn),
        in_specs=[pl.BlockSpec((tm, K), lambda i, j: (i, 0)),
                  pl.BlockSpec((K, tn), lambda i, j: (0, j)),
                  pl.BlockSpec((tm, tn), lambda i, j: (i, j)),
                  pl.BlockSpec((1, 1, tn), lambda i, j: (i // tiles_per_batch, 0, j))],
        out_specs=pl.BlockSpec((tm, tn), lambda i, j: (i, j)),
        compiler_params=_cparams("parallel", "parallel"),
        name=name,
    )(a, w, x2d, gate)


def _softplus(x):
    return jnp.maximum(x, 0.0) + jnp.log1p(jnp.exp(-jnp.abs(x)))


def _lru_kernel(x_ref, gate_ref, cw_ref, cb_ref, wr_ref, br_ref, wi_ref, bi_ref, lam_ref,
                o_ref, a_scr, u_scr):
    S = x_ref.shape[1]
    x = x_ref[0]
    rows = _iota(x.shape, 0)
    xm2 = jnp.where(rows >= 2, pltpu.roll(x, 2, 0), 0.0)
    xm1 = jnp.where(rows >= 1, pltpu.roll(x, 1, 0), 0.0)
    xp1 = jnp.where(rows < S - 1, pltpu.roll(x, S - 1, 0), 0.0)
    xc = (cw_ref[0:1, :] * xm2 + cw_ref[1:2, :] * xm1 + cw_ref[2:3, :] * x
          + cw_ref[3:4, :] * xp1 + cb_ref[...])
    for z in range(2):
        r = _sigmoid(_bdot(xc, wr_ref[z, 0]) + br_ref[z:z + 1, :])
        i = _sigmoid(_bdot(xc, wi_ref[z, 0]) + bi_ref[z:z + 1, :])
        log_a = -LRU_C * r * _softplus(-lam_ref[z:z + 1, :])
        a = jnp.exp(log_a)
        a_scr[z] = a
        u_scr[z] = jnp.sqrt(-jnp.tanh(log_a) * (a * a + 1.0)) * i * xc

    n_tiles = S // SUBLANES
    L = x.shape[1]
    trow = _iota((SUBLANES, L), 0)

    def tile_scan(a, u, reverse):
        for d in (1, 2, 4):
            if reverse:
                keep = trow < SUBLANES - d
                a_s = jnp.where(keep, pltpu.roll(a, SUBLANES - d, 0), 1.0)
                u_s = jnp.where(keep, pltpu.roll(u, SUBLANES - d, 0), 0.0)
            else:
                keep = trow >= d
                a_s = jnp.where(keep, pltpu.roll(a, d, 0), 1.0)
                u_s = jnp.where(keep, pltpu.roll(u, d, 0), 0.0)
            u = u + a * u_s
            a = a * a_s
        return a, u

    def body(k, carry):
        hf, hb = carry
        f0 = pl.multiple_of(k * SUBLANES, SUBLANES)
        b0 = pl.multiple_of((n_tiles - 1 - k) * SUBLANES, SUBLANES)
        af, uf = tile_scan(a_scr[0, pl.ds(f0, SUBLANES), :], u_scr[0, pl.ds(f0, SUBLANES), :], False)
        ab, ub = tile_scan(a_scr[1, pl.ds(b0, SUBLANES), :], u_scr[1, pl.ds(b0, SUBLANES), :], True)
        tf = uf + af * hf
        tb = ub + ab * hb
        u_scr[0, pl.ds(f0, SUBLANES), :] = tf
        u_scr[1, pl.ds(b0, SUBLANES), :] = tb
        return tf[SUBLANES - 1:SUBLANES, :], tb[0:1, :]

    zero = jnp.zeros((1, L), F32)
    lax.fori_loop(0, n_tiles, body, (zero, zero), unroll=4)
    o_ref[0] = ((u_scr[0] + u_scr[1]) * _gelu(gate_ref[0])).astype(o_ref.dtype)


def _lru(proj, B, S, W, x_col, g_col, cw, cb, wr, br, wi, bi, lam):
    nb = W // LANES
    return pl.pallas_call(
        _lru_kernel,
        out_shape=jax.ShapeDtypeStruct((B, S, W), BF16),
        grid=(B, nb),
        in_specs=[pl.BlockSpec((1, S, LANES), lambda b, j: (b, 0, x_col + j)),
                  pl.BlockSpec((1, S, LANES), lambda b, j: (b, 0, g_col + j)),
                  pl.BlockSpec((CONV_WIDTH, LANES), lambda b, j: (0, j)),
                  pl.BlockSpec((1, LANES), lambda b, j: (0, j)),
                  pl.BlockSpec((2, 1, LANES, LANES), lambda b, j: (0, j, 0, 0)),
                  pl.BlockSpec((2, LANES), lambda b, j: (0, j)),
                  pl.BlockSpec((2, 1, LANES, LANES), lambda b, j: (0, j, 0, 0)),
                  pl.BlockSpec((2, LANES), lambda b, j: (0, j)),
                  pl.BlockSpec((2, LANES), lambda b, j: (0, j))],
        out_specs=pl.BlockSpec((1, S, LANES), lambda b, j: (b, 0, j)),
        scratch_shapes=[pltpu.VMEM((2, S, LANES), F32), pltpu.VMEM((2, S, LANES), F32)],
        compiler_params=_cparams("parallel", "parallel"),
        name="rglru",
    )(proj, proj, cw, cb.reshape(1, W), wr, br, wi, bi, lam)


def _rope_kernel(ang_ref, cos_ref, sin_ref):
    ang = ang_ref[0]
    lane = _iota(ang.shape, 1)
    cos_ref[0] = jnp.cos(ang)
    s = jnp.sin(ang)
    sin_ref[0] = jnp.where((lane % 64) < 32, -s, s)


def _rope_tables(ang):
    B, S, _ = ang.shape
    ts = min(512, S)
    spec = pl.BlockSpec((1, ts, LANES), lambda b, i: (b, i, 0))
    return pl.pallas_call(
        _rope_kernel,
        out_shape=(jax.ShapeDtypeStruct((B, S, LANES), F32),) * 2,
        grid=(B, S // ts),
        in_specs=[spec],
        out_specs=(spec, spec),
        compiler_params=_cparams("parallel", "parallel"),
        name="rope_tables",
    )(ang)


def _ret_kernel(q_ref, k_ref, v_ref, g_ref, cos_ref, sin_ref, lg_ref, o_ref,
                qs_scr, ks_scr, kvf_scr, kvb_scr):
    S = q_ref.shape[1]
    C = RET_CHUNK
    N = S // C
    dk = 64
    p = pl.program_id(1)
    lane = _iota((S, LANES), 1)
    first_half = (lane % 64) < 32

    def rope(x):
        swapped = jnp.where(first_half, pltpu.roll(x, LANES - 32, 1), pltpu.roll(x, 32, 1))
        return x * cos_ref[0] + swapped * sin_ref[0]

    qs_scr[...] = rope(q_ref[0]) * (dk ** -0.5)
    ks_scr[...] = rope(k_ref[0])

    pos_r = _iota((C, 1), 0).astype(F32)
    ii = _iota((C, C), 0)
    jj = _iota((C, C), 1)
    dist = jnp.abs(ii - jj).astype(F32)
    clane = _iota((C, LANES), 1)

    for hh in range(2):
        lg = lg_ref[pl.ds(2 * p + hh, 1), :][:, 0:1]
        hmask = (clane // 64) == hh
        intra = jnp.exp(dist * lg)
        dec_kf = jnp.exp((C - 1.0 - pos_r) * lg)
        dec_kb = jnp.exp(pos_r * lg)
        dec_qf = jnp.exp((pos_r + 1.0) * lg)
        dec_qb = jnp.exp((C - pos_r) * lg)
        chunk_decay = jnp.exp(C * lg)

        def kv_body(n, _):
            r0 = pl.multiple_of(n * C, C)
            kc = jnp.where(hmask, ks_scr[pl.ds(r0, C), :], 0.0)
            vc = v_ref[0, pl.ds(r0, C), hh * LANES:(hh + 1) * LANES]
            kvf_scr[n] = _bdot((kc * dec_kf).T, vc)
            kvb_scr[n] = _bdot((kc * dec_kb).T, vc)
            return 0

        lax.fori_loop(0, N, kv_body, 0)

        def fwd_body(n, st):
            cur = kvf_scr[n]
            kvf_scr[n] = st
            return chunk_decay * st + cur

        lax.fori_loop(0, N, fwd_body, jnp.zeros((LANES, LANES), F32))

        def bwd_body(m, st):
            n = N - 1 - m
            cur = kvb_scr[n]
            kvb_scr[n] = st
            return chunk_decay * st + cur

        lax.fori_loop(0, N, bwd_body, jnp.zeros((LANES, LANES), F32))

        def out_body(n, _):
            r0 = pl.multiple_of(n * C, C)
            qc = qs_scr[pl.ds(r0, C), :]
            kc = jnp.where(hmask, ks_scr[pl.ds(r0, C), :], 0.0)
            vc = v_ref[0, pl.ds(r0, C), hh * LANES:(hh + 1) * LANES]
            scores = _bdot_nt(qc, kc) * intra
            o = _bdot(scores, vc)
            o = o + _bdot(qc * dec_qf, kvf_scr[n]) + _bdot(qc * dec_qb, kvb_scr[n])
            mu = jnp.mean(o, axis=-1, keepdims=True)
            var = jnp.mean(jnp.square(o - mu), axis=-1, keepdims=True)
            o = (o - mu) * lax.rsqrt(var + EPS)
            gc = g_ref[0, pl.ds(r0, C), hh * LANES:(hh + 1) * LANES]
            o_ref[0, pl.ds(r0, C), hh * LANES:(hh + 1) * LANES] = (_silu(gc) * o).astype(o_ref.dtype)
            return 0

        lax.fori_loop(0, N, out_body, 0)


def _retention(proj, B, S, W, q_col, k_col, v_col, g_col, cosf, sinf, lg_tab):
    n_pairs = RET_HEADS // 2
    N = S // RET_CHUNK
    return pl.pallas_call(
        _ret_kernel,
        out_shape=jax.ShapeDtypeStruct((B, S, W), BF16),
        grid=(B, n_pairs),
        in_specs=[pl.BlockSpec((1, S, LANES), lambda b, p: (b, 0, q_col + p)),
                  pl.BlockSpec((1, S, LANES), lambda b, p: (b, 0, k_col + p)),
                  pl.BlockSpec((1, S, 2 * LANES), lambda b, p: (b, 0, v_col + p)),
                  pl.BlockSpec((1, S, 2 * LANES), lambda b, p: (b, 0, g_col + p)),
                  pl.BlockSpec((1, S, LANES), lambda b, p: (b, 0, 0)),
                  pl.BlockSpec((1, S, LANES), lambda b, p: (b, 0, 0)),
                  pl.BlockSpec((RET_HEADS, LANES), lambda b, p: (0, 0))],
        out_specs=pl.BlockSpec((1, S, 2 * LANES), lambda b, p: (b, 0, p)),
        scratch_shapes=[pltpu.VMEM((S, LANES), F32), pltpu.VMEM((S, LANES), F32),
                        pltpu.VMEM((N, LANES, LANES), F32), pltpu.VMEM((N, LANES, LANES), F32)],
        compiler_params=_cparams("parallel", "parallel"),
        name="retention",
    )(proj, proj, proj, proj, cosf, sinf, lg_tab)


def _sgu_kernel(u_ref, v_ref, ng_ref, w_ref, bt_ref, o_ref):
    ts = u_ref.shape[1]
    C = SGU_CHUNK
    v = _gelu(v_ref[0])
    mu = jnp.mean(v, axis=-1, keepdims=True)
    var = jnp.mean(jnp.square(v - mu), axis=-1, keepdims=True)
    v = ((v - mu) * lax.rsqrt(var + EPS) * ng_ref[...]).astype(BF16)
    for c in range(ts // C):
        for g in range(SGU_GROUPS):
            vc = v[c * C:(c + 1) * C, g * LANES:(g + 1) * LANES]
            mixed = jnp.dot(w_ref[g].astype(BF16), vc, preferred_element_type=F32) + bt_ref[:, g:g + 1]
            uc = _gelu(u_ref[0, c * C:(c + 1) * C, g * LANES:(g + 1) * LANES])
            o_ref[0, c * C:(c + 1) * C, g * LANES:(g + 1) * LANES] = (uc * mixed).astype(o_ref.dtype)


def _sgu(proj, B, S, W, u_col, v_col, norm_g, w_s, b_s):
    ts = min(512, S)
    return pl.pallas_call(
        _sgu_kernel,
        out_shape=jax.ShapeDtypeStruct((B, S, W), BF16),
        grid=(B, S // ts),
        in_specs=[pl.BlockSpec((1, ts, W), lambda b, i: (b, i, u_col)),
                  pl.BlockSpec((1, ts, W), lambda b, i: (b, i, v_col)),
                  pl.BlockSpec((1, W), lambda b, i: (0, 0)),
                  pl.BlockSpec((SGU_GROUPS, SGU_CHUNK, SGU_CHUNK), lambda b, i: (0, 0, 0)),
                  pl.BlockSpec((SGU_CHUNK, SGU_GROUPS), lambda b, i: (0, 0))],
        out_specs=pl.BlockSpec((1, ts, W), lambda b, i: (b, i, 0)),
        compiler_params=_cparams("parallel", "parallel"),
        name="spatial_gating",
    )(proj, proj, norm_g.reshape(1, W), w_s, b_s.T)


def _head_sum(x, hd):
    seg = (_iota((LANES, LANES), 0) // hd == _iota((LANES, LANES), 1) // hd).astype(F32)
    parts = [jnp.dot(x[:, j * LANES:(j + 1) * LANES], seg, precision=HI, preferred_element_type=F32)
             for j in range(x.shape[-1] // LANES)]
    return jnp.concatenate(parts, axis=1)


def _rwkv_pre_kernel(f_ref, fp_ref, fn_ref, mup_ref, mun_ref, w0_ref, wup_ref, a0_ref, aup_ref,
                     gup_ref, kk_ref, ka_ref, rk_ref,
                     r_out, kk_out, v_out, g_out, bonus_out, lw_out, bb_out, kt_out):
    ts = f_ref.shape[1]
    W = r_out.shape[2]
    i = pl.program_id(1)
    n_i = pl.num_programs(1)
    f = f_ref[0]
    rows = _iota(f.shape, 0)
    prev_row = jnp.where(i > 0, fp_ref[0, SUBLANES - 1:SUBLANES, :], 0.0)
    next_row = jnp.where(i < n_i - 1, fn_ref[0, 0:1, :], 0.0)
    prev = jnp.where(rows >= 1, pltpu.roll(f, 1, 0), prev_row)
    nxt = jnp.where(rows < ts - 1, pltpu.roll(f, ts - 1, 0), next_row)
    f = f + mup_ref[...] * (prev - f) + mun_ref[...] * (nxt - f)

    r = f[:, 0:W]
    k = f[:, W:2 * W]
    v = f[:, 2 * W:3 * W]
    o = 3 * W
    wd = jnp.tanh(f[:, o:o + 2 * DECAY_LORA])
    ad = f[:, o + 2 * DECAY_LORA:o + 2 * DECAY_LORA + 2 * ICLR_LORA]
    gd = f[:, o + 2 * DECAY_LORA + 2 * ICLR_LORA:]

    g_out[0] = _bdot(_sigmoid(gd), gup_ref[...])
    kk = k * kk_ref[...]
    kk = kk * lax.rsqrt(_head_sum(kk * kk, RWKV_HEAD_DIM) + EPS)
    r_out[0] = r
    kk_out[0] = kk
    v_out[0] = v
    bonus = jnp.zeros_like(r)
    for z in range(2):
        w_raw = _bdot(wd[:, z * DECAY_LORA:(z + 1) * DECAY_LORA], wup_ref[z]) + w0_ref[z:z + 1, :]
        lw_out[z, 0] = -DECAY_SCALE * _sigmoid(w_raw)
        a = _sigmoid(_bdot(ad[:, z * ICLR_LORA:(z + 1) * ICLR_LORA], aup_ref[z]) + a0_ref[z:z + 1, :])
        bb_out[z, 0] = a * kk
        kt = k * (1.0 + (a - 1.0) * ka_ref[...])
        kt_out[z, 0] = kt
        bonus = bonus + _head_sum(r * kt * rk_ref[...], RWKV_HEAD_DIM) * v
    bonus_out[0] = bonus


def _rwkv_pre(feat, mu_prev, mu_next, w0, w_up, a0, a_up, g_up, k_k, k_a, r_k):
    B, S, Fw = feat.shape
    W = w0.shape[1]
    ts = min(256, S)
    hb = ts // SUBLANES
    n_hb = S // SUBLANES
    row = lambda a: a.reshape(1, -1)
    full2 = lambda a: pl.BlockSpec(a.shape, lambda b, i: (0, 0))
    full3 = lambda a: pl.BlockSpec(a.shape, lambda b, i: (0, 0, 0))
    bsw = pl.BlockSpec((1, ts, W), lambda b, i: (b, i, 0))
    zsw = pl.BlockSpec((2, 1, ts, W), lambda b, i: (0, b, i, 0))
    sds = jax.ShapeDtypeStruct((B, S, W), F32)
    zds = jax.ShapeDtypeStruct((2, B, S, W), F32)
    args = (row(mu_prev), row(mu_next), w0, w_up, a0, a_up, g_up, row(k_k), row(k_a), row(r_k))
    specs = [full2(args[0]), full2(args[1]), full2(w0), full3(w_up), full2(a0), full3(a_up),
             full2(g_up), full2(args[7]), full2(args[8]), full2(args[9])]
    return pl.pallas_call(
        _rwkv_pre_kernel,
        out_shape=(sds, sds, sds, sds, sds, zds, zds, zds),
        grid=(B, S // ts),
        in_specs=[pl.BlockSpec((1, ts, Fw), lambda b, i: (b, i, 0)),
                  pl.BlockSpec((1, SUBLANES, Fw), lambda b, i: (b, jnp.maximum(i * hb - 1, 0), 0)),
                  pl.BlockSpec((1, SUBLANES, Fw), lambda b, i: (b, jnp.minimum((i + 1) * hb, n_hb - 1), 0)),
                  ] + specs,
        out_specs=(bsw, bsw, bsw, bsw, bsw, zsw, zsw, zsw),
        compiler_params=_cparams("parallel", "parallel"),
        name="rwkv_prepare",
    )(feat, feat, feat, *args)


def _rwkv_scan_kernel(r_ref, kk_ref, v_ref, lw_ref, bb_ref, kt_ref, y_ref, st_scr):
    C = RWKV_CHUNK
    C2 = 2 * C
    z = pl.program_id(0)
    i = pl.program_id(2)
    n_pairs = r_ref.shape[2] // LANES
    fwd = z == 0

    @pl.when(i == 0)
    def _():
        st_scr[...] = jnp.zeros_like(st_scr)

    ti = _iota((C, C), 0)
    si = _iota((C, C), 1)
    sign = jnp.where(fwd, 1, -1)
    cum_mat = (sign * (ti - si) >= 0).astype(F32)
    er = _iota((C2, C2), 0)
    ec = _iota((C2, C2), 1)
    lead = sign * (er % C - ec % C)
    strict = lead > 0
    incl = lead >= 0
    eye2 = (er == ec).astype(F32)
    row_head = _iota((C2, LANES), 0) // C
    lane_head = _iota((C2, LANES), 1) // RWKV_HEAD_DIM
    emask = row_head == lane_head
    diag_l = _iota((LANES, LANES), 0) == _iota((LANES, LANES), 1)

    def expand(x):
        return jnp.where(emask, jnp.concatenate([x, x], axis=0), 0.0)

    for j in range(n_pairs):
        sl = slice(j * LANES, (j + 1) * LANES)
        lw = lw_ref[0, 0, :, sl]
        cum = jnp.dot(cum_mat, lw, precision=HI, preferred_element_type=F32)
        tot = jnp.sum(lw, axis=0, keepdims=True)
        p_incl = jnp.exp(cum)
        p_excl = jnp.exp(cum - lw)
        p_inv = jnp.exp(-cum)
        p_rest = jnp.exp(tot - cum)
        kk = kk_ref[0, :, sl]
        bb = bb_ref[0, 0, :, sl]
        kt = kt_ref[0, 0, :, sl]
        kh_e = expand(kk * p_excl)
        rh_e = expand(r_ref[0, :, sl] * p_incl)
        bh_e = expand(bb * p_inv)
        ktil_e = expand(kt * p_inv)
        b_rest_e = expand(bb * p_rest)
        k_rest_e = expand(kt * p_rest)
        v_e = expand(v_ref[0, :, sl])

        a_b = jnp.where(strict, _bdot_nt(kh_e, bh_e), 0.0)
        a_k = jnp.where(strict, _bdot_nt(kh_e, ktil_e), 0.0)
        a_rk = jnp.where(incl, _bdot_nt(rh_e, ktil_e), 0.0)
        a_rb = jnp.where(incl, _bdot_nt(rh_e, bh_e), 0.0)

        pw = -a_b
        t_inv = eye2 + pw
        for _ in range(int(math.log2(C)) - 1):
            pw = _bdot(pw, pw)
            t_inv = t_inv + _bdot(t_inv, pw)

        w_e = _bdot(t_inv, kh_e)
        u0_e = _bdot(t_inv, _bdot(a_k, v_e))
        st = st_scr[j]
        u_e = _bdot(w_e, st) + u0_e
        y_e = _bdot(rh_e, st) + _bdot(a_rk, v_e) - _bdot(a_rb, u_e)
        y_ref[0, 0, :, sl] = y_e[0:C, :] + y_e[C:C2, :]

        m_t = jnp.where(diag_l, jnp.exp(tot), 0.0) - _bdot(b_rest_e.T, w_e)
        g_t = _bdot(k_rest_e.T, v_e) - _bdot(b_rest_e.T, u0_e)
        st_scr[j] = jnp.dot(m_t, st, precision=HI, preferred_element_type=F32) + g_t


def _rwkv_scan(r, kk, v, lw, bb, kt):
    B, S, W = r.shape
    C = RWKV_CHUNK
    nc = S // C
    chunk = lambda z, i: i + z * (nc - 1 - 2 * i)
    shared = pl.BlockSpec((1, C, W), lambda z, b, i: (b, chunk(z, i), 0))
    perdir = pl.BlockSpec((1, 1, C, W), lambda z, b, i: (z, b, chunk(z, i), 0))
    return pl.pallas_call(
        _rwkv_scan_kernel,
        out_shape=jax.ShapeDtypeStruct((2, B, S, W), F32),
        grid=(2, B, nc),
        in_specs=[shared, shared, shared, perdir, perdir, perdir],
        out_specs=perdir,
        scratch_shapes=[pltpu.VMEM((W // LANES, LANES, LANES), F32)],
        compiler_params=_cparams("parallel", "parallel", "arbitrary"),
        name="rwkv_scan",
    )(r, kk, v, lw, bb, kt)


def _rwkv_post_kernel(y_ref, bonus_ref, g_ref, o_ref):
    hd = RWKV_HEAD_DIM
    acc = bonus_ref[0]
    for z in range(2):
        y = y_ref[z, 0]
        d = y - _head_sum(y, hd) * (1.0 / hd)
        var = _head_sum(d * d, hd) * (1.0 / hd)
        acc = acc + d * lax.rsqrt(var + EPS)
    o_ref[0] = (acc * g_ref[0]).astype(o_ref.dtype)


def _rwkv_post(y, bonus, g):
    _, B, S, W = y.shape
    ts = min(256, S)
    bsw = pl.BlockSpec((1, ts, W), lambda b, i: (b, i, 0))
    return pl.pallas_call(
        _rwkv_post_kernel,
        out_shape=jax.ShapeDtypeStruct((B, S, W), BF16),
        grid=(B, S // ts),
        in_specs=[pl.BlockSpec((2, 1, ts, W), lambda b, i: (0, b, i, 0)), bsw, bsw],
        out_specs=bsw,
        compiler_params=_cparams("parallel", "parallel"),
        name="rwkv_finish",
    )(y, bonus, g)


def _merge_kernel(ya_ref, yb_ref, yc_ref, yd_ref, wb_ref, l0_ref, l1_ref, l2_ref, l3_ref, o_ref):
    acc = None
    for n, (y_ref, l_ref) in enumerate(((ya_ref, l0_ref), (yb_ref, l1_ref), (yc_ref, l2_ref),
                                        (yd_ref, l3_ref))):
        br = jnp.dot(y_ref[...], wb_ref[n], preferred_element_type=F32)
        t = _sigmoid(l_ref[...]) * br
        acc = t if acc is None else acc + t
    o_ref[...] = acc.astype(o_ref.dtype)


def _merge(ys, wb, logits, tm, tn):
    T, W = ys[0].shape
    D = wb.shape[2]
    nj = D // tn
    yspec = pl.BlockSpec((tm, W), lambda i, j: (i, 0))
    lspec = lambda n: pl.BlockSpec((tm, tn), lambda i, j: (i, n * nj + j))
    return pl.pallas_call(
        _merge_kernel,
        out_shape=jax.ShapeDtypeStruct((T, D), BF16),
        grid=(T // tm, nj),
        in_specs=[yspec, yspec, yspec, yspec,
                  pl.BlockSpec((N_BRANCH, W, tn), lambda i, j: (0, 0, j)),
                  lspec(0), lspec(1), lspec(2), lspec(3)],
        out_specs=pl.BlockSpec((tm, tn), lambda i, j: (i, j)),
        compiler_params=_cparams("parallel", "parallel"),
        name="branch_merge",
    )(*ys, wb, logits, logits, logits, logits)


def _router_kernel(h_ref, w_ref, b_ref, o_ref):
    logits = jnp.dot(h_ref[...], w_ref[...], preferred_element_type=F32) + b_ref[...]
    lane = _iota(logits.shape, 1)
    lane_f = lane.astype(F32)
    neg = -3.0e38
    far = float(LANES)
    is_grp = lane < N_GROUPS
    gl = jnp.where(is_grp, logits, neg)
    gmax = jnp.max(gl, axis=-1, keepdims=True)
    gsum = jnp.sum(jnp.where(is_grp, jnp.exp(gl - gmax), 0.0), axis=-1, keepdims=True)
    grp_p = 1.0 / gsum
    grp_i = jnp.min(jnp.where(gl == gmax, lane_f, far), axis=-1, keepdims=True)
    lo = N_GROUPS + grp_i * EXPERTS_PER_GROUP
    in_grp = jnp.logical_and(lane_f >= lo, lane_f < lo + EXPERTS_PER_GROUP)
    el = jnp.where(in_grp, logits, neg)
    m1 = jnp.max(el, axis=-1, keepdims=True)
    i1 = jnp.min(jnp.where(el == m1, lane_f, far), axis=-1, keepdims=True)
    el2 = jnp.where(lane_f == i1, neg, el)
    m2 = jnp.max(el2, axis=-1, keepdims=True)
    i2 = jnp.min(jnp.where(el2 == m2, lane_f, far), axis=-1, keepdims=True)
    t = jnp.exp(m2 - m1)
    w1 = grp_p / (1.0 + t)
    w2 = grp_p * t / (1.0 + t)
    o_ref[...] = jnp.where(lane == 0, w1, jnp.where(lane == 1, w2, jnp.where(
        lane == 2, i1 - N_GROUPS, jnp.where(lane == 3, i2 - N_GROUPS, 0.0))))


def _router(h, w_r, b_r):
    T, D = h.shape
    tm = min(1024, T)
    return pl.pallas_call(
        _router_kernel,
        out_shape=jax.ShapeDtypeStruct((T, LANES), F32),
        grid=(T // tm,),
        in_specs=[pl.BlockSpec((tm, D), lambda i: (i, 0)),
                  pl.BlockSpec((D, LANES), lambda i: (0, 0)),
                  pl.BlockSpec((1, LANES), lambda i: (0, 0))],
        out_specs=pl.BlockSpec((tm, LANES), lambda i: (i, 0)),
        compiler_params=_cparams("parallel"),
        name="router",
    )(h, w_r, b_r)


def _expert_kernel(be_ref, nb_ref, xs_ref, wgu_ref, wd_ref, o_ref, wgu_bf, wd_bf):
    i = pl.program_id(0)
    changed = jnp.logical_or(i == 0, be_ref[i] != be_ref[jnp.maximum(i - 1, 0)])

    @pl.when(changed)
    def _():
        wgu_bf[...] = wgu_ref[0].astype(BF16)
        wd_bf[...] = wd_ref[0].astype(BF16)

    @pl.when(i < nb_ref[0])
    def _():
        gu = jnp.dot(xs_ref[...], wgu_bf[...], preferred_element_type=F32)
        eh = gu.shape[1] // 2
        mid = (_silu(gu[:, :eh]) * gu[:, eh:]).astype(BF16)
        o_ref[...] = jnp.dot(mid, wd_bf[...], preferred_element_type=F32)

    @pl.when(i >= nb_ref[0])
    def _():
        o_ref[...] = jnp.zeros_like(o_ref)


def _experts(xs, block_expert, n_used, w_gu, w_down):
    cap, D = xs.shape
    E, _, H2 = w_gu.shape
    n_blocks = cap // MOE_BLOCK
    grid_spec = pltpu.PrefetchScalarGridSpec(
        num_scalar_prefetch=2,
        grid=(n_blocks,),
        in_specs=[pl.BlockSpec((MOE_BLOCK, D), lambda i, be, nb: (i, 0)),
                  pl.BlockSpec((1, D, H2), lambda i, be, nb: (be[i], 0, 0)),
                  pl.BlockSpec((1, H2 // 2, D), lambda i, be, nb: (be[i], 0, 0))],
        out_specs=pl.BlockSpec((MOE_BLOCK, D), lambda i, be, nb: (i, 0)),
        scratch_shapes=[pltpu.VMEM((D, H2), BF16), pltpu.VMEM((H2 // 2, D), BF16)],
    )
    return pl.pallas_call(
        _expert_kernel,
        out_shape=jax.ShapeDtypeStruct((cap, D), F32),
        grid_spec=grid_spec,
        compiler_params=_cparams("arbitrary"),
        name="experts",
    )(block_expert, n_used, xs, w_gu, w_down)


def _combine_kernel(x_ref, y0_ref, y1_ref, r_ref, g_ref, o_ref):
    w0 = r_ref[:, 0:1]
    w1 = r_ref[:, 1:2]
    o_ref[...] = x_ref[...] + g_ref[0] * (w0 * y0_ref[...] + w1 * y1_ref[...])


def _combine(x2d, y0, y1, route, gate, rows_per_batch):
    T, D = x2d.shape
    tm = min(512, T)
    tiles_per_batch = rows_per_batch // tm
    spec = pl.BlockSpec((tm, D), lambda i: (i, 0))
    return pl.pallas_call(
        _combine_kernel,
        out_shape=jax.ShapeDtypeStruct((T, D), F32),
        grid=(T // tm,),
        in_specs=[spec, spec, spec,
                  pl.BlockSpec((tm, LANES), lambda i: (i, 0)),
                  pl.BlockSpec((1, 1, D), lambda i: (i // tiles_per_batch, 0, 0))],
        out_specs=spec,
        compiler_params=_cparams("parallel"),
        name="moe_combine",
    )(x2d, y0, y1, route, gate)


def _final_norm_kernel(x_ref, g_ref, o_ref):
    x = x_ref[...]
    o_ref[...] = x * lax.rsqrt(jnp.mean(x * x, axis=-1, keepdims=True) + EPS) * g_ref[...]


def _final_norm(x2d, g):
    T, D = x2d.shape
    tm = min(512, T)
    spec = pl.BlockSpec((tm, D), lambda i: (i, 0))
    return pl.pallas_call(
        _final_norm_kernel,
        out_shape=jax.ShapeDtypeStruct((T, D), F32),
        grid=(T // tm,),
        in_specs=[spec, pl.BlockSpec((1, D), lambda i: (0, 0))],
        out_specs=spec,
        compiler_params=_cparams("parallel"),
        name="final_norm",
    )(x2d, g.reshape(1, D))


def _dispatch_plan(e0, e1, T):
    flat_e = jnp.stack([e0, e1], axis=1).reshape(T * TOP_K)
    A = T * TOP_K
    order = jnp.argsort(flat_e)
    sorted_e = flat_e[order]
    counts = jnp.zeros((N_EXPERTS,), jnp.int32).at[flat_e].add(1)
    padded = (counts + MOE_BLOCK - 1) // MOE_BLOCK * MOE_BLOCK
    pad_end = jnp.cumsum(padded)
    pad_start = pad_end - padded
    start = jnp.cumsum(counts) - counts
    dest_sorted = pad_start[sorted_e] + jnp.arange(A, dtype=jnp.int32) - start[sorted_e]
    n_blocks = -(-A // MOE_BLOCK) + N_EXPERTS
    cap = n_blocks * MOE_BLOCK
    slot_token = jnp.full((cap,), T, jnp.int32).at[dest_sorted].set((order // TOP_K).astype(jnp.int32))
    block_expert = jnp.minimum(
        jnp.searchsorted(pad_end, jnp.arange(n_blocks, dtype=jnp.int32) * MOE_BLOCK, side="right"),
        N_EXPERTS - 1).astype(jnp.int32)
    dest = jnp.zeros((A,), jnp.int32).at[order].set(dest_sorted).reshape(T, TOP_K)
    n_used = (pad_end[-1] // MOE_BLOCK).astype(jnp.int32).reshape(1)
    return slot_token, block_expert, n_used, dest


def kernel(x, c, positions, norm1_g, norm2_g, ada_w, ada_b, w_in, lru_conv_w, lru_conv_b, lru_w_r, lru_b_r, lru_w_i, lru_b_i, lru_lambda, sgu_norm_g, sgu_w, sgu_b, rwkv_mu_prev, rwkv_mu_next, rwkv_w0, rwkv_w_up, rwkv_a0, rwkv_a_up, rwkv_g_up, rwkv_k_k, rwkv_k_a, rwkv_r_k, w_branch, w_out, router_grp_w, router_grp_b, router_exp_w, router_exp_b, expert_w_gu, expert_w_down, final_norm_g):
    B, S, D = x.shape
    L = ada_w.shape[0]
    W = D // 2
    T = B * S
    feat_w = rwkv_mu_prev.shape[1]

    c_pad = jnp.zeros((SUBLANES, D), F32).at[:B].set(c)
    mod = _ada(c_pad, ada_w, ada_b)

    dk = D // 2 // (2 * RET_HEADS)
    inv_freq = ROPE_THETA ** (-jnp.arange(0, dk, 2, dtype=F32) / dk)
    ang = positions.astype(F32)[..., None] * inv_freq
    cosf, sinf = _rope_tables(jnp.tile(ang, (1, 1, LANES // (dk // 2))))
    log_gamma = np.log1p(-np.exp2(-5.0 - np.arange(RET_HEADS, dtype=np.float64)))
    lg_tab = jnp.asarray(np.broadcast_to(log_gamma[:, None], (RET_HEADS, LANES)), F32)

    x2d = x.reshape(T, D)
    for l in range(L):
        m = mod[l, :B]
        sh1, sc1, g1, sh2, sc2, g2 = [m[:, None, k * D:(k + 1) * D] for k in range(6)]

        h = _normmod(x2d.reshape(B, S, D), norm1_g[l], sc1, sh1, BF16).reshape(T, D)
        wl = w_in[l]
        n_a = 7 * W
        proj_a = _mm(h, wl[:, :n_a].astype(BF16), 2048, 512, name="proj_a").reshape(B, S, n_a)
        feat = _mm(h, wl[:, n_a:n_a + feat_w].astype(BF16), 2048, feat_w // 3,
                   name="proj_rwkv").reshape(B, S, feat_w)
        logits = _mm(h, wl[:, n_a + feat_w:].astype(BF16), 2048, 512, name="proj_gate")

        y_a = _lru(proj_a, B, S, W, 0, W // LANES, lru_conv_w[l], lru_conv_b[l], lru_w_r[l],
                   lru_b_r[l], lru_w_i[l], lru_b_i[l], lru_lambda[l])
        y_b = _retention(proj_a, B, S, W, 2 * W // LANES, 5 * W // (2 * LANES),
                         3 * W // (2 * LANES), 4 * W // (2 * LANES), cosf, sinf, lg_tab)
        y_c = _sgu(proj_a, B, S, W, 5, 6, sgu_norm_g[l], sgu_w[l], sgu_b[l])
        r, kk, v, g, bonus, lw, bb, kt = _rwkv_pre(
            feat, rwkv_mu_prev[l], rwkv_mu_next[l], rwkv_w0[l], rwkv_w_up[l], rwkv_a0[l],
            rwkv_a_up[l], rwkv_g_up[l], rwkv_k_k[l], rwkv_k_a[l], rwkv_r_k[l].reshape(-1))
        y_d = _rwkv_post(_rwkv_scan(r, kk, v, lw, bb, kt), bonus, g)

        merged = _merge([y.reshape(T, W) for y in (y_a, y_b, y_c, y_d)],
                        w_branch[l].astype(BF16), logits, 512, 512)
        x2d = _mm_residual(merged, w_out[l].astype(BF16), x2d, g1, S, 1024, 512)

        h2 = _normmod(x2d.reshape(B, S, D), norm2_g[l], sc2, sh2, BF16).reshape(T, D)
        n_r = N_GROUPS + N_EXPERTS
        w_r = jnp.zeros((D, LANES), F32).at[:, :N_GROUPS].set(router_grp_w[l]).at[:, N_GROUPS:n_r].set(
            router_exp_w[l]).astype(BF16)
        b_r = jnp.zeros((1, LANES), F32).at[0, :N_GROUPS].set(router_grp_b[l]).at[0, N_GROUPS:n_r].set(
            router_exp_b[l])
        route = _router(h2, w_r, b_r)
        e0 = route[:, 2].astype(jnp.int32)
        e1 = route[:, 3].astype(jnp.int32)
        slot_token, block_expert, n_used, dest = _dispatch_plan(e0, e1, T)
        h2_pad = jnp.concatenate([h2, jnp.zeros((1, D), h2.dtype)], axis=0)
        ys = _experts(h2_pad[slot_token], block_expert, n_used, expert_w_gu[l], expert_w_down[l])
        x2d = _combine(x2d, ys[dest[:, 0]], ys[dest[:, 1]], route, g2, S)

    return _final_norm(x2d, final_norm_g).reshape(B, S, D)
```

```python
import functools
import math

import jax
import jax.numpy as jnp
import numpy as np
from jax import lax
from jax.experimental import pallas as pl
from jax.experimental.pallas import tpu as pltpu

F32 = jnp.float32
BF16 = jnp.bfloat16
EPS = 1e-6

LANES = 128
SUBLANES = 8
VMEM_LIMIT_BYTES = 56 * 1024 * 1024

LRU_BLOCKS = 8
CONV_WIDTH = 4
LRU_C = 8.0
RET_HEADS = 8
RET_CHUNK = 128
ROPE_THETA = 10000.0
SGU_GROUPS = 8
SGU_CHUNK = 128
RWKV_HEAD_DIM = 64
DECAY_LORA = 64
ICLR_LORA = 64
GATE_LORA = 128
DECAY_SCALE = math.exp(-0.5)
N_GROUPS = 4
EXPERTS_PER_GROUP = 8
N_EXPERTS = N_GROUPS * EXPERTS_PER_GROUP
TOP_K = 2
MOE_BLOCK = 256
N_BRANCH = 4

RWKV_CHUNK = 64
HI = lax.Precision.HIGHEST


def _cparams(*sem):
    return pltpu.CompilerParams(dimension_semantics=sem, vmem_limit_bytes=VMEM_LIMIT_BYTES)


def _bdot(a, b):
    return jnp.dot(a.astype(BF16), b.astype(BF16), preferred_element_type=F32)


def _bdot_nt(a, b):
    return lax.dot_general(a.astype(BF16), b.astype(BF16), (((1,), (1,)), ((), ())),
                           preferred_element_type=F32)


def _gelu(x):
    return 0.5 * x * (1.0 + jnp.tanh(math.sqrt(2.0 / math.pi) * (x + 0.044715 * (x * x * x))))


def _sigmoid(x):
    return 1.0 / (1.0 + jnp.exp(-x))


def _silu(x):
    return x * _sigmoid(x)


def _iota(shape, dim):
    return lax.broadcasted_iota(jnp.int32, shape, dim)


def _ada_kernel(c_ref, w_ref, b_ref, o_ref):
    cond = _silu(c_ref[...])
    o_ref[0] = _bdot(cond, w_ref[0]) + b_ref[0]


def _ada(c_pad, ada_w, ada_b):
    L, D, N = ada_w.shape
    R = c_pad.shape[0]
    tn = 1024
    return pl.pallas_call(
        _ada_kernel,
        out_shape=jax.ShapeDtypeStruct((L, R, N), F32),
        grid=(L, N // tn),
        in_specs=[pl.BlockSpec((R, D), lambda l, j: (0, 0)),
                  pl.BlockSpec((1, D, tn), lambda l, j: (l, 0, j)),
                  pl.BlockSpec((1, 1, tn), lambda l, j: (l, 0, j))],
        out_specs=pl.BlockSpec((1, R, tn), lambda l, j: (l, 0, j)),
        compiler_params=_cparams("parallel", "parallel"),
        name="ada_mod",
    )(c_pad, ada_w, ada_b.reshape(L, 1, N))


def _normmod_kernel(x_ref, g_ref, sc_ref, sh_ref, o_ref):
    x = x_ref[0]
    y = x * lax.rsqrt(jnp.mean(x * x, axis=-1, keepdims=True) + EPS) * g_ref[...]
    o_ref[0] = (y * (1.0 + sc_ref[0]) + sh_ref[0]).astype(o_ref.dtype)


def _normmod(x, g, sc, sh, out_dtype):
    B, S, D = x.shape
    ts = min(512, S)
    return pl.pallas_call(
        _normmod_kernel,
        out_shape=jax.ShapeDtypeStruct((B, S, D), out_dtype),
        grid=(B, S // ts),
        in_specs=[pl.BlockSpec((1, ts, D), lambda b, i: (b, i, 0)),
                  pl.BlockSpec((1, D), lambda b, i: (0, 0)),
                  pl.BlockSpec((1, 1, D), lambda b, i: (b, 0, 0)),
                  pl.BlockSpec((1, 1, D), lambda b, i: (b, 0, 0))],
        out_specs=pl.BlockSpec((1, ts, D), lambda b, i: (b, i, 0)),
        compiler_params=_cparams("parallel", "parallel"),
        name="norm_mod",
    )(x, g.reshape(1, D), sc, sh)


def _mm_kernel(a_ref, w_ref, o_ref):
    o_ref[...] = jnp.dot(a_ref[...], w_ref[0].astype(BF16),
                         preferred_element_type=F32).astype(o_ref.dtype)


def _mm(a, w, layer, col0, n_cols, tm, tn, out_dtype=F32, name="proj"):
    M, K = a.shape
    return pl.pallas_call(
        _mm_kernel,
        out_shape=jax.ShapeDtypeStruct((M, n_cols), out_dtype),
        grid=(M // tm, n_cols // tn),
        in_specs=[pl.BlockSpec((tm, K), lambda i, j: (i, 0)),
                  pl.BlockSpec((pl.Element(1), pl.Element(K), pl.Element(tn)),
                               lambda i, j: (layer, 0, pl.multiple_of(col0 + j * tn, LANES)))],
        out_specs=pl.BlockSpec((tm, tn), lambda i, j: (i, j)),
        compiler_params=_cparams("parallel", "parallel"),
        name=name,
    )(a, w)


def _mm_res_kernel(a_ref, w_ref, x_ref, g_ref, o_ref):
    y = jnp.dot(a_ref[...], w_ref[0].astype(BF16), preferred_element_type=F32)
    o_ref[...] = x_ref[...] + g_ref[0] * y


def _mm_residual(a, w, layer, x2d, gate, rows_per_batch, tm, tn, name="out_proj"):
    M, K = a.shape
    N = w.shape[2]
    tiles_per_batch = rows_per_batch // tm
    return pl.pallas_call(
        _mm_res_kernel,
        out_shape=jax.ShapeDtypeStruct((M, N), F32),
        grid=(M // tm, N // tn),
        in_specs=[pl.BlockSpec((tm, K), lambda i, j: (i, 0)),
                  pl.BlockSpec((1, K, tn), lambda i, j: (layer, 0, j)),
                  pl.BlockSpec((tm, tn), lambda i, j: (i, j)),
                  pl.BlockSpec((1, 1, tn), lambda i, j: (i // tiles_per_batch, 0, j))],
        out_specs=pl.BlockSpec((tm, tn), lambda i, j: (i, j)),
        compiler_params=_cparams("parallel", "parallel"),
        name=name,
    )(a, w, x2d, gate)


def _softplus(x):
    return jnp.maximum(x, 0.0) + jnp.log1p(jnp.exp(-jnp.abs(x)))


def _lru_kernel(x_ref, gate_ref, cw_ref, cb_ref, wr_ref, br_ref, wi_ref, bi_ref, lam_ref,
                o_ref, a_scr, u_scr):
    S = x_ref.shape[1]
    x = x_ref[0]
    rows = _iota(x.shape, 0)
    xm2 = jnp.where(rows >= 2, pltpu.roll(x, 2, 0), 0.0)
    xm1 = jnp.where(rows >= 1, pltpu.roll(x, 1, 0), 0.0)
    xp1 = jnp.where(rows < S - 1, pltpu.roll(x, S - 1, 0), 0.0)
    xc = (cw_ref[0:1, :] * xm2 + cw_ref[1:2, :] * xm1 + cw_ref[2:3, :] * x
          + cw_ref[3:4, :] * xp1 + cb_ref[...])
    for z in range(2):
        r = _sigmoid(_bdot(xc, wr_ref[z, 0]) + br_ref[z:z + 1, :])
        i = _sigmoid(_bdot(xc, wi_ref[z, 0]) + bi_ref[z:z + 1, :])
        log_a = -LRU_C * r * _softplus(-lam_ref[z:z + 1, :])
        a = jnp.exp(log_a)
        a_scr[z] = a
        u_scr[z] = jnp.sqrt(-jnp.tanh(log_a) * (a * a + 1.0)) * i * xc

    n_tiles = S // SUBLANES
    L = x.shape[1]
    trow = _iota((SUBLANES, L), 0)

    def tile_scan(a, u, reverse):
        for d in (1, 2, 4):
            if reverse:
                keep = trow < SUBLANES - d
                a_s = jnp.where(keep, pltpu.roll(a, SUBLANES - d, 0), 1.0)
                u_s = jnp.where(keep, pltpu.roll(u, SUBLANES - d, 0), 0.0)
            else:
                keep = trow >= d
                a_s = jnp.where(keep, pltpu.roll(a, d, 0), 1.0)
                u_s = jnp.where(keep, pltpu.roll(u, d, 0), 0.0)
            u = u + a * u_s
            a = a * a_s
        return a, u

    def body(k, carry):
        hf, hb = carry
        f0 = pl.multiple_of(k * SUBLANES, SUBLANES)
        b0 = pl.multiple_of((n_tiles - 1 - k) * SUBLANES, SUBLANES)
        af, uf = tile_scan(a_scr[0, pl.ds(f0, SUBLANES), :], u_scr[0, pl.ds(f0, SUBLANES), :], False)
        ab, ub = tile_scan(a_scr[1, pl.ds(b0, SUBLANES), :], u_scr[1, pl.ds(b0, SUBLANES), :], True)
        tf = uf + af * hf
        tb = ub + ab * hb
        u_scr[0, pl.ds(f0, SUBLANES), :] = tf
        u_scr[1, pl.ds(b0, SUBLANES), :] = tb
        return tf[SUBLANES - 1:SUBLANES, :], tb[0:1, :]

    zero = jnp.zeros((1, L), F32)
    lax.fori_loop(0, n_tiles, body, (zero, zero), unroll=4)
    o_ref[0] = ((u_scr[0] + u_scr[1]) * _gelu(gate_ref[0])).astype(o_ref.dtype)


def _lru(proj, B, S, W, x_col, g_col, cw, cb, wr, br, wi, bi, lam):
    nb = W // LANES
    return pl.pallas_call(
        _lru_kernel,
        out_shape=jax.ShapeDtypeStruct((B, S, W), BF16),
        grid=(B, nb),
        in_specs=[pl.BlockSpec((1, S, LANES), lambda b, j: (b, 0, x_col + j)),
                  pl.BlockSpec((1, S, LANES), lambda b, j: (b, 0, g_col + j)),
                  pl.BlockSpec((CONV_WIDTH, LANES), lambda b, j: (0, j)),
                  pl.BlockSpec((1, LANES), lambda b, j: (0, j)),
                  pl.BlockSpec((2, 1, LANES, LANES), lambda b, j: (0, j, 0, 0)),
                  pl.BlockSpec((2, LANES), lambda b, j: (0, j)),
                  pl.BlockSpec((2, 1, LANES, LANES), lambda b, j: (0, j, 0, 0)),
                  pl.BlockSpec((2, LANES), lambda b, j: (0, j)),
                  pl.BlockSpec((2, LANES), lambda b, j: (0, j))],
        out_specs=pl.BlockSpec((1, S, LANES), lambda b, j: (b, 0, j)),
        scratch_shapes=[pltpu.VMEM((2, S, LANES), F32), pltpu.VMEM((2, S, LANES), F32)],
        compiler_params=_cparams("parallel", "parallel"),
        name="rglru",
    )(proj, proj, cw, cb.reshape(1, W), wr, br, wi, bi, lam)


def _rope_kernel(ang_ref, cos_ref, sin_ref):
    ang = ang_ref[0]
    lane = _iota(ang.shape, 1)
    cos_ref[0] = jnp.cos(ang)
    s = jnp.sin(ang)
    sin_ref[0] = jnp.where((lane % 64) < 32, -s, s)


def _rope_tables(ang):
    B, S, _ = ang.shape
    ts = min(512, S)
    spec = pl.BlockSpec((1, ts, LANES), lambda b, i: (b, i, 0))
    return pl.pallas_call(
        _rope_kernel,
        out_shape=(jax.ShapeDtypeStruct((B, S, LANES), F32),) * 2,
        grid=(B, S // ts),
        in_specs=[spec],
        out_specs=(spec, spec),
        compiler_params=_cparams("parallel", "parallel"),
        name="rope_tables",
    )(ang)


def _ret_kernel(q_ref, k_ref, v_ref, g_ref, cos_ref, sin_ref, lg_ref, o_ref,
                qs_scr, ks_scr, kvf_scr, kvb_scr):
    S = q_ref.shape[1]
    C = RET_CHUNK
    N = S // C
    dk = 64
    p = pl.program_id(1)
    lane = _iota((S, LANES), 1)
    first_half = (lane % 64) < 32

    def rope(x):
        swapped = jnp.where(first_half, pltpu.roll(x, LANES - 32, 1), pltpu.roll(x, 32, 1))
        return x * cos_ref[0] + swapped * sin_ref[0]

    qs_scr[...] = rope(q_ref[0]) * (dk ** -0.5)
    ks_scr[...] = rope(k_ref[0])

    pos_r = _iota((C, 1), 0).astype(F32)
    ii = _iota((C, C), 0)
    jj = _iota((C, C), 1)
    dist = jnp.abs(ii - jj).astype(F32)
    clane = _iota((C, LANES), 1)

    for hh in range(2):
        lg = lg_ref[pl.ds(2 * p + hh, 1), :][:, 0:1]
        hmask = (clane // 64) == hh
        intra = jnp.exp(dist * lg)
        dec_kf = jnp.exp((C - 1.0 - pos_r) * lg)
        dec_kb = jnp.exp(pos_r * lg)
        dec_qf = jnp.exp((pos_r + 1.0) * lg)
        dec_qb = jnp.exp((C - pos_r) * lg)
        chunk_decay = jnp.exp(C * lg)

        def kv_body(n, _):
            r0 = pl.multiple_of(n * C, C)
            kc = jnp.where(hmask, ks_scr[pl.ds(r0, C), :], 0.0)
            vc = v_ref[0, pl.ds(r0, C), hh * LANES:(hh + 1) * LANES]
            kvf_scr[n] = _bdot((kc * dec_kf).T, vc)
            kvb_scr[n] = _bdot((kc * dec_kb).T, vc)
            return 0

        lax.fori_loop(0, N, kv_body, 0)

        def fwd_body(n, st):
            cur = kvf_scr[n]
            kvf_scr[n] = st
            return chunk_decay * st + cur

        lax.fori_loop(0, N, fwd_body, jnp.zeros((LANES, LANES), F32))

        def bwd_body(m, st):
            n = N - 1 - m
            cur = kvb_scr[n]
            kvb_scr[n] = st
            return chunk_decay * st + cur

        lax.fori_loop(0, N, bwd_body, jnp.zeros((LANES, LANES), F32))

        def out_body(n, _):
            r0 = pl.multiple_of(n * C, C)
            qc = qs_scr[pl.ds(r0, C), :]
            kc = jnp.where(hmask, ks_scr[pl.ds(r0, C), :], 0.0)
            vc = v_ref[0, pl.ds(r0, C), hh * LANES:(hh + 1) * LANES]
            scores = _bdot_nt(qc, kc) * intra
            o = _bdot(scores, vc)
            o = o + _bdot(qc * dec_qf, kvf_scr[n]) + _bdot(qc * dec_qb, kvb_scr[n])
            mu = jnp.mean(o, axis=-1, keepdims=True)
            var = jnp.mean(jnp.square(o - mu), axis=-1, keepdims=True)
            o = (o - mu) * lax.rsqrt(var + EPS)
            gc = g_ref[0, pl.ds(r0, C), hh * LANES:(hh + 1) * LANES]
            o_ref[0, pl.ds(r0, C), hh * LANES:(hh + 1) * LANES] = (_silu(gc) * o).astype(o_ref.dtype)
            return 0

        lax.fori_loop(0, N, out_body, 0)


def _retention(proj, B, S, W, q_col, k_col, v_col, g_col, cosf, sinf, lg_tab):
    n_pairs = RET_HEADS // 2
    N = S // RET_CHUNK
    return pl.pallas_call(
        _ret_kernel,
        out_shape=jax.ShapeDtypeStruct((B, S, W), BF16),
        grid=(B, n_pairs),
        in_specs=[pl.BlockSpec((1, S, LANES), lambda b, p: (b, 0, q_col + p)),
                  pl.BlockSpec((1, S, LANES), lambda b, p: (b, 0, k_col + p)),
                  pl.BlockSpec((1, S, 2 * LANES), lambda b, p: (b, 0, v_col + p)),
                  pl.BlockSpec((1, S, 2 * LANES), lambda b, p: (b, 0, g_col + p)),
                  pl.BlockSpec((1, S, LANES), lambda b, p: (b, 0, 0)),
                  pl.BlockSpec((1, S, LANES), lambda b, p: (b, 0, 0)),
                  pl.BlockSpec((RET_HEADS, LANES), lambda b, p: (0, 0))],
        out_specs=pl.BlockSpec((1, S, 2 * LANES), lambda b, p: (b, 0, p)),
        scratch_shapes=[pltpu.VMEM((S, LANES), F32), pltpu.VMEM((S, LANES), F32),
                        pltpu.VMEM((N, LANES, LANES), F32), pltpu.VMEM((N, LANES, LANES), F32)],
        compiler_params=_cparams("parallel", "parallel"),
        name="retention",
    )(proj, proj, proj, proj, cosf, sinf, lg_tab)


def _sgu_kernel(u_ref, v_ref, ng_ref, w_ref, bt_ref, o_ref):
    ts = u_ref.shape[1]
    C = SGU_CHUNK
    v = _gelu(v_ref[0])
    mu = jnp.mean(v, axis=-1, keepdims=True)
    var = jnp.mean(jnp.square(v - mu), axis=-1, keepdims=True)
    v = ((v - mu) * lax.rsqrt(var + EPS) * ng_ref[...]).astype(BF16)
    for c in range(ts // C):
        for g in range(SGU_GROUPS):
            vc = v[c * C:(c + 1) * C, g * LANES:(g + 1) * LANES]
            mixed = jnp.dot(w_ref[g].astype(BF16), vc, preferred_element_type=F32) + bt_ref[:, g:g + 1]
            uc = _gelu(u_ref[0, c * C:(c + 1) * C, g * LANES:(g + 1) * LANES])
            o_ref[0, c * C:(c + 1) * C, g * LANES:(g + 1) * LANES] = (uc * mixed).astype(o_ref.dtype)


def _sgu(proj, B, S, W, u_col, v_col, norm_g, w_s, b_s):
    ts = min(512, S)
    return pl.pallas_call(
        _sgu_kernel,
        out_shape=jax.ShapeDtypeStruct((B, S, W), BF16),
        grid=(B, S // ts),
        in_specs=[pl.BlockSpec((1, ts, W), lambda b, i: (b, i, u_col)),
                  pl.BlockSpec((1, ts, W), lambda b, i: (b, i, v_col)),
                  pl.BlockSpec((1, W), lambda b, i: (0, 0)),
                  pl.BlockSpec((SGU_GROUPS, SGU_CHUNK, SGU_CHUNK), lambda b, i: (0, 0, 0)),
                  pl.BlockSpec((SGU_CHUNK, SGU_GROUPS), lambda b, i: (0, 0))],
        out_specs=pl.BlockSpec((1, ts, W), lambda b, i: (b, i, 0)),
        compiler_params=_cparams("parallel", "parallel"),
        name="spatial_gating",
    )(proj, proj, norm_g.reshape(1, W), w_s, b_s.T)


def _head_sum(x, hd):
    seg = (_iota((LANES, LANES), 0) // hd == _iota((LANES, LANES), 1) // hd).astype(F32)
    parts = [jnp.dot(x[:, j * LANES:(j + 1) * LANES], seg, precision=HI, preferred_element_type=F32)
             for j in range(x.shape[-1] // LANES)]
    return jnp.concatenate(parts, axis=1)


def _rwkv_pre_kernel(f_ref, fp_ref, fn_ref, mup_ref, mun_ref, w0_ref, wup_ref, a0_ref, aup_ref,
                     gup_ref, kk_ref, ka_ref, rk_ref,
                     r_out, kk_out, v_out, g_out, bonus_out, lw_out, bb_out, kt_out):
    ts = f_ref.shape[1]
    W = r_out.shape[2]
    i = pl.program_id(1)
    n_i = pl.num_programs(1)
    f = f_ref[0]
    rows = _iota(f.shape, 0)
    prev_row = jnp.where(i > 0, fp_ref[0, SUBLANES - 1:SUBLANES, :], 0.0)
    next_row = jnp.where(i < n_i - 1, fn_ref[0, 0:1, :], 0.0)
    prev = jnp.where(rows >= 1, pltpu.roll(f, 1, 0), prev_row)
    nxt = jnp.where(rows < ts - 1, pltpu.roll(f, ts - 1, 0), next_row)
    f = f + mup_ref[...] * (prev - f) + mun_ref[...] * (nxt - f)

    r = f[:, 0:W]
    k = f[:, W:2 * W]
    v = f[:, 2 * W:3 * W]
    o = 3 * W
    wd = jnp.tanh(f[:, o:o + 2 * DECAY_LORA])
    ad = f[:, o + 2 * DECAY_LORA:o + 2 * DECAY_LORA + 2 * ICLR_LORA]
    gd = f[:, o + 2 * DECAY_LORA + 2 * ICLR_LORA:]

    g_out[0] = _bdot(_sigmoid(gd), gup_ref[...])
    kk = k * kk_ref[...]
    kk = kk * lax.rsqrt(_head_sum(kk * kk, RWKV_HEAD_DIM) + EPS)
    r_out[0] = r
    kk_out[0] = kk
    v_out[0] = v
    bonus = jnp.zeros_like(r)
    for z in range(2):
        w_raw = _bdot(wd[:, z * DECAY_LORA:(z + 1) * DECAY_LORA], wup_ref[z]) + w0_ref[z:z + 1, :]
        lw_out[z, 0] = -DECAY_SCALE * _sigmoid(w_raw)
        a = _sigmoid(_bdot(ad[:, z * ICLR_LORA:(z + 1) * ICLR_LORA], aup_ref[z]) + a0_ref[z:z + 1, :])
        bb_out[z, 0] = a * kk
        kt = k * (1.0 + (a - 1.0) * ka_ref[...])
        kt_out[z, 0] = kt
        bonus = bonus + _head_sum(r * kt * rk_ref[...], RWKV_HEAD_DIM) * v
    bonus_out[0] = bonus


def _rwkv_pre(feat, mu_prev, mu_next, w0, w_up, a0, a_up, g_up, k_k, k_a, r_k):
    B, S, Fw = feat.shape
    W = w0.shape[1]
    ts = min(256, S)
    hb = ts // SUBLANES
    n_hb = S // SUBLANES
    row = lambda a: a.reshape(1, -1)
    full2 = lambda a: pl.BlockSpec(a.shape, lambda b, i: (0, 0))
    full3 = lambda a: pl.BlockSpec(a.shape, lambda b, i: (0, 0, 0))
    bsw = pl.BlockSpec((1, ts, W), lambda b, i: (b, i, 0))
    zsw = pl.BlockSpec((2, 1, ts, W), lambda b, i: (0, b, i, 0))
    sds = jax.ShapeDtypeStruct((B, S, W), F32)
    zds = jax.ShapeDtypeStruct((2, B, S, W), F32)
    args = (row(mu_prev), row(mu_next), w0, w_up, a0, a_up, g_up, row(k_k), row(k_a), row(r_k))
    specs = [full2(args[0]), full2(args[1]), full2(w0), full3(w_up), full2(a0), full3(a_up),
             full2(g_up), full2(args[7]), full2(args[8]), full2(args[9])]
    return pl.pallas_call(
        _rwkv_pre_kernel,
        out_shape=(sds, sds, sds, sds, sds, zds, zds, zds),
        grid=(B, S // ts),
        in_specs=[pl.BlockSpec((1, ts, Fw), lambda b, i: (b, i, 0)),
                  pl.BlockSpec((1, SUBLANES, Fw), lambda b, i: (b, jnp.maximum(i * hb - 1, 0), 0)),
                  pl.BlockSpec((1, SUBLANES, Fw), lambda b, i: (b, jnp.minimum((i + 1) * hb, n_hb - 1), 0)),
                  ] + specs,
        out_specs=(bsw, bsw, bsw, bsw, bsw, zsw, zsw, zsw),
        compiler_params=_cparams("parallel", "parallel"),
        name="rwkv_prepare",
    )(feat, feat, feat, *args)


def _rwkv_scan_kernel(r_ref, kk_ref, v_ref, lw_ref, bb_ref, kt_ref, y_ref, st_scr):
    C = RWKV_CHUNK
    C2 = 2 * C
    z = pl.program_id(0)
    i = pl.program_id(2)
    n_pairs = r_ref.shape[2] // LANES
    fwd = z == 0

    @pl.when(i == 0)
    def _():
        st_scr[...] = jnp.zeros_like(st_scr)

    ti = _iota((C, C), 0)
    si = _iota((C, C), 1)
    sign = jnp.where(fwd, 1, -1)
    cum_mat = (sign * (ti - si) >= 0).astype(F32)
    er = _iota((C2, C2), 0)
    ec = _iota((C2, C2), 1)
    lead = sign * (er % C - ec % C)
    strict = lead > 0
    incl = lead >= 0
    eye2 = (er == ec).astype(F32)
    row_head = _iota((C2, LANES), 0) // C
    lane_head = _iota((C2, LANES), 1) // RWKV_HEAD_DIM
    emask = row_head == lane_head
    diag_l = _iota((LANES, LANES), 0) == _iota((LANES, LANES), 1)

    def expand(x):
        return jnp.where(emask, jnp.concatenate([x, x], axis=0), 0.0)

    def bf(x):
        return x.astype(BF16)

    def mm(a, b):
        return jnp.dot(a, b, preferred_element_type=F32)

    pairs = range(n_pairs)
    sl = [slice(j * LANES, (j + 1) * LANES) for j in pairs]
    lw = [lw_ref[0, 0, :, s] for s in sl]
    cum = [jnp.dot(cum_mat, x, precision=HI, preferred_element_type=F32) for x in lw]
    tot = [jnp.sum(x, axis=0, keepdims=True) for x in lw]
    p_inv = [jnp.exp(-c) for c in cum]
    p_rest = [jnp.exp(t - c) for t, c in zip(tot, cum)]
    kh_e = [bf(expand(kk_ref[0, :, s] * jnp.exp(c - x))) for s, c, x in zip(sl, cum, lw)]
    rh_e = [bf(expand(r_ref[0, :, s] * jnp.exp(c))) for s, c in zip(sl, cum)]
    bh_e = [bf(expand(bb_ref[0, 0, :, s] * p)) for s, p in zip(sl, p_inv)]
    ktil_e = [bf(expand(kt_ref[0, 0, :, s] * p)) for s, p in zip(sl, p_inv)]
    b_rest_t = [bf(expand(bb_ref[0, 0, :, s] * p).T) for s, p in zip(sl, p_rest)]
    k_rest_t = [bf(expand(kt_ref[0, 0, :, s] * p).T) for s, p in zip(sl, p_rest)]
    v_e = [bf(expand(v_ref[0, :, s])) for s in sl]

    a_all = [lax.dot_general(jnp.concatenate([k, r], axis=0), jnp.concatenate([b, t], axis=0),
                             (((1,), (1,)), ((), ())), preferred_element_type=F32)
             for k, r, b, t in zip(kh_e, rh_e, bh_e, ktil_e)]
    a_b = [jnp.where(strict, a[0:C2, 0:C2], 0.0) for a in a_all]
    a_k = [jnp.where(strict, a[0:C2, C2:], 0.0) for a in a_all]
    a_rb = [jnp.where(incl, a[C2:, 0:C2], 0.0) for a in a_all]
    a_rk = [jnp.where(incl, a[C2:, C2:], 0.0) for a in a_all]

    pw = [-a for a in a_b]
    t_inv = [eye2 + p for p in pw]
    for _ in range(int(math.log2(C)) - 1):
        pw = [mm(p, p) for p in [bf(p) for p in pw]]
        t_inv = [t + mm(bf(t), bf(p)) for t, p in zip(t_inv, pw)]

    akv = [mm(bf(jnp.concatenate([ak, ark], axis=0)), v)
           for ak, ark, v in zip(a_k, a_rk, v_e)]
    tw = [mm(bf(t), jnp.concatenate([k, bf(x[0:C2])], axis=1))
          for t, k, x in zip(t_inv, kh_e, akv)]
    st = [st_scr[j] for j in pairs]
    ws = [mm(jnp.concatenate([bf(x[:, 0:LANES]), r], axis=0), bf(s))
          for x, r, s in zip(tw, rh_e, st)]
    u_e = [x[0:C2] + t[:, LANES:] for x, t in zip(ws, tw)]
    y_e = [x[C2:] + k[C2:] - mm(bf(a), bf(u)) for x, k, a, u in zip(ws, akv, a_rb, u_e)]
    for s, y in zip(sl, y_e):
        y_ref[0, 0, :, s] = y[0:C, :] + y[C:C2, :]

    bw = [mm(b, bf(t)) for b, t in zip(b_rest_t, tw)]
    kv = [mm(k, v) for k, v in zip(k_rest_t, v_e)]
    for j in pairs:
        m_t = jnp.where(diag_l, jnp.exp(tot[j]), 0.0) - bw[j][:, 0:LANES]
        g_t = kv[j] - bw[j][:, LANES:]
        st_scr[j] = jnp.dot(m_t, st[j], precision=HI, preferred_element_type=F32) + g_t


def _rwkv_scan(r, kk, v, lw, bb, kt):
    B, S, W = r.shape
    C = RWKV_CHUNK
    nc = S // C
    chunk = lambda z, i: i + z * (nc - 1 - 2 * i)
    shared = pl.BlockSpec((1, C, W), lambda z, b, i: (b, chunk(z, i), 0))
    perdir = pl.BlockSpec((1, 1, C, W), lambda z, b, i: (z, b, chunk(z, i), 0))
    return pl.pallas_call(
        _rwkv_scan_kernel,
        out_shape=jax.ShapeDtypeStruct((2, B, S, W), F32),
        grid=(2, B, nc),
        in_specs=[shared, shared, shared, perdir, perdir, perdir],
        out_specs=perdir,
        scratch_shapes=[pltpu.VMEM((W // LANES, LANES, LANES), F32)],
        compiler_params=_cparams("parallel", "parallel", "arbitrary"),
        name="rwkv_scan",
    )(r, kk, v, lw, bb, kt)


def _rwkv_post_kernel(y_ref, bonus_ref, g_ref, o_ref):
    hd = RWKV_HEAD_DIM
    acc = bonus_ref[0]
    for z in range(2):
        y = y_ref[z, 0]
        d = y - _head_sum(y, hd) * (1.0 / hd)
        var = _head_sum(d * d, hd) * (1.0 / hd)
        acc = acc + d * lax.rsqrt(var + EPS)
    o_ref[0] = (acc * g_ref[0]).astype(o_ref.dtype)


def _rwkv_post(y, bonus, g):
    _, B, S, W = y.shape
    ts = min(256, S)
    bsw = pl.BlockSpec((1, ts, W), lambda b, i: (b, i, 0))
    return pl.pallas_call(
        _rwkv_post_kernel,
        out_shape=jax.ShapeDtypeStruct((B, S, W), BF16),
        grid=(B, S // ts),
        in_specs=[pl.BlockSpec((2, 1, ts, W), lambda b, i: (0, b, i, 0)), bsw, bsw],
        out_specs=bsw,
        compiler_params=_cparams("parallel", "parallel"),
        name="rwkv_finish",
    )(y, bonus, g)


def _merge_kernel(ya_ref, yb_ref, yc_ref, yd_ref, wb_ref, l0_ref, l1_ref, l2_ref, l3_ref, o_ref,
                  wb_bf):
    @pl.when(pl.program_id(1) == 0)
    def _():
        wb_bf[...] = wb_ref[0].astype(BF16)

    acc = None
    for n, (y_ref, l_ref) in enumerate(((ya_ref, l0_ref), (yb_ref, l1_ref), (yc_ref, l2_ref),
                                        (yd_ref, l3_ref))):
        br = jnp.dot(y_ref[...], wb_bf[n], preferred_element_type=F32)
        t = _sigmoid(l_ref[...]) * br
        acc = t if acc is None else acc + t
    o_ref[...] = acc.astype(o_ref.dtype)


def _merge(ys, wb, layer, logits, tm, tn):
    T, W = ys[0].shape
    D = wb.shape[3]
    nj = D // tn
    yspec = pl.BlockSpec((tm, W), lambda j, i: (i, 0))
    lspec = lambda n: pl.BlockSpec((tm, tn), lambda j, i: (i, n * nj + j))
    return pl.pallas_call(
        _merge_kernel,
        out_shape=jax.ShapeDtypeStruct((T, D), BF16),
        grid=(nj, T // tm),
        in_specs=[yspec, yspec, yspec, yspec,
                  pl.BlockSpec((1, N_BRANCH, W, tn), lambda j, i: (layer, 0, 0, j)),
                  lspec(0), lspec(1), lspec(2), lspec(3)],
        out_specs=pl.BlockSpec((tm, tn), lambda j, i: (i, j)),
        scratch_shapes=[pltpu.VMEM((N_BRANCH, W, tn), BF16)],
        compiler_params=_cparams("parallel", "arbitrary"),
        name="branch_merge",
    )(*ys, wb, logits, logits, logits, logits)


def _router_kernel(h_ref, w_ref, b_ref, o_ref):
    logits = jnp.dot(h_ref[...].astype(BF16), w_ref[...], preferred_element_type=F32) + b_ref[...]
    lane = _iota(logits.shape, 1)
    lane_f = lane.astype(F32)
    neg = -3.0e38
    far = float(LANES)
    is_grp = lane < N_GROUPS
    gl = jnp.where(is_grp, logits, neg)
    gmax = jnp.max(gl, axis=-1, keepdims=True)
    gsum = jnp.sum(jnp.where(is_grp, jnp.exp(gl - gmax), 0.0), axis=-1, keepdims=True)
    grp_p = 1.0 / gsum
    grp_i = jnp.min(jnp.where(gl == gmax, lane_f, far), axis=-1, keepdims=True)
    lo = N_GROUPS + grp_i * EXPERTS_PER_GROUP
    in_grp = jnp.logical_and(lane_f >= lo, lane_f < lo + EXPERTS_PER_GROUP)
    el = jnp.where(in_grp, logits, neg)
    m1 = jnp.max(el, axis=-1, keepdims=True)
    i1 = jnp.min(jnp.where(el == m1, lane_f, far), axis=-1, keepdims=True)
    el2 = jnp.where(lane_f == i1, neg, el)
    m2 = jnp.max(el2, axis=-1, keepdims=True)
    i2 = jnp.min(jnp.where(el2 == m2, lane_f, far), axis=-1, keepdims=True)
    t = jnp.exp(m2 - m1)
    w1 = grp_p / (1.0 + t)
    w2 = grp_p * t / (1.0 + t)
    o_ref[...] = jnp.where(lane == 0, w1, jnp.where(lane == 1, w2, jnp.where(
        lane == 2, i1 - N_GROUPS, jnp.where(lane == 3, i2 - N_GROUPS, 0.0))))


def _router(h, w_r, b_r):
    T, D = h.shape
    tm = min(1024, T)
    return pl.pallas_call(
        _router_kernel,
        out_shape=jax.ShapeDtypeStruct((T, LANES), F32),
        grid=(T // tm,),
        in_specs=[pl.BlockSpec((tm, D), lambda i: (i, 0)),
                  pl.BlockSpec((D, LANES), lambda i: (0, 0)),
                  pl.BlockSpec((1, LANES), lambda i: (0, 0))],
        out_specs=pl.BlockSpec((tm, LANES), lambda i: (i, 0)),
        compiler_params=_cparams("parallel"),
        name="router",
    )(h, w_r, b_r)


def _expert_kernel(be_ref, nb_ref, xs_ref, wgu_ref, wd_ref, o_ref, wgu_bf, wd_bf):
    i = pl.program_id(0)
    changed = jnp.logical_or(i == 0, be_ref[i] != be_ref[jnp.maximum(i - 1, 0)])

    @pl.when(changed)
    def _():
        wgu_bf[...] = wgu_ref[0].astype(BF16)
        wd_bf[...] = wd_ref[0].astype(BF16)

    @pl.when(i < nb_ref[0])
    def _():
        gu = jnp.dot(xs_ref[...].astype(BF16), wgu_bf[...], preferred_element_type=F32)
        eh = gu.shape[1] // 2
        mid = (_silu(gu[:, :eh]) * gu[:, eh:]).astype(BF16)
        o_ref[...] = jnp.dot(mid, wd_bf[...], preferred_element_type=F32)

    @pl.when(i >= nb_ref[0])
    def _():
        o_ref[...] = jnp.zeros_like(o_ref)


def _experts(xs, block_expert, n_used, w_gu, w_down):
    cap, D = xs.shape
    E, _, H2 = w_gu.shape
    n_blocks = cap // MOE_BLOCK
    grid_spec = pltpu.PrefetchScalarGridSpec(
        num_scalar_prefetch=2,
        grid=(n_blocks,),
        in_specs=[pl.BlockSpec((MOE_BLOCK, D), lambda i, be, nb: (i, 0)),
                  pl.BlockSpec((1, D, H2), lambda i, be, nb: (be[i], 0, 0)),
                  pl.BlockSpec((1, H2 // 2, D), lambda i, be, nb: (be[i], 0, 0))],
        out_specs=pl.BlockSpec((MOE_BLOCK, D), lambda i, be, nb: (i, 0)),
        scratch_shapes=[pltpu.VMEM((D, H2), BF16), pltpu.VMEM((H2 // 2, D), BF16)],
    )
    return pl.pallas_call(
        _expert_kernel,
        out_shape=jax.ShapeDtypeStruct((cap, D), F32),
        grid_spec=grid_spec,
        compiler_params=_cparams("arbitrary"),
        name="experts",
    )(block_expert, n_used, xs, w_gu, w_down)


def _row_copy(src_ref, s, dst_ref, d, sem):
    return pltpu.make_async_copy(src_ref.at[pl.ds(s, 1), :], dst_ref.at[pl.ds(d, 1), :], sem)


def _rows_wait(src_ref, dst_ref, n, sem):
    pltpu.make_async_copy(src_ref.at[pl.ds(0, n), :], dst_ref.at[pl.ds(0, n), :], sem).wait()


def _combine_kernel(dest_ref, ys_ref, x_ref, r_ref, g_ref, o_ref, y0_buf, y1_buf, sem):
    tm = x_ref.shape[0]
    base = pl.program_id(0) * tm

    def issue(k, c):
        t = base + k
        _row_copy(ys_ref, dest_ref[TOP_K * t], y0_buf, k, sem.at[0]).start()
        _row_copy(ys_ref, dest_ref[TOP_K * t + 1], y1_buf, k, sem.at[1]).start()
        return c

    lax.fori_loop(0, tm, issue, 0)
    _rows_wait(ys_ref, y0_buf, tm, sem.at[0])
    _rows_wait(ys_ref, y1_buf, tm, sem.at[1])
    w0 = r_ref[:, 0:1]
    w1 = r_ref[:, 1:2]
    o_ref[...] = x_ref[...] + g_ref[0] * (w0 * y0_buf[...] + w1 * y1_buf[...])


def _combine(x2d, ys, dest, route, gate, rows_per_batch):
    T, D = x2d.shape
    tm = min(256, T)
    tiles_per_batch = rows_per_batch // tm
    spec = pl.BlockSpec((tm, D), lambda i, d: (i, 0))
    grid_spec = pltpu.PrefetchScalarGridSpec(
        num_scalar_prefetch=1,
        grid=(T // tm,),
        in_specs=[pl.BlockSpec(memory_space=pl.ANY), spec,
                  pl.BlockSpec((tm, LANES), lambda i, d: (i, 0)),
                  pl.BlockSpec((1, 1, D), lambda i, d: (i // tiles_per_batch, 0, 0))],
        out_specs=spec,
        scratch_shapes=[pltpu.VMEM((tm, D), F32), pltpu.VMEM((tm, D), F32),
                        pltpu.SemaphoreType.DMA((2,))],
    )
    return pl.pallas_call(
        _combine_kernel,
        out_shape=jax.ShapeDtypeStruct((T, D), F32),
        grid_spec=grid_spec,
        compiler_params=_cparams("arbitrary"),
        name="moe_combine",
    )(dest, ys, x2d, route, gate)


def _dispatch_kernel(dest_ref, h_ref, xs_in_ref, xs_ref, sem, *, tm):
    del xs_in_ref
    base = pl.program_id(0) * tm

    def issue(k, c):
        t = base + k
        for kk in range(TOP_K):
            _row_copy(h_ref, t, xs_ref, dest_ref[TOP_K * t + kk], sem).start()
        return c

    lax.fori_loop(0, tm, issue, 0)
    _rows_wait(h_ref, xs_ref, TOP_K * tm, sem)


def _dispatch(h, dest, cap):
    T, D = h.shape
    tm = min(512, T)
    grid_spec = pltpu.PrefetchScalarGridSpec(
        num_scalar_prefetch=1,
        grid=(T // tm,),
        in_specs=[pl.BlockSpec(memory_space=pl.ANY), pl.BlockSpec(memory_space=pl.ANY)],
        out_specs=pl.BlockSpec(memory_space=pl.ANY),
        scratch_shapes=[pltpu.SemaphoreType.DMA(())],
    )
    return pl.pallas_call(
        functools.partial(_dispatch_kernel, tm=tm),
        out_shape=jax.ShapeDtypeStruct((cap, D), F32),
        grid_spec=grid_spec,
        input_output_aliases={2: 0},
        compiler_params=_cparams("arbitrary"),
        name="moe_dispatch",
    )(dest, h, jnp.zeros((cap, D), F32))


def _rank_kernel(route_ref, rank_ref, cnt_ref, carry):
    tm = route_ref.shape[0]
    i = pl.program_id(0)

    @pl.when(i == 0)
    def _():
        carry[...] = jnp.zeros_like(carry)

    r = route_ref[...]
    lane = _iota(r.shape, 1)
    lane_f = lane.astype(F32)
    oh0 = (lane_f == r[:, 2:3]).astype(F32)
    oh1 = (lane_f == r[:, 3:4]).astype(F32)
    both = oh0 + oh1
    earlier = (_iota((tm, tm), 0) > _iota((tm, tm), 1)).astype(BF16)
    prefix = jnp.dot(earlier, both.astype(BF16), preferred_element_type=F32) + carry[0:1, :]
    rank0 = jnp.sum(prefix * oh0, axis=-1, keepdims=True)
    rank1 = jnp.sum((prefix + oh0) * oh1, axis=-1, keepdims=True)
    rank_ref[...] = jnp.where(lane == 0, rank0, jnp.where(lane == 1, rank1, 0.0))
    carry[...] = carry[...] + jnp.sum(both, axis=0, keepdims=True)
    cnt_ref[...] = carry[...]


def _dest_kernel(route_ref, rank_ref, cnt_ref, dest_ref):
    r = route_ref[...]
    lane = _iota(r.shape, 1)
    lane_f = lane.astype(F32)
    padded = jnp.floor((cnt_ref[...] + (MOE_BLOCK - 1.0)) * (1.0 / MOE_BLOCK)) * MOE_BLOCK
    before = (_iota((LANES, LANES), 0) < _iota((LANES, LANES), 1)).astype(F32)
    pad_start = jnp.dot(padded, before, precision=HI, preferred_element_type=F32)[0:1, :]
    oh0 = lane_f == r[:, 2:3]
    oh1 = lane_f == r[:, 3:4]
    d0 = jnp.sum(jnp.where(oh0, pad_start, 0.0), axis=-1, keepdims=True) + rank_ref[:, 0:1]
    d1 = jnp.sum(jnp.where(oh1, pad_start, 0.0), axis=-1, keepdims=True) + rank_ref[:, 1:2]
    dest_ref[...] = jnp.where(lane == 0, d0, jnp.where(lane == 1, d1, 0.0))


def _dispatch_plan(route):
    T = route.shape[0]
    tm = min(512, T)
    slab = pl.BlockSpec((tm, LANES), lambda i: (i, 0))
    cnt_spec = pl.BlockSpec((SUBLANES, LANES), lambda i: (0, 0))
    rank, cnt = pl.pallas_call(
        _rank_kernel,
        out_shape=(jax.ShapeDtypeStruct((T, LANES), F32), jax.ShapeDtypeStruct((SUBLANES, LANES), F32)),
        grid=(T // tm,),
        in_specs=[slab],
        out_specs=(slab, cnt_spec),
        scratch_shapes=[pltpu.VMEM((SUBLANES, LANES), F32)],
        compiler_params=_cparams("arbitrary"),
        name="moe_rank",
    )(route)
    dest = pl.pallas_call(
        _dest_kernel,
        out_shape=jax.ShapeDtypeStruct((T, LANES), F32),
        grid=(T // tm,),
        in_specs=[slab, slab, cnt_spec],
        out_specs=slab,
        compiler_params=_cparams("parallel"),
        name="moe_dest",
    )(route, rank, cnt)
    dest = dest[:, :TOP_K].astype(jnp.int32).reshape(T * TOP_K)
    counts = cnt[0, :N_EXPERTS].astype(jnp.int32)
    pad_end = jnp.cumsum((counts + MOE_BLOCK - 1) // MOE_BLOCK * MOE_BLOCK)
    n_blocks = -(-(T * TOP_K) // MOE_BLOCK) + N_EXPERTS
    block_expert = jnp.minimum(
        jnp.searchsorted(pad_end, jnp.arange(n_blocks, dtype=jnp.int32) * MOE_BLOCK, side="right"),
        N_EXPERTS - 1).astype(jnp.int32)
    n_used = (pad_end[-1] // MOE_BLOCK).astype(jnp.int32).reshape(1)
    return dest, block_expert, n_used, n_blocks * MOE_BLOCK


def _final_norm_kernel(x_ref, g_ref, o_ref):
    x = x_ref[...]
    o_ref[...] = x * lax.rsqrt(jnp.mean(x * x, axis=-1, keepdims=True) + EPS) * g_ref[...]


def _final_norm(x2d, g):
    T, D = x2d.shape
    tm = min(512, T)
    spec = pl.BlockSpec((tm, D), lambda i: (i, 0))
    return pl.pallas_call(
        _final_norm_kernel,
        out_shape=jax.ShapeDtypeStruct((T, D), F32),
        grid=(T // tm,),
        in_specs=[spec, pl.BlockSpec((1, D), lambda i: (0, 0))],
        out_specs=spec,
        compiler_params=_cparams("parallel"),
        name="final_norm",
    )(x2d, g.reshape(1, D))


def kernel(x, c, positions, norm1_g, norm2_g, ada_w, ada_b, w_in, lru_conv_w, lru_conv_b, lru_w_r, lru_b_r, lru_w_i, lru_b_i, lru_lambda, sgu_norm_g, sgu_w, sgu_b, rwkv_mu_prev, rwkv_mu_next, rwkv_w0, rwkv_w_up, rwkv_a0, rwkv_a_up, rwkv_g_up, rwkv_k_k, rwkv_k_a, rwkv_r_k, w_branch, w_out, router_grp_w, router_grp_b, router_exp_w, router_exp_b, expert_w_gu, expert_w_down, final_norm_g):
    B, S, D = x.shape
    L = ada_w.shape[0]
    W = D // 2
    T = B * S
    feat_w = rwkv_mu_prev.shape[1]

    c_pad = jnp.zeros((SUBLANES, D), F32).at[:B].set(c)
    mod = _ada(c_pad, ada_w, ada_b)

    dk = D // 2 // (2 * RET_HEADS)
    inv_freq = ROPE_THETA ** (-jnp.arange(0, dk, 2, dtype=F32) / dk)
    ang = positions.astype(F32)[..., None] * inv_freq
    cosf, sinf = _rope_tables(jnp.tile(ang, (1, 1, LANES // (dk // 2))))
    log_gamma = np.log1p(-np.exp2(-5.0 - np.arange(RET_HEADS, dtype=np.float64)))
    lg_tab = jnp.asarray(np.broadcast_to(log_gamma[:, None], (RET_HEADS, LANES)), F32)

    x2d = x.reshape(T, D)
    for l in range(L):
        m = mod[l, :B]
        sh1, sc1, g1, sh2, sc2, g2 = [m[:, None, k * D:(k + 1) * D] for k in range(6)]

        h = _normmod(x2d.reshape(B, S, D), norm1_g[l], sc1, sh1, BF16).reshape(T, D)
        n_a = 7 * W
        proj_a = _mm(h, w_in, l, 0, n_a, 2048, 512, name="proj_a").reshape(B, S, n_a)
        feat = _mm(h, w_in, l, n_a, feat_w, 1024, feat_w // 3, name="proj_rwkv").reshape(B, S, feat_w)
        logits = _mm(h, w_in, l, n_a + feat_w, N_BRANCH * D, 2048, 512, name="proj_gate")

        y_a = _lru(proj_a, B, S, W, 0, W // LANES, lru_conv_w[l], lru_conv_b[l], lru_w_r[l],
                   lru_b_r[l], lru_w_i[l], lru_b_i[l], lru_lambda[l])
        y_b = _retention(proj_a, B, S, W, 2 * W // LANES, 5 * W // (2 * LANES),
                         3 * W // (2 * LANES), 4 * W // (2 * LANES), cosf, sinf, lg_tab)
        y_c = _sgu(proj_a, B, S, W, 5, 6, sgu_norm_g[l], sgu_w[l], sgu_b[l])
        r, kk, v, g, bonus, lw, bb, kt = _rwkv_pre(
            feat, rwkv_mu_prev[l], rwkv_mu_next[l], rwkv_w0[l], rwkv_w_up[l], rwkv_a0[l],
            rwkv_a_up[l], rwkv_g_up[l], rwkv_k_k[l], rwkv_k_a[l], rwkv_r_k[l].reshape(-1))
        y_d = _rwkv_post(_rwkv_scan(r, kk, v, lw, bb, kt), bonus, g)

        merged = _merge([y.reshape(T, W) for y in (y_a, y_b, y_c, y_d)],
                        w_branch, l, logits, 512, 512)
        x2d = _mm_residual(merged, w_out, l, x2d, g1, S, 1024, 512)

        h2 = _normmod(x2d.reshape(B, S, D), norm2_g[l], sc2, sh2, F32).reshape(T, D)
        n_r = N_GROUPS + N_EXPERTS
        w_r = jnp.zeros((D, LANES), F32).at[:, :N_GROUPS].set(router_grp_w[l]).at[:, N_GROUPS:n_r].set(
            router_exp_w[l]).astype(BF16)
        b_r = jnp.zeros((1, LANES), F32).at[0, :N_GROUPS].set(router_grp_b[l]).at[0, N_GROUPS:n_r].set(
            router_exp_b[l])
        route = _router(h2, w_r, b_r)
        dest, block_expert, n_used, cap = _dispatch_plan(route)
        ys = _experts(_dispatch(h2, dest, cap), block_expert, n_used, expert_w_gu[l], expert_w_down[l])
        x2d = _combine(x2d, ys, dest, route, g2, S)

    return _final_norm(x2d, final_norm_g).reshape(B, S, D)
```

```python
import functools
import math

import jax
import jax.numpy as jnp
import numpy as np
from jax import lax
from jax.experimental import pallas as pl
from jax.experimental.pallas import tpu as pltpu

F32 = jnp.float32
BF16 = jnp.bfloat16
EPS = 1e-6

LANES = 128
SUBLANES = 8
VMEM_LIMIT_BYTES = 56 * 1024 * 1024

LRU_BLOCKS = 8
CONV_WIDTH = 4
LRU_C = 8.0
RET_HEADS = 8
RET_CHUNK = 128
ROPE_THETA = 10000.0
SGU_GROUPS = 8
SGU_CHUNK = 128
RWKV_HEAD_DIM = 64
DECAY_LORA = 64
ICLR_LORA = 64
GATE_LORA = 128
DECAY_SCALE = math.exp(-0.5)
N_GROUPS = 4
EXPERTS_PER_GROUP = 8
N_EXPERTS = N_GROUPS * EXPERTS_PER_GROUP
TOP_K = 2
MOE_BLOCK = 256
N_BRANCH = 4

RWKV_CHUNK = 64
HI = lax.Precision.HIGHEST


def _cparams(*sem):
    return pltpu.CompilerParams(dimension_semantics=sem, vmem_limit_bytes=VMEM_LIMIT_BYTES)


def _bdot(a, b):
    return jnp.dot(a.astype(BF16), b.astype(BF16), preferred_element_type=F32)


def _bdot_nt(a, b):
    return lax.dot_general(a.astype(BF16), b.astype(BF16), (((1,), (1,)), ((), ())),
                           preferred_element_type=F32)


def _gelu(x):
    return 0.5 * x * (1.0 + jnp.tanh(math.sqrt(2.0 / math.pi) * (x + 0.044715 * (x * x * x))))


def _sigmoid(x):
    return 1.0 / (1.0 + jnp.exp(-x))


def _silu(x):
    return x * _sigmoid(x)


def _iota(shape, dim):
    return lax.broadcasted_iota(jnp.int32, shape, dim)


def _ada_kernel(c_ref, w_ref, b_ref, o_ref):
    cond = _silu(c_ref[...])
    o_ref[0] = _bdot(cond, w_ref[0]) + b_ref[0]


def _ada(c_pad, ada_w, ada_b):
    L, D, N = ada_w.shape
    R = c_pad.shape[0]
    tn = 1024
    return pl.pallas_call(
        _ada_kernel,
        out_shape=jax.ShapeDtypeStruct((L, R, N), F32),
        grid=(L, N // tn),
        in_specs=[pl.BlockSpec((R, D), lambda l, j: (0, 0)),
                  pl.BlockSpec((1, D, tn), lambda l, j: (l, 0, j)),
                  pl.BlockSpec((1, 1, tn), lambda l, j: (l, 0, j))],
        out_specs=pl.BlockSpec((1, R, tn), lambda l, j: (l, 0, j)),
        compiler_params=_cparams("parallel", "parallel"),
        name="ada_mod",
    )(c_pad, ada_w, ada_b.reshape(L, 1, N))


def _normmod_kernel(x_ref, g_ref, sc_ref, sh_ref, o_ref):
    x = x_ref[0]
    y = x * lax.rsqrt(jnp.mean(x * x, axis=-1, keepdims=True) + EPS) * g_ref[...]
    o_ref[0] = (y * (1.0 + sc_ref[0]) + sh_ref[0]).astype(o_ref.dtype)


def _normmod(x, g, sc, sh, out_dtype):
    B, S, D = x.shape
    ts = min(512, S)
    return pl.pallas_call(
        _normmod_kernel,
        out_shape=jax.ShapeDtypeStruct((B, S, D), out_dtype),
        grid=(B, S // ts),
        in_specs=[pl.BlockSpec((1, ts, D), lambda b, i: (b, i, 0)),
                  pl.BlockSpec((1, D), lambda b, i: (0, 0)),
                  pl.BlockSpec((1, 1, D), lambda b, i: (b, 0, 0)),
                  pl.BlockSpec((1, 1, D), lambda b, i: (b, 0, 0))],
        out_specs=pl.BlockSpec((1, ts, D), lambda b, i: (b, i, 0)),
        compiler_params=_cparams("parallel", "parallel"),
        name="norm_mod",
    )(x, g.reshape(1, D), sc, sh)


def _mm_kernel(a_ref, w_ref, o_ref):
    o_ref[...] = jnp.dot(a_ref[...], w_ref[0].astype(BF16),
                         preferred_element_type=F32).astype(o_ref.dtype)


def _mm(a, w, layer, col0, n_cols, tm, tn, out_dtype=F32, name="proj"):
    M, K = a.shape
    return pl.pallas_call(
        _mm_kernel,
        out_shape=jax.ShapeDtypeStruct((M, n_cols), out_dtype),
        grid=(M // tm, n_cols // tn),
        in_specs=[pl.BlockSpec((tm, K), lambda i, j: (i, 0)),
                  pl.BlockSpec((pl.Element(1), pl.Element(K), pl.Element(tn)),
                               lambda i, j: (layer, 0, pl.multiple_of(col0 + j * tn, LANES)))],
        out_specs=pl.BlockSpec((tm, tn), lambda i, j: (i, j)),
        compiler_params=_cparams("parallel", "parallel"),
        name=name,
    )(a, w)


def _mm_res_kernel(a_ref, w_ref, x_ref, g_ref, o_ref):
    y = jnp.dot(a_ref[...], w_ref[0].astype(BF16), preferred_element_type=F32)
    o_ref[...] = x_ref[...] + g_ref[0] * y


def _mm_residual(a, w, layer, x2d, gate, rows_per_batch, tm, tn, name="out_proj"):
    M, K = a.shape
    N = w.shape[2]
    tiles_per_batch = rows_per_batch // tm
    return pl.pallas_call(
        _mm_res_kernel,
        out_shape=jax.ShapeDtypeStruct((M, N), F32),
        grid=(M // tm, N // tn),
        in_specs=[pl.BlockSpec((tm, K), lambda i, j: (i, 0)),
                  pl.BlockSpec((1, K, tn), lambda i, j: (layer, 0, j)),
                  pl.BlockSpec((tm, tn), lambda i, j: (i, j)),
                  pl.BlockSpec((1, 1, tn), lambda i, j: (i // tiles_per_batch, 0, j))],
        out_specs=pl.BlockSpec((tm, tn), lambda i, j: (i, j)),
        compiler_params=_cparams("parallel", "parallel"),
        name=name,
    )(a, w, x2d, gate)


def _softplus(x):
    return jnp.maximum(x, 0.0) + jnp.log1p(jnp.exp(-jnp.abs(x)))


def _lru_kernel(x_ref, gate_ref, cw_ref, cb_ref, wr_ref, br_ref, wi_ref, bi_ref, lam_ref,
                o_ref, a_scr, u_scr):
    S = x_ref.shape[1]
    x = x_ref[0]
    rows = _iota(x.shape, 0)
    xm2 = jnp.where(rows >= 2, pltpu.roll(x, 2, 0), 0.0)
    xm1 = jnp.where(rows >= 1, pltpu.roll(x, 1, 0), 0.0)
    xp1 = jnp.where(rows < S - 1, pltpu.roll(x, S - 1, 0), 0.0)
    xc = (cw_ref[0:1, :] * xm2 + cw_ref[1:2, :] * xm1 + cw_ref[2:3, :] * x
          + cw_ref[3:4, :] * xp1 + cb_ref[...])
    for z in range(2):
        r = _sigmoid(_bdot(xc, wr_ref[z, 0]) + br_ref[z:z + 1, :])
        i = _sigmoid(_bdot(xc, wi_ref[z, 0]) + bi_ref[z:z + 1, :])
        log_a = -LRU_C * r * _softplus(-lam_ref[z:z + 1, :])
        a = jnp.exp(log_a)
        a_scr[z] = a
        u_scr[z] = jnp.sqrt(-jnp.tanh(log_a) * (a * a + 1.0)) * i * xc

    n_tiles = S // SUBLANES
    L = x.shape[1]
    trow = _iota((SUBLANES, L), 0)

    def tile_scan(a, u, reverse):
        for d in (1, 2, 4):
            if reverse:
                keep = trow < SUBLANES - d
                a_s = jnp.where(keep, pltpu.roll(a, SUBLANES - d, 0), 1.0)
                u_s = jnp.where(keep, pltpu.roll(u, SUBLANES - d, 0), 0.0)
            else:
                keep = trow >= d
                a_s = jnp.where(keep, pltpu.roll(a, d, 0), 1.0)
                u_s = jnp.where(keep, pltpu.roll(u, d, 0), 0.0)
            u = u + a * u_s
            a = a * a_s
        return a, u

    def body(k, carry):
        hf, hb = carry
        f0 = pl.multiple_of(k * SUBLANES, SUBLANES)
        b0 = pl.multiple_of((n_tiles - 1 - k) * SUBLANES, SUBLANES)
        af, uf = tile_scan(a_scr[0, pl.ds(f0, SUBLANES), :], u_scr[0, pl.ds(f0, SUBLANES), :], False)
        ab, ub = tile_scan(a_scr[1, pl.ds(b0, SUBLANES), :], u_scr[1, pl.ds(b0, SUBLANES), :], True)
        tf = uf + af * hf
        tb = ub + ab * hb
        u_scr[0, pl.ds(f0, SUBLANES), :] = tf
        u_scr[1, pl.ds(b0, SUBLANES), :] = tb
        return tf[SUBLANES - 1:SUBLANES, :], tb[0:1, :]

    zero = jnp.zeros((1, L), F32)
    lax.fori_loop(0, n_tiles, body, (zero, zero), unroll=4)
    o_ref[0] = ((u_scr[0] + u_scr[1]) * _gelu(gate_ref[0])).astype(o_ref.dtype)


def _lru(proj, B, S, W, x_col, g_col, cw, cb, wr, br, wi, bi, lam):
    nb = W // LANES
    return pl.pallas_call(
        _lru_kernel,
        out_shape=jax.ShapeDtypeStruct((B, S, W), BF16),
        grid=(B, nb),
        in_specs=[pl.BlockSpec((1, S, LANES), lambda b, j: (b, 0, x_col + j)),
                  pl.BlockSpec((1, S, LANES), lambda b, j: (b, 0, g_col + j)),
                  pl.BlockSpec((CONV_WIDTH, LANES), lambda b, j: (0, j)),
                  pl.BlockSpec((1, LANES), lambda b, j: (0, j)),
                  pl.BlockSpec((2, 1, LANES, LANES), lambda b, j: (0, j, 0, 0)),
                  pl.BlockSpec((2, LANES), lambda b, j: (0, j)),
                  pl.BlockSpec((2, 1, LANES, LANES), lambda b, j: (0, j, 0, 0)),
                  pl.BlockSpec((2, LANES), lambda b, j: (0, j)),
                  pl.BlockSpec((2, LANES), lambda b, j: (0, j))],
        out_specs=pl.BlockSpec((1, S, LANES), lambda b, j: (b, 0, j)),
        scratch_shapes=[pltpu.VMEM((2, S, LANES), F32), pltpu.VMEM((2, S, LANES), F32)],
        compiler_params=_cparams("parallel", "parallel"),
        name="rglru",
    )(proj, proj, cw, cb.reshape(1, W), wr, br, wi, bi, lam)


def _rope_kernel(ang_ref, cos_ref, sin_ref):
    ang = ang_ref[0]
    lane = _iota(ang.shape, 1)
    cos_ref[0] = jnp.cos(ang)
    s = jnp.sin(ang)
    sin_ref[0] = jnp.where((lane % 64) < 32, -s, s)


def _rope_tables(ang):
    B, S, _ = ang.shape
    ts = min(512, S)
    spec = pl.BlockSpec((1, ts, LANES), lambda b, i: (b, i, 0))
    return pl.pallas_call(
        _rope_kernel,
        out_shape=(jax.ShapeDtypeStruct((B, S, LANES), F32),) * 2,
        grid=(B, S // ts),
        in_specs=[spec],
        out_specs=(spec, spec),
        compiler_params=_cparams("parallel", "parallel"),
        name="rope_tables",
    )(ang)


def _ret_kernel(q_ref, k_ref, v_ref, g_ref, cos_ref, sin_ref, lg_ref, o_ref,
                qs_scr, ks_scr, kvf_scr, kvb_scr):
    S = q_ref.shape[1]
    C = RET_CHUNK
    N = S // C
    dk = 64
    p = pl.program_id(1)
    lane = _iota((S, LANES), 1)
    first_half = (lane % 64) < 32

    def rope(x):
        swapped = jnp.where(first_half, pltpu.roll(x, LANES - 32, 1), pltpu.roll(x, 32, 1))
        return x * cos_ref[0] + swapped * sin_ref[0]

    qs_scr[...] = rope(q_ref[0]) * (dk ** -0.5)
    ks_scr[...] = rope(k_ref[0])

    pos_r = _iota((C, 1), 0).astype(F32)
    ii = _iota((C, C), 0)
    jj = _iota((C, C), 1)
    dist = jnp.abs(ii - jj).astype(F32)
    clane = _iota((C, LANES), 1)

    for hh in range(2):
        lg = lg_ref[pl.ds(2 * p + hh, 1), :][:, 0:1]
        hmask = (clane // 64) == hh
        intra = jnp.exp(dist * lg)
        dec_kf = jnp.exp((C - 1.0 - pos_r) * lg)
        dec_kb = jnp.exp(pos_r * lg)
        dec_qf = jnp.exp((pos_r + 1.0) * lg)
        dec_qb = jnp.exp((C - pos_r) * lg)
        chunk_decay = jnp.exp(C * lg)

        def kv_body(n, _):
            r0 = pl.multiple_of(n * C, C)
            kc = jnp.where(hmask, ks_scr[pl.ds(r0, C), :], 0.0)
            vc = v_ref[0, pl.ds(r0, C), hh * LANES:(hh + 1) * LANES]
            kvf_scr[n] = _bdot((kc * dec_kf).T, vc)
            kvb_scr[n] = _bdot((kc * dec_kb).T, vc)
            return 0

        lax.fori_loop(0, N, kv_body, 0, unroll=2)

        def fwd_body(n, st):
            cur = kvf_scr[n]
            kvf_scr[n] = st
            return chunk_decay * st + cur

        lax.fori_loop(0, N, fwd_body, jnp.zeros((LANES, LANES), F32))

        def bwd_body(m, st):
            n = N - 1 - m
            cur = kvb_scr[n]
            kvb_scr[n] = st
            return chunk_decay * st + cur

        lax.fori_loop(0, N, bwd_body, jnp.zeros((LANES, LANES), F32))

        def out_body(n, _):
            r0 = pl.multiple_of(n * C, C)
            qc = qs_scr[pl.ds(r0, C), :]
            kc = jnp.where(hmask, ks_scr[pl.ds(r0, C), :], 0.0)
            vc = v_ref[0, pl.ds(r0, C), hh * LANES:(hh + 1) * LANES]
            scores = _bdot_nt(qc, kc) * intra
            o = _bdot(scores, vc)
            o = o + _bdot(qc * dec_qf, kvf_scr[n]) + _bdot(qc * dec_qb, kvb_scr[n])
            mu = jnp.mean(o, axis=-1, keepdims=True)
            var = jnp.mean(jnp.square(o - mu), axis=-1, keepdims=True)
            o = (o - mu) * lax.rsqrt(var + EPS)
            gc = g_ref[0, pl.ds(r0, C), hh * LANES:(hh + 1) * LANES]
            o_ref[0, pl.ds(r0, C), hh * LANES:(hh + 1) * LANES] = (_silu(gc) * o).astype(o_ref.dtype)
            return 0

        lax.fori_loop(0, N, out_body, 0, unroll=2)


def _retention(proj, B, S, W, q_col, k_col, v_col, g_col, cosf, sinf, lg_tab):
    n_pairs = RET_HEADS // 2
    N = S // RET_CHUNK
    return pl.pallas_call(
        _ret_kernel,
        out_shape=jax.ShapeDtypeStruct((B, S, W), BF16),
        grid=(B, n_pairs),
        in_specs=[pl.BlockSpec((1, S, LANES), lambda b, p: (b, 0, q_col + p)),
                  pl.BlockSpec((1, S, LANES), lambda b, p: (b, 0, k_col + p)),
                  pl.BlockSpec((1, S, 2 * LANES), lambda b, p: (b, 0, v_col + p)),
                  pl.BlockSpec((1, S, 2 * LANES), lambda b, p: (b, 0, g_col + p)),
                  pl.BlockSpec((1, S, LANES), lambda b, p: (b, 0, 0)),
                  pl.BlockSpec((1, S, LANES), lambda b, p: (b, 0, 0)),
                  pl.BlockSpec((RET_HEADS, LANES), lambda b, p: (0, 0))],
        out_specs=pl.BlockSpec((1, S, 2 * LANES), lambda b, p: (b, 0, p)),
        scratch_shapes=[pltpu.VMEM((S, LANES), F32), pltpu.VMEM((S, LANES), F32),
                        pltpu.VMEM((N, LANES, LANES), F32), pltpu.VMEM((N, LANES, LANES), F32)],
        compiler_params=_cparams("parallel", "parallel"),
        name="retention",
    )(proj, proj, proj, proj, cosf, sinf, lg_tab)


def _sgu_kernel(u_ref, v_ref, ng_ref, w_ref, bt_ref, o_ref):
    ts = u_ref.shape[1]
    C = SGU_CHUNK
    v = _gelu(v_ref[0])
    mu = jnp.mean(v, axis=-1, keepdims=True)
    var = jnp.mean(jnp.square(v - mu), axis=-1, keepdims=True)
    v = ((v - mu) * lax.rsqrt(var + EPS) * ng_ref[...]).astype(BF16)
    for c in range(ts // C):
        for g in range(SGU_GROUPS):
            vc = v[c * C:(c + 1) * C, g * LANES:(g + 1) * LANES]
            mixed = jnp.dot(w_ref[g].astype(BF16), vc, preferred_element_type=F32) + bt_ref[:, g:g + 1]
            uc = _gelu(u_ref[0, c * C:(c + 1) * C, g * LANES:(g + 1) * LANES])
            o_ref[0, c * C:(c + 1) * C, g * LANES:(g + 1) * LANES] = (uc * mixed).astype(o_ref.dtype)


def _sgu(proj, B, S, W, u_col, v_col, norm_g, w_s, b_s):
    ts = min(512, S)
    return pl.pallas_call(
        _sgu_kernel,
        out_shape=jax.ShapeDtypeStruct((B, S, W), BF16),
        grid=(B, S // ts),
        in_specs=[pl.BlockSpec((1, ts, W), lambda b, i: (b, i, u_col)),
                  pl.BlockSpec((1, ts, W), lambda b, i: (b, i, v_col)),
                  pl.BlockSpec((1, W), lambda b, i: (0, 0)),
                  pl.BlockSpec((SGU_GROUPS, SGU_CHUNK, SGU_CHUNK), lambda b, i: (0, 0, 0)),
                  pl.BlockSpec((SGU_CHUNK, SGU_GROUPS), lambda b, i: (0, 0))],
        out_specs=pl.BlockSpec((1, ts, W), lambda b, i: (b, i, 0)),
        compiler_params=_cparams("parallel", "parallel"),
        name="spatial_gating",
    )(proj, proj, norm_g.reshape(1, W), w_s, b_s.T)


def _head_sum(x, hd):
    seg = (_iota((LANES, LANES), 0) // hd == _iota((LANES, LANES), 1) // hd).astype(BF16)
    hi = x.astype(BF16)
    lo = (x - hi.astype(F32)).astype(BF16)
    parts = [jnp.dot(hi[:, j * LANES:(j + 1) * LANES], seg, preferred_element_type=F32)
             + jnp.dot(lo[:, j * LANES:(j + 1) * LANES], seg, preferred_element_type=F32)
             for j in range(x.shape[-1] // LANES)]
    return jnp.concatenate(parts, axis=1)


def _rwkv_pre_kernel(f_ref, fp_ref, fn_ref, mup_ref, mun_ref, w0_ref, wup_ref, a0_ref, aup_ref,
                     gup_ref, kk_ref, ka_ref, rk_ref,
                     r_out, kk_out, v_out, g_out, bonus_out, lw_out, bb_out, kt_out):
    ts = f_ref.shape[1]
    W = r_out.shape[2]
    i = pl.program_id(1)
    n_i = pl.num_programs(1)
    f = f_ref[0]
    rows = _iota(f.shape, 0)
    prev_row = jnp.where(i > 0, fp_ref[0, SUBLANES - 1:SUBLANES, :], 0.0)
    next_row = jnp.where(i < n_i - 1, fn_ref[0, 0:1, :], 0.0)
    prev = jnp.where(rows >= 1, pltpu.roll(f, 1, 0), prev_row)
    nxt = jnp.where(rows < ts - 1, pltpu.roll(f, ts - 1, 0), next_row)
    f = f + mup_ref[...] * (prev - f) + mun_ref[...] * (nxt - f)

    r = f[:, 0:W]
    k = f[:, W:2 * W]
    v = f[:, 2 * W:3 * W]
    o = 3 * W
    wd = jnp.tanh(f[:, o:o + 2 * DECAY_LORA])
    ad = f[:, o + 2 * DECAY_LORA:o + 2 * DECAY_LORA + 2 * ICLR_LORA]
    gd = f[:, o + 2 * DECAY_LORA + 2 * ICLR_LORA:]

    g_out[0] = _bdot(_sigmoid(gd), gup_ref[...])
    kk = k * kk_ref[...]
    kk = kk * lax.rsqrt(_head_sum(kk * kk, RWKV_HEAD_DIM) + EPS)
    r_out[0] = r
    kk_out[0] = kk
    v_out[0] = v
    bonus = jnp.zeros_like(r)
    for z in range(2):
        w_raw = _bdot(wd[:, z * DECAY_LORA:(z + 1) * DECAY_LORA], wup_ref[z]) + w0_ref[z:z + 1, :]
        lw_out[z, 0] = -DECAY_SCALE * _sigmoid(w_raw)
        a = _sigmoid(_bdot(ad[:, z * ICLR_LORA:(z + 1) * ICLR_LORA], aup_ref[z]) + a0_ref[z:z + 1, :])
        bb_out[z, 0] = a * kk
        kt = k * (1.0 + (a - 1.0) * ka_ref[...])
        kt_out[z, 0] = kt
        bonus = bonus + _head_sum(r * kt * rk_ref[...], RWKV_HEAD_DIM) * v
    bonus_out[0] = bonus


def _rwkv_pre(feat, mu_prev, mu_next, w0, w_up, a0, a_up, g_up, k_k, k_a, r_k):
    B, S, Fw = feat.shape
    W = w0.shape[1]
    ts = min(256, S)
    hb = ts // SUBLANES
    n_hb = S // SUBLANES
    row = lambda a: a.reshape(1, -1)
    full2 = lambda a: pl.BlockSpec(a.shape, lambda b, i: (0, 0))
    full3 = lambda a: pl.BlockSpec(a.shape, lambda b, i: (0, 0, 0))
    bsw = pl.BlockSpec((1, ts, W), lambda b, i: (b, i, 0))
    zsw = pl.BlockSpec((2, 1, ts, W), lambda b, i: (0, b, i, 0))
    sds = jax.ShapeDtypeStruct((B, S, W), F32)
    zds = jax.ShapeDtypeStruct((2, B, S, W), F32)
    args = (row(mu_prev), row(mu_next), w0, w_up, a0, a_up, g_up, row(k_k), row(k_a), row(r_k))
    specs = [full2(args[0]), full2(args[1]), full2(w0), full3(w_up), full2(a0), full3(a_up),
             full2(g_up), full2(args[7]), full2(args[8]), full2(args[9])]
    return pl.pallas_call(
        _rwkv_pre_kernel,
        out_shape=(sds, sds, sds, sds, sds, zds, zds, zds),
        grid=(B, S // ts),
        in_specs=[pl.BlockSpec((1, ts, Fw), lambda b, i: (b, i, 0)),
                  pl.BlockSpec((1, SUBLANES, Fw), lambda b, i: (b, jnp.maximum(i * hb - 1, 0), 0)),
                  pl.BlockSpec((1, SUBLANES, Fw), lambda b, i: (b, jnp.minimum((i + 1) * hb, n_hb - 1), 0)),
                  ] + specs,
        out_specs=(bsw, bsw, bsw, bsw, bsw, zsw, zsw, zsw),
        compiler_params=_cparams("parallel", "parallel"),
        name="rwkv_prepare",
    )(feat, feat, feat, *args)


def _rwkv_scan_kernel(r_ref, kk_ref, v_ref, lw_ref, bb_ref, kt_ref, y_ref, st_scr):
    C = RWKV_CHUNK
    C2 = 2 * C
    z = pl.program_id(0)
    i = pl.program_id(2)
    n_pairs = r_ref.shape[2] // LANES
    fwd = z == 0

    @pl.when(i == 0)
    def _():
        st_scr[...] = jnp.zeros_like(st_scr)

    ti = _iota((C, C), 0)
    si = _iota((C, C), 1)
    sign = jnp.where(fwd, 1, -1)
    cum_mat = (sign * (ti - si) >= 0).astype(F32)
    er = _iota((C2, C2), 0)
    ec = _iota((C2, C2), 1)
    lead = sign * (er % C - ec % C)
    strict = lead > 0
    incl = lead >= 0
    eye2 = (er == ec).astype(F32)
    row_head = _iota((C2, LANES), 0) // C
    lane_head = _iota((C2, LANES), 1) // RWKV_HEAD_DIM
    emask = row_head == lane_head
    diag_l = _iota((LANES, LANES), 0) == _iota((LANES, LANES), 1)

    def expand(x):
        return jnp.where(emask, jnp.concatenate([x, x], axis=0), 0.0)

    def bf(x):
        return x.astype(BF16)

    def mm(a, b):
        return jnp.dot(a, b, preferred_element_type=F32)

    pairs = range(n_pairs)
    sl = [slice(j * LANES, (j + 1) * LANES) for j in pairs]
    lw = [lw_ref[0, 0, :, s] for s in sl]
    cum = [jnp.dot(cum_mat, x, precision=HI, preferred_element_type=F32) for x in lw]
    tot = [jnp.sum(x, axis=0, keepdims=True) for x in lw]
    p_inv = [jnp.exp(-c) for c in cum]
    p_rest = [jnp.exp(t - c) for t, c in zip(tot, cum)]
    kh_e = [bf(expand(kk_ref[0, :, s] * jnp.exp(c - x))) for s, c, x in zip(sl, cum, lw)]
    rh_e = [bf(expand(r_ref[0, :, s] * jnp.exp(c))) for s, c in zip(sl, cum)]
    bh_e = [bf(expand(bb_ref[0, 0, :, s] * p)) for s, p in zip(sl, p_inv)]
    ktil_e = [bf(expand(kt_ref[0, 0, :, s] * p)) for s, p in zip(sl, p_inv)]
    b_rest_t = [bf(expand(bb_ref[0, 0, :, s] * p).T) for s, p in zip(sl, p_rest)]
    k_rest_t = [bf(expand(kt_ref[0, 0, :, s] * p).T) for s, p in zip(sl, p_rest)]
    v_e = [bf(expand(v_ref[0, :, s])) for s in sl]

    a_all = [lax.dot_general(jnp.concatenate([k, r], axis=0), jnp.concatenate([b, t], axis=0),
                             (((1,), (1,)), ((), ())), preferred_element_type=F32)
             for k, r, b, t in zip(kh_e, rh_e, bh_e, ktil_e)]
    a_b = [jnp.where(strict, a[0:C2, 0:C2], 0.0) for a in a_all]
    a_k = [jnp.where(strict, a[0:C2, C2:], 0.0) for a in a_all]
    a_rb = [jnp.where(incl, a[C2:, 0:C2], 0.0) for a in a_all]
    a_rk = [jnp.where(incl, a[C2:, C2:], 0.0) for a in a_all]

    pw = [-a for a in a_b]
    t_inv = [eye2 + p for p in pw]
    for _ in range(int(math.log2(C)) - 1):
        pw = [mm(p, p) for p in [bf(p) for p in pw]]
        t_inv = [t + mm(bf(t), bf(p)) for t, p in zip(t_inv, pw)]

    akv = [mm(bf(jnp.concatenate([ak, ark], axis=0)), v)
           for ak, ark, v in zip(a_k, a_rk, v_e)]
    tw = [mm(bf(t), jnp.concatenate([k, bf(x[0:C2])], axis=1))
          for t, k, x in zip(t_inv, kh_e, akv)]
    st = [st_scr[j] for j in pairs]
    ws = [mm(jnp.concatenate([bf(x[:, 0:LANES]), r], axis=0), bf(s))
          for x, r, s in zip(tw, rh_e, st)]
    u_e = [x[0:C2] + t[:, LANES:] for x, t in zip(ws, tw)]
    y_e = [x[C2:] + k[C2:] - mm(bf(a), bf(u)) for x, k, a, u in zip(ws, akv, a_rb, u_e)]
    for s, y in zip(sl, y_e):
        y_ref[0, 0, :, s] = y[0:C, :] + y[C:C2, :]

    bw = [mm(b, bf(t)) for b, t in zip(b_rest_t, tw)]
    kv = [mm(k, v) for k, v in zip(k_rest_t, v_e)]
    for j in pairs:
        m_t = jnp.where(diag_l, jnp.exp(tot[j]), 0.0) - bw[j][:, 0:LANES]
        g_t = kv[j] - bw[j][:, LANES:]
        st_scr[j] = jnp.dot(m_t, st[j], precision=HI, preferred_element_type=F32) + g_t


def _rwkv_scan(r, kk, v, lw, bb, kt):
    B, S, W = r.shape
    C = RWKV_CHUNK
    nc = S // C
    chunk = lambda z, i: i + z * (nc - 1 - 2 * i)
    shared = pl.BlockSpec((1, C, W), lambda z, b, i: (b, chunk(z, i), 0))
    perdir = pl.BlockSpec((1, 1, C, W), lambda z, b, i: (z, b, chunk(z, i), 0))
    return pl.pallas_call(
        _rwkv_scan_kernel,
        out_shape=jax.ShapeDtypeStruct((2, B, S, W), F32),
        grid=(2, B, nc),
        in_specs=[shared, shared, shared, perdir, perdir, perdir],
        out_specs=perdir,
        scratch_shapes=[pltpu.VMEM((W // LANES, LANES, LANES), F32)],
        compiler_params=_cparams("parallel", "parallel", "arbitrary"),
        name="rwkv_scan",
    )(r, kk, v, lw, bb, kt)


def _rwkv_post_kernel(y_ref, bonus_ref, g_ref, o_ref):
    hd = RWKV_HEAD_DIM
    acc = bonus_ref[0]
    for z in range(2):
        y = y_ref[z, 0]
        d = y - _head_sum(y, hd) * (1.0 / hd)
        var = _head_sum(d * d, hd) * (1.0 / hd)
        acc = acc + d * lax.rsqrt(var + EPS)
    o_ref[0] = (acc * g_ref[0]).astype(o_ref.dtype)


def _rwkv_post(y, bonus, g):
    _, B, S, W = y.shape
    ts = min(256, S)
    bsw = pl.BlockSpec((1, ts, W), lambda b, i: (b, i, 0))
    return pl.pallas_call(
        _rwkv_post_kernel,
        out_shape=jax.ShapeDtypeStruct((B, S, W), BF16),
        grid=(B, S // ts),
        in_specs=[pl.BlockSpec((2, 1, ts, W), lambda b, i: (0, b, i, 0)), bsw, bsw],
        out_specs=bsw,
        compiler_params=_cparams("parallel", "parallel"),
        name="rwkv_finish",
    )(y, bonus, g)


def _merge_kernel(ya_ref, yb_ref, yc_ref, yd_ref, wb_ref, l0_ref, l1_ref, l2_ref, l3_ref, o_ref,
                  wb_bf):
    @pl.when(pl.program_id(1) == 0)
    def _():
        wb_bf[...] = wb_ref[0].astype(BF16)

    acc = None
    for n, (y_ref, l_ref) in enumerate(((ya_ref, l0_ref), (yb_ref, l1_ref), (yc_ref, l2_ref),
                                        (yd_ref, l3_ref))):
        br = jnp.dot(y_ref[...], wb_bf[n], preferred_element_type=F32)
        t = _sigmoid(l_ref[...]) * br
        acc = t if acc is None else acc + t
    o_ref[...] = acc.astype(o_ref.dtype)


def _merge(ys, wb, layer, logits, tm, tn):
    T, W = ys[0].shape
    D = wb.shape[3]
    nj = D // tn
    yspec = pl.BlockSpec((tm, W), lambda j, i: (i, 0))
    lspec = lambda n: pl.BlockSpec((tm, tn), lambda j, i: (i, n * nj + j))
    return pl.pallas_call(
        _merge_kernel,
        out_shape=jax.ShapeDtypeStruct((T, D), BF16),
        grid=(nj, T // tm),
        in_specs=[yspec, yspec, yspec, yspec,
                  pl.BlockSpec((1, N_BRANCH, W, tn), lambda j, i: (layer, 0, 0, j)),
                  lspec(0), lspec(1), lspec(2), lspec(3)],
        out_specs=pl.BlockSpec((tm, tn), lambda j, i: (i, j)),
        scratch_shapes=[pltpu.VMEM((N_BRANCH, W, tn), BF16)],
        compiler_params=_cparams("parallel", "arbitrary"),
        name="branch_merge",
    )(*ys, wb, logits, logits, logits, logits)


def _router_kernel(h_ref, w_ref, b_ref, o_ref):
    logits = jnp.dot(h_ref[...].astype(BF16), w_ref[...], preferred_element_type=F32) + b_ref[...]
    lane = _iota(logits.shape, 1)
    lane_f = lane.astype(F32)
    neg = -3.0e38
    far = float(LANES)
    is_grp = lane < N_GROUPS
    gl = jnp.where(is_grp, logits, neg)
    gmax = jnp.max(gl, axis=-1, keepdims=True)
    gsum = jnp.sum(jnp.where(is_grp, jnp.exp(gl - gmax), 0.0), axis=-1, keepdims=True)
    grp_p = 1.0 / gsum
    grp_i = jnp.min(jnp.where(gl == gmax, lane_f, far), axis=-1, keepdims=True)
    lo = N_GROUPS + grp_i * EXPERTS_PER_GROUP
    in_grp = jnp.logical_and(lane_f >= lo, lane_f < lo + EXPERTS_PER_GROUP)
    el = jnp.where(in_grp, logits, neg)
    m1 = jnp.max(el, axis=-1, keepdims=True)
    i1 = jnp.min(jnp.where(el == m1, lane_f, far), axis=-1, keepdims=True)
    el2 = jnp.where(lane_f == i1, neg, el)
    m2 = jnp.max(el2, axis=-1, keepdims=True)
    i2 = jnp.min(jnp.where(el2 == m2, lane_f, far), axis=-1, keepdims=True)
    t = jnp.exp(m2 - m1)
    w1 = grp_p / (1.0 + t)
    w2 = grp_p * t / (1.0 + t)
    o_ref[...] = jnp.where(lane == 0, w1, jnp.where(lane == 1, w2, jnp.where(
        lane == 2, i1 - N_GROUPS, jnp.where(lane == 3, i2 - N_GROUPS, 0.0))))


def _router(h, w_r, b_r):
    T, D = h.shape
    tm = min(1024, T)
    return pl.pallas_call(
        _router_kernel,
        out_shape=jax.ShapeDtypeStruct((T, LANES), F32),
        grid=(T // tm,),
        in_specs=[pl.BlockSpec((tm, D), lambda i: (i, 0)),
                  pl.BlockSpec((D, LANES), lambda i: (0, 0)),
                  pl.BlockSpec((1, LANES), lambda i: (0, 0))],
        out_specs=pl.BlockSpec((tm, LANES), lambda i: (i, 0)),
        compiler_params=_cparams("parallel"),
        name="router",
    )(h, w_r, b_r)


def _expert_kernel(be_ref, nb_ref, xs_ref, wgu_ref, wd_ref, o_ref, wgu_bf, wd_bf):
    i = pl.program_id(0)
    changed = jnp.logical_or(i == 0, be_ref[i] != be_ref[jnp.maximum(i - 1, 0)])

    @pl.when(changed)
    def _():
        wgu_bf[...] = wgu_ref[0, 0].astype(BF16)
        wd_bf[...] = wd_ref[0, 0].astype(BF16)

    @pl.when(i < nb_ref[0])
    def _():
        gu = jnp.dot(xs_ref[...].astype(BF16), wgu_bf[...], preferred_element_type=F32)
        eh = gu.shape[1] // 2
        mid = (_silu(gu[:, :eh]) * gu[:, eh:]).astype(BF16)
        o_ref[...] = jnp.dot(mid, wd_bf[...], preferred_element_type=F32)

    @pl.when(i >= nb_ref[0])
    def _():
        o_ref[...] = jnp.zeros_like(o_ref)


def _experts(xs, block_expert, n_used, w_gu, w_down, layer):
    cap, D = xs.shape
    H2 = w_gu.shape[3]
    n_blocks = cap // MOE_BLOCK
    grid_spec = pltpu.PrefetchScalarGridSpec(
        num_scalar_prefetch=2,
        grid=(n_blocks,),
        in_specs=[pl.BlockSpec((MOE_BLOCK, D), lambda i, be, nb: (i, 0)),
                  pl.BlockSpec((1, 1, D, H2), lambda i, be, nb: (layer, be[i], 0, 0)),
                  pl.BlockSpec((1, 1, H2 // 2, D), lambda i, be, nb: (layer, be[i], 0, 0))],
        out_specs=pl.BlockSpec((MOE_BLOCK, D), lambda i, be, nb: (i, 0)),
        scratch_shapes=[pltpu.VMEM((D, H2), BF16), pltpu.VMEM((H2 // 2, D), BF16)],
    )
    return pl.pallas_call(
        _expert_kernel,
        out_shape=jax.ShapeDtypeStruct((cap, D), F32),
        grid_spec=grid_spec,
        compiler_params=_cparams("arbitrary"),
        name="experts",
    )(block_expert, n_used, xs, w_gu, w_down)


def _row_copy(src_ref, s, dst_ref, d, sem):
    return pltpu.make_async_copy(src_ref.at[pl.ds(s, 1), :], dst_ref.at[pl.ds(d, 1), :], sem)


def _rows_wait(src_ref, dst_ref, n, sem):
    pltpu.make_async_copy(src_ref.at[pl.ds(0, n), :], dst_ref.at[pl.ds(0, n), :], sem).wait()


def _combine_kernel(dest_ref, ys_ref, x_ref, r_ref, g_ref, o_ref, y0_buf, y1_buf, sem):
    tm = x_ref.shape[0]
    base = pl.program_id(0) * tm

    def issue(k, c):
        t = base + k
        _row_copy(ys_ref, dest_ref[TOP_K * t], y0_buf, k, sem.at[0]).start()
        _row_copy(ys_ref, dest_ref[TOP_K * t + 1], y1_buf, k, sem.at[1]).start()
        return c

    lax.fori_loop(0, tm, issue, 0)
    _rows_wait(ys_ref, y0_buf, tm, sem.at[0])
    _rows_wait(ys_ref, y1_buf, tm, sem.at[1])
    w0 = r_ref[:, 0:1]
    w1 = r_ref[:, 1:2]
    o_ref[...] = x_ref[...] + g_ref[0] * (w0 * y0_buf[...] + w1 * y1_buf[...])


def _combine(x2d, ys, dest, route, gate, rows_per_batch):
    T, D = x2d.shape
    tm = min(256, T)
    tiles_per_batch = rows_per_batch // tm
    spec = pl.BlockSpec((tm, D), lambda i, d: (i, 0))
    grid_spec = pltpu.PrefetchScalarGridSpec(
        num_scalar_prefetch=1,
        grid=(T // tm,),
        in_specs=[pl.BlockSpec(memory_space=pl.ANY), spec,
                  pl.BlockSpec((tm, LANES), lambda i, d: (i, 0)),
                  pl.BlockSpec((1, 1, D), lambda i, d: (i // tiles_per_batch, 0, 0))],
        out_specs=spec,
        scratch_shapes=[pltpu.VMEM((tm, D), F32), pltpu.VMEM((tm, D), F32),
                        pltpu.SemaphoreType.DMA((2,))],
    )
    return pl.pallas_call(
        _combine_kernel,
        out_shape=jax.ShapeDtypeStruct((T, D), F32),
        grid_spec=grid_spec,
        compiler_params=_cparams("arbitrary"),
        name="moe_combine",
    )(dest, ys, x2d, route, gate)


def _dispatch_kernel(dest_ref, h_ref, xs_in_ref, xs_ref, sem, *, tm):
    del xs_in_ref
    base = pl.program_id(0) * tm

    def issue(k, c):
        t = base + k
        for kk in range(TOP_K):
            _row_copy(h_ref, k, xs_ref, dest_ref[TOP_K * t + kk], sem.at[kk]).start()
        return c

    lax.fori_loop(0, tm, issue, 0)
    for kk in range(TOP_K):
        _rows_wait(h_ref, xs_ref, tm, sem.at[kk])


def _dispatch(h, dest, cap):
    T, D = h.shape
    tm = min(512, T)
    grid_spec = pltpu.PrefetchScalarGridSpec(
        num_scalar_prefetch=1,
        grid=(T // tm,),
        in_specs=[pl.BlockSpec((tm, D), lambda i, d: (i, 0)), pl.BlockSpec(memory_space=pl.ANY)],
        out_specs=pl.BlockSpec(memory_space=pl.ANY),
        scratch_shapes=[pltpu.SemaphoreType.DMA((TOP_K,))],
    )
    return pl.pallas_call(
        functools.partial(_dispatch_kernel, tm=tm),
        out_shape=jax.ShapeDtypeStruct((cap, D), F32),
        grid_spec=grid_spec,
        input_output_aliases={2: 0},
        compiler_params=_cparams("arbitrary"),
        name="moe_dispatch",
    )(dest, h, jnp.zeros((cap, D), F32))


def _rank_kernel(route_ref, rank_ref, cnt_ref, carry):
    tm = route_ref.shape[0]
    i = pl.program_id(0)

    @pl.when(i == 0)
    def _():
        carry[...] = jnp.zeros_like(carry)

    r = route_ref[...]
    lane = _iota(r.shape, 1)
    lane_f = lane.astype(F32)
    oh0 = (lane_f == r[:, 2:3]).astype(F32)
    oh1 = (lane_f == r[:, 3:4]).astype(F32)
    both = oh0 + oh1
    earlier = (_iota((tm, tm), 0) > _iota((tm, tm), 1)).astype(BF16)
    prefix = jnp.dot(earlier, both.astype(BF16), preferred_element_type=F32) + carry[0:1, :]
    rank0 = jnp.sum(prefix * oh0, axis=-1, keepdims=True)
    rank1 = jnp.sum((prefix + oh0) * oh1, axis=-1, keepdims=True)
    rank_ref[...] = jnp.where(lane == 0, rank0, jnp.where(lane == 1, rank1, 0.0))
    carry[...] = carry[...] + jnp.sum(both, axis=0, keepdims=True)
    cnt_ref[...] = carry[...]


def _dest_kernel(route_ref, rank_ref, cnt_ref, dest_ref):
    r = route_ref[...]
    lane = _iota(r.shape, 1)
    lane_f = lane.astype(F32)
    padded = jnp.floor((cnt_ref[...] + (MOE_BLOCK - 1.0)) * (1.0 / MOE_BLOCK)) * MOE_BLOCK
    before = (_iota((LANES, LANES), 0) < _iota((LANES, LANES), 1)).astype(F32)
    pad_start = jnp.dot(padded, before, precision=HI, preferred_element_type=F32)[0:1, :]
    oh0 = lane_f == r[:, 2:3]
    oh1 = lane_f == r[:, 3:4]
    d0 = jnp.sum(jnp.where(oh0, pad_start, 0.0), axis=-1, keepdims=True) + rank_ref[:, 0:1]
    d1 = jnp.sum(jnp.where(oh1, pad_start, 0.0), axis=-1, keepdims=True) + rank_ref[:, 1:2]
    dest_ref[...] = jnp.where(lane == 0, d0, jnp.where(lane == 1, d1, 0.0))


def _dispatch_plan(route):
    T = route.shape[0]
    tm = min(512, T)
    slab = pl.BlockSpec((tm, LANES), lambda i: (i, 0))
    cnt_spec = pl.BlockSpec((SUBLANES, LANES), lambda i: (0, 0))
    rank, cnt = pl.pallas_call(
        _rank_kernel,
        out_shape=(jax.ShapeDtypeStruct((T, LANES), F32), jax.ShapeDtypeStruct((SUBLANES, LANES), F32)),
        grid=(T // tm,),
        in_specs=[slab],
        out_specs=(slab, cnt_spec),
        scratch_shapes=[pltpu.VMEM((SUBLANES, LANES), F32)],
        compiler_params=_cparams("arbitrary"),
        name="moe_rank",
    )(route)
    dest = pl.pallas_call(
        _dest_kernel,
        out_shape=jax.ShapeDtypeStruct((T, LANES), F32),
        grid=(T // tm,),
        in_specs=[slab, slab, cnt_spec],
        out_specs=slab,
        compiler_params=_cparams("parallel"),
        name="moe_dest",
    )(route, rank, cnt)
    dest = dest[:, :TOP_K].astype(jnp.int32).reshape(T * TOP_K)
    counts = cnt[0, :N_EXPERTS].astype(jnp.int32)
    pad_end = jnp.cumsum((counts + MOE_BLOCK - 1) // MOE_BLOCK * MOE_BLOCK)
    n_blocks = -(-(T * TOP_K) // MOE_BLOCK) + N_EXPERTS
    block_expert = jnp.minimum(
        jnp.searchsorted(pad_end, jnp.arange(n_blocks, dtype=jnp.int32) * MOE_BLOCK, side="right"),
        N_EXPERTS - 1).astype(jnp.int32)
    n_used = (pad_end[-1] // MOE_BLOCK).astype(jnp.int32).reshape(1)
    return dest, block_expert, n_used, n_blocks * MOE_BLOCK


def _final_norm_kernel(x_ref, g_ref, o_ref):
    x = x_ref[...]
    o_ref[...] = x * lax.rsqrt(jnp.mean(x * x, axis=-1, keepdims=True) + EPS) * g_ref[...]


def _final_norm(x2d, g):
    T, D = x2d.shape
    tm = min(512, T)
    spec = pl.BlockSpec((tm, D), lambda i: (i, 0))
    return pl.pallas_call(
        _final_norm_kernel,
        out_shape=jax.ShapeDtypeStruct((T, D), F32),
        grid=(T // tm,),
        in_specs=[spec, pl.BlockSpec((1, D), lambda i: (0, 0))],
        out_specs=spec,
        compiler_params=_cparams("parallel"),
        name="final_norm",
    )(x2d, g.reshape(1, D))


def kernel(x, c, positions, norm1_g, norm2_g, ada_w, ada_b, w_in, lru_conv_w, lru_conv_b, lru_w_r, lru_b_r, lru_w_i, lru_b_i, lru_lambda, sgu_norm_g, sgu_w, sgu_b, rwkv_mu_prev, rwkv_mu_next, rwkv_w0, rwkv_w_up, rwkv_a0, rwkv_a_up, rwkv_g_up, rwkv_k_k, rwkv_k_a, rwkv_r_k, w_branch, w_out, router_grp_w, router_grp_b, router_exp_w, router_exp_b, expert_w_gu, expert_w_down, final_norm_g):
    B, S, D = x.shape
    L = ada_w.shape[0]
    W = D // 2
    T = B * S
    feat_w = rwkv_mu_prev.shape[1]

    c_pad = jnp.zeros((SUBLANES, D), F32).at[:B].set(c)
    mod = _ada(c_pad, ada_w, ada_b)

    dk = D // 2 // (2 * RET_HEADS)
    inv_freq = ROPE_THETA ** (-jnp.arange(0, dk, 2, dtype=F32) / dk)
    ang = positions.astype(F32)[..., None] * inv_freq
    cosf, sinf = _rope_tables(jnp.tile(ang, (1, 1, LANES // (dk // 2))))
    log_gamma = np.log1p(-np.exp2(-5.0 - np.arange(RET_HEADS, dtype=np.float64)))
    lg_tab = jnp.asarray(np.broadcast_to(log_gamma[:, None], (RET_HEADS, LANES)), F32)

    x2d = x.reshape(T, D)
    for l in range(L):
        m = mod[l, :B]
        sh1, sc1, g1, sh2, sc2, g2 = [m[:, None, k * D:(k + 1) * D] for k in range(6)]

        h = _normmod(x2d.reshape(B, S, D), norm1_g[l], sc1, sh1, BF16).reshape(T, D)
        n_a = 7 * W
        proj_a = _mm(h, w_in, l, 0, n_a, 2048, 512, name="proj_a").reshape(B, S, n_a)
        feat = _mm(h, w_in, l, n_a, feat_w, 1024, feat_w // 3, name="proj_rwkv").reshape(B, S, feat_w)
        logits = _mm(h, w_in, l, n_a + feat_w, N_BRANCH * D, 2048, 512, name="proj_gate")

        y_a = _lru(proj_a, B, S, W, 0, W // LANES, lru_conv_w[l], lru_conv_b[l], lru_w_r[l],
                   lru_b_r[l], lru_w_i[l], lru_b_i[l], lru_lambda[l])
        y_b = _retention(proj_a, B, S, W, 2 * W // LANES, 5 * W // (2 * LANES),
                         3 * W // (2 * LANES), 4 * W // (2 * LANES), cosf, sinf, lg_tab)
        y_c = _sgu(proj_a, B, S, W, 5, 6, sgu_norm_g[l], sgu_w[l], sgu_b[l])
        r, kk, v, g, bonus, lw, bb, kt = _rwkv_pre(
            feat, rwkv_mu_prev[l], rwkv_mu_next[l], rwkv_w0[l], rwkv_w_up[l], rwkv_a0[l],
            rwkv_a_up[l], rwkv_g_up[l], rwkv_k_k[l], rwkv_k_a[l], rwkv_r_k[l].reshape(-1))
        y_d = _rwkv_post(_rwkv_scan(r, kk, v, lw, bb, kt), bonus, g)

        merged = _merge([y.reshape(T, W) for y in (y_a, y_b, y_c, y_d)],
                        w_branch, l, logits, 512, 512)
        x2d = _mm_residual(merged, w_out, l, x2d, g1, S, 1024, 512)

        h2 = _normmod(x2d.reshape(B, S, D), norm2_g[l], sc2, sh2, F32).reshape(T, D)
        n_r = N_GROUPS + N_EXPERTS
        w_r = jnp.zeros((D, LANES), F32).at[:, :N_GROUPS].set(router_grp_w[l]).at[:, N_GROUPS:n_r].set(
            router_exp_w[l]).astype(BF16)
        b_r = jnp.zeros((1, LANES), F32).at[0, :N_GROUPS].set(router_grp_b[l]).at[0, N_GROUPS:n_r].set(
            router_exp_b[l])
        route = _router(h2, w_r, b_r)
        dest, block_expert, n_used, cap = _dispatch_plan(route)
        ys = _experts(_dispatch(h2, dest, cap), block_expert, n_used, expert_w_gu, expert_w_down, l)
        x2d = _combine(x2d, ys, dest, route, g2, S)

    return _final_norm(x2d, final_norm_g).reshape(B, S, D)
```

```python
import functools
import math

import jax
import jax.numpy as jnp
import numpy as np
from jax import lax
from jax.experimental import pallas as pl
from jax.experimental.pallas import tpu as pltpu

F32 = jnp.float32
BF16 = jnp.bfloat16
EPS = 1e-6

LANES = 128
SUBLANES = 8
VMEM_LIMIT_BYTES = 56 * 1024 * 1024

LRU_BLOCKS = 8
CONV_WIDTH = 4
LRU_C = 8.0
RET_HEADS = 8
RET_CHUNK = 128
ROPE_THETA = 10000.0
SGU_GROUPS = 8
SGU_CHUNK = 128
RWKV_HEAD_DIM = 64
DECAY_LORA = 64
ICLR_LORA = 64
GATE_LORA = 128
DECAY_SCALE = math.exp(-0.5)
N_GROUPS = 4
EXPERTS_PER_GROUP = 8
N_EXPERTS = N_GROUPS * EXPERTS_PER_GROUP
TOP_K = 2
MOE_BLOCK = 256
N_BRANCH = 4

RWKV_CHUNK = 64
HI = lax.Precision.HIGHEST


def _cparams(*sem):
    return pltpu.CompilerParams(dimension_semantics=sem, vmem_limit_bytes=VMEM_LIMIT_BYTES)


def _bdot(a, b):
    return jnp.dot(a.astype(BF16), b.astype(BF16), preferred_element_type=F32)


def _bdot_nt(a, b):
    return lax.dot_general(a.astype(BF16), b.astype(BF16), (((1,), (1,)), ((), ())),
                           preferred_element_type=F32)


def _gelu(x):
    return 0.5 * x * (1.0 + jnp.tanh(math.sqrt(2.0 / math.pi) * (x + 0.044715 * (x * x * x))))


def _sigmoid(x):
    return 1.0 / (1.0 + jnp.exp(-x))


def _silu(x):
    return x * _sigmoid(x)


def _iota(shape, dim):
    return lax.broadcasted_iota(jnp.int32, shape, dim)


def _ada_kernel(c_ref, w_ref, b_ref, o_ref):
    cond = _silu(c_ref[...])
    o_ref[0] = _bdot(cond, w_ref[0]) + b_ref[0]


def _ada(c_pad, ada_w, ada_b):
    L, D, N = ada_w.shape
    R = c_pad.shape[0]
    tn = 1024
    return pl.pallas_call(
        _ada_kernel,
        out_shape=jax.ShapeDtypeStruct((L, R, N), F32),
        grid=(L, N // tn),
        in_specs=[pl.BlockSpec((R, D), lambda l, j: (0, 0)),
                  pl.BlockSpec((1, D, tn), lambda l, j: (l, 0, j)),
                  pl.BlockSpec((1, 1, tn), lambda l, j: (l, 0, j))],
        out_specs=pl.BlockSpec((1, R, tn), lambda l, j: (l, 0, j)),
        compiler_params=_cparams("parallel", "parallel"),
        name="ada_mod",
    )(c_pad, ada_w, ada_b.reshape(L, 1, N))


def _normmod_kernel(x_ref, g_ref, sc_ref, sh_ref, o_ref):
    x = x_ref[0]
    y = x * lax.rsqrt(jnp.mean(x * x, axis=-1, keepdims=True) + EPS) * g_ref[...]
    o_ref[0] = (y * (1.0 + sc_ref[0]) + sh_ref[0]).astype(o_ref.dtype)


def _normmod(x, g, sc, sh, out_dtype):
    B, S, D = x.shape
    ts = min(512, S)
    return pl.pallas_call(
        _normmod_kernel,
        out_shape=jax.ShapeDtypeStruct((B, S, D), out_dtype),
        grid=(B, S // ts),
        in_specs=[pl.BlockSpec((1, ts, D), lambda b, i: (b, i, 0)),
                  pl.BlockSpec((1, D), lambda b, i: (0, 0)),
                  pl.BlockSpec((1, 1, D), lambda b, i: (b, 0, 0)),
                  pl.BlockSpec((1, 1, D), lambda b, i: (b, 0, 0))],
        out_specs=pl.BlockSpec((1, ts, D), lambda b, i: (b, i, 0)),
        compiler_params=_cparams("parallel", "parallel"),
        name="norm_mod",
    )(x, g.reshape(1, D), sc, sh)


def _mm_kernel(a_ref, w_ref, o_ref):
    o_ref[...] = jnp.dot(a_ref[...], w_ref[0].astype(BF16),
                         preferred_element_type=F32).astype(o_ref.dtype)


def _mm(a, w, layer, col0, n_cols, tm, tn, out_dtype=F32, name="proj"):
    M, K = a.shape
    return pl.pallas_call(
        _mm_kernel,
        out_shape=jax.ShapeDtypeStruct((M, n_cols), out_dtype),
        grid=(M // tm, n_cols // tn),
        in_specs=[pl.BlockSpec((tm, K), lambda i, j: (i, 0)),
                  pl.BlockSpec((pl.Element(1), pl.Element(K), pl.Element(tn)),
                               lambda i, j: (layer, 0, pl.multiple_of(col0 + j * tn, LANES)))],
        out_specs=pl.BlockSpec((tm, tn), lambda i, j: (i, j)),
        compiler_params=_cparams("parallel", "parallel"),
        name=name,
    )(a, w)


def _mm_res_kernel(a_ref, w_ref, x_ref, g_ref, o_ref):
    y = jnp.dot(a_ref[...], w_ref[0].astype(BF16), preferred_element_type=F32)
    o_ref[...] = x_ref[...] + g_ref[0] * y


def _mm_residual(a, w, layer, x2d, gate, rows_per_batch, tm, tn, name="out_proj"):
    M, K = a.shape
    N = w.shape[2]
    tiles_per_batch = rows_per_batch // tm
    return pl.pallas_call(
        _mm_res_kernel,
        out_shape=jax.ShapeDtypeStruct((M, N), F32),
        grid=(M // tm, N // tn),
        in_specs=[pl.BlockSpec((tm, K), lambda i, j: (i, 0)),
                  pl.BlockSpec((1, K, tn), lambda i, j: (layer, 0, j)),
                  pl.BlockSpec((tm, tn), lambda i, j: (i, j)),
                  pl.BlockSpec((1, 1, tn), lambda i, j: (i // tiles_per_batch, 0, j))],
        out_specs=pl.BlockSpec((tm, tn), lambda i, j: (i, j)),
        compiler_params=_cparams("parallel", "parallel"),
        name=name,
    )(a, w, x2d, gate)


def _softplus(x):
    return jnp.maximum(x, 0.0) + jnp.log1p(jnp.exp(-jnp.abs(x)))


def _lru_kernel(x_ref, gate_ref, cw_ref, cb_ref, wr_ref, br_ref, wi_ref, bi_ref, lam_ref,
                o_ref, a_scr, u_scr):
    S = x_ref.shape[1]
    x = x_ref[0]
    rows = _iota(x.shape, 0)
    xm2 = jnp.where(rows >= 2, pltpu.roll(x, 2, 0), 0.0)
    xm1 = jnp.where(rows >= 1, pltpu.roll(x, 1, 0), 0.0)
    xp1 = jnp.where(rows < S - 1, pltpu.roll(x, S - 1, 0), 0.0)
    xc = (cw_ref[0:1, :] * xm2 + cw_ref[1:2, :] * xm1 + cw_ref[2:3, :] * x
          + cw_ref[3:4, :] * xp1 + cb_ref[...])
    for z in range(2):
        r = _sigmoid(_bdot(xc, wr_ref[z, 0]) + br_ref[z:z + 1, :])
        i = _sigmoid(_bdot(xc, wi_ref[z, 0]) + bi_ref[z:z + 1, :])
        log_a = -LRU_C * r * _softplus(-lam_ref[z:z + 1, :])
        a = jnp.exp(log_a)
        a_scr[z] = a
        u_scr[z] = jnp.sqrt(-jnp.tanh(log_a) * (a * a + 1.0)) * i * xc

    n_tiles = S // SUBLANES
    L = x.shape[1]
    trow = _iota((SUBLANES, L), 0)

    def tile_scan(a, u, reverse):
        for d in (1, 2, 4):
            if reverse:
                keep = trow < SUBLANES - d
                a_s = jnp.where(keep, pltpu.roll(a, SUBLANES - d, 0), 1.0)
                u_s = jnp.where(keep, pltpu.roll(u, SUBLANES - d, 0), 0.0)
            else:
                keep = trow >= d
                a_s = jnp.where(keep, pltpu.roll(a, d, 0), 1.0)
                u_s = jnp.where(keep, pltpu.roll(u, d, 0), 0.0)
            u = u + a * u_s
            a = a * a_s
        return a, u

    def body(k, carry):
        hf, hb = carry
        f0 = pl.multiple_of(k * SUBLANES, SUBLANES)
        b0 = pl.multiple_of((n_tiles - 1 - k) * SUBLANES, SUBLANES)
        af, uf = tile_scan(a_scr[0, pl.ds(f0, SUBLANES), :], u_scr[0, pl.ds(f0, SUBLANES), :], False)
        ab, ub = tile_scan(a_scr[1, pl.ds(b0, SUBLANES), :], u_scr[1, pl.ds(b0, SUBLANES), :], True)
        tf = uf + af * hf
        tb = ub + ab * hb
        u_scr[0, pl.ds(f0, SUBLANES), :] = tf
        u_scr[1, pl.ds(b0, SUBLANES), :] = tb
        return tf[SUBLANES - 1:SUBLANES, :], tb[0:1, :]

    zero = jnp.zeros((1, L), F32)
    lax.fori_loop(0, n_tiles, body, (zero, zero), unroll=4)
    o_ref[0] = ((u_scr[0] + u_scr[1]) * _gelu(gate_ref[0])).astype(o_ref.dtype)


def _lru(proj, B, S, W, x_col, g_col, cw, cb, wr, br, wi, bi, lam):
    nb = W // LANES
    return pl.pallas_call(
        _lru_kernel,
        out_shape=jax.ShapeDtypeStruct((B, S, W), BF16),
        grid=(B, nb),
        in_specs=[pl.BlockSpec((1, S, LANES), lambda b, j: (b, 0, x_col + j)),
                  pl.BlockSpec((1, S, LANES), lambda b, j: (b, 0, g_col + j)),
                  pl.BlockSpec((CONV_WIDTH, LANES), lambda b, j: (0, j)),
                  pl.BlockSpec((1, LANES), lambda b, j: (0, j)),
                  pl.BlockSpec((2, 1, LANES, LANES), lambda b, j: (0, j, 0, 0)),
                  pl.BlockSpec((2, LANES), lambda b, j: (0, j)),
                  pl.BlockSpec((2, 1, LANES, LANES), lambda b, j: (0, j, 0, 0)),
                  pl.BlockSpec((2, LANES), lambda b, j: (0, j)),
                  pl.BlockSpec((2, LANES), lambda b, j: (0, j))],
        out_specs=pl.BlockSpec((1, S, LANES), lambda b, j: (b, 0, j)),
        scratch_shapes=[pltpu.VMEM((2, S, LANES), F32), pltpu.VMEM((2, S, LANES), F32)],
        compiler_params=_cparams("parallel", "parallel"),
        name="rglru",
    )(proj, proj, cw, cb.reshape(1, W), wr, br, wi, bi, lam)


def _rope_kernel(ang_ref, cos_ref, sin_ref):
    ang = ang_ref[0]
    lane = _iota(ang.shape, 1)
    cos_ref[0] = jnp.cos(ang)
    s = jnp.sin(ang)
    sin_ref[0] = jnp.where((lane % 64) < 32, -s, s)


def _rope_tables(ang):
    B, S, _ = ang.shape
    ts = min(512, S)
    spec = pl.BlockSpec((1, ts, LANES), lambda b, i: (b, i, 0))
    return pl.pallas_call(
        _rope_kernel,
        out_shape=(jax.ShapeDtypeStruct((B, S, LANES), F32),) * 2,
        grid=(B, S // ts),
        in_specs=[spec],
        out_specs=(spec, spec),
        compiler_params=_cparams("parallel", "parallel"),
        name="rope_tables",
    )(ang)


def _ret_kernel(q_ref, k_ref, v_ref, g_ref, cos_ref, sin_ref, lg_ref, o_ref,
                qs_scr, ks_scr, kv_scr):
    S = q_ref.shape[1]
    C = RET_CHUNK
    N = S // C
    dk = 64
    p = pl.program_id(1)
    lane = _iota((S, LANES), 1)
    first_half = (lane % 64) < 32

    def rope(x):
        swapped = jnp.where(first_half, pltpu.roll(x, LANES - 32, 1), pltpu.roll(x, 32, 1))
        return x * cos_ref[0] + swapped * sin_ref[0]

    qs_scr[...] = rope(q_ref[0]) * (dk ** -0.5)
    ks_scr[...] = rope(k_ref[0])

    pos_r = _iota((C, 1), 0).astype(F32)
    pos_l = _iota((1, C), 1).astype(F32)
    ii = _iota((C, C), 0)
    jj = _iota((C, C), 1)
    dist = jnp.abs(ii - jj).astype(F32)
    clane = _iota((C, LANES), 1)

    for hh in range(2):
        lg = lg_ref[pl.ds(2 * p + hh, 1), :][:, 0:1]
        hmask = (clane // 64) == hh
        intra = jnp.exp(dist * lg)
        dec_kf_row = jnp.exp((C - 1.0 - pos_l) * lg)
        dec_kb_row = jnp.exp(pos_l * lg)
        dec_qf = jnp.exp((pos_r + 1.0) * lg)
        dec_qb = jnp.exp((C - pos_r) * lg)
        chunk_decay = jnp.exp(C * lg)

        def kv_body(n, _):
            r0 = pl.multiple_of(n * C, C)
            kt = jnp.where(hmask, ks_scr[pl.ds(r0, C), :], 0.0).T
            vc = v_ref[0, pl.ds(r0, C), hh * LANES:(hh + 1) * LANES]
            kv_scr[n] = _bdot(jnp.concatenate([kt * dec_kf_row, kt * dec_kb_row], axis=0), vc)
            return 0

        lax.fori_loop(0, N, kv_body, 0, unroll=2)

        def fwd_body(n, st):
            cur = kv_scr[n, 0:LANES, :]
            kv_scr[n, 0:LANES, :] = st
            return chunk_decay * st + cur

        lax.fori_loop(0, N, fwd_body, jnp.zeros((LANES, LANES), F32))

        def bwd_body(m, st):
            n = N - 1 - m
            cur = kv_scr[n, LANES:, :]
            kv_scr[n, LANES:, :] = st
            return chunk_decay * st + cur

        lax.fori_loop(0, N, bwd_body, jnp.zeros((LANES, LANES), F32))

        def out_body(n, _):
            r0 = pl.multiple_of(n * C, C)
            qc = qs_scr[pl.ds(r0, C), :]
            kc = jnp.where(hmask, ks_scr[pl.ds(r0, C), :], 0.0)
            vc = v_ref[0, pl.ds(r0, C), hh * LANES:(hh + 1) * LANES]
            scores = _bdot_nt(qc, kc) * intra
            o = _bdot(scores, vc) + _bdot(jnp.concatenate([qc * dec_qf, qc * dec_qb], axis=1), kv_scr[n])
            mu = jnp.mean(o, axis=-1, keepdims=True)
            var = jnp.mean(o * o, axis=-1, keepdims=True) - mu * mu
            o = (o - mu) * lax.rsqrt(var + EPS)
            gc = g_ref[0, pl.ds(r0, C), hh * LANES:(hh + 1) * LANES]
            o_ref[0, pl.ds(r0, C), hh * LANES:(hh + 1) * LANES] = (_silu(gc) * o).astype(o_ref.dtype)
            return 0

        lax.fori_loop(0, N, out_body, 0, unroll=4)


def _retention(proj, B, S, W, q_col, k_col, v_col, g_col, cosf, sinf, lg_tab):
    n_pairs = RET_HEADS // 2
    N = S // RET_CHUNK
    return pl.pallas_call(
        _ret_kernel,
        out_shape=jax.ShapeDtypeStruct((B, S, W), BF16),
        grid=(B, n_pairs),
        in_specs=[pl.BlockSpec((1, S, LANES), lambda b, p: (b, 0, q_col + p)),
                  pl.BlockSpec((1, S, LANES), lambda b, p: (b, 0, k_col + p)),
                  pl.BlockSpec((1, S, 2 * LANES), lambda b, p: (b, 0, v_col + p)),
                  pl.BlockSpec((1, S, 2 * LANES), lambda b, p: (b, 0, g_col + p)),
                  pl.BlockSpec((1, S, LANES), lambda b, p: (b, 0, 0)),
                  pl.BlockSpec((1, S, LANES), lambda b, p: (b, 0, 0)),
                  pl.BlockSpec((RET_HEADS, LANES), lambda b, p: (0, 0))],
        out_specs=pl.BlockSpec((1, S, 2 * LANES), lambda b, p: (b, 0, p)),
        scratch_shapes=[pltpu.VMEM((S, LANES), F32), pltpu.VMEM((S, LANES), F32),
                        pltpu.VMEM((N, 2 * LANES, LANES), F32)],
        compiler_params=_cparams("parallel", "parallel"),
        name="retention",
    )(proj, proj, proj, proj, cosf, sinf, lg_tab)


def _sgu_kernel(u_ref, v_ref, ng_ref, w_ref, bt_ref, o_ref):
    ts = u_ref.shape[1]
    C = SGU_CHUNK
    v = _gelu(v_ref[0])
    mu = jnp.mean(v, axis=-1, keepdims=True)
    var = jnp.mean(jnp.square(v - mu), axis=-1, keepdims=True)
    v = ((v - mu) * lax.rsqrt(var + EPS) * ng_ref[...]).astype(BF16)
    for c in range(ts // C):
        for g in range(SGU_GROUPS):
            vc = v[c * C:(c + 1) * C, g * LANES:(g + 1) * LANES]
            mixed = jnp.dot(w_ref[g].astype(BF16), vc, preferred_element_type=F32) + bt_ref[:, g:g + 1]
            uc = _gelu(u_ref[0, c * C:(c + 1) * C, g * LANES:(g + 1) * LANES])
            o_ref[0, c * C:(c + 1) * C, g * LANES:(g + 1) * LANES] = (uc * mixed).astype(o_ref.dtype)


def _sgu(proj, B, S, W, u_col, v_col, norm_g, w_s, b_s):
    ts = min(512, S)
    return pl.pallas_call(
        _sgu_kernel,
        out_shape=jax.ShapeDtypeStruct((B, S, W), BF16),
        grid=(B, S // ts),
        in_specs=[pl.BlockSpec((1, ts, W), lambda b, i: (b, i, u_col)),
                  pl.BlockSpec((1, ts, W), lambda b, i: (b, i, v_col)),
                  pl.BlockSpec((1, W), lambda b, i: (0, 0)),
                  pl.BlockSpec((SGU_GROUPS, SGU_CHUNK, SGU_CHUNK), lambda b, i: (0, 0, 0)),
                  pl.BlockSpec((SGU_CHUNK, SGU_GROUPS), lambda b, i: (0, 0))],
        out_specs=pl.BlockSpec((1, ts, W), lambda b, i: (b, i, 0)),
        compiler_params=_cparams("parallel", "parallel"),
        name="spatial_gating",
    )(proj, proj, norm_g.reshape(1, W), w_s, b_s.T)


def _head_sum(x, hd):
    seg = (_iota((LANES, LANES), 0) // hd == _iota((LANES, LANES), 1) // hd).astype(BF16)
    hi = x.astype(BF16)
    lo = (x - hi.astype(F32)).astype(BF16)
    parts = [jnp.dot(hi[:, j * LANES:(j + 1) * LANES], seg, preferred_element_type=F32)
             + jnp.dot(lo[:, j * LANES:(j + 1) * LANES], seg, preferred_element_type=F32)
             for j in range(x.shape[-1] // LANES)]
    return jnp.concatenate(parts, axis=1)


def _rwkv_pre_kernel(f_ref, fp_ref, fn_ref, mup_ref, mun_ref, w0_ref, wup_ref, a0_ref, aup_ref,
                     gup_ref, kk_ref, ka_ref, rk_ref,
                     r_out, kk_out, v_out, g_out, bonus_out, lw_out, bb_out, kt_out):
    ts = f_ref.shape[1]
    W = r_out.shape[2]
    i = pl.program_id(1)
    n_i = pl.num_programs(1)
    f = f_ref[0]
    rows = _iota(f.shape, 0)
    prev_row = jnp.where(i > 0, fp_ref[0, SUBLANES - 1:SUBLANES, :], 0.0)
    next_row = jnp.where(i < n_i - 1, fn_ref[0, 0:1, :], 0.0)
    prev = jnp.where(rows >= 1, pltpu.roll(f, 1, 0), prev_row)
    nxt = jnp.where(rows < ts - 1, pltpu.roll(f, ts - 1, 0), next_row)
    f = f + mup_ref[...] * (prev - f) + mun_ref[...] * (nxt - f)

    r = f[:, 0:W]
    k = f[:, W:2 * W]
    v = f[:, 2 * W:3 * W]
    o = 3 * W
    wd = jnp.tanh(f[:, o:o + 2 * DECAY_LORA])
    ad = f[:, o + 2 * DECAY_LORA:o + 2 * DECAY_LORA + 2 * ICLR_LORA]
    gd = f[:, o + 2 * DECAY_LORA + 2 * ICLR_LORA:]

    g_out[0] = _bdot(_sigmoid(gd), gup_ref[...])
    kk = k * kk_ref[...]
    kk = kk * lax.rsqrt(_head_sum(kk * kk, RWKV_HEAD_DIM) + EPS)
    r_out[0] = r
    kk_out[0] = kk
    v_out[0] = v
    bonus = jnp.zeros_like(r)
    for z in range(2):
        w_raw = _bdot(wd[:, z * DECAY_LORA:(z + 1) * DECAY_LORA], wup_ref[z]) + w0_ref[z:z + 1, :]
        lw_out[z, 0] = -DECAY_SCALE * _sigmoid(w_raw)
        a = _sigmoid(_bdot(ad[:, z * ICLR_LORA:(z + 1) * ICLR_LORA], aup_ref[z]) + a0_ref[z:z + 1, :])
        bb_out[z, 0] = a * kk
        kt = k * (1.0 + (a - 1.0) * ka_ref[...])
        kt_out[z, 0] = kt
        bonus = bonus + _head_sum(r * kt * rk_ref[...], RWKV_HEAD_DIM) * v
    bonus_out[0] = bonus


def _rwkv_pre(feat, mu_prev, mu_next, w0, w_up, a0, a_up, g_up, k_k, k_a, r_k):
    B, S, Fw = feat.shape
    W = w0.shape[1]
    ts = min(256, S)
    hb = ts // SUBLANES
    n_hb = S // SUBLANES
    row = lambda a: a.reshape(1, -1)
    full2 = lambda a: pl.BlockSpec(a.shape, lambda b, i: (0, 0))
    full3 = lambda a: pl.BlockSpec(a.shape, lambda b, i: (0, 0, 0))
    bsw = pl.BlockSpec((1, ts, W), lambda b, i: (b, i, 0))
    zsw = pl.BlockSpec((2, 1, ts, W), lambda b, i: (0, b, i, 0))
    sds = jax.ShapeDtypeStruct((B, S, W), F32)
    zds = jax.ShapeDtypeStruct((2, B, S, W), F32)
    args = (row(mu_prev), row(mu_next), w0, w_up, a0, a_up, g_up, row(k_k), row(k_a), row(r_k))
    specs = [full2(args[0]), full2(args[1]), full2(w0), full3(w_up), full2(a0), full3(a_up),
             full2(g_up), full2(args[7]), full2(args[8]), full2(args[9])]
    return pl.pallas_call(
        _rwkv_pre_kernel,
        out_shape=(sds, sds, sds, sds, sds, zds, zds, zds),
        grid=(B, S // ts),
        in_specs=[pl.BlockSpec((1, ts, Fw), lambda b, i: (b, i, 0)),
                  pl.BlockSpec((1, SUBLANES, Fw), lambda b, i: (b, jnp.maximum(i * hb - 1, 0), 0)),
                  pl.BlockSpec((1, SUBLANES, Fw), lambda b, i: (b, jnp.minimum((i + 1) * hb, n_hb - 1), 0)),
                  ] + specs,
        out_specs=(bsw, bsw, bsw, bsw, bsw, zsw, zsw, zsw),
        compiler_params=_cparams("parallel", "parallel"),
        name="rwkv_prepare",
    )(feat, feat, feat, *args)


def _rwkv_scan_kernel(r_ref, kk_ref, v_ref, lw_ref, bb_ref, kt_ref, y_ref, st_scr):
    C = RWKV_CHUNK
    C2 = 2 * C
    z = pl.program_id(0)
    i = pl.program_id(2)
    n_pairs = r_ref.shape[2] // LANES
    fwd = z == 0

    @pl.when(i == 0)
    def _():
        st_scr[...] = jnp.zeros_like(st_scr)

    ti = _iota((C, C), 0)
    si = _iota((C, C), 1)
    sign = jnp.where(fwd, 1, -1)
    cum_mat = (sign * (ti - si) >= 0).astype(BF16)
    er = _iota((C2, C2), 0)
    ec = _iota((C2, C2), 1)
    lead = sign * (er % C - ec % C)
    strict = lead > 0
    incl = lead >= 0
    eye2 = (er == ec).astype(F32)
    row_head = _iota((C2, LANES), 0) // C
    lane_head = _iota((C2, LANES), 1) // RWKV_HEAD_DIM
    emask = row_head == lane_head
    diag_l = _iota((LANES, LANES), 0) == _iota((LANES, LANES), 1)

    def expand(x):
        return jnp.where(emask, jnp.concatenate([x, x], axis=0), 0.0)

    def bf(x):
        return x.astype(BF16)

    def mm(a, b):
        return jnp.dot(a, b, preferred_element_type=F32)

    pairs = range(n_pairs)
    sl = [slice(j * LANES, (j + 1) * LANES) for j in pairs]
    lw = [lw_ref[0, 0, :, s] for s in sl]
    def cumsum3(x):
        hi = bf(x)
        r1 = x - hi.astype(F32)
        mid = bf(r1)
        lo = bf(r1 - mid.astype(F32))
        return mm(cum_mat, hi) + (mm(cum_mat, mid) + mm(cum_mat, lo))

    cum = [cumsum3(x) for x in lw]
    tot = [jnp.sum(x, axis=0, keepdims=True) for x in lw]
    p_inv = [jnp.exp(-c) for c in cum]
    p_rest = [jnp.exp(t - c) for t, c in zip(tot, cum)]
    kh_e = [bf(expand(kk_ref[0, :, s] * jnp.exp(c - x))) for s, c, x in zip(sl, cum, lw)]
    rh_e = [bf(expand(r_ref[0, :, s] * jnp.exp(c))) for s, c in zip(sl, cum)]
    bh_e = [bf(expand(bb_ref[0, 0, :, s] * p)) for s, p in zip(sl, p_inv)]
    ktil_e = [bf(expand(kt_ref[0, 0, :, s] * p)) for s, p in zip(sl, p_inv)]
    b_rest_t = [bf(expand(bb_ref[0, 0, :, s] * p).T) for s, p in zip(sl, p_rest)]
    k_rest_t = [bf(expand(kt_ref[0, 0, :, s] * p).T) for s, p in zip(sl, p_rest)]
    v_e = [bf(expand(v_ref[0, :, s])) for s in sl]

    a_all = [lax.dot_general(jnp.concatenate([k, r], axis=0), jnp.concatenate([b, t], axis=0),
                             (((1,), (1,)), ((), ())), preferred_element_type=F32)
             for k, r, b, t in zip(kh_e, rh_e, bh_e, ktil_e)]
    a_b = [jnp.where(strict, a[0:C2, 0:C2], 0.0) for a in a_all]
    a_k = [jnp.where(strict, a[0:C2, C2:], 0.0) for a in a_all]
    a_rb = [jnp.where(incl, a[C2:, 0:C2], 0.0) for a in a_all]
    a_rk = [jnp.where(incl, a[C2:, C2:], 0.0) for a in a_all]

    pw = [-a for a in a_b]
    t_inv = [eye2 + p for p in pw]
    for _ in range(int(math.log2(C)) - 1):
        pw = [mm(p, p) for p in [bf(p) for p in pw]]
        t_inv = [t + mm(bf(t), bf(p)) for t, p in zip(t_inv, pw)]

    akv = [mm(bf(jnp.concatenate([ak, ark], axis=0)), v)
           for ak, ark, v in zip(a_k, a_rk, v_e)]
    tw = [mm(bf(t), jnp.concatenate([k, bf(x[0:C2])], axis=1))
          for t, k, x in zip(t_inv, kh_e, akv)]
    st = [st_scr[j] for j in pairs]
    ws = [mm(jnp.concatenate([bf(x[:, 0:LANES]), r], axis=0), bf(s))
          for x, r, s in zip(tw, rh_e, st)]
    u_e = [x[0:C2] + t[:, LANES:] for x, t in zip(ws, tw)]
    y_e = [x[C2:] + k[C2:] - mm(bf(a), bf(u)) for x, k, a, u in zip(ws, akv, a_rb, u_e)]
    for s, y in zip(sl, y_e):
        y_ref[0, 0, :, s] = y[0:C, :] + y[C:C2, :]

    bw = [mm(b, bf(t)) for b, t in zip(b_rest_t, tw)]
    kv = [mm(k, v) for k, v in zip(k_rest_t, v_e)]
    for j in pairs:
        m_t = jnp.where(diag_l, jnp.exp(tot[j]), 0.0) - bw[j][:, 0:LANES]
        g_t = kv[j] - bw[j][:, LANES:]
        st_scr[j] = mm(bf(m_t), bf(st[j])) + g_t


def _rwkv_scan(r, kk, v, lw, bb, kt):
    B, S, W = r.shape
    C = RWKV_CHUNK
    nc = S // C
    chunk = lambda z, i: i + z * (nc - 1 - 2 * i)
    shared = pl.BlockSpec((1, C, W), lambda z, b, i: (b, chunk(z, i), 0))
    perdir = pl.BlockSpec((1, 1, C, W), lambda z, b, i: (z, b, chunk(z, i), 0))
    return pl.pallas_call(
        _rwkv_scan_kernel,
        out_shape=jax.ShapeDtypeStruct((2, B, S, W), F32),
        grid=(2, B, nc),
        in_specs=[shared, shared, shared, perdir, perdir, perdir],
        out_specs=perdir,
        scratch_shapes=[pltpu.VMEM((W // LANES, LANES, LANES), F32)],
        compiler_params=_cparams("parallel", "parallel", "arbitrary"),
        name="rwkv_scan",
    )(r, kk, v, lw, bb, kt)


def _rwkv_post_kernel(y_ref, bonus_ref, g_ref, o_ref):
    hd = RWKV_HEAD_DIM
    acc = bonus_ref[0]
    for z in range(2):
        y = y_ref[z, 0]
        d = y - _head_sum(y, hd) * (1.0 / hd)
        var = _head_sum(d * d, hd) * (1.0 / hd)
        acc = acc + d * lax.rsqrt(var + EPS)
    o_ref[0] = (acc * g_ref[0]).astype(o_ref.dtype)


def _rwkv_post(y, bonus, g):
    _, B, S, W = y.shape
    ts = min(256, S)
    bsw = pl.BlockSpec((1, ts, W), lambda b, i: (b, i, 0))
    return pl.pallas_call(
        _rwkv_post_kernel,
        out_shape=jax.ShapeDtypeStruct((B, S, W), BF16),
        grid=(B, S // ts),
        in_specs=[pl.BlockSpec((2, 1, ts, W), lambda b, i: (0, b, i, 0)), bsw, bsw],
        out_specs=bsw,
        compiler_params=_cparams("parallel", "parallel"),
        name="rwkv_finish",
    )(y, bonus, g)


def _merge_kernel(ya_ref, yb_ref, yc_ref, yd_ref, wb_ref, l0_ref, l1_ref, l2_ref, l3_ref, o_ref,
                  wb_bf):
    @pl.when(pl.program_id(1) == 0)
    def _():
        wb_bf[...] = wb_ref[0].astype(BF16)

    acc = None
    for n, (y_ref, l_ref) in enumerate(((ya_ref, l0_ref), (yb_ref, l1_ref), (yc_ref, l2_ref),
                                        (yd_ref, l3_ref))):
        br = jnp.dot(y_ref[...], wb_bf[n], preferred_element_type=F32)
        t = _sigmoid(l_ref[...]) * br
        acc = t if acc is None else acc + t
    o_ref[...] = acc.astype(o_ref.dtype)


def _merge(ys, wb, layer, logits, tm, tn):
    T, W = ys[0].shape
    D = wb.shape[3]
    nj = D // tn
    yspec = pl.BlockSpec((tm, W), lambda j, i: (i, 0))
    lspec = lambda n: pl.BlockSpec((tm, tn), lambda j, i: (i, n * nj + j))
    return pl.pallas_call(
        _merge_kernel,
        out_shape=jax.ShapeDtypeStruct((T, D), BF16),
        grid=(nj, T // tm),
        in_specs=[yspec, yspec, yspec, yspec,
                  pl.BlockSpec((1, N_BRANCH, W, tn), lambda j, i: (layer, 0, 0, j)),
                  lspec(0), lspec(1), lspec(2), lspec(3)],
        out_specs=pl.BlockSpec((tm, tn), lambda j, i: (i, j)),
        scratch_shapes=[pltpu.VMEM((N_BRANCH, W, tn), BF16)],
        compiler_params=_cparams("parallel", "arbitrary"),
        name="branch_merge",
    )(*ys, wb, logits, logits, logits, logits)


def _router_kernel(h_ref, w_ref, b_ref, o_ref):
    logits = jnp.dot(h_ref[...].astype(BF16), w_ref[...], preferred_element_type=F32) + b_ref[...]
    lane = _iota(logits.shape, 1)
    lane_f = lane.astype(F32)
    neg = -3.0e38
    far = float(LANES)
    is_grp = lane < N_GROUPS
    gl = jnp.where(is_grp, logits, neg)
    gmax = jnp.max(gl, axis=-1, keepdims=True)
    gsum = jnp.sum(jnp.where(is_grp, jnp.exp(gl - gmax), 0.0), axis=-1, keepdims=True)
    grp_p = 1.0 / gsum
    grp_i = jnp.min(jnp.where(gl == gmax, lane_f, far), axis=-1, keepdims=True)
    lo = N_GROUPS + grp_i * EXPERTS_PER_GROUP
    in_grp = jnp.logical_and(lane_f >= lo, lane_f < lo + EXPERTS_PER_GROUP)
    el = jnp.where(in_grp, logits, neg)
    m1 = jnp.max(el, axis=-1, keepdims=True)
    i1 = jnp.min(jnp.where(el == m1, lane_f, far), axis=-1, keepdims=True)
    el2 = jnp.where(lane_f == i1, neg, el)
    m2 = jnp.max(el2, axis=-1, keepdims=True)
    i2 = jnp.min(jnp.where(el2 == m2, lane_f, far), axis=-1, keepdims=True)
    t = jnp.exp(m2 - m1)
    w1 = grp_p / (1.0 + t)
    w2 = grp_p * t / (1.0 + t)
    o_ref[...] = jnp.where(lane == 0, w1, jnp.where(lane == 1, w2, jnp.where(
        lane == 2, i1 - N_GROUPS, jnp.where(lane == 3, i2 - N_GROUPS, 0.0))))


def _router(h, w_r, b_r):
    T, D = h.shape
    tm = min(1024, T)
    return pl.pallas_call(
        _router_kernel,
        out_shape=jax.ShapeDtypeStruct((T, LANES), F32),
        grid=(T // tm,),
        in_specs=[pl.BlockSpec((tm, D), lambda i: (i, 0)),
                  pl.BlockSpec((D, LANES), lambda i: (0, 0)),
                  pl.BlockSpec((1, LANES), lambda i: (0, 0))],
        out_specs=pl.BlockSpec((tm, LANES), lambda i: (i, 0)),
        compiler_params=_cparams("parallel"),
        name="router",
    )(h, w_r, b_r)


def _expert_kernel(be_ref, nb_ref, xs_ref, wgu_ref, wd_ref, o_ref, wgu_bf, wd_bf):
    i = pl.program_id(0)
    changed = jnp.logical_or(i == 0, be_ref[i] != be_ref[jnp.maximum(i - 1, 0)])

    @pl.when(changed)
    def _():
        wgu_bf[...] = wgu_ref[0, 0].astype(BF16)
        wd_bf[...] = wd_ref[0, 0].astype(BF16)

    @pl.when(i < nb_ref[0])
    def _():
        gu = jnp.dot(xs_ref[...].astype(BF16), wgu_bf[...], preferred_element_type=F32)
        eh = gu.shape[1] // 2
        mid = (_silu(gu[:, :eh]) * gu[:, eh:]).astype(BF16)
        o_ref[...] = jnp.dot(mid, wd_bf[...], preferred_element_type=F32)

    @pl.when(i >= nb_ref[0])
    def _():
        o_ref[...] = jnp.zeros_like(o_ref)


def _experts(xs, block_expert, n_used, w_gu, w_down, layer):
    cap, D = xs.shape
    H2 = w_gu.shape[3]
    n_blocks = cap // MOE_BLOCK
    grid_spec = pltpu.PrefetchScalarGridSpec(
        num_scalar_prefetch=2,
        grid=(n_blocks,),
        in_specs=[pl.BlockSpec((MOE_BLOCK, D), lambda i, be, nb: (i, 0)),
                  pl.BlockSpec((1, 1, D, H2), lambda i, be, nb: (layer, be[i], 0, 0)),
                  pl.BlockSpec((1, 1, H2 // 2, D), lambda i, be, nb: (layer, be[i], 0, 0))],
        out_specs=pl.BlockSpec((MOE_BLOCK, D), lambda i, be, nb: (i, 0)),
        scratch_shapes=[pltpu.VMEM((D, H2), BF16), pltpu.VMEM((H2 // 2, D), BF16)],
    )
    return pl.pallas_call(
        _expert_kernel,
        out_shape=jax.ShapeDtypeStruct((cap, D), F32),
        grid_spec=grid_spec,
        compiler_params=_cparams("arbitrary"),
        name="experts",
    )(block_expert, n_used, xs, w_gu, w_down)


def _row_copy(src_ref, s, dst_ref, d, sem):
    return pltpu.make_async_copy(src_ref.at[pl.ds(s, 1), :], dst_ref.at[pl.ds(d, 1), :], sem)


def _rows_wait(src_ref, dst_ref, n, sem):
    pltpu.make_async_copy(src_ref.at[pl.ds(0, n), :], dst_ref.at[pl.ds(0, n), :], sem).wait()


def _combine_kernel(dest_ref, ys_ref, x_ref, r_ref, g_ref, o_ref, y0_buf, y1_buf, sem):
    tm = x_ref.shape[0]
    base = pl.program_id(0) * tm

    def issue(k, c):
        t = base + k
        _row_copy(ys_ref, dest_ref[TOP_K * t], y0_buf, k, sem.at[0]).start()
        _row_copy(ys_ref, dest_ref[TOP_K * t + 1], y1_buf, k, sem.at[1]).start()
        return c

    lax.fori_loop(0, tm, issue, 0)
    _rows_wait(ys_ref, y0_buf, tm, sem.at[0])
    _rows_wait(ys_ref, y1_buf, tm, sem.at[1])
    w0 = r_ref[:, 0:1]
    w1 = r_ref[:, 1:2]
    o_ref[...] = x_ref[...] + g_ref[0] * (w0 * y0_buf[...] + w1 * y1_buf[...])


def _combine(x2d, ys, dest, route, gate, rows_per_batch):
    T, D = x2d.shape
    tm = min(256, T)
    tiles_per_batch = rows_per_batch // tm
    spec = pl.BlockSpec((tm, D), lambda i, d: (i, 0))
    grid_spec = pltpu.PrefetchScalarGridSpec(
        num_scalar_prefetch=1,
        grid=(T // tm,),
        in_specs=[pl.BlockSpec(memory_space=pl.ANY), spec,
                  pl.BlockSpec((tm, LANES), lambda i, d: (i, 0)),
                  pl.BlockSpec((1, 1, D), lambda i, d: (i // tiles_per_batch, 0, 0))],
        out_specs=spec,
        scratch_shapes=[pltpu.VMEM((tm, D), F32), pltpu.VMEM((tm, D), F32),
                        pltpu.SemaphoreType.DMA((2,))],
    )
    return pl.pallas_call(
        _combine_kernel,
        out_shape=jax.ShapeDtypeStruct((T, D), F32),
        grid_spec=grid_spec,
        compiler_params=_cparams("arbitrary"),
        name="moe_combine",
    )(dest, ys, x2d, route, gate)


def _dispatch_kernel(dest_ref, h_ref, xs_in_ref, xs_ref, sem, *, tm):
    del xs_in_ref
    base = pl.program_id(0) * tm

    def issue(k, c):
        t = base + k
        for kk in range(TOP_K):
            _row_copy(h_ref, k, xs_ref, dest_ref[TOP_K * t + kk], sem.at[kk]).start()
        return c

    lax.fori_loop(0, tm, issue, 0)
    for kk in range(TOP_K):
        _rows_wait(h_ref, xs_ref, tm, sem.at[kk])


def _dispatch(h, dest, cap):
    T, D = h.shape
    tm = min(512, T)
    grid_spec = pltpu.PrefetchScalarGridSpec(
        num_scalar_prefetch=1,
        grid=(T // tm,),
        in_specs=[pl.BlockSpec((tm, D), lambda i, d: (i, 0)), pl.BlockSpec(memory_space=pl.ANY)],
        out_specs=pl.BlockSpec(memory_space=pl.ANY),
        scratch_shapes=[pltpu.SemaphoreType.DMA((TOP_K,))],
    )
    return pl.pallas_call(
        functools.partial(_dispatch_kernel, tm=tm),
        out_shape=jax.ShapeDtypeStruct((cap, D), F32),
        grid_spec=grid_spec,
        input_output_aliases={2: 0},
        compiler_params=_cparams("arbitrary"),
        name="moe_dispatch",
    )(dest, h, jnp.zeros((cap, D), F32))


def _rank_kernel(route_ref, rank_ref, cnt_ref, carry):
    tm = route_ref.shape[0]
    i = pl.program_id(0)

    @pl.when(i == 0)
    def _():
        carry[...] = jnp.zeros_like(carry)

    r = route_ref[...]
    lane = _iota(r.shape, 1)
    lane_f = lane.astype(F32)
    oh0 = (lane_f == r[:, 2:3]).astype(F32)
    oh1 = (lane_f == r[:, 3:4]).astype(F32)
    both = oh0 + oh1
    earlier = (_iota((tm, tm), 0) > _iota((tm, tm), 1)).astype(BF16)
    prefix = jnp.dot(earlier, both.astype(BF16), preferred_element_type=F32) + carry[0:1, :]
    rank0 = jnp.sum(prefix * oh0, axis=-1, keepdims=True)
    rank1 = jnp.sum((prefix + oh0) * oh1, axis=-1, keepdims=True)
    rank_ref[...] = jnp.where(lane == 0, rank0, jnp.where(lane == 1, rank1, 0.0))
    carry[...] = carry[...] + jnp.sum(both, axis=0, keepdims=True)
    cnt_ref[...] = carry[...]


def _dest_kernel(route_ref, rank_ref, cnt_ref, dest_ref):
    r = route_ref[...]
    lane = _iota(r.shape, 1)
    lane_f = lane.astype(F32)
    padded = jnp.floor((cnt_ref[...] + (MOE_BLOCK - 1.0)) * (1.0 / MOE_BLOCK)) * MOE_BLOCK
    before = (_iota((LANES, LANES), 0) < _iota((LANES, LANES), 1)).astype(F32)
    pad_start = jnp.dot(padded, before, precision=HI, preferred_element_type=F32)[0:1, :]
    oh0 = lane_f == r[:, 2:3]
    oh1 = lane_f == r[:, 3:4]
    d0 = jnp.sum(jnp.where(oh0, pad_start, 0.0), axis=-1, keepdims=True) + rank_ref[:, 0:1]
    d1 = jnp.sum(jnp.where(oh1, pad_start, 0.0), axis=-1, keepdims=True) + rank_ref[:, 1:2]
    dest_ref[...] = jnp.where(lane == 0, d0, jnp.where(lane == 1, d1, 0.0))


def _dispatch_plan(route):
    T = route.shape[0]
    tm = min(512, T)
    slab = pl.BlockSpec((tm, LANES), lambda i: (i, 0))
    cnt_spec = pl.BlockSpec((SUBLANES, LANES), lambda i: (0, 0))
    rank, cnt = pl.pallas_call(
        _rank_kernel,
        out_shape=(jax.ShapeDtypeStruct((T, LANES), F32), jax.ShapeDtypeStruct((SUBLANES, LANES), F32)),
        grid=(T // tm,),
        in_specs=[slab],
        out_specs=(slab, cnt_spec),
        scratch_shapes=[pltpu.VMEM((SUBLANES, LANES), F32)],
        compiler_params=_cparams("arbitrary"),
        name="moe_rank",
    )(route)
    dest = pl.pallas_call(
        _dest_kernel,
        out_shape=jax.ShapeDtypeStruct((T, LANES), F32),
        grid=(T // tm,),
        in_specs=[slab, slab, cnt_spec],
        out_specs=slab,
        compiler_params=_cparams("parallel"),
        name="moe_dest",
    )(route, rank, cnt)
    dest = dest[:, :TOP_K].astype(jnp.int32).reshape(T * TOP_K)
    counts = cnt[0, :N_EXPERTS].astype(jnp.int32)
    pad_end = jnp.cumsum((counts + MOE_BLOCK - 1) // MOE_BLOCK * MOE_BLOCK)
    n_blocks = -(-(T * TOP_K) // MOE_BLOCK) + N_EXPERTS
    block_expert = jnp.minimum(
        jnp.searchsorted(pad_end, jnp.arange(n_blocks, dtype=jnp.int32) * MOE_BLOCK, side="right"),
        N_EXPERTS - 1).astype(jnp.int32)
    n_used = (pad_end[-1] // MOE_BLOCK).astype(jnp.int32).reshape(1)
    return dest, block_expert, n_used, n_blocks * MOE_BLOCK


def _final_norm_kernel(x_ref, g_ref, o_ref):
    x = x_ref[...]
    o_ref[...] = x * lax.rsqrt(jnp.mean(x * x, axis=-1, keepdims=True) + EPS) * g_ref[...]


def _final_norm(x2d, g):
    T, D = x2d.shape
    tm = min(512, T)
    spec = pl.BlockSpec((tm, D), lambda i: (i, 0))
    return pl.pallas_call(
        _final_norm_kernel,
        out_shape=jax.ShapeDtypeStruct((T, D), F32),
        grid=(T // tm,),
        in_specs=[spec, pl.BlockSpec((1, D), lambda i: (0, 0))],
        out_specs=spec,
        compiler_params=_cparams("parallel"),
        name="final_norm",
    )(x2d, g.reshape(1, D))


def kernel(x, c, positions, norm1_g, norm2_g, ada_w, ada_b, w_in, lru_conv_w, lru_conv_b, lru_w_r, lru_b_r, lru_w_i, lru_b_i, lru_lambda, sgu_norm_g, sgu_w, sgu_b, rwkv_mu_prev, rwkv_mu_next, rwkv_w0, rwkv_w_up, rwkv_a0, rwkv_a_up, rwkv_g_up, rwkv_k_k, rwkv_k_a, rwkv_r_k, w_branch, w_out, router_grp_w, router_grp_b, router_exp_w, router_exp_b, expert_w_gu, expert_w_down, final_norm_g):
    B, S, D = x.shape
    L = ada_w.shape[0]
    W = D // 2
    T = B * S
    feat_w = rwkv_mu_prev.shape[1]

    c_pad = jnp.zeros((SUBLANES, D), F32).at[:B].set(c)
    mod = _ada(c_pad, ada_w, ada_b)

    dk = D // 2 // (2 * RET_HEADS)
    inv_freq = ROPE_THETA ** (-jnp.arange(0, dk, 2, dtype=F32) / dk)
    ang = positions.astype(F32)[..., None] * inv_freq
    cosf, sinf = _rope_tables(jnp.tile(ang, (1, 1, LANES // (dk // 2))))
    log_gamma = np.log1p(-np.exp2(-5.0 - np.arange(RET_HEADS, dtype=np.float64)))
    lg_tab = jnp.asarray(np.broadcast_to(log_gamma[:, None], (RET_HEADS, LANES)), F32)

    x2d = x.reshape(T, D)
    for l in range(L):
        m = mod[l, :B]
        sh1, sc1, g1, sh2, sc2, g2 = [m[:, None, k * D:(k + 1) * D] for k in range(6)]

        h = _normmod(x2d.reshape(B, S, D), norm1_g[l], sc1, sh1, BF16).reshape(T, D)
        n_a = 7 * W
        proj_a = _mm(h, w_in, l, 0, n_a, 2048, 512, name="proj_a").reshape(B, S, n_a)
        feat = _mm(h, w_in, l, n_a, feat_w, 1024, feat_w // 3, name="proj_rwkv").reshape(B, S, feat_w)
        logits = _mm(h, w_in, l, n_a + feat_w, N_BRANCH * D, 2048, 512, name="proj_gate")

        y_a = _lru(proj_a, B, S, W, 0, W // LANES, lru_conv_w[l], lru_conv_b[l], lru_w_r[l],
                   lru_b_r[l], lru_w_i[l], lru_b_i[l], lru_lambda[l])
        y_b = _retention(proj_a, B, S, W, 2 * W // LANES, 5 * W // (2 * LANES),
                         3 * W // (2 * LANES), 4 * W // (2 * LANES), cosf, sinf, lg_tab)
        y_c = _sgu(proj_a, B, S, W, 5, 6, sgu_norm_g[l], sgu_w[l], sgu_b[l])
        r, kk, v, g, bonus, lw, bb, kt = _rwkv_pre(
            feat, rwkv_mu_prev[l], rwkv_mu_next[l], rwkv_w0[l], rwkv_w_up[l], rwkv_a0[l],
            rwkv_a_up[l], rwkv_g_up[l], rwkv_k_k[l], rwkv_k_a[l], rwkv_r_k[l].reshape(-1))
        y_d = _rwkv_post(_rwkv_scan(r, kk, v, lw, bb, kt), bonus, g)

        merged = _merge([y.reshape(T, W) for y in (y_a, y_b, y_c, y_d)],
                        w_branch, l, logits, 1024, 256)
        x2d = _mm_residual(merged, w_out, l, x2d, g1, S, 2048, 512)

        h2 = _normmod(x2d.reshape(B, S, D), norm2_g[l], sc2, sh2, F32).reshape(T, D)
        n_r = N_GROUPS + N_EXPERTS
        w_r = jnp.zeros((D, LANES), F32).at[:, :N_GROUPS].set(router_grp_w[l]).at[:, N_GROUPS:n_r].set(
            router_exp_w[l]).astype(BF16)
        b_r = jnp.zeros((1, LANES), F32).at[0, :N_GROUPS].set(router_grp_b[l]).at[0, N_GROUPS:n_r].set(
            router_exp_b[l])
        route = _router(h2, w_r, b_r)
        dest, block_expert, n_used, cap = _dispatch_plan(route)
        ys = _experts(_dispatch(h2, dest, cap), block_expert, n_used, expert_w_gu, expert_w_down, l)
        x2d = _combine(x2d, ys, dest, route, g2, S)

    return _final_norm(x2d, final_norm_g).reshape(B, S, D)
```

```python
import functools
import math

import jax
import jax.numpy as jnp
import numpy as np
from jax import lax
from jax.experimental import pallas as pl
from jax.experimental.pallas import tpu as pltpu

F32 = jnp.float32
BF16 = jnp.bfloat16
EPS = 1e-6

LANES = 128
SUBLANES = 8
VMEM_LIMIT_BYTES = 56 * 1024 * 1024

LRU_BLOCKS = 8
CONV_WIDTH = 4
LRU_C = 8.0
RET_HEADS = 8
RET_CHUNK = 128
ROPE_THETA = 10000.0
SGU_GROUPS = 8
SGU_CHUNK = 128
RWKV_HEAD_DIM = 64
DECAY_LORA = 64
ICLR_LORA = 64
GATE_LORA = 128
DECAY_SCALE = math.exp(-0.5)
N_GROUPS = 4
EXPERTS_PER_GROUP = 8
N_EXPERTS = N_GROUPS * EXPERTS_PER_GROUP
TOP_K = 2
MOE_BLOCK = 256
N_BRANCH = 4

RWKV_CHUNK = 64
HI = lax.Precision.HIGHEST


def _cparams(*sem):
    return pltpu.CompilerParams(dimension_semantics=sem, vmem_limit_bytes=VMEM_LIMIT_BYTES)


def _bdot(a, b):
    return jnp.dot(a.astype(BF16), b.astype(BF16), preferred_element_type=F32)


def _bdot_nt(a, b):
    return lax.dot_general(a.astype(BF16), b.astype(BF16), (((1,), (1,)), ((), ())),
                           preferred_element_type=F32)


def _gelu(x):
    return 0.5 * x * (1.0 + jnp.tanh(math.sqrt(2.0 / math.pi) * (x + 0.044715 * (x * x * x))))


def _sigmoid(x):
    return 1.0 / (1.0 + jnp.exp(-x))


def _silu(x):
    return x * _sigmoid(x)


def _iota(shape, dim):
    return lax.broadcasted_iota(jnp.int32, shape, dim)


def _ada_kernel(c_ref, w_ref, b_ref, o_ref):
    cond = _silu(c_ref[...])
    o_ref[0] = _bdot(cond, w_ref[0]) + b_ref[0]


def _ada(c_pad, ada_w, ada_b):
    L, D, N = ada_w.shape
    R = c_pad.shape[0]
    tn = 1024
    return pl.pallas_call(
        _ada_kernel,
        out_shape=jax.ShapeDtypeStruct((L, R, N), F32),
        grid=(L, N // tn),
        in_specs=[pl.BlockSpec((R, D), lambda l, j: (0, 0)),
                  pl.BlockSpec((1, D, tn), lambda l, j: (l, 0, j)),
                  pl.BlockSpec((1, 1, tn), lambda l, j: (l, 0, j))],
        out_specs=pl.BlockSpec((1, R, tn), lambda l, j: (l, 0, j)),
        compiler_params=_cparams("parallel", "parallel"),
        name="ada_mod",
    )(c_pad, ada_w, ada_b.reshape(L, 1, N))


def _pack_bf16_pairs(y):
    half = y.shape[1] // 2
    bits = pltpu.bitcast(y.astype(BF16).astype(F32), jnp.uint32)
    return (bits[:, :half] >> 16) | (bits[:, half:] & jnp.uint32(0xFFFF0000))


def _unpack_bf16_pairs(u):
    lo = pltpu.bitcast(u << 16, F32)
    hi = pltpu.bitcast(u & jnp.uint32(0xFFFF0000), F32)
    return jnp.concatenate([lo, hi], axis=1).astype(BF16)


def _normmod_kernel(x_ref, g_ref, sc_ref, sh_ref, o_ref):
    x = x_ref[0]
    y = x * lax.rsqrt(jnp.mean(x * x, axis=-1, keepdims=True) + EPS) * g_ref[...]
    y = y * (1.0 + sc_ref[0]) + sh_ref[0]
    if o_ref.dtype == jnp.uint32:
        o_ref[0] = _pack_bf16_pairs(y)
    else:
        o_ref[0] = y.astype(o_ref.dtype)


def _normmod(x, g, sc, sh, out_dtype):
    B, S, D = x.shape
    ts = min(512, S)
    d_out = D // 2 if out_dtype == jnp.uint32 else D
    return pl.pallas_call(
        _normmod_kernel,
        out_shape=jax.ShapeDtypeStruct((B, S, d_out), out_dtype),
        grid=(B, S // ts),
        in_specs=[pl.BlockSpec((1, ts, D), lambda b, i: (b, i, 0)),
                  pl.BlockSpec((1, D), lambda b, i: (0, 0)),
                  pl.BlockSpec((1, 1, D), lambda b, i: (b, 0, 0)),
                  pl.BlockSpec((1, 1, D), lambda b, i: (b, 0, 0))],
        out_specs=pl.BlockSpec((1, ts, d_out), lambda b, i: (b, i, 0)),
        compiler_params=_cparams("parallel", "parallel"),
        name="norm_mod",
    )(x, g.reshape(1, D), sc, sh)


def _mm_kernel(a_ref, w_ref, o_ref):
    o_ref[...] = jnp.dot(a_ref[...], w_ref[0].astype(BF16),
                         preferred_element_type=F32).astype(o_ref.dtype)


def _mm(a, w, layer, col0, n_cols, tm, tn, out_dtype=F32, name="proj"):
    M, K = a.shape
    return pl.pallas_call(
        _mm_kernel,
        out_shape=jax.ShapeDtypeStruct((M, n_cols), out_dtype),
        grid=(M // tm, n_cols // tn),
        in_specs=[pl.BlockSpec((tm, K), lambda i, j: (i, 0)),
                  pl.BlockSpec((pl.Element(1), pl.Element(K), pl.Element(tn)),
                               lambda i, j: (layer, 0, pl.multiple_of(col0 + j * tn, LANES)))],
        out_specs=pl.BlockSpec((tm, tn), lambda i, j: (i, j)),
        compiler_params=_cparams("parallel", "parallel"),
        name=name,
    )(a, w)


def _mm_res_kernel(a_ref, w_ref, x_ref, g_ref, o_ref):
    y = jnp.dot(a_ref[...], w_ref[0].astype(BF16), preferred_element_type=F32)
    o_ref[...] = x_ref[...] + g_ref[0] * y


def _mm_residual(a, w, layer, x2d, gate, rows_per_batch, tm, tn, name="out_proj"):
    M, K = a.shape
    N = w.shape[2]
    tiles_per_batch = rows_per_batch // tm
    return pl.pallas_call(
        _mm_res_kernel,
        out_shape=jax.ShapeDtypeStruct((M, N), F32),
        grid=(M // tm, N // tn),
        in_specs=[pl.BlockSpec((tm, K), lambda i, j: (i, 0)),
                  pl.BlockSpec((1, K, tn), lambda i, j: (layer, 0, j)),
                  pl.BlockSpec((tm, tn), lambda i, j: (i, j)),
                  pl.BlockSpec((1, 1, tn), lambda i, j: (i // tiles_per_batch, 0, j))],
        out_specs=pl.BlockSpec((tm, tn), lambda i, j: (i, j)),
        compiler_params=_cparams("parallel", "parallel"),
        name=name,
    )(a, w, x2d, gate)


def _softplus(x):
    return jnp.maximum(x, 0.0) + jnp.log1p(jnp.exp(-jnp.abs(x)))


def _lru_kernel(x_ref, gate_ref, cw_ref, cb_ref, wr_ref, br_ref, wi_ref, bi_ref, lam_ref,
                o_ref, a_scr, u_scr):
    S = x_ref.shape[1]
    x = x_ref[0]
    rows = _iota(x.shape, 0)
    xm2 = jnp.where(rows >= 2, pltpu.roll(x, 2, 0), 0.0)
    xm1 = jnp.where(rows >= 1, pltpu.roll(x, 1, 0), 0.0)
    xp1 = jnp.where(rows < S - 1, pltpu.roll(x, S - 1, 0), 0.0)
    xc = (cw_ref[0:1, :] * xm2 + cw_ref[1:2, :] * xm1 + cw_ref[2:3, :] * x
          + cw_ref[3:4, :] * xp1 + cb_ref[...])
    for z in range(2):
        r = _sigmoid(_bdot(xc, wr_ref[z, 0]) + br_ref[z:z + 1, :])
        i = _sigmoid(_bdot(xc, wi_ref[z, 0]) + bi_ref[z:z + 1, :])
        log_a = -LRU_C * r * _softplus(-lam_ref[z:z + 1, :])
        a = jnp.exp(log_a)
        a_scr[z] = a
        u_scr[z] = jnp.sqrt(-jnp.tanh(log_a) * (a * a + 1.0)) * i * xc

    n_tiles = S // SUBLANES
    L = x.shape[1]
    trow = _iota((SUBLANES, L), 0)

    def tile_scan(a, u, reverse):
        for d in (1, 2, 4):
            if reverse:
                keep = trow < SUBLANES - d
                a_s = jnp.where(keep, pltpu.roll(a, SUBLANES - d, 0), 1.0)
                u_s = jnp.where(keep, pltpu.roll(u, SUBLANES - d, 0), 0.0)
            else:
                keep = trow >= d
                a_s = jnp.where(keep, pltpu.roll(a, d, 0), 1.0)
                u_s = jnp.where(keep, pltpu.roll(u, d, 0), 0.0)
            u = u + a * u_s
            a = a * a_s
        return a, u

    def body(k, carry):
        hf, hb = carry
        f0 = pl.multiple_of(k * SUBLANES, SUBLANES)
        b0 = pl.multiple_of((n_tiles - 1 - k) * SUBLANES, SUBLANES)
        af, uf = tile_scan(a_scr[0, pl.ds(f0, SUBLANES), :], u_scr[0, pl.ds(f0, SUBLANES), :], False)
        ab, ub = tile_scan(a_scr[1, pl.ds(b0, SUBLANES), :], u_scr[1, pl.ds(b0, SUBLANES), :], True)
        tf = uf + af * hf
        tb = ub + ab * hb
        u_scr[0, pl.ds(f0, SUBLANES), :] = tf
        u_scr[1, pl.ds(b0, SUBLANES), :] = tb
        return tf[SUBLANES - 1:SUBLANES, :], tb[0:1, :]

    zero = jnp.zeros((1, L), F32)
    lax.fori_loop(0, n_tiles, body, (zero, zero), unroll=4)
    o_ref[0] = ((u_scr[0] + u_scr[1]) * _gelu(gate_ref[0])).astype(o_ref.dtype)


def _lru(proj, B, S, W, x_col, g_col, cw, cb, wr, br, wi, bi, lam):
    nb = W // LANES
    return pl.pallas_call(
        _lru_kernel,
        out_shape=jax.ShapeDtypeStruct((B, S, W), BF16),
        grid=(B, nb),
        in_specs=[pl.BlockSpec((1, S, LANES), lambda b, j: (b, 0, x_col + j)),
                  pl.BlockSpec((1, S, LANES), lambda b, j: (b, 0, g_col + j)),
                  pl.BlockSpec((CONV_WIDTH, LANES), lambda b, j: (0, j)),
                  pl.BlockSpec((1, LANES), lambda b, j: (0, j)),
                  pl.BlockSpec((2, 1, LANES, LANES), lambda b, j: (0, j, 0, 0)),
                  pl.BlockSpec((2, LANES), lambda b, j: (0, j)),
                  pl.BlockSpec((2, 1, LANES, LANES), lambda b, j: (0, j, 0, 0)),
                  pl.BlockSpec((2, LANES), lambda b, j: (0, j)),
                  pl.BlockSpec((2, LANES), lambda b, j: (0, j))],
        out_specs=pl.BlockSpec((1, S, LANES), lambda b, j: (b, 0, j)),
        scratch_shapes=[pltpu.VMEM((2, S, LANES), F32), pltpu.VMEM((2, S, LANES), F32)],
        compiler_params=_cparams("parallel", "parallel"),
        name="rglru",
    )(proj, proj, cw, cb.reshape(1, W), wr, br, wi, bi, lam)


def _rope_kernel(ang_ref, cos_ref, sin_ref):
    ang = ang_ref[0]
    lane = _iota(ang.shape, 1)
    cos_ref[0] = jnp.cos(ang)
    s = jnp.sin(ang)
    sin_ref[0] = jnp.where((lane % 64) < 32, -s, s)


def _rope_tables(ang):
    B, S, _ = ang.shape
    ts = min(512, S)
    spec = pl.BlockSpec((1, ts, LANES), lambda b, i: (b, i, 0))
    return pl.pallas_call(
        _rope_kernel,
        out_shape=(jax.ShapeDtypeStruct((B, S, LANES), F32),) * 2,
        grid=(B, S // ts),
        in_specs=[spec],
        out_specs=(spec, spec),
        compiler_params=_cparams("parallel", "parallel"),
        name="rope_tables",
    )(ang)


def _ret_kernel(q_ref, k_ref, v_ref, g_ref, cos_ref, sin_ref, lg_ref, o_ref,
                qs_scr, ks_scr, kv_scr):
    S = q_ref.shape[1]
    C = RET_CHUNK
    N = S // C
    dk = 64
    p = pl.program_id(1)
    lane = _iota((S, LANES), 1)
    first_half = (lane % 64) < 32

    def rope(x):
        swapped = jnp.where(first_half, pltpu.roll(x, LANES - 32, 1), pltpu.roll(x, 32, 1))
        return x * cos_ref[0] + swapped * sin_ref[0]

    qs_scr[...] = rope(q_ref[0]) * (dk ** -0.5)
    ks_scr[...] = rope(k_ref[0])

    pos_r = _iota((C, 1), 0).astype(F32)
    pos_l = _iota((1, C), 1).astype(F32)
    ii = _iota((C, C), 0)
    jj = _iota((C, C), 1)
    dist = jnp.abs(ii - jj).astype(F32)
    clane = _iota((C, LANES), 1)

    for hh in range(2):
        lg = lg_ref[pl.ds(2 * p + hh, 1), :][:, 0:1]
        hmask = (clane // 64) == hh
        intra = jnp.exp(dist * lg)
        dec_kf_row = jnp.exp((C - 1.0 - pos_l) * lg)
        dec_kb_row = jnp.exp(pos_l * lg)
        dec_qf = jnp.exp((pos_r + 1.0) * lg)
        dec_qb = jnp.exp((C - pos_r) * lg)
        chunk_decay = jnp.exp(C * lg)

        def kv_body(n, _):
            r0 = pl.multiple_of(n * C, C)
            kt = jnp.where(hmask, ks_scr[pl.ds(r0, C), :], 0.0).T
            vc = v_ref[0, pl.ds(r0, C), hh * LANES:(hh + 1) * LANES]
            kv_scr[n] = _bdot(jnp.concatenate([kt * dec_kf_row, kt * dec_kb_row], axis=0), vc)
            return 0

        lax.fori_loop(0, N, kv_body, 0, unroll=2)

        def fwd_body(n, st):
            cur = kv_scr[n, 0:LANES, :]
            kv_scr[n, 0:LANES, :] = st
            return chunk_decay * st + cur

        lax.fori_loop(0, N, fwd_body, jnp.zeros((LANES, LANES), F32))

        def bwd_body(m, st):
            n = N - 1 - m
            cur = kv_scr[n, LANES:, :]
            kv_scr[n, LANES:, :] = st
            return chunk_decay * st + cur

        lax.fori_loop(0, N, bwd_body, jnp.zeros((LANES, LANES), F32))

        def out_body(n, _):
            r0 = pl.multiple_of(n * C, C)
            qc = qs_scr[pl.ds(r0, C), :]
            kc = jnp.where(hmask, ks_scr[pl.ds(r0, C), :], 0.0)
            vc = v_ref[0, pl.ds(r0, C), hh * LANES:(hh + 1) * LANES]
            scores = _bdot_nt(qc, kc) * intra
            o = _bdot(scores, vc) + _bdot(jnp.concatenate([qc * dec_qf, qc * dec_qb], axis=1), kv_scr[n])
            mu = jnp.mean(o, axis=-1, keepdims=True)
            var = jnp.mean(o * o, axis=-1, keepdims=True) - mu * mu
            o = (o - mu) * lax.rsqrt(var + EPS)
            gc = g_ref[0, pl.ds(r0, C), hh * LANES:(hh + 1) * LANES]
            o_ref[0, pl.ds(r0, C), hh * LANES:(hh + 1) * LANES] = (_silu(gc) * o).astype(o_ref.dtype)
            return 0

        lax.fori_loop(0, N, out_body, 0, unroll=4)


def _retention(proj, B, S, W, q_col, k_col, v_col, g_col, cosf, sinf, lg_tab):
    n_pairs = RET_HEADS // 2
    N = S // RET_CHUNK
    return pl.pallas_call(
        _ret_kernel,
        out_shape=jax.ShapeDtypeStruct((B, S, W), BF16),
        grid=(B, n_pairs),
        in_specs=[pl.BlockSpec((1, S, LANES), lambda b, p: (b, 0, q_col + p)),
                  pl.BlockSpec((1, S, LANES), lambda b, p: (b, 0, k_col + p)),
                  pl.BlockSpec((1, S, 2 * LANES), lambda b, p: (b, 0, v_col + p)),
                  pl.BlockSpec((1, S, 2 * LANES), lambda b, p: (b, 0, g_col + p)),
                  pl.BlockSpec((1, S, LANES), lambda b, p: (b, 0, 0)),
                  pl.BlockSpec((1, S, LANES), lambda b, p: (b, 0, 0)),
                  pl.BlockSpec((RET_HEADS, LANES), lambda b, p: (0, 0))],
        out_specs=pl.BlockSpec((1, S, 2 * LANES), lambda b, p: (b, 0, p)),
        scratch_shapes=[pltpu.VMEM((S, LANES), F32), pltpu.VMEM((S, LANES), F32),
                        pltpu.VMEM((N, 2 * LANES, LANES), F32)],
        compiler_params=_cparams("parallel", "parallel"),
        name="retention",
    )(proj, proj, proj, proj, cosf, sinf, lg_tab)


def _sgu_kernel(u_ref, v_ref, ng_ref, w_ref, bt_ref, o_ref):
    ts = u_ref.shape[1]
    C = SGU_CHUNK
    v = _gelu(v_ref[0])
    mu = jnp.mean(v, axis=-1, keepdims=True)
    var = jnp.mean(jnp.square(v - mu), axis=-1, keepdims=True)
    v = ((v - mu) * lax.rsqrt(var + EPS) * ng_ref[...]).astype(BF16)
    for c in range(ts // C):
        for g in range(SGU_GROUPS):
            vc = v[c * C:(c + 1) * C, g * LANES:(g + 1) * LANES]
            mixed = jnp.dot(w_ref[g].astype(BF16), vc, preferred_element_type=F32) + bt_ref[:, g:g + 1]
            uc = _gelu(u_ref[0, c * C:(c + 1) * C, g * LANES:(g + 1) * LANES])
            o_ref[0, c * C:(c + 1) * C, g * LANES:(g + 1) * LANES] = (uc * mixed).astype(o_ref.dtype)


def _sgu(proj, B, S, W, u_col, v_col, norm_g, w_s, b_s):
    ts = min(512, S)
    return pl.pallas_call(
        _sgu_kernel,
        out_shape=jax.ShapeDtypeStruct((B, S, W), BF16),
        grid=(B, S // ts),
        in_specs=[pl.BlockSpec((1, ts, W), lambda b, i: (b, i, u_col)),
                  pl.BlockSpec((1, ts, W), lambda b, i: (b, i, v_col)),
                  pl.BlockSpec((1, W), lambda b, i: (0, 0)),
                  pl.BlockSpec((SGU_GROUPS, SGU_CHUNK, SGU_CHUNK), lambda b, i: (0, 0, 0)),
                  pl.BlockSpec((SGU_CHUNK, SGU_GROUPS), lambda b, i: (0, 0))],
        out_specs=pl.BlockSpec((1, ts, W), lambda b, i: (b, i, 0)),
        compiler_params=_cparams("parallel", "parallel"),
        name="spatial_gating",
    )(proj, proj, norm_g.reshape(1, W), w_s, b_s.T)


def _head_sum(x, hd):
    seg = (_iota((LANES, LANES), 0) // hd == _iota((LANES, LANES), 1) // hd).astype(BF16)
    hi = x.astype(BF16)
    lo = (x - hi.astype(F32)).astype(BF16)
    parts = [jnp.dot(hi[:, j * LANES:(j + 1) * LANES], seg, preferred_element_type=F32)
             + jnp.dot(lo[:, j * LANES:(j + 1) * LANES], seg, preferred_element_type=F32)
             for j in range(x.shape[-1] // LANES)]
    return jnp.concatenate(parts, axis=1)


def _rwkv_pre_kernel(f_ref, fp_ref, fn_ref, mup_ref, mun_ref, w0_ref, wup_ref, a0_ref, aup_ref,
                     gup_ref, kk_ref, ka_ref, rk_ref,
                     r_out, kk_out, v_out, g_out, bonus_out, lw_out, bb_out, kt_out):
    ts = f_ref.shape[1]
    W = r_out.shape[2]
    i = pl.program_id(1)
    n_i = pl.num_programs(1)
    f = f_ref[0]
    rows = _iota(f.shape, 0)
    prev_row = jnp.where(i > 0, fp_ref[0, SUBLANES - 1:SUBLANES, :], 0.0)
    next_row = jnp.where(i < n_i - 1, fn_ref[0, 0:1, :], 0.0)
    prev = jnp.where(rows >= 1, pltpu.roll(f, 1, 0), prev_row)
    nxt = jnp.where(rows < ts - 1, pltpu.roll(f, ts - 1, 0), next_row)
    f = f + mup_ref[...] * (prev - f) + mun_ref[...] * (nxt - f)

    r = f[:, 0:W]
    k = f[:, W:2 * W]
    v = f[:, 2 * W:3 * W]
    o = 3 * W
    wd = jnp.tanh(f[:, o:o + 2 * DECAY_LORA])
    ad = f[:, o + 2 * DECAY_LORA:o + 2 * DECAY_LORA + 2 * ICLR_LORA]
    gd = f[:, o + 2 * DECAY_LORA + 2 * ICLR_LORA:]

    g_out[0] = _bdot(_sigmoid(gd), gup_ref[...])
    kk = k * kk_ref[...]
    kk = kk * lax.rsqrt(_head_sum(kk * kk, RWKV_HEAD_DIM) + EPS)
    r_out[0] = r.astype(r_out.dtype)
    kk_out[0] = kk.astype(kk_out.dtype)
    v_out[0] = v.astype(v_out.dtype)
    bonus = jnp.zeros_like(r)
    for z in range(2):
        w_raw = _bdot(wd[:, z * DECAY_LORA:(z + 1) * DECAY_LORA], wup_ref[z]) + w0_ref[z:z + 1, :]
        lw_out[z, 0] = -DECAY_SCALE * _sigmoid(w_raw)
        a = _sigmoid(_bdot(ad[:, z * ICLR_LORA:(z + 1) * ICLR_LORA], aup_ref[z]) + a0_ref[z:z + 1, :])
        bb_out[z, 0] = (a * kk).astype(bb_out.dtype)
        kt = k * (1.0 + (a - 1.0) * ka_ref[...])
        kt_out[z, 0] = kt.astype(kt_out.dtype)
        bonus = bonus + _head_sum(r * kt * rk_ref[...], RWKV_HEAD_DIM) * v
    bonus_out[0] = bonus


def _rwkv_pre(feat, mu_prev, mu_next, w0, w_up, a0, a_up, g_up, k_k, k_a, r_k):
    B, S, Fw = feat.shape
    W = w0.shape[1]
    ts = min(256, S)
    hb = ts // SUBLANES
    n_hb = S // SUBLANES
    row = lambda a: a.reshape(1, -1)
    full2 = lambda a: pl.BlockSpec(a.shape, lambda b, i: (0, 0))
    full3 = lambda a: pl.BlockSpec(a.shape, lambda b, i: (0, 0, 0))
    bsw = pl.BlockSpec((1, ts, W), lambda b, i: (b, i, 0))
    zsw = pl.BlockSpec((2, 1, ts, W), lambda b, i: (0, b, i, 0))
    sds = jax.ShapeDtypeStruct((B, S, W), F32)
    zds = jax.ShapeDtypeStruct((2, B, S, W), F32)
    sdh = jax.ShapeDtypeStruct((B, S, W), BF16)
    zdh = jax.ShapeDtypeStruct((2, B, S, W), BF16)
    args = (row(mu_prev), row(mu_next), w0, w_up, a0, a_up, g_up, row(k_k), row(k_a), row(r_k))
    specs = [full2(args[0]), full2(args[1]), full2(w0), full3(w_up), full2(a0), full3(a_up),
             full2(g_up), full2(args[7]), full2(args[8]), full2(args[9])]
    return pl.pallas_call(
        _rwkv_pre_kernel,
        out_shape=(sdh, sdh, sdh, sds, sds, zds, zdh, zdh),
        grid=(B, S // ts),
        in_specs=[pl.BlockSpec((1, ts, Fw), lambda b, i: (b, i, 0)),
                  pl.BlockSpec((1, SUBLANES, Fw), lambda b, i: (b, jnp.maximum(i * hb - 1, 0), 0)),
                  pl.BlockSpec((1, SUBLANES, Fw), lambda b, i: (b, jnp.minimum((i + 1) * hb, n_hb - 1), 0)),
                  ] + specs,
        out_specs=(bsw, bsw, bsw, bsw, bsw, zsw, zsw, zsw),
        compiler_params=_cparams("parallel", "parallel"),
        name="rwkv_prepare",
    )(feat, feat, feat, *args)


def _rwkv_scan_kernel(r_ref, kk_ref, v_ref, lw_ref, bb_ref, kt_ref, y_ref, st_scr):
    C = RWKV_CHUNK
    C2 = 2 * C
    z = pl.program_id(0)
    i = pl.program_id(2)
    n_pairs = r_ref.shape[2] // LANES
    fwd = z == 0

    @pl.when(i == 0)
    def _():
        st_scr[...] = jnp.zeros_like(st_scr)

    ti = _iota((C, C), 0)
    si = _iota((C, C), 1)
    sign = jnp.where(fwd, 1, -1)
    cum_mat = (sign * (ti - si) >= 0).astype(BF16)
    er = _iota((C2, C2), 0)
    ec = _iota((C2, C2), 1)
    lead = sign * (er % C - ec % C)
    strict = lead > 0
    incl = lead >= 0
    eye2 = (er == ec).astype(F32)
    row_head = _iota((C2, LANES), 0) // C
    lane_head = _iota((C2, LANES), 1) // RWKV_HEAD_DIM
    emask = row_head == lane_head
    diag_l = _iota((LANES, LANES), 0) == _iota((LANES, LANES), 1)

    def expand(x):
        return jnp.where(emask, jnp.concatenate([x, x], axis=0), 0.0)

    def bf(x):
        return x.astype(BF16)

    def mm(a, b):
        return jnp.dot(a, b, preferred_element_type=F32)

    pairs = range(n_pairs)
    sl = [slice(j * LANES, (j + 1) * LANES) for j in pairs]
    lw = [lw_ref[0, 0, :, s] for s in sl]
    def cumsum3(x):
        hi = bf(x)
        r1 = x - hi.astype(F32)
        mid = bf(r1)
        lo = bf(r1 - mid.astype(F32))
        return mm(cum_mat, hi) + (mm(cum_mat, mid) + mm(cum_mat, lo))

    cum = [cumsum3(x) for x in lw]
    tot = [jnp.sum(x, axis=0, keepdims=True) for x in lw]
    p_inv = [jnp.exp(-c) for c in cum]
    p_rest = [jnp.exp(t - c) for t, c in zip(tot, cum)]
    kh_e = [bf(expand(kk_ref[0, :, s] * jnp.exp(c - x))) for s, c, x in zip(sl, cum, lw)]
    rh_e = [bf(expand(r_ref[0, :, s] * jnp.exp(c))) for s, c in zip(sl, cum)]
    bh_e = [bf(expand(bb_ref[0, 0, :, s] * p)) for s, p in zip(sl, p_inv)]
    ktil_e = [bf(expand(kt_ref[0, 0, :, s] * p)) for s, p in zip(sl, p_inv)]
    b_rest_t = [bf(expand(bb_ref[0, 0, :, s] * p).T) for s, p in zip(sl, p_rest)]
    k_rest_t = [bf(expand(kt_ref[0, 0, :, s] * p).T) for s, p in zip(sl, p_rest)]
    v_e = [bf(expand(v_ref[0, :, s])) for s in sl]

    a_all = [lax.dot_general(jnp.concatenate([k, r], axis=0), jnp.concatenate([b, t], axis=0),
                             (((1,), (1,)), ((), ())), preferred_element_type=F32)
             for k, r, b, t in zip(kh_e, rh_e, bh_e, ktil_e)]
    a_b = [jnp.where(strict, a[0:C2, 0:C2], 0.0) for a in a_all]
    a_k = [jnp.where(strict, a[0:C2, C2:], 0.0) for a in a_all]
    a_rb = [jnp.where(incl, a[C2:, 0:C2], 0.0) for a in a_all]
    a_rk = [jnp.where(incl, a[C2:, C2:], 0.0) for a in a_all]

    pw = [-a for a in a_b]
    t_inv = [eye2 + p for p in pw]
    for _ in range(int(math.log2(C)) - 1):
        pw = [mm(p, p) for p in [bf(p) for p in pw]]
        t_inv = [t + mm(bf(t), bf(p)) for t, p in zip(t_inv, pw)]

    akv = [mm(bf(jnp.concatenate([ak, ark], axis=0)), v)
           for ak, ark, v in zip(a_k, a_rk, v_e)]
    tw = [mm(bf(t), jnp.concatenate([k, bf(x[0:C2])], axis=1))
          for t, k, x in zip(t_inv, kh_e, akv)]
    st = [st_scr[j] for j in pairs]
    ws = [mm(jnp.concatenate([bf(x[:, 0:LANES]), r], axis=0), bf(s))
          for x, r, s in zip(tw, rh_e, st)]
    u_e = [x[0:C2] + t[:, LANES:] for x, t in zip(ws, tw)]
    y_e = [x[C2:] + k[C2:] - mm(bf(a), bf(u)) for x, k, a, u in zip(ws, akv, a_rb, u_e)]
    for s, y in zip(sl, y_e):
        y_ref[0, 0, :, s] = y[0:C, :] + y[C:C2, :]

    bw = [mm(b, bf(t)) for b, t in zip(b_rest_t, tw)]
    kv = [mm(k, v) for k, v in zip(k_rest_t, v_e)]
    for j in pairs:
        m_t = jnp.where(diag_l, jnp.exp(tot[j]), 0.0) - bw[j][:, 0:LANES]
        g_t = kv[j] - bw[j][:, LANES:]
        st_scr[j] = mm(bf(m_t), bf(st[j])) + g_t


def _rwkv_scan(r, kk, v, lw, bb, kt):
    B, S, W = r.shape
    C = RWKV_CHUNK
    nc = S // C
    chunk = lambda z, i: i + z * (nc - 1 - 2 * i)
    shared = pl.BlockSpec((1, C, W), lambda z, b, i: (b, chunk(z, i), 0))
    perdir = pl.BlockSpec((1, 1, C, W), lambda z, b, i: (z, b, chunk(z, i), 0))
    return pl.pallas_call(
        _rwkv_scan_kernel,
        out_shape=jax.ShapeDtypeStruct((2, B, S, W), F32),
        grid=(2, B, nc),
        in_specs=[shared, shared, shared, perdir, perdir, perdir],
        out_specs=perdir,
        scratch_shapes=[pltpu.VMEM((W // LANES, LANES, LANES), F32)],
        compiler_params=_cparams("parallel", "parallel", "arbitrary"),
        name="rwkv_scan",
    )(r, kk, v, lw, bb, kt)


def _rwkv_post_kernel(y_ref, bonus_ref, g_ref, o_ref):
    hd = RWKV_HEAD_DIM
    acc = bonus_ref[0]
    for z in range(2):
        y = y_ref[z, 0]
        d = y - _head_sum(y, hd) * (1.0 / hd)
        var = _head_sum(d * d, hd) * (1.0 / hd)
        acc = acc + d * lax.rsqrt(var + EPS)
    o_ref[0] = (acc * g_ref[0]).astype(o_ref.dtype)


def _rwkv_post(y, bonus, g):
    _, B, S, W = y.shape
    ts = min(256, S)
    bsw = pl.BlockSpec((1, ts, W), lambda b, i: (b, i, 0))
    return pl.pallas_call(
        _rwkv_post_kernel,
        out_shape=jax.ShapeDtypeStruct((B, S, W), BF16),
        grid=(B, S // ts),
        in_specs=[pl.BlockSpec((2, 1, ts, W), lambda b, i: (0, b, i, 0)), bsw, bsw],
        out_specs=bsw,
        compiler_params=_cparams("parallel", "parallel"),
        name="rwkv_finish",
    )(y, bonus, g)


def _merge_kernel(ya_ref, yb_ref, yc_ref, yd_ref, wb_ref, l0_ref, l1_ref, l2_ref, l3_ref, o_ref,
                  wb_bf):
    @pl.when(pl.program_id(1) == 0)
    def _():
        wb_bf[...] = wb_ref[0].astype(BF16)

    acc = None
    for n, (y_ref, l_ref) in enumerate(((ya_ref, l0_ref), (yb_ref, l1_ref), (yc_ref, l2_ref),
                                        (yd_ref, l3_ref))):
        br = jnp.dot(y_ref[...], wb_bf[n], preferred_element_type=F32)
        t = _sigmoid(l_ref[...]) * br
        acc = t if acc is None else acc + t
    o_ref[...] = acc.astype(o_ref.dtype)


def _merge(ys, wb, layer, logits, tm, tn):
    T, W = ys[0].shape
    D = wb.shape[3]
    nj = D // tn
    yspec = pl.BlockSpec((tm, W), lambda j, i: (i, 0))
    lspec = lambda n: pl.BlockSpec((tm, tn), lambda j, i: (i, n * nj + j))
    return pl.pallas_call(
        _merge_kernel,
        out_shape=jax.ShapeDtypeStruct((T, D), BF16),
        grid=(nj, T // tm),
        in_specs=[yspec, yspec, yspec, yspec,
                  pl.BlockSpec((1, N_BRANCH, W, tn), lambda j, i: (layer, 0, 0, j)),
                  lspec(0), lspec(1), lspec(2), lspec(3)],
        out_specs=pl.BlockSpec((tm, tn), lambda j, i: (i, j)),
        scratch_shapes=[pltpu.VMEM((N_BRANCH, W, tn), BF16)],
        compiler_params=_cparams("parallel", "arbitrary"),
        name="branch_merge",
    )(*ys, wb, logits, logits, logits, logits)


def _router_kernel(h_ref, w_ref, b_ref, o_ref):
    logits = jnp.dot(_unpack_bf16_pairs(h_ref[...]), w_ref[...], preferred_element_type=F32) + b_ref[...]
    lane = _iota(logits.shape, 1)
    lane_f = lane.astype(F32)
    neg = -3.0e38
    far = float(LANES)
    is_grp = lane < N_GROUPS
    gl = jnp.where(is_grp, logits, neg)
    gmax = jnp.max(gl, axis=-1, keepdims=True)
    gsum = jnp.sum(jnp.where(is_grp, jnp.exp(gl - gmax), 0.0), axis=-1, keepdims=True)
    grp_p = 1.0 / gsum
    grp_i = jnp.min(jnp.where(gl == gmax, lane_f, far), axis=-1, keepdims=True)
    lo = N_GROUPS + grp_i * EXPERTS_PER_GROUP
    in_grp = jnp.logical_and(lane_f >= lo, lane_f < lo + EXPERTS_PER_GROUP)
    el = jnp.where(in_grp, logits, neg)
    m1 = jnp.max(el, axis=-1, keepdims=True)
    i1 = jnp.min(jnp.where(el == m1, lane_f, far), axis=-1, keepdims=True)
    el2 = jnp.where(lane_f == i1, neg, el)
    m2 = jnp.max(el2, axis=-1, keepdims=True)
    i2 = jnp.min(jnp.where(el2 == m2, lane_f, far), axis=-1, keepdims=True)
    t = jnp.exp(m2 - m1)
    w1 = grp_p / (1.0 + t)
    w2 = grp_p * t / (1.0 + t)
    o_ref[...] = jnp.where(lane == 0, w1, jnp.where(lane == 1, w2, jnp.where(
        lane == 2, i1 - N_GROUPS, jnp.where(lane == 3, i2 - N_GROUPS, 0.0))))


def _router(h, w_r, b_r):
    T = h.shape[0]
    D = w_r.shape[0]
    tm = min(1024, T)
    return pl.pallas_call(
        _router_kernel,
        out_shape=jax.ShapeDtypeStruct((T, LANES), F32),
        grid=(T // tm,),
        in_specs=[pl.BlockSpec((tm, D // 2), lambda i: (i, 0)),
                  pl.BlockSpec((D, LANES), lambda i: (0, 0)),
                  pl.BlockSpec((1, LANES), lambda i: (0, 0))],
        out_specs=pl.BlockSpec((tm, LANES), lambda i: (i, 0)),
        compiler_params=_cparams("parallel"),
        name="router",
    )(h, w_r, b_r)


def _expert_kernel(be_ref, nb_ref, xs_ref, wgu_ref, wd_ref, o_ref, wgu_bf, wd_bf):
    i = pl.program_id(0)
    used = i < nb_ref[0]
    changed = jnp.logical_or(i == 0, be_ref[i] != be_ref[jnp.maximum(i - 1, 0)])

    @pl.when(jnp.logical_and(used, changed))
    def _():
        wgu_bf[...] = wgu_ref[0, 0].astype(BF16)
        wd_bf[...] = wd_ref[0, 0].astype(BF16)

    @pl.when(used)
    def _():
        gu = jnp.dot(_unpack_bf16_pairs(xs_ref[...]), wgu_bf[...], preferred_element_type=F32)
        eh = gu.shape[1] // 2
        mid = (_silu(gu[:, :eh]) * gu[:, eh:]).astype(BF16)
        o_ref[...] = jnp.dot(mid, wd_bf[...], preferred_element_type=F32)

    @pl.when(jnp.logical_not(used))
    def _():
        o_ref[...] = jnp.zeros_like(o_ref)


def _experts(xs, block_expert, n_used, w_gu, w_down, layer):
    cap = xs.shape[0]
    D, H2 = w_gu.shape[2], w_gu.shape[3]
    n_blocks = cap // MOE_BLOCK
    blk = lambda i, nb: jnp.minimum(i, nb[0] - 1)
    grid_spec = pltpu.PrefetchScalarGridSpec(
        num_scalar_prefetch=2,
        grid=(n_blocks,),
        in_specs=[pl.BlockSpec((MOE_BLOCK, D // 2), lambda i, be, nb: (blk(i, nb), 0)),
                  pl.BlockSpec((1, 1, D, H2), lambda i, be, nb: (layer, be[blk(i, nb)], 0, 0)),
                  pl.BlockSpec((1, 1, H2 // 2, D), lambda i, be, nb: (layer, be[blk(i, nb)], 0, 0))],
        out_specs=pl.BlockSpec((MOE_BLOCK, D), lambda i, be, nb: (i, 0)),
        scratch_shapes=[pltpu.VMEM((D, H2), BF16), pltpu.VMEM((H2 // 2, D), BF16)],
    )
    return pl.pallas_call(
        _expert_kernel,
        out_shape=jax.ShapeDtypeStruct((cap, D), F32),
        grid_spec=grid_spec,
        compiler_params=_cparams("arbitrary"),
        name="experts",
    )(block_expert, n_used, xs, w_gu, w_down)


def _row_copy(src_ref, s, dst_ref, d, sem):
    return pltpu.make_async_copy(src_ref.at[pl.ds(s, 1), :], dst_ref.at[pl.ds(d, 1), :], sem)


def _rows_wait(src_ref, dst_ref, n, sem):
    pltpu.make_async_copy(src_ref.at[pl.ds(0, n), :], dst_ref.at[pl.ds(0, n), :], sem).wait()


def _combine_kernel(dest_ref, ys_ref, x_ref, r_ref, g_ref, o_ref, y0_buf, y1_buf, sem):
    tm = x_ref.shape[0]
    base = pl.program_id(0) * tm

    def issue(k, c):
        t = base + k
        _row_copy(ys_ref, dest_ref[TOP_K * t], y0_buf, k, sem.at[0]).start()
        _row_copy(ys_ref, dest_ref[TOP_K * t + 1], y1_buf, k, sem.at[1]).start()
        return c

    lax.fori_loop(0, tm, issue, 0)
    _rows_wait(ys_ref, y0_buf, tm, sem.at[0])
    _rows_wait(ys_ref, y1_buf, tm, sem.at[1])
    w0 = r_ref[:, 0:1]
    w1 = r_ref[:, 1:2]
    o_ref[...] = x_ref[...] + g_ref[0] * (w0 * y0_buf[...] + w1 * y1_buf[...])


def _combine(x2d, ys, dest, route, gate, rows_per_batch):
    T, D = x2d.shape
    tm = min(256, T)
    tiles_per_batch = rows_per_batch // tm
    spec = pl.BlockSpec((tm, D), lambda i, d: (i, 0))
    grid_spec = pltpu.PrefetchScalarGridSpec(
        num_scalar_prefetch=1,
        grid=(T // tm,),
        in_specs=[pl.BlockSpec(memory_space=pl.ANY), spec,
                  pl.BlockSpec((tm, LANES), lambda i, d: (i, 0)),
                  pl.BlockSpec((1, 1, D), lambda i, d: (i // tiles_per_batch, 0, 0))],
        out_specs=spec,
        scratch_shapes=[pltpu.VMEM((tm, D), F32), pltpu.VMEM((tm, D), F32),
                        pltpu.SemaphoreType.DMA((2,))],
    )
    return pl.pallas_call(
        _combine_kernel,
        out_shape=jax.ShapeDtypeStruct((T, D), F32),
        grid_spec=grid_spec,
        compiler_params=_cparams("arbitrary"),
        name="moe_combine",
    )(dest, ys, x2d, route, gate)


def _dispatch_kernel(dest_ref, h_ref, xs_in_ref, xs_ref, sem, *, tm):
    del xs_in_ref
    base = pl.program_id(0) * tm

    def issue(k, c):
        t = base + k
        for kk in range(TOP_K):
            _row_copy(h_ref, k, xs_ref, dest_ref[TOP_K * t + kk], sem.at[kk]).start()
        return c

    lax.fori_loop(0, tm, issue, 0)
    for kk in range(TOP_K):
        _rows_wait(h_ref, xs_ref, tm, sem.at[kk])


def _dispatch(h, dest, cap):
    T, Dp = h.shape
    tm = min(512, T)
    grid_spec = pltpu.PrefetchScalarGridSpec(
        num_scalar_prefetch=1,
        grid=(T // tm,),
        in_specs=[pl.BlockSpec((tm, Dp), lambda i, d: (i, 0)), pl.BlockSpec(memory_space=pl.ANY)],
        out_specs=pl.BlockSpec(memory_space=pl.ANY),
        scratch_shapes=[pltpu.SemaphoreType.DMA((TOP_K,))],
    )
    return pl.pallas_call(
        functools.partial(_dispatch_kernel, tm=tm),
        out_shape=jax.ShapeDtypeStruct((cap, Dp), h.dtype),
        grid_spec=grid_spec,
        input_output_aliases={2: 0},
        compiler_params=_cparams("arbitrary"),
        name="moe_dispatch",
    )(dest, h, jnp.zeros((cap, Dp), h.dtype))


def _rank_kernel(route_ref, rank_ref, cnt_ref, carry):
    tm = route_ref.shape[0]
    i = pl.program_id(0)

    @pl.when(i == 0)
    def _():
        carry[...] = jnp.zeros_like(carry)

    r = route_ref[...]
    lane = _iota(r.shape, 1)
    lane_f = lane.astype(F32)
    oh0 = (lane_f == r[:, 2:3]).astype(F32)
    oh1 = (lane_f == r[:, 3:4]).astype(F32)
    both = oh0 + oh1
    earlier = (_iota((tm, tm), 0) > _iota((tm, tm), 1)).astype(BF16)
    prefix = jnp.dot(earlier, both.astype(BF16), preferred_element_type=F32) + carry[0:1, :]
    rank0 = jnp.sum(prefix * oh0, axis=-1, keepdims=True)
    rank1 = jnp.sum((prefix + oh0) * oh1, axis=-1, keepdims=True)
    rank_ref[...] = jnp.where(lane == 0, rank0, jnp.where(lane == 1, rank1, 0.0))
    carry[...] = carry[...] + jnp.sum(both, axis=0, keepdims=True)
    cnt_ref[...] = carry[...]


def _dest_kernel(route_ref, rank_ref, cnt_ref, dest_ref):
    r = route_ref[...]
    lane = _iota(r.shape, 1)
    lane_f = lane.astype(F32)
    padded = jnp.floor((cnt_ref[...] + (MOE_BLOCK - 1.0)) * (1.0 / MOE_BLOCK)) * MOE_BLOCK
    before = (_iota((LANES, LANES), 0) < _iota((LANES, LANES), 1)).astype(F32)
    pad_start = jnp.dot(padded, before, precision=HI, preferred_element_type=F32)[0:1, :]
    oh0 = lane_f == r[:, 2:3]
    oh1 = lane_f == r[:, 3:4]
    d0 = jnp.sum(jnp.where(oh0, pad_start, 0.0), axis=-1, keepdims=True) + rank_ref[:, 0:1]
    d1 = jnp.sum(jnp.where(oh1, pad_start, 0.0), axis=-1, keepdims=True) + rank_ref[:, 1:2]
    dest_ref[...] = jnp.where(lane == 0, d0, jnp.where(lane == 1, d1, 0.0))


def _dispatch_plan(route):
    T = route.shape[0]
    tm = min(512, T)
    slab = pl.BlockSpec((tm, LANES), lambda i: (i, 0))
    cnt_spec = pl.BlockSpec((SUBLANES, LANES), lambda i: (0, 0))
    rank, cnt = pl.pallas_call(
        _rank_kernel,
        out_shape=(jax.ShapeDtypeStruct((T, LANES), F32), jax.ShapeDtypeStruct((SUBLANES, LANES), F32)),
        grid=(T // tm,),
        in_specs=[slab],
        out_specs=(slab, cnt_spec),
        scratch_shapes=[pltpu.VMEM((SUBLANES, LANES), F32)],
        compiler_params=_cparams("arbitrary"),
        name="moe_rank",
    )(route)
    dest = pl.pallas_call(
        _dest_kernel,
        out_shape=jax.ShapeDtypeStruct((T, LANES), F32),
        grid=(T // tm,),
        in_specs=[slab, slab, cnt_spec],
        out_specs=slab,
        compiler_params=_cparams("parallel"),
        name="moe_dest",
    )(route, rank, cnt)
    dest = dest[:, :TOP_K].astype(jnp.int32).reshape(T * TOP_K)
    counts = cnt[0, :N_EXPERTS].astype(jnp.int32)
    padded = (counts + MOE_BLOCK - 1) // MOE_BLOCK * MOE_BLOCK
    pad_end = jnp.cumsum(padded)
    n_blocks = -(-(T * TOP_K) // MOE_BLOCK) + N_EXPERTS
    block_row0 = jnp.arange(n_blocks, dtype=jnp.int32) * MOE_BLOCK
    block_expert = jnp.minimum(jnp.searchsorted(pad_end, block_row0, side="right"),
                               N_EXPERTS - 1).astype(jnp.int32)
    n_used = (pad_end[-1] // MOE_BLOCK).astype(jnp.int32).reshape(1)
    return dest, block_expert, n_used, n_blocks * MOE_BLOCK


def _final_norm_kernel(x_ref, g_ref, o_ref):
    x = x_ref[...]
    o_ref[...] = x * lax.rsqrt(jnp.mean(x * x, axis=-1, keepdims=True) + EPS) * g_ref[...]


def _final_norm(x2d, g):
    T, D = x2d.shape
    tm = min(512, T)
    spec = pl.BlockSpec((tm, D), lambda i: (i, 0))
    return pl.pallas_call(
        _final_norm_kernel,
        out_shape=jax.ShapeDtypeStruct((T, D), F32),
        grid=(T // tm,),
        in_specs=[spec, pl.BlockSpec((1, D), lambda i: (0, 0))],
        out_specs=spec,
        compiler_params=_cparams("parallel"),
        name="final_norm",
    )(x2d, g.reshape(1, D))


def kernel(x, c, positions, norm1_g, norm2_g, ada_w, ada_b, w_in, lru_conv_w, lru_conv_b, lru_w_r, lru_b_r, lru_w_i, lru_b_i, lru_lambda, sgu_norm_g, sgu_w, sgu_b, rwkv_mu_prev, rwkv_mu_next, rwkv_w0, rwkv_w_up, rwkv_a0, rwkv_a_up, rwkv_g_up, rwkv_k_k, rwkv_k_a, rwkv_r_k, w_branch, w_out, router_grp_w, router_grp_b, router_exp_w, router_exp_b, expert_w_gu, expert_w_down, final_norm_g):
    B, S, D = x.shape
    L = ada_w.shape[0]
    W = D // 2
    T = B * S
    feat_w = rwkv_mu_prev.shape[1]

    c_pad = jnp.zeros((SUBLANES, D), F32).at[:B].set(c)
    mod = _ada(c_pad, ada_w, ada_b)

    dk = D // 2 // (2 * RET_HEADS)
    inv_freq = ROPE_THETA ** (-jnp.arange(0, dk, 2, dtype=F32) / dk)
    ang = positions.astype(F32)[..., None] * inv_freq
    cosf, sinf = _rope_tables(jnp.tile(ang, (1, 1, LANES // (dk // 2))))
    log_gamma = np.log1p(-np.exp2(-5.0 - np.arange(RET_HEADS, dtype=np.float64)))
    lg_tab = jnp.asarray(np.broadcast_to(log_gamma[:, None], (RET_HEADS, LANES)), F32)

    x2d = x.reshape(T, D)
    for l in range(L):
        m = mod[l, :B]
        sh1, sc1, g1, sh2, sc2, g2 = [m[:, None, k * D:(k + 1) * D] for k in range(6)]

        h = _normmod(x2d.reshape(B, S, D), norm1_g[l], sc1, sh1, BF16).reshape(T, D)
        n_a = 7 * W
        proj_a = _mm(h, w_in, l, 0, n_a, 2048, 512, name="proj_a").reshape(B, S, n_a)
        feat = _mm(h, w_in, l, n_a, feat_w, 1024, feat_w // 3, name="proj_rwkv").reshape(B, S, feat_w)
        logits = _mm(h, w_in, l, n_a + feat_w, N_BRANCH * D, 2048, 512, name="proj_gate")

        y_a = _lru(proj_a, B, S, W, 0, W // LANES, lru_conv_w[l], lru_conv_b[l], lru_w_r[l],
                   lru_b_r[l], lru_w_i[l], lru_b_i[l], lru_lambda[l])
        y_b = _retention(proj_a, B, S, W, 2 * W // LANES, 5 * W // (2 * LANES),
                         3 * W // (2 * LANES), 4 * W // (2 * LANES), cosf, sinf, lg_tab)
        y_c = _sgu(proj_a, B, S, W, 5, 6, sgu_norm_g[l], sgu_w[l], sgu_b[l])
        r, kk, v, g, bonus, lw, bb, kt = _rwkv_pre(
            feat, rwkv_mu_prev[l], rwkv_mu_next[l], rwkv_w0[l], rwkv_w_up[l], rwkv_a0[l],
            rwkv_a_up[l], rwkv_g_up[l], rwkv_k_k[l], rwkv_k_a[l], rwkv_r_k[l].reshape(-1))
        y_d = _rwkv_post(_rwkv_scan(r, kk, v, lw, bb, kt), bonus, g)

        merged = _merge([y.reshape(T, W) for y in (y_a, y_b, y_c, y_d)],
                        w_branch, l, logits, 512, 512)
        x2d = _mm_residual(merged, w_out, l, x2d, g1, S, 2048, 512)

        h2 = _normmod(x2d.reshape(B, S, D), norm2_g[l], sc2, sh2, jnp.uint32).reshape(T, D // 2)
        n_r = N_GROUPS + N_EXPERTS
        w_r = jnp.zeros((D, LANES), F32).at[:, :N_GROUPS].set(router_grp_w[l]).at[:, N_GROUPS:n_r].set(
            router_exp_w[l]).astype(BF16)
        b_r = jnp.zeros((1, LANES), F32).at[0, :N_GROUPS].set(router_grp_b[l]).at[0, N_GROUPS:n_r].set(
            router_exp_b[l])
        route = _router(h2, w_r, b_r)
        dest, block_expert, n_used, cap = _dispatch_plan(route)
        ys = _experts(_dispatch(h2, dest, cap), block_expert, n_used, expert_w_gu, expert_w_down, l)
        x2d = _combine(x2d, ys, dest, route, g2, S)

    return _final_norm(x2d, final_norm_g).reshape(B, S, D)
```

```python
import functools
import math

import jax
import jax.numpy as jnp
import numpy as np
from jax import lax
from jax.experimental import pallas as pl
from jax.experimental.pallas import tpu as pltpu

F32 = jnp.float32
BF16 = jnp.bfloat16
EPS = 1e-6

LANES = 128
SUBLANES = 8
VMEM_LIMIT_BYTES = 56 * 1024 * 1024

LRU_BLOCKS = 8
CONV_WIDTH = 4
LRU_C = 8.0
RET_HEADS = 8
RET_CHUNK = 128
ROPE_THETA = 10000.0
SGU_GROUPS = 8
SGU_CHUNK = 128
RWKV_HEAD_DIM = 64
DECAY_LORA = 64
ICLR_LORA = 64
GATE_LORA = 128
DECAY_SCALE = math.exp(-0.5)
N_GROUPS = 4
EXPERTS_PER_GROUP = 8
N_EXPERTS = N_GROUPS * EXPERTS_PER_GROUP
TOP_K = 2
MOE_BLOCK = 256
N_BRANCH = 4

RWKV_CHUNK = 64
HI = lax.Precision.HIGHEST


def _cparams(*sem):
    return pltpu.CompilerParams(dimension_semantics=sem, vmem_limit_bytes=VMEM_LIMIT_BYTES)


def _bdot(a, b):
    return jnp.dot(a.astype(BF16), b.astype(BF16), preferred_element_type=F32)


def _bdot_nt(a, b):
    return lax.dot_general(a.astype(BF16), b.astype(BF16), (((1,), (1,)), ((), ())),
                           preferred_element_type=F32)


def _gelu(x):
    return 0.5 * x * (1.0 + jnp.tanh(math.sqrt(2.0 / math.pi) * (x + 0.044715 * (x * x * x))))


def _sigmoid(x):
    return 1.0 / (1.0 + jnp.exp(-x))


def _silu(x):
    return x * _sigmoid(x)


def _iota(shape, dim):
    return lax.broadcasted_iota(jnp.int32, shape, dim)


def _ada_kernel(c_ref, w_ref, b_ref, o_ref):
    cond = _silu(c_ref[...])
    o_ref[0] = _bdot(cond, w_ref[0]) + b_ref[0]


def _ada(c_pad, ada_w, ada_b):
    L, D, N = ada_w.shape
    R = c_pad.shape[0]
    tn = 1024
    return pl.pallas_call(
        _ada_kernel,
        out_shape=jax.ShapeDtypeStruct((L, R, N), F32),
        grid=(L, N // tn),
        in_specs=[pl.BlockSpec((R, D), lambda l, j: (0, 0)),
                  pl.BlockSpec((1, D, tn), lambda l, j: (l, 0, j)),
                  pl.BlockSpec((1, 1, tn), lambda l, j: (l, 0, j))],
        out_specs=pl.BlockSpec((1, R, tn), lambda l, j: (l, 0, j)),
        compiler_params=_cparams("parallel", "parallel"),
        name="ada_mod",
    )(c_pad, ada_w, ada_b.reshape(L, 1, N))


def _pack_bf16_pairs(y):
    half = y.shape[1] // 2
    bits = pltpu.bitcast(y.astype(BF16).astype(F32), jnp.uint32)
    return (bits[:, :half] >> 16) | (bits[:, half:] & jnp.uint32(0xFFFF0000))


def _unpack_bf16_pairs(u):
    lo = pltpu.bitcast(u << 16, F32)
    hi = pltpu.bitcast(u & jnp.uint32(0xFFFF0000), F32)
    return jnp.concatenate([lo, hi], axis=1).astype(BF16)


def _normmod_kernel(x_ref, g_ref, sc_ref, sh_ref, o_ref):
    x = x_ref[0]
    y = x * lax.rsqrt(jnp.mean(x * x, axis=-1, keepdims=True) + EPS) * g_ref[...]
    y = y * (1.0 + sc_ref[0]) + sh_ref[0]
    if o_ref.dtype == jnp.uint32:
        o_ref[0] = _pack_bf16_pairs(y)
    else:
        o_ref[0] = y.astype(o_ref.dtype)


def _normmod(x, g, sc, sh, out_dtype):
    B, S, D = x.shape
    ts = min(512, S)
    d_out = D // 2 if out_dtype == jnp.uint32 else D
    return pl.pallas_call(
        _normmod_kernel,
        out_shape=jax.ShapeDtypeStruct((B, S, d_out), out_dtype),
        grid=(B, S // ts),
        in_specs=[pl.BlockSpec((1, ts, D), lambda b, i: (b, i, 0)),
                  pl.BlockSpec((1, D), lambda b, i: (0, 0)),
                  pl.BlockSpec((1, 1, D), lambda b, i: (b, 0, 0)),
                  pl.BlockSpec((1, 1, D), lambda b, i: (b, 0, 0))],
        out_specs=pl.BlockSpec((1, ts, d_out), lambda b, i: (b, i, 0)),
        compiler_params=_cparams("parallel", "parallel"),
        name="norm_mod",
    )(x, g.reshape(1, D), sc, sh)


def _mm_kernel(a_ref, w_ref, o_ref):
    o_ref[...] = jnp.dot(a_ref[...], w_ref[0].astype(BF16),
                         preferred_element_type=F32).astype(o_ref.dtype)


def _mm(a, w, layer, col0, n_cols, tm, tn, out_dtype=F32, name="proj"):
    M, K = a.shape
    return pl.pallas_call(
        _mm_kernel,
        out_shape=jax.ShapeDtypeStruct((M, n_cols), out_dtype),
        grid=(M // tm, n_cols // tn),
        in_specs=[pl.BlockSpec((tm, K), lambda i, j: (i, 0)),
                  pl.BlockSpec((pl.Element(1), pl.Element(K), pl.Element(tn)),
                               lambda i, j: (layer, 0, pl.multiple_of(col0 + j * tn, LANES)))],
        out_specs=pl.BlockSpec((tm, tn), lambda i, j: (i, j)),
        compiler_params=_cparams("parallel", "parallel"),
        name=name,
    )(a, w)


def _mm_res_kernel(a_ref, w_ref, x_ref, g_ref, o_ref):
    y = jnp.dot(a_ref[...], w_ref[0].astype(BF16), preferred_element_type=F32)
    o_ref[...] = x_ref[...] + g_ref[0] * y


def _mm_residual(a, w, layer, x2d, gate, rows_per_batch, tm, tn, name="out_proj"):
    M, K = a.shape
    N = w.shape[2]
    tiles_per_batch = rows_per_batch // tm
    return pl.pallas_call(
        _mm_res_kernel,
        out_shape=jax.ShapeDtypeStruct((M, N), F32),
        grid=(M // tm, N // tn),
        in_specs=[pl.BlockSpec((tm, K), lambda i, j: (i, 0)),
                  pl.BlockSpec((1, K, tn), lambda i, j: (layer, 0, j)),
                  pl.BlockSpec((tm, tn), lambda i, j: (i, j)),
                  pl.BlockSpec((1, 1, tn), lambda i, j: (i // tiles_per_batch, 0, j))],
        out_specs=pl.BlockSpec((tm, tn), lambda i, j: (i, j)),
        compiler_params=_cparams("parallel", "parallel"),
        name=name,
    )(a, w, x2d, gate)


def _softplus(x):
    return jnp.maximum(x, 0.0) + jnp.log1p(jnp.exp(-jnp.abs(x)))


def _lru_kernel(x_ref, gate_ref, cw_ref, cb_ref, wr_ref, br_ref, wi_ref, bi_ref, lam_ref,
                o_ref, a_scr, u_scr):
    S = x_ref.shape[1]
    x = x_ref[0]
    rows = _iota(x.shape, 0)
    xm2 = jnp.where(rows >= 2, pltpu.roll(x, 2, 0), 0.0)
    xm1 = jnp.where(rows >= 1, pltpu.roll(x, 1, 0), 0.0)
    xp1 = jnp.where(rows < S - 1, pltpu.roll(x, S - 1, 0), 0.0)
    xc = (cw_ref[0:1, :] * xm2 + cw_ref[1:2, :] * xm1 + cw_ref[2:3, :] * x
          + cw_ref[3:4, :] * xp1 + cb_ref[...])
    for z in range(2):
        r = _sigmoid(_bdot(xc, wr_ref[z, 0]) + br_ref[z:z + 1, :])
        i = _sigmoid(_bdot(xc, wi_ref[z, 0]) + bi_ref[z:z + 1, :])
        log_a = -LRU_C * r * _softplus(-lam_ref[z:z + 1, :])
        a = jnp.exp(log_a)
        a_scr[z] = a
        u_scr[z] = jnp.sqrt(-jnp.tanh(log_a) * (a * a + 1.0)) * i * xc

    n_tiles = S // SUBLANES
    L = x.shape[1]
    trow = _iota((SUBLANES, L), 0)

    def tile_scan(a, u, reverse):
        for d in (1, 2, 4):
            if reverse:
                keep = trow < SUBLANES - d
                a_s = jnp.where(keep, pltpu.roll(a, SUBLANES - d, 0), 1.0)
                u_s = jnp.where(keep, pltpu.roll(u, SUBLANES - d, 0), 0.0)
            else:
                keep = trow >= d
                a_s = jnp.where(keep, pltpu.roll(a, d, 0), 1.0)
                u_s = jnp.where(keep, pltpu.roll(u, d, 0), 0.0)
            u = u + a * u_s
            a = a * a_s
        return a, u

    def body(k, carry):
        hf, hb = carry
        f0 = pl.multiple_of(k * SUBLANES, SUBLANES)
        b0 = pl.multiple_of((n_tiles - 1 - k) * SUBLANES, SUBLANES)
        af, uf = tile_scan(a_scr[0, pl.ds(f0, SUBLANES), :], u_scr[0, pl.ds(f0, SUBLANES), :], False)
        ab, ub = tile_scan(a_scr[1, pl.ds(b0, SUBLANES), :], u_scr[1, pl.ds(b0, SUBLANES), :], True)
        tf = uf + af * hf
        tb = ub + ab * hb
        u_scr[0, pl.ds(f0, SUBLANES), :] = tf
        u_scr[1, pl.ds(b0, SUBLANES), :] = tb
        return tf[SUBLANES - 1:SUBLANES, :], tb[0:1, :]

    zero = jnp.zeros((1, L), F32)
    lax.fori_loop(0, n_tiles, body, (zero, zero), unroll=8)
    o_ref[0] = ((u_scr[0] + u_scr[1]) * _gelu(gate_ref[0])).astype(o_ref.dtype)


def _lru(proj, B, S, W, x_col, g_col, cw, cb, wr, br, wi, bi, lam):
    nb = W // LANES
    return pl.pallas_call(
        _lru_kernel,
        out_shape=jax.ShapeDtypeStruct((B, S, W), BF16),
        grid=(B, nb),
        in_specs=[pl.BlockSpec((1, S, LANES), lambda b, j: (b, 0, x_col + j)),
                  pl.BlockSpec((1, S, LANES), lambda b, j: (b, 0, g_col + j)),
                  pl.BlockSpec((CONV_WIDTH, LANES), lambda b, j: (0, j)),
                  pl.BlockSpec((1, LANES), lambda b, j: (0, j)),
                  pl.BlockSpec((2, 1, LANES, LANES), lambda b, j: (0, j, 0, 0)),
                  pl.BlockSpec((2, LANES), lambda b, j: (0, j)),
                  pl.BlockSpec((2, 1, LANES, LANES), lambda b, j: (0, j, 0, 0)),
                  pl.BlockSpec((2, LANES), lambda b, j: (0, j)),
                  pl.BlockSpec((2, LANES), lambda b, j: (0, j))],
        out_specs=pl.BlockSpec((1, S, LANES), lambda b, j: (b, 0, j)),
        scratch_shapes=[pltpu.VMEM((2, S, LANES), F32), pltpu.VMEM((2, S, LANES), F32)],
        compiler_params=_cparams("parallel", "parallel"),
        name="rglru",
    )(proj, proj, cw, cb.reshape(1, W), wr, br, wi, bi, lam)


def _rope_kernel(ang_ref, cos_ref, sin_ref):
    ang = ang_ref[0]
    lane = _iota(ang.shape, 1)
    cos_ref[0] = jnp.cos(ang)
    s = jnp.sin(ang)
    sin_ref[0] = jnp.where((lane % 64) < 32, -s, s)


def _rope_tables(ang):
    B, S, _ = ang.shape
    ts = min(512, S)
    spec = pl.BlockSpec((1, ts, LANES), lambda b, i: (b, i, 0))
    return pl.pallas_call(
        _rope_kernel,
        out_shape=(jax.ShapeDtypeStruct((B, S, LANES), F32),) * 2,
        grid=(B, S // ts),
        in_specs=[spec],
        out_specs=(spec, spec),
        compiler_params=_cparams("parallel", "parallel"),
        name="rope_tables",
    )(ang)


def _ret_kernel(q_ref, k_ref, v_ref, g_ref, cos_ref, sin_ref, lg_ref, o_ref,
                qs_scr, ks_scr, kv_scr):
    S = q_ref.shape[1]
    C = RET_CHUNK
    N = S // C
    dk = 64
    p = pl.program_id(1)
    lane = _iota((S, LANES), 1)
    first_half = (lane % 64) < 32

    def rope(x):
        swapped = jnp.where(first_half, pltpu.roll(x, LANES - 32, 1), pltpu.roll(x, 32, 1))
        return x * cos_ref[0] + swapped * sin_ref[0]

    qs_scr[...] = rope(q_ref[0]) * (dk ** -0.5)
    ks_scr[...] = rope(k_ref[0])

    pos_r = _iota((C, 1), 0).astype(F32)
    pos_l = _iota((1, C), 1).astype(F32)
    ii = _iota((C, C), 0)
    jj = _iota((C, C), 1)
    dist = jnp.abs(ii - jj).astype(F32)
    clane = _iota((C, LANES), 1)

    for hh in range(2):
        lg = lg_ref[pl.ds(2 * p + hh, 1), :][:, 0:1]
        hmask = (clane // 64) == hh
        intra = jnp.exp(dist * lg)
        dec_kf_row = jnp.exp((C - 1.0 - pos_l) * lg)
        dec_kb_row = jnp.exp(pos_l * lg)
        dec_qf = jnp.exp((pos_r + 1.0) * lg)
        dec_qb = jnp.exp((C - pos_r) * lg)
        chunk_decay = jnp.exp(C * lg)

        def kv_body(n, _):
            r0 = pl.multiple_of(n * C, C)
            kt = jnp.where(hmask, ks_scr[pl.ds(r0, C), :], 0.0).T
            vc = v_ref[0, pl.ds(r0, C), hh * LANES:(hh + 1) * LANES]
            kv_scr[n] = _bdot(jnp.concatenate([kt * dec_kf_row, kt * dec_kb_row], axis=0), vc)
            return 0

        lax.fori_loop(0, N, kv_body, 0, unroll=2)

        def fwd_body(n, st):
            cur = kv_scr[n, 0:LANES, :]
            kv_scr[n, 0:LANES, :] = st
            return chunk_decay * st + cur

        lax.fori_loop(0, N, fwd_body, jnp.zeros((LANES, LANES), F32))

        def bwd_body(m, st):
            n = N - 1 - m
            cur = kv_scr[n, LANES:, :]
            kv_scr[n, LANES:, :] = st
            return chunk_decay * st + cur

        lax.fori_loop(0, N, bwd_body, jnp.zeros((LANES, LANES), F32))

        def out_body(n, _):
            r0 = pl.multiple_of(n * C, C)
            qc = qs_scr[pl.ds(r0, C), :]
            kc = jnp.where(hmask, ks_scr[pl.ds(r0, C), :], 0.0)
            vc = v_ref[0, pl.ds(r0, C), hh * LANES:(hh + 1) * LANES]
            scores = _bdot_nt(qc, kc) * intra
            o = _bdot(scores, vc) + _bdot(jnp.concatenate([qc * dec_qf, qc * dec_qb], axis=1), kv_scr[n])
            mu = jnp.mean(o, axis=-1, keepdims=True)
            var = jnp.mean(o * o, axis=-1, keepdims=True) - mu * mu
            o = (o - mu) * lax.rsqrt(var + EPS)
            gc = g_ref[0, pl.ds(r0, C), hh * LANES:(hh + 1) * LANES]
            o_ref[0, pl.ds(r0, C), hh * LANES:(hh + 1) * LANES] = (_silu(gc) * o).astype(o_ref.dtype)
            return 0

        lax.fori_loop(0, N, out_body, 0, unroll=4)


def _retention(proj, B, S, W, q_col, k_col, v_col, g_col, cosf, sinf, lg_tab):
    n_pairs = RET_HEADS // 2
    N = S // RET_CHUNK
    return pl.pallas_call(
        _ret_kernel,
        out_shape=jax.ShapeDtypeStruct((B, S, W), BF16),
        grid=(B, n_pairs),
        in_specs=[pl.BlockSpec((1, S, LANES), lambda b, p: (b, 0, q_col + p)),
                  pl.BlockSpec((1, S, LANES), lambda b, p: (b, 0, k_col + p)),
                  pl.BlockSpec((1, S, 2 * LANES), lambda b, p: (b, 0, v_col + p)),
                  pl.BlockSpec((1, S, 2 * LANES), lambda b, p: (b, 0, g_col + p)),
                  pl.BlockSpec((1, S, LANES), lambda b, p: (b, 0, 0)),
                  pl.BlockSpec((1, S, LANES), lambda b, p: (b, 0, 0)),
                  pl.BlockSpec((RET_HEADS, LANES), lambda b, p: (0, 0))],
        out_specs=pl.BlockSpec((1, S, 2 * LANES), lambda b, p: (b, 0, p)),
        scratch_shapes=[pltpu.VMEM((S, LANES), F32), pltpu.VMEM((S, LANES), F32),
                        pltpu.VMEM((N, 2 * LANES, LANES), F32)],
        compiler_params=_cparams("parallel", "parallel"),
        name="retention",
    )(proj, proj, proj, proj, cosf, sinf, lg_tab)


def _sgu_kernel(u_ref, v_ref, ng_ref, w_ref, bt_ref, o_ref):
    ts = u_ref.shape[1]
    C = SGU_CHUNK
    v = _gelu(v_ref[0])
    mu = jnp.mean(v, axis=-1, keepdims=True)
    var = jnp.mean(jnp.square(v - mu), axis=-1, keepdims=True)
    v = ((v - mu) * lax.rsqrt(var + EPS) * ng_ref[...]).astype(BF16)
    for c in range(ts // C):
        for g in range(SGU_GROUPS):
            vc = v[c * C:(c + 1) * C, g * LANES:(g + 1) * LANES]
            mixed = jnp.dot(w_ref[g].astype(BF16), vc, preferred_element_type=F32) + bt_ref[:, g:g + 1]
            uc = _gelu(u_ref[0, c * C:(c + 1) * C, g * LANES:(g + 1) * LANES])
            o_ref[0, c * C:(c + 1) * C, g * LANES:(g + 1) * LANES] = (uc * mixed).astype(o_ref.dtype)


def _sgu(proj, B, S, W, u_col, v_col, norm_g, w_s, b_s):
    ts = min(512, S)
    return pl.pallas_call(
        _sgu_kernel,
        out_shape=jax.ShapeDtypeStruct((B, S, W), BF16),
        grid=(B, S // ts),
        in_specs=[pl.BlockSpec((1, ts, W), lambda b, i: (b, i, u_col)),
                  pl.BlockSpec((1, ts, W), lambda b, i: (b, i, v_col)),
                  pl.BlockSpec((1, W), lambda b, i: (0, 0)),
                  pl.BlockSpec((SGU_GROUPS, SGU_CHUNK, SGU_CHUNK), lambda b, i: (0, 0, 0)),
                  pl.BlockSpec((SGU_CHUNK, SGU_GROUPS), lambda b, i: (0, 0))],
        out_specs=pl.BlockSpec((1, ts, W), lambda b, i: (b, i, 0)),
        compiler_params=_cparams("parallel", "parallel"),
        name="spatial_gating",
    )(proj, proj, norm_g.reshape(1, W), w_s, b_s.T)


def _head_sum(x, hd):
    seg = (_iota((LANES, LANES), 0) // hd == _iota((LANES, LANES), 1) // hd).astype(BF16)
    hi = x.astype(BF16)
    lo = (x - hi.astype(F32)).astype(BF16)
    parts = [jnp.dot(hi[:, j * LANES:(j + 1) * LANES], seg, preferred_element_type=F32)
             + jnp.dot(lo[:, j * LANES:(j + 1) * LANES], seg, preferred_element_type=F32)
             for j in range(x.shape[-1] // LANES)]
    return jnp.concatenate(parts, axis=1)


def _rwkv_pre_kernel(f_ref, fp_ref, fn_ref, mup_ref, mun_ref, w0_ref, wup_ref, a0_ref, aup_ref,
                     gup_ref, kk_ref, ka_ref, rk_ref,
                     r_out, kk_out, v_out, g_out, bonus_out, lw_out, bb_out, kt_out):
    ts = f_ref.shape[1]
    W = r_out.shape[2]
    i = pl.program_id(1)
    n_i = pl.num_programs(1)
    f = f_ref[0]
    rows = _iota(f.shape, 0)
    prev_row = jnp.where(i > 0, fp_ref[0, SUBLANES - 1:SUBLANES, :], 0.0)
    next_row = jnp.where(i < n_i - 1, fn_ref[0, 0:1, :], 0.0)
    prev = jnp.where(rows >= 1, pltpu.roll(f, 1, 0), prev_row)
    nxt = jnp.where(rows < ts - 1, pltpu.roll(f, ts - 1, 0), next_row)
    f = f + mup_ref[...] * (prev - f) + mun_ref[...] * (nxt - f)

    r = f[:, 0:W]
    k = f[:, W:2 * W]
    v = f[:, 2 * W:3 * W]
    o = 3 * W
    wd = jnp.tanh(f[:, o:o + 2 * DECAY_LORA])
    ad = f[:, o + 2 * DECAY_LORA:o + 2 * DECAY_LORA + 2 * ICLR_LORA]
    gd = f[:, o + 2 * DECAY_LORA + 2 * ICLR_LORA:]

    g_out[0] = _bdot(_sigmoid(gd), gup_ref[...])
    kk = k * kk_ref[...]
    kk = kk * lax.rsqrt(_head_sum(kk * kk, RWKV_HEAD_DIM) + EPS)
    r_out[0] = r.astype(r_out.dtype)
    kk_out[0] = kk.astype(kk_out.dtype)
    v_out[0] = v.astype(v_out.dtype)
    bonus = jnp.zeros_like(r)
    for z in range(2):
        w_raw = _bdot(wd[:, z * DECAY_LORA:(z + 1) * DECAY_LORA], wup_ref[z]) + w0_ref[z:z + 1, :]
        lw_out[z, 0] = -DECAY_SCALE * _sigmoid(w_raw)
        a = _sigmoid(_bdot(ad[:, z * ICLR_LORA:(z + 1) * ICLR_LORA], aup_ref[z]) + a0_ref[z:z + 1, :])
        bb_out[z, 0] = (a * kk).astype(bb_out.dtype)
        kt = k * (1.0 + (a - 1.0) * ka_ref[...])
        kt_out[z, 0] = kt.astype(kt_out.dtype)
        bonus = bonus + _head_sum(r * kt * rk_ref[...], RWKV_HEAD_DIM) * v
    bonus_out[0] = bonus


def _rwkv_pre(feat, mu_prev, mu_next, w0, w_up, a0, a_up, g_up, k_k, k_a, r_k):
    B, S, Fw = feat.shape
    W = w0.shape[1]
    ts = min(256, S)
    hb = ts // SUBLANES
    n_hb = S // SUBLANES
    row = lambda a: a.reshape(1, -1)
    full2 = lambda a: pl.BlockSpec(a.shape, lambda b, i: (0, 0))
    full3 = lambda a: pl.BlockSpec(a.shape, lambda b, i: (0, 0, 0))
    bsw = pl.BlockSpec((1, ts, W), lambda b, i: (b, i, 0))
    zsw = pl.BlockSpec((2, 1, ts, W), lambda b, i: (0, b, i, 0))
    sds = jax.ShapeDtypeStruct((B, S, W), F32)
    zds = jax.ShapeDtypeStruct((2, B, S, W), F32)
    sdh = jax.ShapeDtypeStruct((B, S, W), BF16)
    zdh = jax.ShapeDtypeStruct((2, B, S, W), BF16)
    args = (row(mu_prev), row(mu_next), w0, w_up, a0, a_up, g_up, row(k_k), row(k_a), row(r_k))
    specs = [full2(args[0]), full2(args[1]), full2(w0), full3(w_up), full2(a0), full3(a_up),
             full2(g_up), full2(args[7]), full2(args[8]), full2(args[9])]
    return pl.pallas_call(
        _rwkv_pre_kernel,
        out_shape=(sdh, sdh, sdh, sds, sds, zds, zdh, zdh),
        grid=(B, S // ts),
        in_specs=[pl.BlockSpec((1, ts, Fw), lambda b, i: (b, i, 0)),
                  pl.BlockSpec((1, SUBLANES, Fw), lambda b, i: (b, jnp.maximum(i * hb - 1, 0), 0)),
                  pl.BlockSpec((1, SUBLANES, Fw), lambda b, i: (b, jnp.minimum((i + 1) * hb, n_hb - 1), 0)),
                  ] + specs,
        out_specs=(bsw, bsw, bsw, bsw, bsw, zsw, zsw, zsw),
        compiler_params=_cparams("parallel", "parallel"),
        name="rwkv_prepare",
    )(feat, feat, feat, *args)


def _rwkv_scan_kernel(r_ref, kk_ref, v_ref, lw_ref, bb_ref, kt_ref, y_ref, st_scr):
    C = RWKV_CHUNK
    C2 = 2 * C
    z = pl.program_id(0)
    i = pl.program_id(2)
    n_pairs = r_ref.shape[2] // LANES
    fwd = z == 0

    @pl.when(i == 0)
    def _():
        st_scr[...] = jnp.zeros_like(st_scr)

    ti = _iota((C, C), 0)
    si = _iota((C, C), 1)
    sign = jnp.where(fwd, 1, -1)
    cum_mat = (sign * (ti - si) >= 0).astype(BF16)
    er = _iota((C2, C2), 0)
    ec = _iota((C2, C2), 1)
    lead = sign * (er % C - ec % C)
    strict = lead > 0
    incl = lead >= 0
    eye2 = (er == ec).astype(F32)
    row_head = _iota((C2, LANES), 0) // C
    lane_head = _iota((C2, LANES), 1) // RWKV_HEAD_DIM
    emask = row_head == lane_head
    diag_l = _iota((LANES, LANES), 0) == _iota((LANES, LANES), 1)

    def expand(x):
        return jnp.where(emask, jnp.concatenate([x, x], axis=0), 0.0)

    def bf(x):
        return x.astype(BF16)

    def mm(a, b):
        return jnp.dot(a, b, preferred_element_type=F32)

    pairs = range(n_pairs)
    sl = [slice(j * LANES, (j + 1) * LANES) for j in pairs]
    lw = [lw_ref[0, 0, :, s] for s in sl]
    def cumsum3(x):
        hi = bf(x)
        r1 = x - hi.astype(F32)
        mid = bf(r1)
        lo = bf(r1 - mid.astype(F32))
        return mm(cum_mat, hi) + (mm(cum_mat, mid) + mm(cum_mat, lo))

    cum = [cumsum3(x) for x in lw]
    tot = [jnp.sum(x, axis=0, keepdims=True) for x in lw]
    p_inv = [jnp.exp(-c) for c in cum]
    p_rest = [jnp.exp(t - c) for t, c in zip(tot, cum)]
    kh_e = [bf(expand(kk_ref[0, :, s] * jnp.exp(c - x))) for s, c, x in zip(sl, cum, lw)]
    rh_e = [bf(expand(r_ref[0, :, s] * jnp.exp(c))) for s, c in zip(sl, cum)]
    bh_e = [bf(expand(bb_ref[0, 0, :, s] * p)) for s, p in zip(sl, p_inv)]
    ktil_e = [bf(expand(kt_ref[0, 0, :, s] * p)) for s, p in zip(sl, p_inv)]
    b_rest_t = [bf(expand(bb_ref[0, 0, :, s] * p).T) for s, p in zip(sl, p_rest)]
    k_rest_t = [bf(expand(kt_ref[0, 0, :, s] * p).T) for s, p in zip(sl, p_rest)]
    v_e = [bf(expand(v_ref[0, :, s])) for s in sl]

    a_all = [lax.dot_general(jnp.concatenate([k, r], axis=0), jnp.concatenate([b, t], axis=0),
                             (((1,), (1,)), ((), ())), preferred_element_type=F32)
             for k, r, b, t in zip(kh_e, rh_e, bh_e, ktil_e)]
    a_b = [jnp.where(strict, a[0:C2, 0:C2], 0.0) for a in a_all]
    a_k = [jnp.where(strict, a[0:C2, C2:], 0.0) for a in a_all]
    a_rb = [jnp.where(incl, a[C2:, 0:C2], 0.0) for a in a_all]
    a_rk = [jnp.where(incl, a[C2:, C2:], 0.0) for a in a_all]

    pw = [bf(-a) for a in a_b]
    t_inv = [eye2] * n_pairs
    n_dbl = int(math.log2(C))
    for k in range(n_dbl):
        if k < n_dbl - 1:
            both = [mm(p, jnp.concatenate([p, bf(t)], axis=1)) for p, t in zip(pw, t_inv)]
            pw = [bf(x[:, 0:C2]) for x in both]
            t_inv = [t + x[:, C2:] for t, x in zip(t_inv, both)]
        else:
            t_inv = [t + mm(p, bf(t)) for p, t in zip(pw, t_inv)]

    akv = [mm(bf(jnp.concatenate([ak, ark], axis=0)), v)
           for ak, ark, v in zip(a_k, a_rk, v_e)]
    tw = [mm(bf(t), jnp.concatenate([k, bf(x[0:C2])], axis=1))
          for t, k, x in zip(t_inv, kh_e, akv)]
    st = [st_scr[j] for j in pairs]
    ws = [mm(jnp.concatenate([bf(x[:, 0:LANES]), r], axis=0), bf(s))
          for x, r, s in zip(tw, rh_e, st)]
    u_e = [x[0:C2] + t[:, LANES:] for x, t in zip(ws, tw)]
    y_e = [x[C2:] + k[C2:] - mm(bf(a), bf(u)) for x, k, a, u in zip(ws, akv, a_rb, u_e)]
    for s, y in zip(sl, y_e):
        y_ref[0, 0, :, s] = y[0:C, :] + y[C:C2, :]

    bw = [mm(b, bf(t)) for b, t in zip(b_rest_t, tw)]
    kv = [mm(k, v) for k, v in zip(k_rest_t, v_e)]
    for j in pairs:
        m_t = jnp.where(diag_l, jnp.exp(tot[j]), 0.0) - bw[j][:, 0:LANES]
        g_t = kv[j] - bw[j][:, LANES:]
        st_scr[j] = mm(bf(m_t), bf(st[j])) + g_t


def _rwkv_scan(r, kk, v, lw, bb, kt):
    B, S, W = r.shape
    C = RWKV_CHUNK
    nc = S // C
    chunk = lambda z, i: i + z * (nc - 1 - 2 * i)
    shared = pl.BlockSpec((1, C, W), lambda z, b, i: (b, chunk(z, i), 0))
    perdir = pl.BlockSpec((1, 1, C, W), lambda z, b, i: (z, b, chunk(z, i), 0))
    return pl.pallas_call(
        _rwkv_scan_kernel,
        out_shape=jax.ShapeDtypeStruct((2, B, S, W), F32),
        grid=(2, B, nc),
        in_specs=[shared, shared, shared, perdir, perdir, perdir],
        out_specs=perdir,
        scratch_shapes=[pltpu.VMEM((W // LANES, LANES, LANES), F32)],
        compiler_params=_cparams("parallel", "parallel", "arbitrary"),
        name="rwkv_scan",
    )(r, kk, v, lw, bb, kt)


def _rwkv_post_kernel(y_ref, bonus_ref, g_ref, o_ref):
    hd = RWKV_HEAD_DIM
    acc = bonus_ref[0]
    for z in range(2):
        y = y_ref[z, 0]
        d = y - _head_sum(y, hd) * (1.0 / hd)
        var = _head_sum(d * d, hd) * (1.0 / hd)
        acc = acc + d * lax.rsqrt(var + EPS)
    o_ref[0] = (acc * g_ref[0]).astype(o_ref.dtype)


def _rwkv_post(y, bonus, g):
    _, B, S, W = y.shape
    ts = min(256, S)
    bsw = pl.BlockSpec((1, ts, W), lambda b, i: (b, i, 0))
    return pl.pallas_call(
        _rwkv_post_kernel,
        out_shape=jax.ShapeDtypeStruct((B, S, W), BF16),
        grid=(B, S // ts),
        in_specs=[pl.BlockSpec((2, 1, ts, W), lambda b, i: (0, b, i, 0)), bsw, bsw],
        out_specs=bsw,
        compiler_params=_cparams("parallel", "parallel"),
        name="rwkv_finish",
    )(y, bonus, g)


def _merge_kernel(ya_ref, yb_ref, yc_ref, yd_ref, wb_ref, l0_ref, l1_ref, l2_ref, l3_ref, o_ref,
                  wb_bf):
    @pl.when(pl.program_id(1) == 0)
    def _():
        wb_bf[...] = wb_ref[0].astype(BF16)

    acc = None
    for n, (y_ref, l_ref) in enumerate(((ya_ref, l0_ref), (yb_ref, l1_ref), (yc_ref, l2_ref),
                                        (yd_ref, l3_ref))):
        br = jnp.dot(y_ref[...], wb_bf[n], preferred_element_type=F32)
        t = _sigmoid(l_ref[...]) * br
        acc = t if acc is None else acc + t
    o_ref[...] = acc.astype(o_ref.dtype)


def _merge(ys, wb, layer, logits, tm, tn):
    T, W = ys[0].shape
    D = wb.shape[3]
    nj = D // tn
    yspec = pl.BlockSpec((tm, W), lambda j, i: (i, 0))
    lspec = lambda n: pl.BlockSpec((tm, tn), lambda j, i: (i, n * nj + j))
    return pl.pallas_call(
        _merge_kernel,
        out_shape=jax.ShapeDtypeStruct((T, D), BF16),
        grid=(nj, T // tm),
        in_specs=[yspec, yspec, yspec, yspec,
                  pl.BlockSpec((1, N_BRANCH, W, tn), lambda j, i: (layer, 0, 0, j)),
                  lspec(0), lspec(1), lspec(2), lspec(3)],
        out_specs=pl.BlockSpec((tm, tn), lambda j, i: (i, j)),
        scratch_shapes=[pltpu.VMEM((N_BRANCH, W, tn), BF16)],
        compiler_params=_cparams("parallel", "arbitrary"),
        name="branch_merge",
    )(*ys, wb, logits, logits, logits, logits)


def _router_kernel(h_ref, w_ref, b_ref, o_ref):
    logits = jnp.dot(_unpack_bf16_pairs(h_ref[...]), w_ref[...], preferred_element_type=F32) + b_ref[...]
    lane = _iota(logits.shape, 1)
    lane_f = lane.astype(F32)
    neg = -3.0e38
    far = float(LANES)
    is_grp = lane < N_GROUPS
    gl = jnp.where(is_grp, logits, neg)
    gmax = jnp.max(gl, axis=-1, keepdims=True)
    gsum = jnp.sum(jnp.where(is_grp, jnp.exp(gl - gmax), 0.0), axis=-1, keepdims=True)
    grp_p = 1.0 / gsum
    grp_i = jnp.min(jnp.where(gl == gmax, lane_f, far), axis=-1, keepdims=True)
    lo = N_GROUPS + grp_i * EXPERTS_PER_GROUP
    in_grp = jnp.logical_and(lane_f >= lo, lane_f < lo + EXPERTS_PER_GROUP)
    el = jnp.where(in_grp, logits, neg)
    m1 = jnp.max(el, axis=-1, keepdims=True)
    i1 = jnp.min(jnp.where(el == m1, lane_f, far), axis=-1, keepdims=True)
    el2 = jnp.where(lane_f == i1, neg, el)
    m2 = jnp.max(el2, axis=-1, keepdims=True)
    i2 = jnp.min(jnp.where(el2 == m2, lane_f, far), axis=-1, keepdims=True)
    t = jnp.exp(m2 - m1)
    w1 = grp_p / (1.0 + t)
    w2 = grp_p * t / (1.0 + t)
    o_ref[...] = jnp.where(lane == 0, w1, jnp.where(lane == 1, w2, jnp.where(
        lane == 2, i1 - N_GROUPS, jnp.where(lane == 3, i2 - N_GROUPS, 0.0))))


def _router(h, w_r, b_r):
    T = h.shape[0]
    D = w_r.shape[0]
    tm = min(1024, T)
    return pl.pallas_call(
        _router_kernel,
        out_shape=jax.ShapeDtypeStruct((T, LANES), F32),
        grid=(T // tm,),
        in_specs=[pl.BlockSpec((tm, D // 2), lambda i: (i, 0)),
                  pl.BlockSpec((D, LANES), lambda i: (0, 0)),
                  pl.BlockSpec((1, LANES), lambda i: (0, 0))],
        out_specs=pl.BlockSpec((tm, LANES), lambda i: (i, 0)),
        compiler_params=_cparams("parallel"),
        name="router",
    )(h, w_r, b_r)


def _expert_kernel(be_ref, nb_ref, xs_ref, wgu_ref, wd_ref, o_ref, wgu_bf, wd_bf):
    i = pl.program_id(0)
    used = i < nb_ref[0]
    changed = jnp.logical_or(i == 0, be_ref[i] != be_ref[jnp.maximum(i - 1, 0)])

    @pl.when(jnp.logical_and(used, changed))
    def _():
        wgu_bf[...] = wgu_ref[0, 0].astype(BF16)
        wd_bf[...] = wd_ref[0, 0].astype(BF16)

    @pl.when(used)
    def _():
        gu = jnp.dot(_unpack_bf16_pairs(xs_ref[...]), wgu_bf[...], preferred_element_type=F32)
        eh = gu.shape[1] // 2
        mid = (_silu(gu[:, :eh]) * gu[:, eh:]).astype(BF16)
        o_ref[...] = jnp.dot(mid, wd_bf[...], preferred_element_type=F32)

    @pl.when(jnp.logical_not(used))
    def _():
        o_ref[...] = jnp.zeros_like(o_ref)


def _experts(xs, block_expert, n_used, w_gu, w_down, layer):
    cap = xs.shape[0]
    D, H2 = w_gu.shape[2], w_gu.shape[3]
    n_blocks = cap // MOE_BLOCK
    blk = lambda i, nb: jnp.minimum(i, nb[0] - 1)
    grid_spec = pltpu.PrefetchScalarGridSpec(
        num_scalar_prefetch=2,
        grid=(n_blocks,),
        in_specs=[pl.BlockSpec((MOE_BLOCK, D // 2), lambda i, be, nb: (blk(i, nb), 0)),
                  pl.BlockSpec((1, 1, D, H2), lambda i, be, nb: (layer, be[blk(i, nb)], 0, 0)),
                  pl.BlockSpec((1, 1, H2 // 2, D), lambda i, be, nb: (layer, be[blk(i, nb)], 0, 0))],
        out_specs=pl.BlockSpec((MOE_BLOCK, D), lambda i, be, nb: (i, 0)),
        scratch_shapes=[pltpu.VMEM((D, H2), BF16), pltpu.VMEM((H2 // 2, D), BF16)],
    )
    return pl.pallas_call(
        _expert_kernel,
        out_shape=jax.ShapeDtypeStruct((cap, D), F32),
        grid_spec=grid_spec,
        compiler_params=_cparams("arbitrary"),
        name="experts",
    )(block_expert, n_used, xs, w_gu, w_down)


def _row_copy(src_ref, s, dst_ref, d, sem):
    return pltpu.make_async_copy(src_ref.at[pl.ds(s, 1), :], dst_ref.at[pl.ds(d, 1), :], sem)


def _rows_wait(src_ref, dst_ref, n, sem):
    pltpu.make_async_copy(src_ref.at[pl.ds(0, n), :], dst_ref.at[pl.ds(0, n), :], sem).wait()


def _combine_kernel(dest_ref, ys_ref, x_ref, r_ref, g_ref, ng_ref, o_ref, y0_buf, y1_buf, sem, *,
                    final_norm):
    tm = x_ref.shape[0]
    base = pl.program_id(0) * tm

    def issue(k, c):
        t = base + k
        _row_copy(ys_ref, dest_ref[TOP_K * t], y0_buf, k, sem.at[0]).start()
        _row_copy(ys_ref, dest_ref[TOP_K * t + 1], y1_buf, k, sem.at[1]).start()
        return c

    lax.fori_loop(0, tm, issue, 0)
    _rows_wait(ys_ref, y0_buf, tm, sem.at[0])
    _rows_wait(ys_ref, y1_buf, tm, sem.at[1])
    w0 = r_ref[:, 0:1]
    w1 = r_ref[:, 1:2]
    x = x_ref[...] + g_ref[0] * (w0 * y0_buf[...] + w1 * y1_buf[...])
    if final_norm:
        x = x * lax.rsqrt(jnp.mean(x * x, axis=-1, keepdims=True) + EPS) * ng_ref[...]
    o_ref[...] = x


def _combine(x2d, ys, dest, route, gate, rows_per_batch, norm_g, final_norm):
    T, D = x2d.shape
    tm = min(256, T)
    tiles_per_batch = rows_per_batch // tm
    spec = pl.BlockSpec((tm, D), lambda i, d: (i, 0))
    grid_spec = pltpu.PrefetchScalarGridSpec(
        num_scalar_prefetch=1,
        grid=(T // tm,),
        in_specs=[pl.BlockSpec(memory_space=pl.ANY), spec,
                  pl.BlockSpec((tm, LANES), lambda i, d: (i, 0)),
                  pl.BlockSpec((1, 1, D), lambda i, d: (i // tiles_per_batch, 0, 0)),
                  pl.BlockSpec((1, D), lambda i, d: (0, 0))],
        out_specs=spec,
        scratch_shapes=[pltpu.VMEM((tm, D), F32), pltpu.VMEM((tm, D), F32),
                        pltpu.SemaphoreType.DMA((2,))],
    )
    return pl.pallas_call(
        functools.partial(_combine_kernel, final_norm=final_norm),
        out_shape=jax.ShapeDtypeStruct((T, D), F32),
        grid_spec=grid_spec,
        compiler_params=_cparams("arbitrary"),
        name="moe_combine",
    )(dest, ys, x2d, route, gate, norm_g.reshape(1, D))


def _dispatch_kernel(dest_ref, h_ref, xs_in_ref, xs_ref, sem, *, tm):
    del xs_in_ref
    base = pl.program_id(0) * tm

    def issue(k, c):
        t = base + k
        for kk in range(TOP_K):
            _row_copy(h_ref, k, xs_ref, dest_ref[TOP_K * t + kk], sem.at[kk]).start()
        return c

    lax.fori_loop(0, tm, issue, 0)
    for kk in range(TOP_K):
        _rows_wait(h_ref, xs_ref, tm, sem.at[kk])


def _dispatch(h, dest, cap):
    T, Dp = h.shape
    tm = min(512, T)
    grid_spec = pltpu.PrefetchScalarGridSpec(
        num_scalar_prefetch=1,
        grid=(T // tm,),
        in_specs=[pl.BlockSpec((tm, Dp), lambda i, d: (i, 0)), pl.BlockSpec(memory_space=pl.ANY)],
        out_specs=pl.BlockSpec(memory_space=pl.ANY),
        scratch_shapes=[pltpu.SemaphoreType.DMA((TOP_K,))],
    )
    return pl.pallas_call(
        functools.partial(_dispatch_kernel, tm=tm),
        out_shape=jax.ShapeDtypeStruct((cap, Dp), h.dtype),
        grid_spec=grid_spec,
        input_output_aliases={2: 0},
        compiler_params=_cparams("arbitrary"),
        name="moe_dispatch",
    )(dest, h, jnp.zeros((cap, Dp), h.dtype))


def _rank_kernel(route_ref, rank_ref, cnt_ref, carry):
    tm = route_ref.shape[0]
    i = pl.program_id(0)

    @pl.when(i == 0)
    def _():
        carry[...] = jnp.zeros_like(carry)

    r = route_ref[...]
    lane = _iota(r.shape, 1)
    lane_f = lane.astype(F32)
    oh0 = (lane_f == r[:, 2:3]).astype(F32)
    oh1 = (lane_f == r[:, 3:4]).astype(F32)
    both = oh0 + oh1
    earlier = (_iota((tm, tm), 0) > _iota((tm, tm), 1)).astype(BF16)
    prefix = jnp.dot(earlier, both.astype(BF16), preferred_element_type=F32) + carry[0:1, :]
    rank0 = jnp.sum(prefix * oh0, axis=-1, keepdims=True)
    rank1 = jnp.sum((prefix + oh0) * oh1, axis=-1, keepdims=True)
    rank_ref[...] = jnp.where(lane == 0, rank0, jnp.where(lane == 1, rank1, 0.0))
    carry[...] = carry[...] + jnp.sum(both, axis=0, keepdims=True)
    cnt_ref[...] = carry[...]


def _dest_kernel(route_ref, rank_ref, cnt_ref, dest_ref):
    r = route_ref[...]
    lane = _iota(r.shape, 1)
    lane_f = lane.astype(F32)
    padded = jnp.floor((cnt_ref[...] + (MOE_BLOCK - 1.0)) * (1.0 / MOE_BLOCK)) * MOE_BLOCK
    before = (_iota((LANES, LANES), 0) < _iota((LANES, LANES), 1)).astype(F32)
    pad_start = jnp.dot(padded, before, precision=HI, preferred_element_type=F32)[0:1, :]
    oh0 = lane_f == r[:, 2:3]
    oh1 = lane_f == r[:, 3:4]
    d0 = jnp.sum(jnp.where(oh0, pad_start, 0.0), axis=-1, keepdims=True) + rank_ref[:, 0:1]
    d1 = jnp.sum(jnp.where(oh1, pad_start, 0.0), axis=-1, keepdims=True) + rank_ref[:, 1:2]
    dest_ref[...] = jnp.where(lane == 0, d0, jnp.where(lane == 1, d1, 0.0))


def _dispatch_plan(route):
    T = route.shape[0]
    tm = min(512, T)
    slab = pl.BlockSpec((tm, LANES), lambda i: (i, 0))
    cnt_spec = pl.BlockSpec((SUBLANES, LANES), lambda i: (0, 0))
    rank, cnt = pl.pallas_call(
        _rank_kernel,
        out_shape=(jax.ShapeDtypeStruct((T, LANES), F32), jax.ShapeDtypeStruct((SUBLANES, LANES), F32)),
        grid=(T // tm,),
        in_specs=[slab],
        out_specs=(slab, cnt_spec),
        scratch_shapes=[pltpu.VMEM((SUBLANES, LANES), F32)],
        compiler_params=_cparams("arbitrary"),
        name="moe_rank",
    )(route)
    dest = pl.pallas_call(
        _dest_kernel,
        out_shape=jax.ShapeDtypeStruct((T, LANES), F32),
        grid=(T // tm,),
        in_specs=[slab, slab, cnt_spec],
        out_specs=slab,
        compiler_params=_cparams("parallel"),
        name="moe_dest",
    )(route, rank, cnt)
    dest = dest[:, :TOP_K].astype(jnp.int32).reshape(T * TOP_K)
    counts = cnt[0, :N_EXPERTS].astype(jnp.int32)
    padded = (counts + MOE_BLOCK - 1) // MOE_BLOCK * MOE_BLOCK
    pad_end = jnp.cumsum(padded)
    n_blocks = -(-(T * TOP_K) // MOE_BLOCK) + N_EXPERTS
    block_row0 = jnp.arange(n_blocks, dtype=jnp.int32) * MOE_BLOCK
    block_expert = jnp.minimum(jnp.searchsorted(pad_end, block_row0, side="right"),
                               N_EXPERTS - 1).astype(jnp.int32)
    n_used = (pad_end[-1] // MOE_BLOCK).astype(jnp.int32).reshape(1)
    return dest, block_expert, n_used, n_blocks * MOE_BLOCK


def kernel(x, c, positions, norm1_g, norm2_g, ada_w, ada_b, w_in, lru_conv_w, lru_conv_b, lru_w_r, lru_b_r, lru_w_i, lru_b_i, lru_lambda, sgu_norm_g, sgu_w, sgu_b, rwkv_mu_prev, rwkv_mu_next, rwkv_w0, rwkv_w_up, rwkv_a0, rwkv_a_up, rwkv_g_up, rwkv_k_k, rwkv_k_a, rwkv_r_k, w_branch, w_out, router_grp_w, router_grp_b, router_exp_w, router_exp_b, expert_w_gu, expert_w_down, final_norm_g):
    B, S, D = x.shape
    L = ada_w.shape[0]
    W = D // 2
    T = B * S
    feat_w = rwkv_mu_prev.shape[1]

    c_pad = jnp.zeros((SUBLANES, D), F32).at[:B].set(c)
    mod = _ada(c_pad, ada_w, ada_b)

    dk = D // 2 // (2 * RET_HEADS)
    inv_freq = ROPE_THETA ** (-jnp.arange(0, dk, 2, dtype=F32) / dk)
    ang = positions.astype(F32)[..., None] * inv_freq
    cosf, sinf = _rope_tables(jnp.tile(ang, (1, 1, LANES // (dk // 2))))
    log_gamma = np.log1p(-np.exp2(-5.0 - np.arange(RET_HEADS, dtype=np.float64)))
    lg_tab = jnp.asarray(np.broadcast_to(log_gamma[:, None], (RET_HEADS, LANES)), F32)

    x2d = x.reshape(T, D)
    for l in range(L):
        m = mod[l, :B]
        sh1, sc1, g1, sh2, sc2, g2 = [m[:, None, k * D:(k + 1) * D] for k in range(6)]

        h = _normmod(x2d.reshape(B, S, D), norm1_g[l], sc1, sh1, BF16).reshape(T, D)
        n_a = 7 * W
        proj_a = _mm(h, w_in, l, 0, n_a, 2048, 512, name="proj_a").reshape(B, S, n_a)
        feat = _mm(h, w_in, l, n_a, feat_w, 1024, feat_w // 3, name="proj_rwkv").reshape(B, S, feat_w)
        logits = _mm(h, w_in, l, n_a + feat_w, N_BRANCH * D, 2048, 512, name="proj_gate")

        y_a = _lru(proj_a, B, S, W, 0, W // LANES, lru_conv_w[l], lru_conv_b[l], lru_w_r[l],
                   lru_b_r[l], lru_w_i[l], lru_b_i[l], lru_lambda[l])
        y_b = _retention(proj_a, B, S, W, 2 * W // LANES, 5 * W // (2 * LANES),
                         3 * W // (2 * LANES), 4 * W // (2 * LANES), cosf, sinf, lg_tab)
        y_c = _sgu(proj_a, B, S, W, 5, 6, sgu_norm_g[l], sgu_w[l], sgu_b[l])
        r, kk, v, g, bonus, lw, bb, kt = _rwkv_pre(
            feat, rwkv_mu_prev[l], rwkv_mu_next[l], rwkv_w0[l], rwkv_w_up[l], rwkv_a0[l],
            rwkv_a_up[l], rwkv_g_up[l], rwkv_k_k[l], rwkv_k_a[l], rwkv_r_k[l].reshape(-1))
        y_d = _rwkv_post(_rwkv_scan(r, kk, v, lw, bb, kt), bonus, g)

        merged = _merge([y.reshape(T, W) for y in (y_a, y_b, y_c, y_d)],
                        w_branch, l, logits, 512, 512)
        x2d = _mm_residual(merged, w_out, l, x2d, g1, S, 2048, 512)

        h2 = _normmod(x2d.reshape(B, S, D), norm2_g[l], sc2, sh2, jnp.uint32).reshape(T, D // 2)
        n_r = N_GROUPS + N_EXPERTS
        w_r = jnp.zeros((D, LANES), F32).at[:, :N_GROUPS].set(router_grp_w[l]).at[:, N_GROUPS:n_r].set(
            router_exp_w[l]).astype(BF16)
        b_r = jnp.zeros((1, LANES), F32).at[0, :N_GROUPS].set(router_grp_b[l]).at[0, N_GROUPS:n_r].set(
            router_exp_b[l])
        route = _router(h2, w_r, b_r)
        dest, block_expert, n_used, cap = _dispatch_plan(route)
        ys = _experts(_dispatch(h2, dest, cap), block_expert, n_used, expert_w_gu, expert_w_down, l)
        x2d = _combine(x2d, ys, dest, route, g2, S, final_norm_g, l == L - 1)

    return x2d.reshape(B, S, D)
```

```python
import functools
import math

import jax
import jax.numpy as jnp
import numpy as np
from jax import lax
from jax.experimental import pallas as pl
from jax.experimental.pallas import tpu as pltpu

F32 = jnp.float32
BF16 = jnp.bfloat16
EPS = 1e-6

LANES = 128
SUBLANES = 8
VMEM_LIMIT_BYTES = 56 * 1024 * 1024

LRU_BLOCKS = 8
CONV_WIDTH = 4
LRU_C = 8.0
RET_HEADS = 8
RET_CHUNK = 128
ROPE_THETA = 10000.0
SGU_GROUPS = 8
SGU_CHUNK = 128
RWKV_HEAD_DIM = 64
DECAY_LORA = 64
ICLR_LORA = 64
GATE_LORA = 128
DECAY_SCALE = math.exp(-0.5)
N_GROUPS = 4
EXPERTS_PER_GROUP = 8
N_EXPERTS = N_GROUPS * EXPERTS_PER_GROUP
TOP_K = 2
MOE_BLOCK = 256
N_BRANCH = 4

RWKV_CHUNK = 64
HI = lax.Precision.HIGHEST


def _cparams(*sem):
    return pltpu.CompilerParams(dimension_semantics=sem, vmem_limit_bytes=VMEM_LIMIT_BYTES)


def _bdot(a, b):
    return jnp.dot(a.astype(BF16), b.astype(BF16), preferred_element_type=F32)


def _bdot_nt(a, b):
    return lax.dot_general(a.astype(BF16), b.astype(BF16), (((1,), (1,)), ((), ())),
                           preferred_element_type=F32)


def _gelu(x):
    return 0.5 * x * (1.0 + jnp.tanh(math.sqrt(2.0 / math.pi) * (x + 0.044715 * (x * x * x))))


def _sigmoid(x):
    return 1.0 / (1.0 + jnp.exp(-x))


def _silu(x):
    return x * _sigmoid(x)


def _iota(shape, dim):
    return lax.broadcasted_iota(jnp.int32, shape, dim)


def _ada_kernel(c_ref, w_ref, b_ref, o_ref):
    cond = _silu(c_ref[...])
    o_ref[0] = _bdot(cond, w_ref[0]) + b_ref[0]


def _ada(c_pad, ada_w, ada_b):
    L, D, N = ada_w.shape
    R = c_pad.shape[0]
    tn = 1024
    return pl.pallas_call(
        _ada_kernel,
        out_shape=jax.ShapeDtypeStruct((L, R, N), F32),
        grid=(L, N // tn),
        in_specs=[pl.BlockSpec((R, D), lambda l, j: (0, 0)),
                  pl.BlockSpec((1, D, tn), lambda l, j: (l, 0, j)),
                  pl.BlockSpec((1, 1, tn), lambda l, j: (l, 0, j))],
        out_specs=pl.BlockSpec((1, R, tn), lambda l, j: (l, 0, j)),
        compiler_params=_cparams("parallel", "parallel"),
        name="ada_mod",
    )(c_pad, ada_w, ada_b.reshape(L, 1, N))


def _pack_bf16_pairs(y):
    half = y.shape[1] // 2
    bits = pltpu.bitcast(y.astype(BF16).astype(F32), jnp.uint32)
    return (bits[:, :half] >> 16) | (bits[:, half:] & jnp.uint32(0xFFFF0000))


def _unpack_bf16_pairs(u):
    lo = pltpu.bitcast(u << 16, F32)
    hi = pltpu.bitcast(u & jnp.uint32(0xFFFF0000), F32)
    return jnp.concatenate([lo, hi], axis=1).astype(BF16)


def _normmod_kernel(x_ref, g_ref, sc_ref, sh_ref, o_ref):
    x = x_ref[0]
    y = x * lax.rsqrt(jnp.mean(x * x, axis=-1, keepdims=True) + EPS) * g_ref[...]
    y = y * (1.0 + sc_ref[0]) + sh_ref[0]
    if o_ref.dtype == jnp.uint32:
        o_ref[0] = _pack_bf16_pairs(y)
    else:
        o_ref[0] = y.astype(o_ref.dtype)


def _normmod(x, g, sc, sh, out_dtype):
    B, S, D = x.shape
    ts = min(512, S)
    d_out = D // 2 if out_dtype == jnp.uint32 else D
    return pl.pallas_call(
        _normmod_kernel,
        out_shape=jax.ShapeDtypeStruct((B, S, d_out), out_dtype),
        grid=(B, S // ts),
        in_specs=[pl.BlockSpec((1, ts, D), lambda b, i: (b, i, 0)),
                  pl.BlockSpec((1, D), lambda b, i: (0, 0)),
                  pl.BlockSpec((1, 1, D), lambda b, i: (b, 0, 0)),
                  pl.BlockSpec((1, 1, D), lambda b, i: (b, 0, 0))],
        out_specs=pl.BlockSpec((1, ts, d_out), lambda b, i: (b, i, 0)),
        compiler_params=_cparams("parallel", "parallel"),
        name="norm_mod",
    )(x, g.reshape(1, D), sc, sh)


def _mm_kernel(a_ref, w_ref, o_ref):
    o_ref[...] = jnp.dot(a_ref[...], w_ref[0].astype(BF16),
                         preferred_element_type=F32).astype(o_ref.dtype)


def _mm(a, w, layer, col0, n_cols, tm, tn, out_dtype=F32, name="proj"):
    M, K = a.shape
    return pl.pallas_call(
        _mm_kernel,
        out_shape=jax.ShapeDtypeStruct((M, n_cols), out_dtype),
        grid=(M // tm, n_cols // tn),
        in_specs=[pl.BlockSpec((tm, K), lambda i, j: (i, 0)),
                  pl.BlockSpec((pl.Element(1), pl.Element(K), pl.Element(tn)),
                               lambda i, j: (layer, 0, pl.multiple_of(col0 + j * tn, LANES)))],
        out_specs=pl.BlockSpec((tm, tn), lambda i, j: (i, j)),
        compiler_params=_cparams("parallel", "parallel"),
        name=name,
    )(a, w)


def _mm_res_kernel(a_ref, w_ref, x_ref, g_ref, o_ref):
    y = jnp.dot(a_ref[...], w_ref[0].astype(BF16), preferred_element_type=F32)
    o_ref[...] = x_ref[...] + g_ref[0] * y


def _mm_residual(a, w, layer, x2d, gate, rows_per_batch, tm, tn, name="out_proj"):
    M, K = a.shape
    N = w.shape[2]
    tiles_per_batch = rows_per_batch // tm
    return pl.pallas_call(
        _mm_res_kernel,
        out_shape=jax.ShapeDtypeStruct((M, N), F32),
        grid=(M // tm, N // tn),
        in_specs=[pl.BlockSpec((tm, K), lambda i, j: (i, 0)),
                  pl.BlockSpec((1, K, tn), lambda i, j: (layer, 0, j)),
                  pl.BlockSpec((tm, tn), lambda i, j: (i, j)),
                  pl.BlockSpec((1, 1, tn), lambda i, j: (i // tiles_per_batch, 0, j))],
        out_specs=pl.BlockSpec((tm, tn), lambda i, j: (i, j)),
        compiler_params=_cparams("parallel", "parallel"),
        name=name,
    )(a, w, x2d, gate)


def _softplus(x):
    return jnp.maximum(x, 0.0) + jnp.log1p(jnp.exp(-jnp.abs(x)))


def _lru_kernel(x_ref, gate_ref, cw_ref, cb_ref, wr_ref, br_ref, wi_ref, bi_ref, lam_ref,
                o_ref, a_scr, u_scr, h_scr):
    S = x_ref.shape[1]
    x = x_ref[0]
    rows = _iota(x.shape, 0)
    xm2 = jnp.where(rows >= 2, pltpu.roll(x, 2, 0), 0.0)
    xm1 = jnp.where(rows >= 1, pltpu.roll(x, 1, 0), 0.0)
    xp1 = jnp.where(rows < S - 1, pltpu.roll(x, S - 1, 0), 0.0)
    xc = (cw_ref[0:1, :] * xm2 + cw_ref[1:2, :] * xm1 + cw_ref[2:3, :] * x
          + cw_ref[3:4, :] * xp1 + cb_ref[...])
    for z in range(2):
        r = _sigmoid(_bdot(xc, wr_ref[z, 0]) + br_ref[z:z + 1, :])
        i = _sigmoid(_bdot(xc, wi_ref[z, 0]) + bi_ref[z:z + 1, :])
        log_a = -LRU_C * r * _softplus(-lam_ref[z:z + 1, :])
        a = jnp.exp(log_a)
        a_scr[z] = a
        u_scr[z] = jnp.sqrt(-jnp.tanh(log_a) * (a * a + 1.0)) * i * xc

    n_tiles = S // SUBLANES
    L = x.shape[1]
    trow = _iota((SUBLANES, L), 0)

    def tile_scan(a, u, reverse):
        for d in (1, 2, 4):
            if reverse:
                keep = trow < SUBLANES - d
                a_s = jnp.where(keep, pltpu.roll(a, SUBLANES - d, 0), 1.0)
                u_s = jnp.where(keep, pltpu.roll(u, SUBLANES - d, 0), 0.0)
            else:
                keep = trow >= d
                a_s = jnp.where(keep, pltpu.roll(a, d, 0), 1.0)
                u_s = jnp.where(keep, pltpu.roll(u, d, 0), 0.0)
            u = u + a * u_s
            a = a * a_s
        return a, u

    def body(k, carry):
        hf, hb = carry
        f0 = pl.multiple_of(k * SUBLANES, SUBLANES)
        b0 = pl.multiple_of((n_tiles - 1 - k) * SUBLANES, SUBLANES)
        af, uf = tile_scan(a_scr[0, pl.ds(f0, SUBLANES), :], u_scr[0, pl.ds(f0, SUBLANES), :], False)
        ab, ub = tile_scan(a_scr[1, pl.ds(b0, SUBLANES), :], u_scr[1, pl.ds(b0, SUBLANES), :], True)
        tf = uf + af * hf
        tb = ub + ab * hb
        h_scr[0, pl.ds(f0, SUBLANES), :] = tf
        h_scr[1, pl.ds(b0, SUBLANES), :] = tb
        return tf[SUBLANES - 1:SUBLANES, :], tb[0:1, :]

    zero = jnp.zeros((1, L), F32)
    lax.fori_loop(0, n_tiles, body, (zero, zero), unroll=8)
    o_ref[0] = ((h_scr[0] + h_scr[1]) * _gelu(gate_ref[0])).astype(o_ref.dtype)


def _lru(proj, B, S, W, x_col, g_col, cw, cb, wr, br, wi, bi, lam):
    nb = W // LANES
    return pl.pallas_call(
        _lru_kernel,
        out_shape=jax.ShapeDtypeStruct((B, S, W), BF16),
        grid=(B, nb),
        in_specs=[pl.BlockSpec((1, S, LANES), lambda b, j: (b, 0, x_col + j)),
                  pl.BlockSpec((1, S, LANES), lambda b, j: (b, 0, g_col + j)),
                  pl.BlockSpec((CONV_WIDTH, LANES), lambda b, j: (0, j)),
                  pl.BlockSpec((1, LANES), lambda b, j: (0, j)),
                  pl.BlockSpec((2, 1, LANES, LANES), lambda b, j: (0, j, 0, 0)),
                  pl.BlockSpec((2, LANES), lambda b, j: (0, j)),
                  pl.BlockSpec((2, 1, LANES, LANES), lambda b, j: (0, j, 0, 0)),
                  pl.BlockSpec((2, LANES), lambda b, j: (0, j)),
                  pl.BlockSpec((2, LANES), lambda b, j: (0, j))],
        out_specs=pl.BlockSpec((1, S, LANES), lambda b, j: (b, 0, j)),
        scratch_shapes=[pltpu.VMEM((2, S, LANES), F32)] * 3,
        compiler_params=_cparams("parallel", "parallel"),
        name="rglru",
    )(proj, proj, cw, cb.reshape(1, W), wr, br, wi, bi, lam)


def _rope_kernel(ang_ref, cos_ref, sin_ref):
    ang = ang_ref[0]
    lane = _iota(ang.shape, 1)
    cos_ref[0] = jnp.cos(ang)
    s = jnp.sin(ang)
    sin_ref[0] = jnp.where((lane % 64) < 32, -s, s)


def _rope_tables(ang):
    B, S, _ = ang.shape
    ts = min(512, S)
    spec = pl.BlockSpec((1, ts, LANES), lambda b, i: (b, i, 0))
    return pl.pallas_call(
        _rope_kernel,
        out_shape=(jax.ShapeDtypeStruct((B, S, LANES), F32),) * 2,
        grid=(B, S // ts),
        in_specs=[spec],
        out_specs=(spec, spec),
        compiler_params=_cparams("parallel", "parallel"),
        name="rope_tables",
    )(ang)


def _ret_kernel(q_ref, k_ref, v_ref, g_ref, cos_ref, sin_ref, lg_ref, o_ref,
                qs_scr, ks_scr, kv_scr):
    S = q_ref.shape[1]
    C = RET_CHUNK
    N = S // C
    dk = 64
    p = pl.program_id(1)
    lane = _iota((S, LANES), 1)
    first_half = (lane % 64) < 32

    def rope(x):
        swapped = jnp.where(first_half, pltpu.roll(x, LANES - 32, 1), pltpu.roll(x, 32, 1))
        return x * cos_ref[0] + swapped * sin_ref[0]

    qs_scr[...] = rope(q_ref[0]) * (dk ** -0.5)
    ks_scr[...] = rope(k_ref[0])

    pos_r = _iota((C, 1), 0).astype(F32)
    pos_l = _iota((1, C), 1).astype(F32)
    ii = _iota((C, C), 0)
    jj = _iota((C, C), 1)
    dist = jnp.abs(ii - jj).astype(F32)
    clane = _iota((C, LANES), 1)

    for hh in range(2):
        lg = lg_ref[pl.ds(2 * p + hh, 1), :][:, 0:1]
        hmask = (clane // 64) == hh
        intra = jnp.exp(dist * lg)
        dec_kf_row = jnp.exp((C - 1.0 - pos_l) * lg)
        dec_kb_row = jnp.exp(pos_l * lg)
        dec_qf = jnp.exp((pos_r + 1.0) * lg)
        dec_qb = jnp.exp((C - pos_r) * lg)
        chunk_decay = jnp.exp(C * lg)

        def kv_body(n, _):
            r0 = pl.multiple_of(n * C, C)
            kt = jnp.where(hmask, ks_scr[pl.ds(r0, C), :], 0.0).T
            vc = v_ref[0, pl.ds(r0, C), hh * LANES:(hh + 1) * LANES]
            kv_scr[n] = _bdot(jnp.concatenate([kt * dec_kf_row, kt * dec_kb_row], axis=0), vc)
            return 0

        lax.fori_loop(0, N, kv_body, 0, unroll=2)

        def fwd_body(n, st):
            cur = kv_scr[n, 0:LANES, :]
            kv_scr[n, 0:LANES, :] = st
            return chunk_decay * st + cur

        lax.fori_loop(0, N, fwd_body, jnp.zeros((LANES, LANES), F32))

        def bwd_body(m, st):
            n = N - 1 - m
            cur = kv_scr[n, LANES:, :]
            kv_scr[n, LANES:, :] = st
            return chunk_decay * st + cur

        lax.fori_loop(0, N, bwd_body, jnp.zeros((LANES, LANES), F32))

        def out_body(n, _):
            r0 = pl.multiple_of(n * C, C)
            qc = qs_scr[pl.ds(r0, C), :]
            kc = jnp.where(hmask, ks_scr[pl.ds(r0, C), :], 0.0)
            vc = v_ref[0, pl.ds(r0, C), hh * LANES:(hh + 1) * LANES]
            scores = _bdot_nt(qc, kc) * intra
            o = _bdot(scores, vc) + _bdot(jnp.concatenate([qc * dec_qf, qc * dec_qb], axis=1), kv_scr[n])
            mu = jnp.mean(o, axis=-1, keepdims=True)
            var = jnp.mean(o * o, axis=-1, keepdims=True) - mu * mu
            o = (o - mu) * lax.rsqrt(var + EPS)
            gc = g_ref[0, pl.ds(r0, C), hh * LANES:(hh + 1) * LANES]
            o_ref[0, pl.ds(r0, C), hh * LANES:(hh + 1) * LANES] = (_silu(gc) * o).astype(o_ref.dtype)
            return 0

        lax.fori_loop(0, N, out_body, 0, unroll=4)


def _retention(proj, B, S, W, q_col, k_col, v_col, g_col, cosf, sinf, lg_tab):
    n_pairs = RET_HEADS // 2
    N = S // RET_CHUNK
    return pl.pallas_call(
        _ret_kernel,
        out_shape=jax.ShapeDtypeStruct((B, S, W), BF16),
        grid=(B, n_pairs),
        in_specs=[pl.BlockSpec((1, S, LANES), lambda b, p: (b, 0, q_col + p)),
                  pl.BlockSpec((1, S, LANES), lambda b, p: (b, 0, k_col + p)),
                  pl.BlockSpec((1, S, 2 * LANES), lambda b, p: (b, 0, v_col + p)),
                  pl.BlockSpec((1, S, 2 * LANES), lambda b, p: (b, 0, g_col + p)),
                  pl.BlockSpec((1, S, LANES), lambda b, p: (b, 0, 0)),
                  pl.BlockSpec((1, S, LANES), lambda b, p: (b, 0, 0)),
                  pl.BlockSpec((RET_HEADS, LANES), lambda b, p: (0, 0))],
        out_specs=pl.BlockSpec((1, S, 2 * LANES), lambda b, p: (b, 0, p)),
        scratch_shapes=[pltpu.VMEM((S, LANES), F32), pltpu.VMEM((S, LANES), F32),
                        pltpu.VMEM((N, 2 * LANES, LANES), F32)],
        compiler_params=_cparams("parallel", "parallel"),
        name="retention",
    )(proj, proj, proj, proj, cosf, sinf, lg_tab)


def _sgu_kernel(u_ref, v_ref, ng_ref, w_ref, bt_ref, o_ref):
    ts = u_ref.shape[1]
    C = SGU_CHUNK
    v = _gelu(v_ref[0])
    mu = jnp.mean(v, axis=-1, keepdims=True)
    var = jnp.mean(jnp.square(v - mu), axis=-1, keepdims=True)
    v = ((v - mu) * lax.rsqrt(var + EPS) * ng_ref[...]).astype(BF16)
    for c in range(ts // C):
        for g in range(SGU_GROUPS):
            vc = v[c * C:(c + 1) * C, g * LANES:(g + 1) * LANES]
            mixed = jnp.dot(w_ref[g].astype(BF16), vc, preferred_element_type=F32) + bt_ref[:, g:g + 1]
            uc = _gelu(u_ref[0, c * C:(c + 1) * C, g * LANES:(g + 1) * LANES])
            o_ref[0, c * C:(c + 1) * C, g * LANES:(g + 1) * LANES] = (uc * mixed).astype(o_ref.dtype)


def _sgu(proj, B, S, W, u_col, v_col, norm_g, w_s, b_s):
    ts = min(512, S)
    return pl.pallas_call(
        _sgu_kernel,
        out_shape=jax.ShapeDtypeStruct((B, S, W), BF16),
        grid=(B, S // ts),
        in_specs=[pl.BlockSpec((1, ts, W), lambda b, i: (b, i, u_col)),
                  pl.BlockSpec((1, ts, W), lambda b, i: (b, i, v_col)),
                  pl.BlockSpec((1, W), lambda b, i: (0, 0)),
                  pl.BlockSpec((SGU_GROUPS, SGU_CHUNK, SGU_CHUNK), lambda b, i: (0, 0, 0)),
                  pl.BlockSpec((SGU_CHUNK, SGU_GROUPS), lambda b, i: (0, 0))],
        out_specs=pl.BlockSpec((1, ts, W), lambda b, i: (b, i, 0)),
        compiler_params=_cparams("parallel", "parallel"),
        name="spatial_gating",
    )(proj, proj, norm_g.reshape(1, W), w_s, b_s.T)


def _head_sum(x, hd):
    seg = (_iota((LANES, LANES), 0) // hd == _iota((LANES, LANES), 1) // hd).astype(BF16)
    hi = x.astype(BF16)
    lo = (x - hi.astype(F32)).astype(BF16)
    parts = [jnp.dot(hi[:, j * LANES:(j + 1) * LANES], seg, preferred_element_type=F32)
             + jnp.dot(lo[:, j * LANES:(j + 1) * LANES], seg, preferred_element_type=F32)
             for j in range(x.shape[-1] // LANES)]
    return jnp.concatenate(parts, axis=1)


def _rwkv_pre_kernel(f_ref, fp_ref, fn_ref, mup_ref, mun_ref, w0_ref, wup_ref, a0_ref, aup_ref,
                     gup_ref, kk_ref, ka_ref, rk_ref,
                     r_out, kk_out, v_out, g_out, bonus_out, lw_out, bb_out, kt_out):
    ts = f_ref.shape[1]
    W = r_out.shape[2]
    i = pl.program_id(1)
    n_i = pl.num_programs(1)
    f = f_ref[0]
    rows = _iota(f.shape, 0)
    prev_row = jnp.where(i > 0, fp_ref[0, SUBLANES - 1:SUBLANES, :], 0.0)
    next_row = jnp.where(i < n_i - 1, fn_ref[0, 0:1, :], 0.0)
    prev = jnp.where(rows >= 1, pltpu.roll(f, 1, 0), prev_row)
    nxt = jnp.where(rows < ts - 1, pltpu.roll(f, ts - 1, 0), next_row)
    f = f + mup_ref[...] * (prev - f) + mun_ref[...] * (nxt - f)

    r = f[:, 0:W]
    k = f[:, W:2 * W]
    v = f[:, 2 * W:3 * W]
    o = 3 * W
    wd = jnp.tanh(f[:, o:o + 2 * DECAY_LORA])
    ad = f[:, o + 2 * DECAY_LORA:o + 2 * DECAY_LORA + 2 * ICLR_LORA]
    gd = f[:, o + 2 * DECAY_LORA + 2 * ICLR_LORA:]

    g_out[0] = _bdot(_sigmoid(gd), gup_ref[...])
    kk = k * kk_ref[...]
    kk = kk * lax.rsqrt(_head_sum(kk * kk, RWKV_HEAD_DIM) + EPS)
    r_out[0] = r.astype(r_out.dtype)
    kk_out[0] = kk.astype(kk_out.dtype)
    v_out[0] = v.astype(v_out.dtype)
    bonus = jnp.zeros_like(r)
    for z in range(2):
        w_raw = _bdot(wd[:, z * DECAY_LORA:(z + 1) * DECAY_LORA], wup_ref[z]) + w0_ref[z:z + 1, :]
        lw_out[z, 0] = -DECAY_SCALE * _sigmoid(w_raw)
        a = _sigmoid(_bdot(ad[:, z * ICLR_LORA:(z + 1) * ICLR_LORA], aup_ref[z]) + a0_ref[z:z + 1, :])
        bb_out[z, 0] = (a * kk).astype(bb_out.dtype)
        kt = k * (1.0 + (a - 1.0) * ka_ref[...])
        kt_out[z, 0] = kt.astype(kt_out.dtype)
        bonus = bonus + _head_sum(r * kt * rk_ref[...], RWKV_HEAD_DIM) * v
    bonus_out[0] = bonus


def _rwkv_pre(feat, mu_prev, mu_next, w0, w_up, a0, a_up, g_up, k_k, k_a, r_k):
    B, S, Fw = feat.shape
    W = w0.shape[1]
    ts = min(256, S)
    hb = ts // SUBLANES
    n_hb = S // SUBLANES
    row = lambda a: a.reshape(1, -1)
    full2 = lambda a: pl.BlockSpec(a.shape, lambda b, i: (0, 0))
    full3 = lambda a: pl.BlockSpec(a.shape, lambda b, i: (0, 0, 0))
    bsw = pl.BlockSpec((1, ts, W), lambda b, i: (b, i, 0))
    zsw = pl.BlockSpec((2, 1, ts, W), lambda b, i: (0, b, i, 0))
    sds = jax.ShapeDtypeStruct((B, S, W), F32)
    zds = jax.ShapeDtypeStruct((2, B, S, W), F32)
    sdh = jax.ShapeDtypeStruct((B, S, W), BF16)
    zdh = jax.ShapeDtypeStruct((2, B, S, W), BF16)
    args = (row(mu_prev), row(mu_next), w0, w_up, a0, a_up, g_up, row(k_k), row(k_a), row(r_k))
    specs = [full2(args[0]), full2(args[1]), full2(w0), full3(w_up), full2(a0), full3(a_up),
             full2(g_up), full2(args[7]), full2(args[8]), full2(args[9])]
    return pl.pallas_call(
        _rwkv_pre_kernel,
        out_shape=(sdh, sdh, sdh, sds, sds, zds, zdh, zdh),
        grid=(B, S // ts),
        in_specs=[pl.BlockSpec((1, ts, Fw), lambda b, i: (b, i, 0)),
                  pl.BlockSpec((1, SUBLANES, Fw), lambda b, i: (b, jnp.maximum(i * hb - 1, 0), 0)),
                  pl.BlockSpec((1, SUBLANES, Fw), lambda b, i: (b, jnp.minimum((i + 1) * hb, n_hb - 1), 0)),
                  ] + specs,
        out_specs=(bsw, bsw, bsw, bsw, bsw, zsw, zsw, zsw),
        compiler_params=_cparams("parallel", "parallel"),
        name="rwkv_prepare",
    )(feat, feat, feat, *args)


def _rwkv_scan_kernel(r_ref, kk_ref, v_ref, lw_ref, bb_ref, kt_ref, y_ref, st_scr):
    C = RWKV_CHUNK
    C2 = 2 * C
    z = pl.program_id(0)
    i = pl.program_id(2)
    n_pairs = r_ref.shape[2] // LANES
    fwd = z == 0

    @pl.when(i == 0)
    def _():
        st_scr[...] = jnp.zeros_like(st_scr)

    ti = _iota((C, C), 0)
    si = _iota((C, C), 1)
    sign = jnp.where(fwd, 1, -1)
    cum_mat = (sign * (ti - si) >= 0).astype(BF16)
    er = _iota((C2, C2), 0)
    ec = _iota((C2, C2), 1)
    lead = sign * (er % C - ec % C)
    strict = lead > 0
    incl = lead >= 0
    eye2 = (er == ec).astype(F32)
    row_head = _iota((C2, LANES), 0) // C
    lane_head = _iota((C2, LANES), 1) // RWKV_HEAD_DIM
    emask = row_head == lane_head
    diag_l = _iota((LANES, LANES), 0) == _iota((LANES, LANES), 1)

    def expand(x):
        return jnp.where(emask, jnp.concatenate([x, x], axis=0), 0.0)

    def bf(x):
        return x.astype(BF16)

    def mm(a, b):
        return jnp.dot(a, b, preferred_element_type=F32)

    pairs = range(n_pairs)
    sl = [slice(j * LANES, (j + 1) * LANES) for j in pairs]
    lw = [lw_ref[0, 0, :, s] for s in sl]
    def cumsum3(x):
        hi = bf(x)
        r1 = x - hi.astype(F32)
        mid = bf(r1)
        lo = bf(r1 - mid.astype(F32))
        return mm(cum_mat, hi) + (mm(cum_mat, mid) + mm(cum_mat, lo))

    cum = [cumsum3(x) for x in lw]
    tot = [jnp.sum(x, axis=0, keepdims=True) for x in lw]
    p_inv = [jnp.exp(-c) for c in cum]
    p_rest = [jnp.exp(t - c) for t, c in zip(tot, cum)]
    kh_e = [bf(expand(kk_ref[0, :, s] * jnp.exp(c - x))) for s, c, x in zip(sl, cum, lw)]
    rh_e = [bf(expand(r_ref[0, :, s] * jnp.exp(c))) for s, c in zip(sl, cum)]
    bh_e = [bf(expand(bb_ref[0, 0, :, s] * p)) for s, p in zip(sl, p_inv)]
    ktil_e = [bf(expand(kt_ref[0, 0, :, s] * p)) for s, p in zip(sl, p_inv)]
    b_rest_t = [bf(expand(bb_ref[0, 0, :, s] * p).T) for s, p in zip(sl, p_rest)]
    k_rest_t = [bf(expand(kt_ref[0, 0, :, s] * p).T) for s, p in zip(sl, p_rest)]
    v_e = [bf(expand(v_ref[0, :, s])) for s in sl]

    a_all = [lax.dot_general(jnp.concatenate([k, r], axis=0), jnp.concatenate([b, t], axis=0),
                             (((1,), (1,)), ((), ())), preferred_element_type=F32)
             for k, r, b, t in zip(kh_e, rh_e, bh_e, ktil_e)]
    a_b = [jnp.where(strict, a[0:C2, 0:C2], 0.0) for a in a_all]
    a_k = [jnp.where(strict, a[0:C2, C2:], 0.0) for a in a_all]
    a_rb = [jnp.where(incl, a[C2:, 0:C2], 0.0) for a in a_all]
    a_rk = [jnp.where(incl, a[C2:, C2:], 0.0) for a in a_all]

    pw = [bf(-a) for a in a_b]
    t_inv = [eye2] * n_pairs
    n_dbl = int(math.log2(C))
    for k in range(n_dbl):
        if k < n_dbl - 1:
            both = [mm(p, jnp.concatenate([p, bf(t)], axis=1)) for p, t in zip(pw, t_inv)]
            pw = [bf(x[:, 0:C2]) for x in both]
            t_inv = [t + x[:, C2:] for t, x in zip(t_inv, both)]
        else:
            t_inv = [t + mm(p, bf(t)) for p, t in zip(pw, t_inv)]

    akv = [mm(bf(jnp.concatenate([ak, ark], axis=0)), v)
           for ak, ark, v in zip(a_k, a_rk, v_e)]
    tw = [mm(bf(t), jnp.concatenate([k, bf(x[0:C2])], axis=1))
          for t, k, x in zip(t_inv, kh_e, akv)]
    st = [st_scr[j] for j in pairs]
    ws = [mm(jnp.concatenate([bf(x[:, 0:LANES]), r], axis=0), bf(s))
          for x, r, s in zip(tw, rh_e, st)]
    u_e = [x[0:C2] + t[:, LANES:] for x, t in zip(ws, tw)]
    y_e = [x[C2:] + k[C2:] - mm(bf(a), bf(u)) for x, k, a, u in zip(ws, akv, a_rb, u_e)]
    for s, y in zip(sl, y_e):
        y_ref[0, 0, :, s] = (y[0:C, :] + y[C:C2, :]).astype(y_ref.dtype)

    bw = [mm(b, bf(t)) for b, t in zip(b_rest_t, tw)]
    kv = [mm(k, v) for k, v in zip(k_rest_t, v_e)]
    for j in pairs:
        m_t = jnp.where(diag_l, jnp.exp(tot[j]), 0.0) - bw[j][:, 0:LANES]
        g_t = kv[j] - bw[j][:, LANES:]
        st_scr[j] = mm(bf(m_t), bf(st[j])) + g_t


def _rwkv_scan(r, kk, v, lw, bb, kt):
    B, S, W = r.shape
    C = RWKV_CHUNK
    nc = S // C
    chunk = lambda z, i: i + z * (nc - 1 - 2 * i)
    shared = pl.BlockSpec((1, C, W), lambda z, b, i: (b, chunk(z, i), 0))
    perdir = pl.BlockSpec((1, 1, C, W), lambda z, b, i: (z, b, chunk(z, i), 0))
    return pl.pallas_call(
        _rwkv_scan_kernel,
        out_shape=jax.ShapeDtypeStruct((2, B, S, W), BF16),
        grid=(2, B, nc),
        in_specs=[shared, shared, shared, perdir, perdir, perdir],
        out_specs=perdir,
        scratch_shapes=[pltpu.VMEM((W // LANES, LANES, LANES), F32)],
        compiler_params=_cparams("parallel", "parallel", "arbitrary"),
        name="rwkv_scan",
    )(r, kk, v, lw, bb, kt)


def _rwkv_post_kernel(y_ref, bonus_ref, g_ref, o_ref):
    hd = RWKV_HEAD_DIM
    acc = bonus_ref[0]
    for z in range(2):
        y = y_ref[z, 0].astype(F32)
        d = y - _head_sum(y, hd) * (1.0 / hd)
        var = _head_sum(d * d, hd) * (1.0 / hd)
        acc = acc + d * lax.rsqrt(var + EPS)
    o_ref[0] = (acc * g_ref[0]).astype(o_ref.dtype)


def _rwkv_post(y, bonus, g):
    _, B, S, W = y.shape
    ts = min(256, S)
    bsw = pl.BlockSpec((1, ts, W), lambda b, i: (b, i, 0))
    return pl.pallas_call(
        _rwkv_post_kernel,
        out_shape=jax.ShapeDtypeStruct((B, S, W), BF16),
        grid=(B, S // ts),
        in_specs=[pl.BlockSpec((2, 1, ts, W), lambda b, i: (0, b, i, 0)), bsw, bsw],
        out_specs=bsw,
        compiler_params=_cparams("parallel", "parallel"),
        name="rwkv_finish",
    )(y, bonus, g)


def _merge_kernel(ya_ref, yb_ref, yc_ref, yd_ref, wb_ref, l0_ref, l1_ref, l2_ref, l3_ref, o_ref,
                  wb_bf):
    @pl.when(pl.program_id(1) == 0)
    def _():
        wb_bf[...] = wb_ref[0].astype(BF16)

    acc = None
    for n, (y_ref, l_ref) in enumerate(((ya_ref, l0_ref), (yb_ref, l1_ref), (yc_ref, l2_ref),
                                        (yd_ref, l3_ref))):
        br = jnp.dot(y_ref[...], wb_bf[n], preferred_element_type=F32)
        t = _sigmoid(l_ref[...]) * br
        acc = t if acc is None else acc + t
    o_ref[...] = acc.astype(o_ref.dtype)


def _merge(ys, wb, layer, logits, tm, tn):
    T, W = ys[0].shape
    D = wb.shape[3]
    nj = D // tn
    yspec = pl.BlockSpec((tm, W), lambda j, i: (i, 0))
    lspec = lambda n: pl.BlockSpec((tm, tn), lambda j, i: (i, n * nj + j))
    return pl.pallas_call(
        _merge_kernel,
        out_shape=jax.ShapeDtypeStruct((T, D), BF16),
        grid=(nj, T // tm),
        in_specs=[yspec, yspec, yspec, yspec,
                  pl.BlockSpec((1, N_BRANCH, W, tn), lambda j, i: (layer, 0, 0, j)),
                  lspec(0), lspec(1), lspec(2), lspec(3)],
        out_specs=pl.BlockSpec((tm, tn), lambda j, i: (i, j)),
        scratch_shapes=[pltpu.VMEM((N_BRANCH, W, tn), BF16)],
        compiler_params=_cparams("parallel", "arbitrary"),
        name="branch_merge",
    )(*ys, wb, logits, logits, logits, logits)


def _router_kernel(h_ref, w_ref, b_ref, o_ref):
    logits = jnp.dot(_unpack_bf16_pairs(h_ref[...]), w_ref[...], preferred_element_type=F32) + b_ref[...]
    lane = _iota(logits.shape, 1)
    lane_f = lane.astype(F32)
    neg = -3.0e38
    far = float(LANES)
    is_grp = lane < N_GROUPS
    gl = jnp.where(is_grp, logits, neg)
    gmax = jnp.max(gl, axis=-1, keepdims=True)
    gsum = jnp.sum(jnp.where(is_grp, jnp.exp(gl - gmax), 0.0), axis=-1, keepdims=True)
    grp_p = 1.0 / gsum
    grp_i = jnp.min(jnp.where(gl == gmax, lane_f, far), axis=-1, keepdims=True)
    lo = N_GROUPS + grp_i * EXPERTS_PER_GROUP
    in_grp = jnp.logical_and(lane_f >= lo, lane_f < lo + EXPERTS_PER_GROUP)
    el = jnp.where(in_grp, logits, neg)
    m1 = jnp.max(el, axis=-1, keepdims=True)
    i1 = jnp.min(jnp.where(el == m1, lane_f, far), axis=-1, keepdims=True)
    el2 = jnp.where(lane_f == i1, neg, el)
    m2 = jnp.max(el2, axis=-1, keepdims=True)
    i2 = jnp.min(jnp.where(el2 == m2, lane_f, far), axis=-1, keepdims=True)
    t = jnp.exp(m2 - m1)
    w1 = grp_p / (1.0 + t)
    w2 = grp_p * t / (1.0 + t)
    o_ref[...] = jnp.where(lane == 0, w1, jnp.where(lane == 1, w2, jnp.where(
        lane == 2, i1 - N_GROUPS, jnp.where(lane == 3, i2 - N_GROUPS, 0.0))))


def _router(h, w_r, b_r):
    T = h.shape[0]
    D = w_r.shape[0]
    tm = min(1024, T)
    return pl.pallas_call(
        _router_kernel,
        out_shape=jax.ShapeDtypeStruct((T, LANES), F32),
        grid=(T // tm,),
        in_specs=[pl.BlockSpec((tm, D // 2), lambda i: (i, 0)),
                  pl.BlockSpec((D, LANES), lambda i: (0, 0)),
                  pl.BlockSpec((1, LANES), lambda i: (0, 0))],
        out_specs=pl.BlockSpec((tm, LANES), lambda i: (i, 0)),
        compiler_params=_cparams("parallel"),
        name="router",
    )(h, w_r, b_r)


def _expert_kernel(be_ref, nb_ref, xs_ref, wgu_ref, wd_ref, o_ref, wgu_bf, wd_bf):
    i = pl.program_id(0)
    used = i < nb_ref[0]
    changed = jnp.logical_or(i == 0, be_ref[i] != be_ref[jnp.maximum(i - 1, 0)])

    @pl.when(jnp.logical_and(used, changed))
    def _():
        wgu_bf[...] = wgu_ref[0, 0].astype(BF16)
        wd_bf[...] = wd_ref[0, 0].astype(BF16)

    @pl.when(used)
    def _():
        gu = jnp.dot(_unpack_bf16_pairs(xs_ref[...]), wgu_bf[...], preferred_element_type=F32)
        eh = gu.shape[1] // 2
        mid = (_silu(gu[:, :eh]) * gu[:, eh:]).astype(BF16)
        o_ref[...] = jnp.dot(mid, wd_bf[...], preferred_element_type=F32)

    @pl.when(jnp.logical_not(used))
    def _():
        o_ref[...] = jnp.zeros_like(o_ref)


def _experts(xs, block_expert, n_used, w_gu, w_down, layer):
    cap = xs.shape[0]
    D, H2 = w_gu.shape[2], w_gu.shape[3]
    n_blocks = cap // MOE_BLOCK
    blk = lambda i, nb: jnp.minimum(i, nb[0] - 1)
    grid_spec = pltpu.PrefetchScalarGridSpec(
        num_scalar_prefetch=2,
        grid=(n_blocks,),
        in_specs=[pl.BlockSpec((MOE_BLOCK, D // 2), lambda i, be, nb: (blk(i, nb), 0)),
                  pl.BlockSpec((1, 1, D, H2), lambda i, be, nb: (layer, be[blk(i, nb)], 0, 0)),
                  pl.BlockSpec((1, 1, H2 // 2, D), lambda i, be, nb: (layer, be[blk(i, nb)], 0, 0))],
        out_specs=pl.BlockSpec((MOE_BLOCK, D), lambda i, be, nb: (i, 0)),
        scratch_shapes=[pltpu.VMEM((D, H2), BF16), pltpu.VMEM((H2 // 2, D), BF16)],
    )
    return pl.pallas_call(
        _expert_kernel,
        out_shape=jax.ShapeDtypeStruct((cap, D), F32),
        grid_spec=grid_spec,
        compiler_params=_cparams("arbitrary"),
        name="experts",
    )(block_expert, n_used, xs, w_gu, w_down)


def _row_copy(src_ref, s, dst_ref, d, sem):
    return pltpu.make_async_copy(src_ref.at[pl.ds(s, 1), :], dst_ref.at[pl.ds(d, 1), :], sem)


def _rows_wait(src_ref, dst_ref, n, sem):
    pltpu.make_async_copy(src_ref.at[pl.ds(0, n), :], dst_ref.at[pl.ds(0, n), :], sem).wait()


def _combine_kernel(dest_ref, ys_ref, x_ref, r_ref, g_ref, ng_ref, o_ref, y0_buf, y1_buf, sem, *,
                    final_norm):
    tm = x_ref.shape[0]
    base = pl.program_id(0) * tm

    def issue(k, c):
        t = base + k
        _row_copy(ys_ref, dest_ref[TOP_K * t], y0_buf, k, sem.at[0]).start()
        _row_copy(ys_ref, dest_ref[TOP_K * t + 1], y1_buf, k, sem.at[1]).start()
        return c

    lax.fori_loop(0, tm, issue, 0, unroll=4)
    _rows_wait(ys_ref, y0_buf, tm, sem.at[0])
    _rows_wait(ys_ref, y1_buf, tm, sem.at[1])
    w0 = r_ref[:, 0:1]
    w1 = r_ref[:, 1:2]
    x = x_ref[...] + g_ref[0] * (w0 * y0_buf[...] + w1 * y1_buf[...])
    if final_norm:
        x = x * lax.rsqrt(jnp.mean(x * x, axis=-1, keepdims=True) + EPS) * ng_ref[...]
    o_ref[...] = x


def _combine(x2d, ys, dest, route, gate, rows_per_batch, norm_g, final_norm):
    T, D = x2d.shape
    tm = min(256, T)
    tiles_per_batch = rows_per_batch // tm
    spec = pl.BlockSpec((tm, D), lambda i, d: (i, 0))
    grid_spec = pltpu.PrefetchScalarGridSpec(
        num_scalar_prefetch=1,
        grid=(T // tm,),
        in_specs=[pl.BlockSpec(memory_space=pl.ANY), spec,
                  pl.BlockSpec((tm, LANES), lambda i, d: (i, 0)),
                  pl.BlockSpec((1, 1, D), lambda i, d: (i // tiles_per_batch, 0, 0)),
                  pl.BlockSpec((1, D), lambda i, d: (0, 0))],
        out_specs=spec,
        scratch_shapes=[pltpu.VMEM((tm, D), F32), pltpu.VMEM((tm, D), F32),
                        pltpu.SemaphoreType.DMA((2,))],
    )
    return pl.pallas_call(
        functools.partial(_combine_kernel, final_norm=final_norm),
        out_shape=jax.ShapeDtypeStruct((T, D), F32),
        grid_spec=grid_spec,
        compiler_params=_cparams("arbitrary"),
        name="moe_combine",
    )(dest, ys, x2d, route, gate, norm_g.reshape(1, D))


def _dispatch_kernel(dest_ref, h_ref, xs_in_ref, xs_ref, sem, *, tm):
    del xs_in_ref
    base = pl.program_id(0) * tm

    def issue(k, c):
        t = base + k
        for kk in range(TOP_K):
            _row_copy(h_ref, k, xs_ref, dest_ref[TOP_K * t + kk], sem.at[kk]).start()
        return c

    lax.fori_loop(0, tm, issue, 0, unroll=4)
    for kk in range(TOP_K):
        _rows_wait(h_ref, xs_ref, tm, sem.at[kk])


def _dispatch(h, dest, cap):
    T, Dp = h.shape
    tm = min(512, T)
    grid_spec = pltpu.PrefetchScalarGridSpec(
        num_scalar_prefetch=1,
        grid=(T // tm,),
        in_specs=[pl.BlockSpec((tm, Dp), lambda i, d: (i, 0)), pl.BlockSpec(memory_space=pl.ANY)],
        out_specs=pl.BlockSpec(memory_space=pl.ANY),
        scratch_shapes=[pltpu.SemaphoreType.DMA((TOP_K,))],
    )
    return pl.pallas_call(
        functools.partial(_dispatch_kernel, tm=tm),
        out_shape=jax.ShapeDtypeStruct((cap, Dp), h.dtype),
        grid_spec=grid_spec,
        input_output_aliases={2: 0},
        compiler_params=_cparams("arbitrary"),
        name="moe_dispatch",
    )(dest, h, jnp.zeros((cap, Dp), h.dtype))


def _rank_kernel(route_ref, rank_ref, cnt_ref, carry):
    tm = route_ref.shape[0]
    i = pl.program_id(0)

    @pl.when(i == 0)
    def _():
        carry[...] = jnp.zeros_like(carry)

    r = route_ref[...]
    lane = _iota(r.shape, 1)
    lane_f = lane.astype(F32)
    oh0 = (lane_f == r[:, 2:3]).astype(F32)
    oh1 = (lane_f == r[:, 3:4]).astype(F32)
    both = oh0 + oh1
    earlier = (_iota((tm, tm), 0) > _iota((tm, tm), 1)).astype(BF16)
    prefix = jnp.dot(earlier, both.astype(BF16), preferred_element_type=F32) + carry[0:1, :]
    rank0 = jnp.sum(prefix * oh0, axis=-1, keepdims=True)
    rank1 = jnp.sum((prefix + oh0) * oh1, axis=-1, keepdims=True)
    rank_ref[...] = jnp.where(lane == 0, rank0, jnp.where(lane == 1, rank1, 0.0))
    carry[...] = carry[...] + jnp.sum(both, axis=0, keepdims=True)
    cnt_ref[...] = carry[...]


def _dest_kernel(route_ref, rank_ref, cnt_ref, dest_ref):
    r = route_ref[...]
    lane = _iota(r.shape, 1)
    lane_f = lane.astype(F32)
    padded = jnp.floor((cnt_ref[...] + (MOE_BLOCK - 1.0)) * (1.0 / MOE_BLOCK)) * MOE_BLOCK
    before = (_iota((LANES, LANES), 0) < _iota((LANES, LANES), 1)).astype(F32)
    pad_start = jnp.dot(padded, before, precision=HI, preferred_element_type=F32)[0:1, :]
    oh0 = lane_f == r[:, 2:3]
    oh1 = lane_f == r[:, 3:4]
    d0 = jnp.sum(jnp.where(oh0, pad_start, 0.0), axis=-1, keepdims=True) + rank_ref[:, 0:1]
    d1 = jnp.sum(jnp.where(oh1, pad_start, 0.0), axis=-1, keepdims=True) + rank_ref[:, 1:2]
    dest_ref[...] = jnp.where(lane == 0, d0, jnp.where(lane == 1, d1, 0.0))


def _dispatch_plan(route):
    T = route.shape[0]
    tm = min(512, T)
    slab = pl.BlockSpec((tm, LANES), lambda i: (i, 0))
    cnt_spec = pl.BlockSpec((SUBLANES, LANES), lambda i: (0, 0))
    rank, cnt = pl.pallas_call(
        _rank_kernel,
        out_shape=(jax.ShapeDtypeStruct((T, LANES), F32), jax.ShapeDtypeStruct((SUBLANES, LANES), F32)),
        grid=(T // tm,),
        in_specs=[slab],
        out_specs=(slab, cnt_spec),
        scratch_shapes=[pltpu.VMEM((SUBLANES, LANES), F32)],
        compiler_params=_cparams("arbitrary"),
        name="moe_rank",
    )(route)
    dest = pl.pallas_call(
        _dest_kernel,
        out_shape=jax.ShapeDtypeStruct((T, LANES), F32),
        grid=(T // tm,),
        in_specs=[slab, slab, cnt_spec],
        out_specs=slab,
        compiler_params=_cparams("parallel"),
        name="moe_dest",
    )(route, rank, cnt)
    dest = dest[:, :TOP_K].astype(jnp.int32).reshape(T * TOP_K)
    counts = cnt[0, :N_EXPERTS].astype(jnp.int32)
    padded = (counts + MOE_BLOCK - 1) // MOE_BLOCK * MOE_BLOCK
    pad_end = jnp.cumsum(padded)
    n_blocks = -(-(T * TOP_K) // MOE_BLOCK) + N_EXPERTS
    block_row0 = jnp.arange(n_blocks, dtype=jnp.int32) * MOE_BLOCK
    block_expert = jnp.minimum(jnp.searchsorted(pad_end, block_row0, side="right"),
                               N_EXPERTS - 1).astype(jnp.int32)
    n_used = (pad_end[-1] // MOE_BLOCK).astype(jnp.int32).reshape(1)
    return dest, block_expert, n_used, n_blocks * MOE_BLOCK


def kernel(x, c, positions, norm1_g, norm2_g, ada_w, ada_b, w_in, lru_conv_w, lru_conv_b, lru_w_r, lru_b_r, lru_w_i, lru_b_i, lru_lambda, sgu_norm_g, sgu_w, sgu_b, rwkv_mu_prev, rwkv_mu_next, rwkv_w0, rwkv_w_up, rwkv_a0, rwkv_a_up, rwkv_g_up, rwkv_k_k, rwkv_k_a, rwkv_r_k, w_branch, w_out, router_grp_w, router_grp_b, router_exp_w, router_exp_b, expert_w_gu, expert_w_down, final_norm_g):
    B, S, D = x.shape
    L = ada_w.shape[0]
    W = D // 2
    T = B * S
    feat_w = rwkv_mu_prev.shape[1]

    c_pad = jnp.zeros((SUBLANES, D), F32).at[:B].set(c)
    mod = _ada(c_pad, ada_w, ada_b)

    dk = D // 2 // (2 * RET_HEADS)
    inv_freq = ROPE_THETA ** (-jnp.arange(0, dk, 2, dtype=F32) / dk)
    ang = positions.astype(F32)[..., None] * inv_freq
    cosf, sinf = _rope_tables(jnp.tile(ang, (1, 1, LANES // (dk // 2))))
    log_gamma = np.log1p(-np.exp2(-5.0 - np.arange(RET_HEADS, dtype=np.float64)))
    lg_tab = jnp.asarray(np.broadcast_to(log_gamma[:, None], (RET_HEADS, LANES)), F32)

    x2d = x.reshape(T, D)
    for l in range(L):
        m = mod[l, :B]
        sh1, sc1, g1, sh2, sc2, g2 = [m[:, None, k * D:(k + 1) * D] for k in range(6)]

        h = _normmod(x2d.reshape(B, S, D), norm1_g[l], sc1, sh1, BF16).reshape(T, D)
        n_a = 7 * W
        proj_a = _mm(h, w_in, l, 0, n_a, 2048, 512, name="proj_a").reshape(B, S, n_a)
        feat = _mm(h, w_in, l, n_a, feat_w, 1024, feat_w // 3, name="proj_rwkv").reshape(B, S, feat_w)
        logits = _mm(h, w_in, l, n_a + feat_w, N_BRANCH * D, 2048, 512, name="proj_gate")

        y_a = _lru(proj_a, B, S, W, 0, W // LANES, lru_conv_w[l], lru_conv_b[l], lru_w_r[l],
                   lru_b_r[l], lru_w_i[l], lru_b_i[l], lru_lambda[l])
        y_b = _retention(proj_a, B, S, W, 2 * W // LANES, 5 * W // (2 * LANES),
                         3 * W // (2 * LANES), 4 * W // (2 * LANES), cosf, sinf, lg_tab)
        y_c = _sgu(proj_a, B, S, W, 5, 6, sgu_norm_g[l], sgu_w[l], sgu_b[l])
        r, kk, v, g, bonus, lw, bb, kt = _rwkv_pre(
            feat, rwkv_mu_prev[l], rwkv_mu_next[l], rwkv_w0[l], rwkv_w_up[l], rwkv_a0[l],
            rwkv_a_up[l], rwkv_g_up[l], rwkv_k_k[l], rwkv_k_a[l], rwkv_r_k[l].reshape(-1))
        y_d = _rwkv_post(_rwkv_scan(r, kk, v, lw, bb, kt), bonus, g)

        merged = _merge([y.reshape(T, W) for y in (y_a, y_b, y_c, y_d)],
                        w_branch, l, logits, 512, 512)
        x2d = _mm_residual(merged, w_out, l, x2d, g1, S, 2048, 512)

        h2 = _normmod(x2d.reshape(B, S, D), norm2_g[l], sc2, sh2, jnp.uint32).reshape(T, D // 2)
        n_r = N_GROUPS + N_EXPERTS
        w_r = jnp.zeros((D, LANES), F32).at[:, :N_GROUPS].set(router_grp_w[l]).at[:, N_GROUPS:n_r].set(
            router_exp_w[l]).astype(BF16)
        b_r = jnp.zeros((1, LANES), F32).at[0, :N_GROUPS].set(router_grp_b[l]).at[0, N_GROUPS:n_r].set(
            router_exp_b[l])
        route = _router(h2, w_r, b_r)
        dest, block_expert, n_used, cap = _dispatch_plan(route)
        ys = _experts(_dispatch(h2, dest, cap), block_expert, n_used, expert_w_gu, expert_w_down, l)
        x2d = _combine(x2d, ys, dest, route, g2, S, final_norm_g, l == L - 1)

    return x2d.reshape(B, S, D)
```

```python
import functools
import math

import jax
import jax.numpy as jnp
import numpy as np
from jax import lax
from jax.experimental import pallas as pl
from jax.experimental.pallas import tpu as pltpu

F32 = jnp.float32
BF16 = jnp.bfloat16
EPS = 1e-6

LANES = 128
SUBLANES = 8
VMEM_LIMIT_BYTES = 56 * 1024 * 1024

LRU_BLOCKS = 8
CONV_WIDTH = 4
LRU_C = 8.0
RET_HEADS = 8
RET_CHUNK = 128
ROPE_THETA = 10000.0
SGU_GROUPS = 8
SGU_CHUNK = 128
RWKV_HEAD_DIM = 64
DECAY_LORA = 64
ICLR_LORA = 64
GATE_LORA = 128
DECAY_SCALE = math.exp(-0.5)
N_GROUPS = 4
EXPERTS_PER_GROUP = 8
N_EXPERTS = N_GROUPS * EXPERTS_PER_GROUP
TOP_K = 2
MOE_BLOCK = 256
N_BRANCH = 4

RWKV_CHUNK = 64
HI = lax.Precision.HIGHEST


def _cparams(*sem):
    return pltpu.CompilerParams(dimension_semantics=sem, vmem_limit_bytes=VMEM_LIMIT_BYTES)


def _bdot(a, b):
    return jnp.dot(a.astype(BF16), b.astype(BF16), preferred_element_type=F32)


def _bdot_nt(a, b):
    return lax.dot_general(a.astype(BF16), b.astype(BF16), (((1,), (1,)), ((), ())),
                           preferred_element_type=F32)


def _gelu(x):
    return 0.5 * x * (1.0 + jnp.tanh(math.sqrt(2.0 / math.pi) * (x + 0.044715 * (x * x * x))))


def _sigmoid(x):
    return 1.0 / (1.0 + jnp.exp(-x))


def _silu(x):
    return x * _sigmoid(x)


def _iota(shape, dim):
    return lax.broadcasted_iota(jnp.int32, shape, dim)


def _ada_kernel(c_ref, w_ref, b_ref, o_ref):
    cond = _silu(c_ref[...])
    o_ref[0] = _bdot(cond, w_ref[0]) + b_ref[0]


def _ada(c_pad, ada_w, ada_b):
    L, D, N = ada_w.shape
    R = c_pad.shape[0]
    tn = 1024
    return pl.pallas_call(
        _ada_kernel,
        out_shape=jax.ShapeDtypeStruct((L, R, N), F32),
        grid=(L, N // tn),
        in_specs=[pl.BlockSpec((R, D), lambda l, j: (0, 0)),
                  pl.BlockSpec((1, D, tn), lambda l, j: (l, 0, j)),
                  pl.BlockSpec((1, 1, tn), lambda l, j: (l, 0, j))],
        out_specs=pl.BlockSpec((1, R, tn), lambda l, j: (l, 0, j)),
        compiler_params=_cparams("parallel", "parallel"),
        name="ada_mod",
    )(c_pad, ada_w, ada_b.reshape(L, 1, N))


def _pack_bf16_pairs(y):
    half = y.shape[1] // 2
    bits = pltpu.bitcast(y.astype(BF16).astype(F32), jnp.uint32)
    return (bits[:, :half] >> 16) | (bits[:, half:] & jnp.uint32(0xFFFF0000))


def _unpack_bf16_pairs(u):
    lo = pltpu.bitcast(u << 16, F32)
    hi = pltpu.bitcast(u & jnp.uint32(0xFFFF0000), F32)
    return jnp.concatenate([lo, hi], axis=1).astype(BF16)


def _normmod_kernel(x_ref, g_ref, sc_ref, sh_ref, o_ref):
    x = x_ref[0]
    y = x * lax.rsqrt(jnp.mean(x * x, axis=-1, keepdims=True) + EPS) * g_ref[...]
    y = y * (1.0 + sc_ref[0]) + sh_ref[0]
    if o_ref.dtype == jnp.uint32:
        o_ref[0] = _pack_bf16_pairs(y)
    else:
        o_ref[0] = y.astype(o_ref.dtype)


def _normmod(x, g, sc, sh, out_dtype):
    B, S, D = x.shape
    ts = min(512, S)
    d_out = D // 2 if out_dtype == jnp.uint32 else D
    return pl.pallas_call(
        _normmod_kernel,
        out_shape=jax.ShapeDtypeStruct((B, S, d_out), out_dtype),
        grid=(B, S // ts),
        in_specs=[pl.BlockSpec((1, ts, D), lambda b, i: (b, i, 0)),
                  pl.BlockSpec((1, D), lambda b, i: (0, 0)),
                  pl.BlockSpec((1, 1, D), lambda b, i: (b, 0, 0)),
                  pl.BlockSpec((1, 1, D), lambda b, i: (b, 0, 0))],
        out_specs=pl.BlockSpec((1, ts, d_out), lambda b, i: (b, i, 0)),
        compiler_params=_cparams("parallel", "parallel"),
        name="norm_mod",
    )(x, g.reshape(1, D), sc, sh)


def _mm_kernel(a_ref, w_ref, o_ref):
    o_ref[...] = jnp.dot(a_ref[...], w_ref[0].astype(BF16),
                         preferred_element_type=F32).astype(o_ref.dtype)


def _mm(a, w, layer, col0, n_cols, tm, tn, out_dtype=F32, name="proj"):
    M, K = a.shape
    return pl.pallas_call(
        _mm_kernel,
        out_shape=jax.ShapeDtypeStruct((M, n_cols), out_dtype),
        grid=(M // tm, n_cols // tn),
        in_specs=[pl.BlockSpec((tm, K), lambda i, j: (i, 0)),
                  pl.BlockSpec((pl.Element(1), pl.Element(K), pl.Element(tn)),
                               lambda i, j: (layer, 0, pl.multiple_of(col0 + j * tn, LANES)))],
        out_specs=pl.BlockSpec((tm, tn), lambda i, j: (i, j)),
        compiler_params=_cparams("parallel", "parallel"),
        name=name,
    )(a, w)


def _mm_res_kernel(a_ref, w_ref, x_ref, g_ref, o_ref):
    y = jnp.dot(a_ref[...], w_ref[0].astype(BF16), preferred_element_type=F32)
    o_ref[...] = x_ref[...] + g_ref[0] * y


def _mm_residual(a, w, layer, x2d, gate, rows_per_batch, tm, tn, name="out_proj"):
    M, K = a.shape
    N = w.shape[2]
    tiles_per_batch = rows_per_batch // tm
    return pl.pallas_call(
        _mm_res_kernel,
        out_shape=jax.ShapeDtypeStruct((M, N), F32),
        grid=(M // tm, N // tn),
        in_specs=[pl.BlockSpec((tm, K), lambda i, j: (i, 0)),
                  pl.BlockSpec((1, K, tn), lambda i, j: (layer, 0, j)),
                  pl.BlockSpec((tm, tn), lambda i, j: (i, j)),
                  pl.BlockSpec((1, 1, tn), lambda i, j: (i // tiles_per_batch, 0, j))],
        out_specs=pl.BlockSpec((tm, tn), lambda i, j: (i, j)),
        compiler_params=_cparams("parallel", "parallel"),
        name=name,
    )(a, w, x2d, gate)


def _softplus(x):
    return jnp.maximum(x, 0.0) + jnp.log1p(jnp.exp(-jnp.abs(x)))


def _lru_kernel(x_ref, gate_ref, cw_ref, cb_ref, wr_ref, br_ref, wi_ref, bi_ref, lam_ref,
                o_ref, a_scr, u_scr, h_scr):
    S = x_ref.shape[1]
    x = x_ref[0]
    rows = _iota(x.shape, 0)
    xm2 = jnp.where(rows >= 2, pltpu.roll(x, 2, 0), 0.0)
    xm1 = jnp.where(rows >= 1, pltpu.roll(x, 1, 0), 0.0)
    xp1 = jnp.where(rows < S - 1, pltpu.roll(x, S - 1, 0), 0.0)
    xc = (cw_ref[0:1, :] * xm2 + cw_ref[1:2, :] * xm1 + cw_ref[2:3, :] * x
          + cw_ref[3:4, :] * xp1 + cb_ref[...])
    for z in range(2):
        r = _sigmoid(_bdot(xc, wr_ref[z, 0]) + br_ref[z:z + 1, :])
        i = _sigmoid(_bdot(xc, wi_ref[z, 0]) + bi_ref[z:z + 1, :])
        log_a = -LRU_C * r * _softplus(-lam_ref[z:z + 1, :])
        a = jnp.exp(log_a)
        a_scr[z] = a
        u_scr[z] = jnp.sqrt(-jnp.tanh(log_a) * (a * a + 1.0)) * i * xc

    n_tiles = S // SUBLANES
    L = x.shape[1]
    trow = _iota((SUBLANES, L), 0)

    def tile_scan(a, u, reverse):
        for d in (1, 2, 4):
            if reverse:
                keep = trow < SUBLANES - d
                a_s = jnp.where(keep, pltpu.roll(a, SUBLANES - d, 0), 1.0)
                u_s = jnp.where(keep, pltpu.roll(u, SUBLANES - d, 0), 0.0)
            else:
                keep = trow >= d
                a_s = jnp.where(keep, pltpu.roll(a, d, 0), 1.0)
                u_s = jnp.where(keep, pltpu.roll(u, d, 0), 0.0)
            u = u + a * u_s
            a = a * a_s
        return a, u

    def body(k, carry):
        hf, hb = carry
        f0 = pl.multiple_of(k * SUBLANES, SUBLANES)
        b0 = pl.multiple_of((n_tiles - 1 - k) * SUBLANES, SUBLANES)
        af, uf = tile_scan(a_scr[0, pl.ds(f0, SUBLANES), :], u_scr[0, pl.ds(f0, SUBLANES), :], False)
        ab, ub = tile_scan(a_scr[1, pl.ds(b0, SUBLANES), :], u_scr[1, pl.ds(b0, SUBLANES), :], True)
        tf = uf + af * hf
        tb = ub + ab * hb
        h_scr[0, pl.ds(f0, SUBLANES), :] = tf
        h_scr[1, pl.ds(b0, SUBLANES), :] = tb
        return tf[SUBLANES - 1:SUBLANES, :], tb[0:1, :]

    zero = jnp.zeros((1, L), F32)
    lax.fori_loop(0, n_tiles, body, (zero, zero), unroll=8)
    o_ref[0] = ((h_scr[0] + h_scr[1]) * _gelu(gate_ref[0])).astype(o_ref.dtype)


def _lru(proj, B, S, W, x_col, g_col, cw, cb, wr, br, wi, bi, lam):
    nb = W // LANES
    return pl.pallas_call(
        _lru_kernel,
        out_shape=jax.ShapeDtypeStruct((B, S, W), BF16),
        grid=(B, nb),
        in_specs=[pl.BlockSpec((1, S, LANES), lambda b, j: (b, 0, x_col + j)),
                  pl.BlockSpec((1, S, LANES), lambda b, j: (b, 0, g_col + j)),
                  pl.BlockSpec((CONV_WIDTH, LANES), lambda b, j: (0, j)),
                  pl.BlockSpec((1, LANES), lambda b, j: (0, j)),
                  pl.BlockSpec((2, 1, LANES, LANES), lambda b, j: (0, j, 0, 0)),
                  pl.BlockSpec((2, LANES), lambda b, j: (0, j)),
                  pl.BlockSpec((2, 1, LANES, LANES), lambda b, j: (0, j, 0, 0)),
                  pl.BlockSpec((2, LANES), lambda b, j: (0, j)),
                  pl.BlockSpec((2, LANES), lambda b, j: (0, j))],
        out_specs=pl.BlockSpec((1, S, LANES), lambda b, j: (b, 0, j)),
        scratch_shapes=[pltpu.VMEM((2, S, LANES), F32)] * 3,
        compiler_params=_cparams("parallel", "parallel"),
        name="rglru",
    )(proj, proj, cw, cb.reshape(1, W), wr, br, wi, bi, lam)


def _rope_kernel(ang_ref, cos_ref, sin_ref):
    ang = ang_ref[0]
    lane = _iota(ang.shape, 1)
    cos_ref[0] = jnp.cos(ang)
    s = jnp.sin(ang)
    sin_ref[0] = jnp.where((lane % 64) < 32, -s, s)


def _rope_tables(ang):
    B, S, _ = ang.shape
    ts = min(512, S)
    spec = pl.BlockSpec((1, ts, LANES), lambda b, i: (b, i, 0))
    return pl.pallas_call(
        _rope_kernel,
        out_shape=(jax.ShapeDtypeStruct((B, S, LANES), F32),) * 2,
        grid=(B, S // ts),
        in_specs=[spec],
        out_specs=(spec, spec),
        compiler_params=_cparams("parallel", "parallel"),
        name="rope_tables",
    )(ang)


def _ret_kernel(q_ref, k_ref, v_ref, g_ref, cos_ref, sin_ref, lg_ref, o_ref,
                qs_scr, ks_scr, kv_scr):
    S = q_ref.shape[1]
    C = RET_CHUNK
    N = S // C
    dk = 64
    p = pl.program_id(1)
    lane = _iota((S, LANES), 1)
    first_half = (lane % 64) < 32

    def rope(x):
        swapped = jnp.where(first_half, pltpu.roll(x, LANES - 32, 1), pltpu.roll(x, 32, 1))
        return x * cos_ref[0] + swapped * sin_ref[0]

    qs_scr[...] = rope(q_ref[0]) * (dk ** -0.5)
    ks_scr[...] = rope(k_ref[0])

    pos_r = _iota((C, 1), 0).astype(F32)
    pos_l = _iota((1, C), 1).astype(F32)
    ii = _iota((C, C), 0)
    jj = _iota((C, C), 1)
    dist = jnp.abs(ii - jj).astype(F32)
    clane = _iota((C, LANES), 1)

    for hh in range(2):
        lg = lg_ref[pl.ds(2 * p + hh, 1), :][:, 0:1]
        hmask = (clane // 64) == hh
        intra = jnp.exp(dist * lg)
        dec_kf_row = jnp.exp((C - 1.0 - pos_l) * lg)
        dec_kb_row = jnp.exp(pos_l * lg)
        dec_qf = jnp.exp((pos_r + 1.0) * lg)
        dec_qb = jnp.exp((C - pos_r) * lg)
        chunk_decay = jnp.exp(C * lg)

        def kv_body(n, _):
            r0 = pl.multiple_of(n * C, C)
            kt = jnp.where(hmask, ks_scr[pl.ds(r0, C), :], 0.0).T
            vc = v_ref[0, pl.ds(r0, C), hh * LANES:(hh + 1) * LANES]
            kv_scr[n] = _bdot(jnp.concatenate([kt * dec_kf_row, kt * dec_kb_row], axis=0), vc)
            return 0

        lax.fori_loop(0, N, kv_body, 0, unroll=4)

        def fwd_body(n, st):
            cur = kv_scr[n, 0:LANES, :]
            kv_scr[n, 0:LANES, :] = st
            return chunk_decay * st + cur

        lax.fori_loop(0, N, fwd_body, jnp.zeros((LANES, LANES), F32))

        def bwd_body(m, st):
            n = N - 1 - m
            cur = kv_scr[n, LANES:, :]
            kv_scr[n, LANES:, :] = st
            return chunk_decay * st + cur

        lax.fori_loop(0, N, bwd_body, jnp.zeros((LANES, LANES), F32))

        def out_body(n, _):
            r0 = pl.multiple_of(n * C, C)
            qc = qs_scr[pl.ds(r0, C), :]
            kc = jnp.where(hmask, ks_scr[pl.ds(r0, C), :], 0.0)
            vc = v_ref[0, pl.ds(r0, C), hh * LANES:(hh + 1) * LANES]
            scores = _bdot_nt(qc, kc) * intra
            o = _bdot(scores, vc) + _bdot(jnp.concatenate([qc * dec_qf, qc * dec_qb], axis=1), kv_scr[n])
            mu = jnp.mean(o, axis=-1, keepdims=True)
            var = jnp.mean(o * o, axis=-1, keepdims=True) - mu * mu
            o = (o - mu) * lax.rsqrt(var + EPS)
            gc = g_ref[0, pl.ds(r0, C), hh * LANES:(hh + 1) * LANES]
            o_ref[0, pl.ds(r0, C), hh * LANES:(hh + 1) * LANES] = (_silu(gc) * o).astype(o_ref.dtype)
            return 0

        lax.fori_loop(0, N, out_body, 0, unroll=4)


def _retention(proj, B, S, W, q_col, k_col, v_col, g_col, cosf, sinf, lg_tab):
    n_pairs = RET_HEADS // 2
    N = S // RET_CHUNK
    return pl.pallas_call(
        _ret_kernel,
        out_shape=jax.ShapeDtypeStruct((B, S, W), BF16),
        grid=(B, n_pairs),
        in_specs=[pl.BlockSpec((1, S, LANES), lambda b, p: (b, 0, q_col + p)),
                  pl.BlockSpec((1, S, LANES), lambda b, p: (b, 0, k_col + p)),
                  pl.BlockSpec((1, S, 2 * LANES), lambda b, p: (b, 0, v_col + p)),
                  pl.BlockSpec((1, S, 2 * LANES), lambda b, p: (b, 0, g_col + p)),
                  pl.BlockSpec((1, S, LANES), lambda b, p: (b, 0, 0)),
                  pl.BlockSpec((1, S, LANES), lambda b, p: (b, 0, 0)),
                  pl.BlockSpec((RET_HEADS, LANES), lambda b, p: (0, 0))],
        out_specs=pl.BlockSpec((1, S, 2 * LANES), lambda b, p: (b, 0, p)),
        scratch_shapes=[pltpu.VMEM((S, LANES), F32), pltpu.VMEM((S, LANES), F32),
                        pltpu.VMEM((N, 2 * LANES, LANES), F32)],
        compiler_params=_cparams("parallel", "parallel"),
        name="retention",
    )(proj, proj, proj, proj, cosf, sinf, lg_tab)


def _sgu_kernel(u_ref, v_ref, ng_ref, w_ref, bt_ref, o_ref):
    ts = u_ref.shape[1]
    C = SGU_CHUNK
    v = _gelu(v_ref[0])
    mu = jnp.mean(v, axis=-1, keepdims=True)
    var = jnp.mean(jnp.square(v - mu), axis=-1, keepdims=True)
    v = ((v - mu) * lax.rsqrt(var + EPS) * ng_ref[...]).astype(BF16)
    for c in range(ts // C):
        for g in range(SGU_GROUPS):
            vc = v[c * C:(c + 1) * C, g * LANES:(g + 1) * LANES]
            mixed = jnp.dot(w_ref[g].astype(BF16), vc, preferred_element_type=F32) + bt_ref[:, g:g + 1]
            uc = _gelu(u_ref[0, c * C:(c + 1) * C, g * LANES:(g + 1) * LANES])
            o_ref[0, c * C:(c + 1) * C, g * LANES:(g + 1) * LANES] = (uc * mixed).astype(o_ref.dtype)


def _sgu(proj, B, S, W, u_col, v_col, norm_g, w_s, b_s):
    ts = min(512, S)
    return pl.pallas_call(
        _sgu_kernel,
        out_shape=jax.ShapeDtypeStruct((B, S, W), BF16),
        grid=(B, S // ts),
        in_specs=[pl.BlockSpec((1, ts, W), lambda b, i: (b, i, u_col)),
                  pl.BlockSpec((1, ts, W), lambda b, i: (b, i, v_col)),
                  pl.BlockSpec((1, W), lambda b, i: (0, 0)),
                  pl.BlockSpec((SGU_GROUPS, SGU_CHUNK, SGU_CHUNK), lambda b, i: (0, 0, 0)),
                  pl.BlockSpec((SGU_CHUNK, SGU_GROUPS), lambda b, i: (0, 0))],
        out_specs=pl.BlockSpec((1, ts, W), lambda b, i: (b, i, 0)),
        compiler_params=_cparams("parallel", "parallel"),
        name="spatial_gating",
    )(proj, proj, norm_g.reshape(1, W), w_s, b_s.T)


def _head_sum(x, hd):
    seg = (_iota((LANES, LANES), 0) // hd == _iota((LANES, LANES), 1) // hd).astype(BF16)
    hi = x.astype(BF16)
    lo = (x - hi.astype(F32)).astype(BF16)
    parts = [jnp.dot(hi[:, j * LANES:(j + 1) * LANES], seg, preferred_element_type=F32)
             + jnp.dot(lo[:, j * LANES:(j + 1) * LANES], seg, preferred_element_type=F32)
             for j in range(x.shape[-1] // LANES)]
    return jnp.concatenate(parts, axis=1)


def _rwkv_pre_kernel(f_ref, fp_ref, fn_ref, mup_ref, mun_ref, w0_ref, wup_ref, a0_ref, aup_ref,
                     gup_ref, kk_ref, ka_ref, rk_ref,
                     r_out, kk_out, v_out, g_out, bonus_out, lw_out, bb_out, kt_out):
    ts = f_ref.shape[1]
    W = r_out.shape[2]
    i = pl.program_id(1)
    n_i = pl.num_programs(1)
    f = f_ref[0]
    rows = _iota(f.shape, 0)
    prev_row = jnp.where(i > 0, fp_ref[0, SUBLANES - 1:SUBLANES, :], 0.0)
    next_row = jnp.where(i < n_i - 1, fn_ref[0, 0:1, :], 0.0)
    prev = jnp.where(rows >= 1, pltpu.roll(f, 1, 0), prev_row)
    nxt = jnp.where(rows < ts - 1, pltpu.roll(f, ts - 1, 0), next_row)
    f = f + mup_ref[...] * (prev - f) + mun_ref[...] * (nxt - f)

    r = f[:, 0:W]
    k = f[:, W:2 * W]
    v = f[:, 2 * W:3 * W]
    o = 3 * W
    wd = jnp.tanh(f[:, o:o + 2 * DECAY_LORA])
    ad = f[:, o + 2 * DECAY_LORA:o + 2 * DECAY_LORA + 2 * ICLR_LORA]
    gd = f[:, o + 2 * DECAY_LORA + 2 * ICLR_LORA:]

    g_out[0] = _bdot(_sigmoid(gd), gup_ref[...])
    kk = k * kk_ref[...]
    kk = kk * lax.rsqrt(_head_sum(kk * kk, RWKV_HEAD_DIM) + EPS)
    r_out[0] = r.astype(r_out.dtype)
    kk_out[0] = kk.astype(kk_out.dtype)
    v_out[0] = v.astype(v_out.dtype)
    bonus = jnp.zeros_like(r)
    for z in range(2):
        w_raw = _bdot(wd[:, z * DECAY_LORA:(z + 1) * DECAY_LORA], wup_ref[z]) + w0_ref[z:z + 1, :]
        lw_out[z, 0] = -DECAY_SCALE * _sigmoid(w_raw)
        a = _sigmoid(_bdot(ad[:, z * ICLR_LORA:(z + 1) * ICLR_LORA], aup_ref[z]) + a0_ref[z:z + 1, :])
        bb_out[z, 0] = (a * kk).astype(bb_out.dtype)
        kt = k * (1.0 + (a - 1.0) * ka_ref[...])
        kt_out[z, 0] = kt.astype(kt_out.dtype)
        bonus = bonus + _head_sum(r * kt * rk_ref[...], RWKV_HEAD_DIM) * v
    bonus_out[0] = bonus


def _rwkv_pre(feat, mu_prev, mu_next, w0, w_up, a0, a_up, g_up, k_k, k_a, r_k):
    B, S, Fw = feat.shape
    W = w0.shape[1]
    ts = min(256, S)
    hb = ts // SUBLANES
    n_hb = S // SUBLANES
    row = lambda a: a.reshape(1, -1)
    full2 = lambda a: pl.BlockSpec(a.shape, lambda b, i: (0, 0))
    full3 = lambda a: pl.BlockSpec(a.shape, lambda b, i: (0, 0, 0))
    bsw = pl.BlockSpec((1, ts, W), lambda b, i: (b, i, 0))
    zsw = pl.BlockSpec((2, 1, ts, W), lambda b, i: (0, b, i, 0))
    sds = jax.ShapeDtypeStruct((B, S, W), F32)
    zds = jax.ShapeDtypeStruct((2, B, S, W), F32)
    sdh = jax.ShapeDtypeStruct((B, S, W), BF16)
    zdh = jax.ShapeDtypeStruct((2, B, S, W), BF16)
    args = (row(mu_prev), row(mu_next), w0, w_up, a0, a_up, g_up, row(k_k), row(k_a), row(r_k))
    specs = [full2(args[0]), full2(args[1]), full2(w0), full3(w_up), full2(a0), full3(a_up),
             full2(g_up), full2(args[7]), full2(args[8]), full2(args[9])]
    return pl.pallas_call(
        _rwkv_pre_kernel,
        out_shape=(sdh, sdh, sdh, sds, sds, zds, zdh, zdh),
        grid=(B, S // ts),
        in_specs=[pl.BlockSpec((1, ts, Fw), lambda b, i: (b, i, 0)),
                  pl.BlockSpec((1, SUBLANES, Fw), lambda b, i: (b, jnp.maximum(i * hb - 1, 0), 0)),
                  pl.BlockSpec((1, SUBLANES, Fw), lambda b, i: (b, jnp.minimum((i + 1) * hb, n_hb - 1), 0)),
                  ] + specs,
        out_specs=(bsw, bsw, bsw, bsw, bsw, zsw, zsw, zsw),
        compiler_params=_cparams("parallel", "parallel"),
        name="rwkv_prepare",
    )(feat, feat, feat, *args)


def _rwkv_scan_kernel(r_ref, kk_ref, v_ref, lw_ref, bb_ref, kt_ref, y_ref, st_scr):
    C = RWKV_CHUNK
    C2 = 2 * C
    z = pl.program_id(0)
    i = pl.program_id(2)
    n_pairs = r_ref.shape[2] // LANES
    fwd = z == 0

    @pl.when(i == 0)
    def _():
        st_scr[...] = jnp.zeros_like(st_scr)

    ti = _iota((C, C), 0)
    si = _iota((C, C), 1)
    sign = jnp.where(fwd, 1, -1)
    cum_mat = (sign * (ti - si) >= 0).astype(BF16)
    er = _iota((C2, C2), 0)
    ec = _iota((C2, C2), 1)
    lead = sign * (er % C - ec % C)
    strict = lead > 0
    incl = lead >= 0
    eye2 = (er == ec).astype(F32)
    row_head = _iota((C2, LANES), 0) // C
    lane_head = _iota((C2, LANES), 1) // RWKV_HEAD_DIM
    emask = row_head == lane_head
    diag_l = _iota((LANES, LANES), 0) == _iota((LANES, LANES), 1)

    def expand(x):
        return jnp.where(emask, jnp.concatenate([x, x], axis=0), 0.0)

    def bf(x):
        return x.astype(BF16)

    def mm(a, b):
        return jnp.dot(a, b, preferred_element_type=F32)

    pairs = range(n_pairs)
    sl = [slice(j * LANES, (j + 1) * LANES) for j in pairs]
    lw = [lw_ref[0, 0, :, s] for s in sl]
    def cumsum3(x):
        hi = bf(x)
        r1 = x - hi.astype(F32)
        mid = bf(r1)
        lo = bf(r1 - mid.astype(F32))
        return mm(cum_mat, hi) + (mm(cum_mat, mid) + mm(cum_mat, lo))

    cum = [cumsum3(x) for x in lw]
    tot = [jnp.sum(x, axis=0, keepdims=True) for x in lw]
    p_inv = [jnp.exp(-c) for c in cum]
    p_rest = [jnp.exp(t - c) for t, c in zip(tot, cum)]
    kh_e = [bf(expand(kk_ref[0, :, s] * jnp.exp(c - x))) for s, c, x in zip(sl, cum, lw)]
    rh_e = [bf(expand(r_ref[0, :, s] * jnp.exp(c))) for s, c in zip(sl, cum)]
    bh_e = [bf(expand(bb_ref[0, 0, :, s] * p)) for s, p in zip(sl, p_inv)]
    ktil_e = [bf(expand(kt_ref[0, 0, :, s] * p)) for s, p in zip(sl, p_inv)]
    b_rest_t = [bf(expand(bb_ref[0, 0, :, s] * p).T) for s, p in zip(sl, p_rest)]
    k_rest_t = [bf(expand(kt_ref[0, 0, :, s] * p).T) for s, p in zip(sl, p_rest)]
    v_e = [bf(expand(v_ref[0, :, s])) for s in sl]

    a_all = [lax.dot_general(jnp.concatenate([k, r], axis=0), jnp.concatenate([b, t], axis=0),
                             (((1,), (1,)), ((), ())), preferred_element_type=F32)
             for k, r, b, t in zip(kh_e, rh_e, bh_e, ktil_e)]
    a_b = [jnp.where(strict, a[0:C2, 0:C2], 0.0) for a in a_all]
    a_k = [jnp.where(strict, a[0:C2, C2:], 0.0) for a in a_all]
    a_rb = [jnp.where(incl, a[C2:, 0:C2], 0.0) for a in a_all]
    a_rk = [jnp.where(incl, a[C2:, C2:], 0.0) for a in a_all]

    pw = [bf(-a) for a in a_b]
    t_inv = [eye2] * n_pairs
    n_dbl = int(math.log2(C))
    for k in range(n_dbl):
        if k < n_dbl - 1:
            both = [mm(p, jnp.concatenate([p, bf(t)], axis=1)) for p, t in zip(pw, t_inv)]
            pw = [bf(x[:, 0:C2]) for x in both]
            t_inv = [t + x[:, C2:] for t, x in zip(t_inv, both)]
        else:
            t_inv = [t + mm(p, bf(t)) for p, t in zip(pw, t_inv)]

    akv = [mm(bf(jnp.concatenate([ak, ark], axis=0)), v)
           for ak, ark, v in zip(a_k, a_rk, v_e)]
    tw = [mm(bf(t), jnp.concatenate([k, bf(x[0:C2])], axis=1))
          for t, k, x in zip(t_inv, kh_e, akv)]
    st = [st_scr[j] for j in pairs]
    ws = [mm(jnp.concatenate([bf(x[:, 0:LANES]), r], axis=0), bf(s))
          for x, r, s in zip(tw, rh_e, st)]
    u_e = [x[0:C2] + t[:, LANES:] for x, t in zip(ws, tw)]
    y_e = [x[C2:] + k[C2:] - mm(bf(a), bf(u)) for x, k, a, u in zip(ws, akv, a_rb, u_e)]
    for s, y in zip(sl, y_e):
        y_ref[0, 0, :, s] = (y[0:C, :] + y[C:C2, :]).astype(y_ref.dtype)

    bw = [mm(b, bf(t)) for b, t in zip(b_rest_t, tw)]
    kv = [mm(k, v) for k, v in zip(k_rest_t, v_e)]
    for j in pairs:
        m_t = jnp.where(diag_l, jnp.exp(tot[j]), 0.0) - bw[j][:, 0:LANES]
        g_t = kv[j] - bw[j][:, LANES:]
        st_scr[j] = mm(bf(m_t), bf(st[j])) + g_t


def _rwkv_scan(r, kk, v, lw, bb, kt):
    B, S, W = r.shape
    C = RWKV_CHUNK
    nc = S // C
    chunk = lambda z, i: i + z * (nc - 1 - 2 * i)
    shared = pl.BlockSpec((1, C, W), lambda z, b, i: (b, chunk(z, i), 0))
    perdir = pl.BlockSpec((1, 1, C, W), lambda z, b, i: (z, b, chunk(z, i), 0))
    return pl.pallas_call(
        _rwkv_scan_kernel,
        out_shape=jax.ShapeDtypeStruct((2, B, S, W), BF16),
        grid=(2, B, nc),
        in_specs=[shared, shared, shared, perdir, perdir, perdir],
        out_specs=perdir,
        scratch_shapes=[pltpu.VMEM((W // LANES, LANES, LANES), F32)],
        compiler_params=_cparams("parallel", "parallel", "arbitrary"),
        name="rwkv_scan",
    )(r, kk, v, lw, bb, kt)


def _rwkv_post_kernel(y_ref, bonus_ref, g_ref, o_ref):
    hd = RWKV_HEAD_DIM
    acc = bonus_ref[0]
    for z in range(2):
        y = y_ref[z, 0].astype(F32)
        d = y - _head_sum(y, hd) * (1.0 / hd)
        var = _head_sum(d * d, hd) * (1.0 / hd)
        acc = acc + d * lax.rsqrt(var + EPS)
    o_ref[0] = (acc * g_ref[0]).astype(o_ref.dtype)


def _rwkv_post(y, bonus, g):
    _, B, S, W = y.shape
    ts = min(256, S)
    bsw = pl.BlockSpec((1, ts, W), lambda b, i: (b, i, 0))
    return pl.pallas_call(
        _rwkv_post_kernel,
        out_shape=jax.ShapeDtypeStruct((B, S, W), BF16),
        grid=(B, S // ts),
        in_specs=[pl.BlockSpec((2, 1, ts, W), lambda b, i: (0, b, i, 0)), bsw, bsw],
        out_specs=bsw,
        compiler_params=_cparams("parallel", "parallel"),
        name="rwkv_finish",
    )(y, bonus, g)


def _merge_kernel(ya_ref, yb_ref, yc_ref, yd_ref, wb_ref, l0_ref, l1_ref, l2_ref, l3_ref, o_ref,
                  wb_bf):
    @pl.when(pl.program_id(1) == 0)
    def _():
        wb_bf[...] = wb_ref[0].astype(BF16)

    acc = None
    for n, (y_ref, l_ref) in enumerate(((ya_ref, l0_ref), (yb_ref, l1_ref), (yc_ref, l2_ref),
                                        (yd_ref, l3_ref))):
        br = jnp.dot(y_ref[...], wb_bf[n], preferred_element_type=F32)
        t = _sigmoid(l_ref[...].astype(F32)) * br
        acc = t if acc is None else acc + t
    o_ref[...] = acc.astype(o_ref.dtype)


def _merge(ys, wb, layer, logits, tm, tn):
    T, W = ys[0].shape
    D = wb.shape[3]
    nj = D // tn
    yspec = pl.BlockSpec((tm, W), lambda j, i: (i, 0))
    lspec = lambda n: pl.BlockSpec((tm, tn), lambda j, i: (i, n * nj + j))
    return pl.pallas_call(
        _merge_kernel,
        out_shape=jax.ShapeDtypeStruct((T, D), BF16),
        grid=(nj, T // tm),
        in_specs=[yspec, yspec, yspec, yspec,
                  pl.BlockSpec((1, N_BRANCH, W, tn), lambda j, i: (layer, 0, 0, j)),
                  lspec(0), lspec(1), lspec(2), lspec(3)],
        out_specs=pl.BlockSpec((tm, tn), lambda j, i: (i, j)),
        scratch_shapes=[pltpu.VMEM((N_BRANCH, W, tn), BF16)],
        compiler_params=_cparams("parallel", "arbitrary"),
        name="branch_merge",
    )(*ys, wb, logits, logits, logits, logits)


def _router_kernel(h_ref, w_ref, b_ref, o_ref):
    logits = jnp.dot(_unpack_bf16_pairs(h_ref[...]), w_ref[...], preferred_element_type=F32) + b_ref[...]
    lane = _iota(logits.shape, 1)
    lane_f = lane.astype(F32)
    neg = -3.0e38
    far = float(LANES)
    is_grp = lane < N_GROUPS
    gl = jnp.where(is_grp, logits, neg)
    gmax = jnp.max(gl, axis=-1, keepdims=True)
    gsum = jnp.sum(jnp.where(is_grp, jnp.exp(gl - gmax), 0.0), axis=-1, keepdims=True)
    grp_p = 1.0 / gsum
    grp_i = jnp.min(jnp.where(gl == gmax, lane_f, far), axis=-1, keepdims=True)
    lo = N_GROUPS + grp_i * EXPERTS_PER_GROUP
    in_grp = jnp.logical_and(lane_f >= lo, lane_f < lo + EXPERTS_PER_GROUP)
    el = jnp.where(in_grp, logits, neg)
    m1 = jnp.max(el, axis=-1, keepdims=True)
    i1 = jnp.min(jnp.where(el == m1, lane_f, far), axis=-1, keepdims=True)
    el2 = jnp.where(lane_f == i1, neg, el)
    m2 = jnp.max(el2, axis=-1, keepdims=True)
    i2 = jnp.min(jnp.where(el2 == m2, lane_f, far), axis=-1, keepdims=True)
    t = jnp.exp(m2 - m1)
    w1 = grp_p / (1.0 + t)
    w2 = grp_p * t / (1.0 + t)
    o_ref[...] = jnp.where(lane == 0, w1, jnp.where(lane == 1, w2, jnp.where(
        lane == 2, i1 - N_GROUPS, jnp.where(lane == 3, i2 - N_GROUPS, 0.0))))


def _router(h, w_r, b_r):
    T = h.shape[0]
    D = w_r.shape[0]
    tm = min(1024, T)
    return pl.pallas_call(
        _router_kernel,
        out_shape=jax.ShapeDtypeStruct((T, LANES), F32),
        grid=(T // tm,),
        in_specs=[pl.BlockSpec((tm, D // 2), lambda i: (i, 0)),
                  pl.BlockSpec((D, LANES), lambda i: (0, 0)),
                  pl.BlockSpec((1, LANES), lambda i: (0, 0))],
        out_specs=pl.BlockSpec((tm, LANES), lambda i: (i, 0)),
        compiler_params=_cparams("parallel"),
        name="router",
    )(h, w_r, b_r)


def _expert_kernel(be_ref, nb_ref, first_ref, slot_ref, nxt_ref, xs_ref, wgu_hbm, wd_hbm, o_ref,
                   wgu_f32, wd_f32, wgu_bf, wd_bf, sem, *, layer):
    i = pl.program_id(0)
    used = i < nb_ref[0]

    def weight_copies(e, s):
        return (pltpu.make_async_copy(wgu_hbm.at[layer, e], wgu_f32.at[s], sem.at[s, 0]),
                pltpu.make_async_copy(wd_hbm.at[layer, e], wd_f32.at[s], sem.at[s, 1]))

    @pl.when(jnp.logical_and(used, first_ref[i] == 1))
    def _():
        s = slot_ref[i]

        @pl.when(i == 0)
        def _():
            for cp in weight_copies(be_ref[i], s):
                cp.start()

        for cp in weight_copies(be_ref[i], s):
            cp.wait()

        @pl.when(nxt_ref[i] >= 0)
        def _():
            for cp in weight_copies(nxt_ref[i], 1 - s):
                cp.start()

        wgu_bf[...] = wgu_f32[s].astype(BF16)
        wd_bf[...] = wd_f32[s].astype(BF16)

    @pl.when(used)
    def _():
        gu = jnp.dot(_unpack_bf16_pairs(xs_ref[...]), wgu_bf[...], preferred_element_type=F32)
        eh = gu.shape[1] // 2
        mid = (_silu(gu[:, :eh]) * gu[:, eh:]).astype(BF16)
        o_ref[...] = jnp.dot(mid, wd_bf[...], preferred_element_type=F32)

    @pl.when(jnp.logical_not(used))
    def _():
        o_ref[...] = jnp.zeros_like(o_ref)


def _expert_runs(block_expert, n_used):
    n_blocks = block_expert.shape[0]
    idx = jnp.arange(n_blocks, dtype=jnp.int32)
    changed = jnp.logical_or(idx == 0, block_expert != jnp.roll(block_expert, 1))
    first = jnp.logical_and(changed, idx < n_used[0]).astype(jnp.int32)
    slot = (jnp.cumsum(first) - 1) % 2
    start_pos = jnp.where(first == 1, idx, n_blocks)
    later = lax.cummin(start_pos, axis=0, reverse=True)
    next_pos = jnp.concatenate([later[1:], jnp.full((1,), n_blocks, jnp.int32)])
    nxt = jnp.where(next_pos < n_blocks, block_expert[jnp.minimum(next_pos, n_blocks - 1)], -1)
    return first, slot.astype(jnp.int32), nxt.astype(jnp.int32)


def _experts(xs, block_expert, n_used, w_gu, w_down, layer):
    cap = xs.shape[0]
    D, H2 = w_gu.shape[2], w_gu.shape[3]
    n_blocks = cap // MOE_BLOCK
    first, slot, nxt = _expert_runs(block_expert, n_used)
    grid_spec = pltpu.PrefetchScalarGridSpec(
        num_scalar_prefetch=5,
        grid=(n_blocks,),
        in_specs=[pl.BlockSpec((MOE_BLOCK, D // 2),
                               lambda i, be, nb, fi, sl, nx: (jnp.minimum(i, nb[0] - 1), 0)),
                  pl.BlockSpec(memory_space=pl.ANY),
                  pl.BlockSpec(memory_space=pl.ANY)],
        out_specs=pl.BlockSpec((MOE_BLOCK, D), lambda i, be, nb, fi, sl, nx: (i, 0)),
        scratch_shapes=[pltpu.VMEM((2, D, H2), F32), pltpu.VMEM((2, H2 // 2, D), F32),
                        pltpu.VMEM((D, H2), BF16), pltpu.VMEM((H2 // 2, D), BF16),
                        pltpu.SemaphoreType.DMA((2, 2))],
    )
    return pl.pallas_call(
        functools.partial(_expert_kernel, layer=layer),
        out_shape=jax.ShapeDtypeStruct((cap, D), F32),
        grid_spec=grid_spec,
        compiler_params=_cparams("arbitrary"),
        name="experts",
    )(block_expert, n_used, first, slot, nxt, xs, w_gu, w_down)


def _row_copy(src_ref, s, dst_ref, d, sem):
    return pltpu.make_async_copy(src_ref.at[pl.ds(s, 1), :], dst_ref.at[pl.ds(d, 1), :], sem)


def _rows_wait(src_ref, dst_ref, n, sem):
    pltpu.make_async_copy(src_ref.at[pl.ds(0, n), :], dst_ref.at[pl.ds(0, n), :], sem).wait()


def _combine_kernel(dest_ref, ys_ref, x_ref, r_ref, g_ref, ng_ref, o_ref, y0_buf, y1_buf, sem, *,
                    final_norm):
    tm = x_ref.shape[0]
    base = pl.program_id(0) * tm

    def issue(k, c):
        t = base + k
        _row_copy(ys_ref, dest_ref[TOP_K * t], y0_buf, k, sem.at[0]).start()
        _row_copy(ys_ref, dest_ref[TOP_K * t + 1], y1_buf, k, sem.at[1]).start()
        return c

    lax.fori_loop(0, tm, issue, 0, unroll=4)
    _rows_wait(ys_ref, y0_buf, tm, sem.at[0])
    _rows_wait(ys_ref, y1_buf, tm, sem.at[1])
    w0 = r_ref[:, 0:1]
    w1 = r_ref[:, 1:2]
    x = x_ref[...] + g_ref[0] * (w0 * y0_buf[...] + w1 * y1_buf[...])
    if final_norm:
        x = x * lax.rsqrt(jnp.mean(x * x, axis=-1, keepdims=True) + EPS) * ng_ref[...]
    o_ref[...] = x


def _combine(x2d, ys, dest, route, gate, rows_per_batch, norm_g, final_norm):
    T, D = x2d.shape
    tm = min(256, T)
    tiles_per_batch = rows_per_batch // tm
    spec = pl.BlockSpec((tm, D), lambda i, d: (i, 0))
    grid_spec = pltpu.PrefetchScalarGridSpec(
        num_scalar_prefetch=1,
        grid=(T // tm,),
        in_specs=[pl.BlockSpec(memory_space=pl.ANY), spec,
                  pl.BlockSpec((tm, LANES), lambda i, d: (i, 0)),
                  pl.BlockSpec((1, 1, D), lambda i, d: (i // tiles_per_batch, 0, 0)),
                  pl.BlockSpec((1, D), lambda i, d: (0, 0))],
        out_specs=spec,
        scratch_shapes=[pltpu.VMEM((tm, D), F32), pltpu.VMEM((tm, D), F32),
                        pltpu.SemaphoreType.DMA((2,))],
    )
    return pl.pallas_call(
        functools.partial(_combine_kernel, final_norm=final_norm),
        out_shape=jax.ShapeDtypeStruct((T, D), F32),
        grid_spec=grid_spec,
        compiler_params=_cparams("arbitrary"),
        name="moe_combine",
    )(dest, ys, x2d, route, gate, norm_g.reshape(1, D))


def _dispatch_kernel(dest_ref, h_ref, xs_in_ref, xs_ref, sem, *, tm):
    del xs_in_ref
    base = pl.program_id(0) * tm

    def issue(k, c):
        t = base + k
        for kk in range(TOP_K):
            _row_copy(h_ref, k, xs_ref, dest_ref[TOP_K * t + kk], sem.at[kk]).start()
        return c

    lax.fori_loop(0, tm, issue, 0, unroll=4)
    for kk in range(TOP_K):
        _rows_wait(h_ref, xs_ref, tm, sem.at[kk])


def _dispatch(h, dest, cap):
    T, Dp = h.shape
    tm = min(512, T)
    grid_spec = pltpu.PrefetchScalarGridSpec(
        num_scalar_prefetch=1,
        grid=(T // tm,),
        in_specs=[pl.BlockSpec((tm, Dp), lambda i, d: (i, 0)), pl.BlockSpec(memory_space=pl.ANY)],
        out_specs=pl.BlockSpec(memory_space=pl.ANY),
        scratch_shapes=[pltpu.SemaphoreType.DMA((TOP_K,))],
    )
    return pl.pallas_call(
        functools.partial(_dispatch_kernel, tm=tm),
        out_shape=jax.ShapeDtypeStruct((cap, Dp), h.dtype),
        grid_spec=grid_spec,
        input_output_aliases={2: 0},
        compiler_params=_cparams("arbitrary"),
        name="moe_dispatch",
    )(dest, h, jnp.zeros((cap, Dp), h.dtype))


def _rank_kernel(route_ref, rank_ref, cnt_ref, carry):
    tm = route_ref.shape[0]
    i = pl.program_id(0)

    @pl.when(i == 0)
    def _():
        carry[...] = jnp.zeros_like(carry)

    r = route_ref[...]
    lane = _iota(r.shape, 1)
    lane_f = lane.astype(F32)
    oh0 = (lane_f == r[:, 2:3]).astype(F32)
    oh1 = (lane_f == r[:, 3:4]).astype(F32)
    both = oh0 + oh1
    earlier = (_iota((tm, tm), 0) > _iota((tm, tm), 1)).astype(BF16)
    prefix = jnp.dot(earlier, both.astype(BF16), preferred_element_type=F32) + carry[0:1, :]
    rank0 = jnp.sum(prefix * oh0, axis=-1, keepdims=True)
    rank1 = jnp.sum((prefix + oh0) * oh1, axis=-1, keepdims=True)
    rank_ref[...] = jnp.where(lane == 0, rank0, jnp.where(lane == 1, rank1, 0.0))
    carry[...] = carry[...] + jnp.sum(both, axis=0, keepdims=True)
    cnt_ref[...] = carry[...]


def _dest_kernel(route_ref, rank_ref, cnt_ref, dest_ref):
    r = route_ref[...]
    lane = _iota(r.shape, 1)
    lane_f = lane.astype(F32)
    padded = jnp.floor((cnt_ref[...] + (MOE_BLOCK - 1.0)) * (1.0 / MOE_BLOCK)) * MOE_BLOCK
    before = (_iota((LANES, LANES), 0) < _iota((LANES, LANES), 1)).astype(F32)
    pad_start = jnp.dot(padded, before, precision=HI, preferred_element_type=F32)[0:1, :]
    oh0 = lane_f == r[:, 2:3]
    oh1 = lane_f == r[:, 3:4]
    d0 = jnp.sum(jnp.where(oh0, pad_start, 0.0), axis=-1, keepdims=True) + rank_ref[:, 0:1]
    d1 = jnp.sum(jnp.where(oh1, pad_start, 0.0), axis=-1, keepdims=True) + rank_ref[:, 1:2]
    dest_ref[...] = jnp.where(lane == 0, d0, jnp.where(lane == 1, d1, 0.0))


def _dispatch_plan(route):
    T = route.shape[0]
    tm = min(512, T)
    slab = pl.BlockSpec((tm, LANES), lambda i: (i, 0))
    cnt_spec = pl.BlockSpec((SUBLANES, LANES), lambda i: (0, 0))
    rank, cnt = pl.pallas_call(
        _rank_kernel,
        out_shape=(jax.ShapeDtypeStruct((T, LANES), F32), jax.ShapeDtypeStruct((SUBLANES, LANES), F32)),
        grid=(T // tm,),
        in_specs=[slab],
        out_specs=(slab, cnt_spec),
        scratch_shapes=[pltpu.VMEM((SUBLANES, LANES), F32)],
        compiler_params=_cparams("arbitrary"),
        name="moe_rank",
    )(route)
    dest = pl.pallas_call(
        _dest_kernel,
        out_shape=jax.ShapeDtypeStruct((T, LANES), F32),
        grid=(T // tm,),
        in_specs=[slab, slab, cnt_spec],
        out_specs=slab,
        compiler_params=_cparams("parallel"),
        name="moe_dest",
    )(route, rank, cnt)
    dest = dest[:, :TOP_K].astype(jnp.int32).reshape(T * TOP_K)
    counts = cnt[0, :N_EXPERTS].astype(jnp.int32)
    padded = (counts + MOE_BLOCK - 1) // MOE_BLOCK * MOE_BLOCK
    pad_end = jnp.cumsum(padded)
    n_blocks = -(-(T * TOP_K) // MOE_BLOCK) + N_EXPERTS
    block_row0 = jnp.arange(n_blocks, dtype=jnp.int32) * MOE_BLOCK
    block_expert = jnp.minimum(jnp.searchsorted(pad_end, block_row0, side="right"),
                               N_EXPERTS - 1).astype(jnp.int32)
    n_used = (pad_end[-1] // MOE_BLOCK).astype(jnp.int32).reshape(1)
    return dest, block_expert, n_used, n_blocks * MOE_BLOCK


def kernel(x, c, positions, norm1_g, norm2_g, ada_w, ada_b, w_in, lru_conv_w, lru_conv_b, lru_w_r, lru_b_r, lru_w_i, lru_b_i, lru_lambda, sgu_norm_g, sgu_w, sgu_b, rwkv_mu_prev, rwkv_mu_next, rwkv_w0, rwkv_w_up, rwkv_a0, rwkv_a_up, rwkv_g_up, rwkv_k_k, rwkv_k_a, rwkv_r_k, w_branch, w_out, router_grp_w, router_grp_b, router_exp_w, router_exp_b, expert_w_gu, expert_w_down, final_norm_g):
    B, S, D = x.shape
    L = ada_w.shape[0]
    W = D // 2
    T = B * S
    feat_w = rwkv_mu_prev.shape[1]

    c_pad = jnp.zeros((SUBLANES, D), F32).at[:B].set(c)
    mod = _ada(c_pad, ada_w, ada_b)

    dk = D // 2 // (2 * RET_HEADS)
    inv_freq = ROPE_THETA ** (-jnp.arange(0, dk, 2, dtype=F32) / dk)
    ang = positions.astype(F32)[..., None] * inv_freq
    cosf, sinf = _rope_tables(jnp.tile(ang, (1, 1, LANES // (dk // 2))))
    log_gamma = np.log1p(-np.exp2(-5.0 - np.arange(RET_HEADS, dtype=np.float64)))
    lg_tab = jnp.asarray(np.broadcast_to(log_gamma[:, None], (RET_HEADS, LANES)), F32)

    x2d = x.reshape(T, D)
    for l in range(L):
        m = mod[l, :B]
        sh1, sc1, g1, sh2, sc2, g2 = [m[:, None, k * D:(k + 1) * D] for k in range(6)]

        h = _normmod(x2d.reshape(B, S, D), norm1_g[l], sc1, sh1, BF16).reshape(T, D)
        n_a = 7 * W
        proj_a = _mm(h, w_in, l, 0, n_a, 2048, 512, name="proj_a").reshape(B, S, n_a)
        feat = _mm(h, w_in, l, n_a, feat_w, 1024, feat_w // 3, name="proj_rwkv").reshape(B, S, feat_w)
        logits = _mm(h, w_in, l, n_a + feat_w, N_BRANCH * D, 2048, 512, out_dtype=BF16,
                     name="proj_gate")

        y_a = _lru(proj_a, B, S, W, 0, W // LANES, lru_conv_w[l], lru_conv_b[l], lru_w_r[l],
                   lru_b_r[l], lru_w_i[l], lru_b_i[l], lru_lambda[l])
        y_b = _retention(proj_a, B, S, W, 2 * W // LANES, 5 * W // (2 * LANES),
                         3 * W // (2 * LANES), 4 * W // (2 * LANES), cosf, sinf, lg_tab)
        y_c = _sgu(proj_a, B, S, W, 5, 6, sgu_norm_g[l], sgu_w[l], sgu_b[l])
        r, kk, v, g, bonus, lw, bb, kt = _rwkv_pre(
            feat, rwkv_mu_prev[l], rwkv_mu_next[l], rwkv_w0[l], rwkv_w_up[l], rwkv_a0[l],
            rwkv_a_up[l], rwkv_g_up[l], rwkv_k_k[l], rwkv_k_a[l], rwkv_r_k[l].reshape(-1))
        y_d = _rwkv_post(_rwkv_scan(r, kk, v, lw, bb, kt), bonus, g)

        merged = _merge([y.reshape(T, W) for y in (y_a, y_b, y_c, y_d)],
                        w_branch, l, logits, 1024, 512)
        x2d = _mm_residual(merged, w_out, l, x2d, g1, S, 2048, 512)

        h2 = _normmod(x2d.reshape(B, S, D), norm2_g[l], sc2, sh2, jnp.uint32).reshape(T, D // 2)
        n_r = N_GROUPS + N_EXPERTS
        w_r = jnp.zeros((D, LANES), F32).at[:, :N_GROUPS].set(router_grp_w[l]).at[:, N_GROUPS:n_r].set(
            router_exp_w[l]).astype(BF16)
        b_r = jnp.zeros((1, LANES), F32).at[0, :N_GROUPS].set(router_grp_b[l]).at[0, N_GROUPS:n_r].set(
            router_exp_b[l])
        route = _router(h2, w_r, b_r)
        dest, block_expert, n_used, cap = _dispatch_plan(route)
        ys = _experts(_dispatch(h2, dest, cap), block_expert, n_used, expert_w_gu, expert_w_down, l)
        x2d = _combine(x2d, ys, dest, route, g2, S, final_norm_g, l == L - 1)

    return x2d.reshape(B, S, D)
```

```python
import functools
import math

import jax
import jax.numpy as jnp
import numpy as np
from jax import lax
from jax.experimental import pallas as pl
from jax.experimental.pallas import tpu as pltpu

F32 = jnp.float32
BF16 = jnp.bfloat16
EPS = 1e-6

LANES = 128
SUBLANES = 8
VMEM_LIMIT_BYTES = 56 * 1024 * 1024

LRU_BLOCKS = 8
CONV_WIDTH = 4
LRU_C = 8.0
RET_HEADS = 8
RET_CHUNK = 128
ROPE_THETA = 10000.0
SGU_GROUPS = 8
SGU_CHUNK = 128
RWKV_HEAD_DIM = 64
DECAY_LORA = 64
ICLR_LORA = 64
GATE_LORA = 128
DECAY_SCALE = math.exp(-0.5)
N_GROUPS = 4
EXPERTS_PER_GROUP = 8
N_EXPERTS = N_GROUPS * EXPERTS_PER_GROUP
TOP_K = 2
MOE_BLOCK = 256
N_BRANCH = 4

RWKV_CHUNK = 64
HI = lax.Precision.HIGHEST


def _cparams(*sem):
    return pltpu.CompilerParams(dimension_semantics=sem, vmem_limit_bytes=VMEM_LIMIT_BYTES)


def _bdot(a, b):
    return jnp.dot(a.astype(BF16), b.astype(BF16), preferred_element_type=F32)


def _bdot_nt(a, b):
    return lax.dot_general(a.astype(BF16), b.astype(BF16), (((1,), (1,)), ((), ())),
                           preferred_element_type=F32)


def _gelu(x):
    return 0.5 * x * (1.0 + jnp.tanh(math.sqrt(2.0 / math.pi) * (x + 0.044715 * (x * x * x))))


def _sigmoid(x):
    return 1.0 / (1.0 + jnp.exp(-x))


def _silu(x):
    return x * _sigmoid(x)


def _iota(shape, dim):
    return lax.broadcasted_iota(jnp.int32, shape, dim)


def _ada_kernel(c_ref, w_ref, b_ref, o_ref):
    cond = _silu(c_ref[...])
    o_ref[0] = _bdot(cond, w_ref[0]) + b_ref[0]


def _ada(c_pad, ada_w, ada_b):
    L, D, N = ada_w.shape
    R = c_pad.shape[0]
    tn = 1024
    return pl.pallas_call(
        _ada_kernel,
        out_shape=jax.ShapeDtypeStruct((L, R, N), F32),
        grid=(L, N // tn),
        in_specs=[pl.BlockSpec((R, D), lambda l, j: (0, 0)),
                  pl.BlockSpec((1, D, tn), lambda l, j: (l, 0, j)),
                  pl.BlockSpec((1, 1, tn), lambda l, j: (l, 0, j))],
        out_specs=pl.BlockSpec((1, R, tn), lambda l, j: (l, 0, j)),
        compiler_params=_cparams("parallel", "parallel"),
        name="ada_mod",
    )(c_pad, ada_w, ada_b.reshape(L, 1, N))


def _pack_bf16_pairs(y):
    half = y.shape[1] // 2
    bits = pltpu.bitcast(y.astype(BF16).astype(F32), jnp.uint32)
    return (bits[:, :half] >> 16) | (bits[:, half:] & jnp.uint32(0xFFFF0000))


def _unpack_bf16_pairs(u):
    lo = pltpu.bitcast(u << 16, F32)
    hi = pltpu.bitcast(u & jnp.uint32(0xFFFF0000), F32)
    return jnp.concatenate([lo, hi], axis=1).astype(BF16)


def _normmod_kernel(x_ref, g_ref, sc_ref, sh_ref, o_ref):
    x = x_ref[0]
    y = x * lax.rsqrt(jnp.mean(x * x, axis=-1, keepdims=True) + EPS) * g_ref[...]
    y = y * (1.0 + sc_ref[0]) + sh_ref[0]
    if o_ref.dtype == jnp.uint32:
        o_ref[0] = _pack_bf16_pairs(y)
    else:
        o_ref[0] = y.astype(o_ref.dtype)


def _normmod(x, g, sc, sh, out_dtype):
    B, S, D = x.shape
    ts = min(512, S)
    d_out = D // 2 if out_dtype == jnp.uint32 else D
    return pl.pallas_call(
        _normmod_kernel,
        out_shape=jax.ShapeDtypeStruct((B, S, d_out), out_dtype),
        grid=(B, S // ts),
        in_specs=[pl.BlockSpec((1, ts, D), lambda b, i: (b, i, 0)),
                  pl.BlockSpec((1, D), lambda b, i: (0, 0)),
                  pl.BlockSpec((1, 1, D), lambda b, i: (b, 0, 0)),
                  pl.BlockSpec((1, 1, D), lambda b, i: (b, 0, 0))],
        out_specs=pl.BlockSpec((1, ts, d_out), lambda b, i: (b, i, 0)),
        compiler_params=_cparams("parallel", "parallel"),
        name="norm_mod",
    )(x, g.reshape(1, D), sc, sh)


def _mm_kernel(a_ref, w_ref, o_ref):
    o_ref[...] = jnp.dot(a_ref[...], w_ref[0].astype(BF16),
                         preferred_element_type=F32).astype(o_ref.dtype)


def _mm(a, w, layer, col0, n_cols, tm, tn, out_dtype=F32, name="proj"):
    M, K = a.shape
    return pl.pallas_call(
        _mm_kernel,
        out_shape=jax.ShapeDtypeStruct((M, n_cols), out_dtype),
        grid=(M // tm, n_cols // tn),
        in_specs=[pl.BlockSpec((tm, K), lambda i, j: (i, 0)),
                  pl.BlockSpec((pl.Element(1), pl.Element(K), pl.Element(tn)),
                               lambda i, j: (layer, 0, pl.multiple_of(col0 + j * tn, LANES)))],
        out_specs=pl.BlockSpec((tm, tn), lambda i, j: (i, j)),
        compiler_params=_cparams("parallel", "parallel"),
        name=name,
    )(a, w)


def _mm_res_kernel(a_ref, w_ref, x_ref, g_ref, o_ref):
    y = jnp.dot(a_ref[...], w_ref[0].astype(BF16), preferred_element_type=F32)
    o_ref[...] = x_ref[...] + g_ref[0] * y


def _mm_residual(a, w, layer, x2d, gate, rows_per_batch, tm, tn, name="out_proj"):
    M, K = a.shape
    N = w.shape[2]
    tiles_per_batch = rows_per_batch // tm
    return pl.pallas_call(
        _mm_res_kernel,
        out_shape=jax.ShapeDtypeStruct((M, N), F32),
        grid=(M // tm, N // tn),
        in_specs=[pl.BlockSpec((tm, K), lambda i, j: (i, 0)),
                  pl.BlockSpec((1, K, tn), lambda i, j: (layer, 0, j)),
                  pl.BlockSpec((tm, tn), lambda i, j: (i, j)),
                  pl.BlockSpec((1, 1, tn), lambda i, j: (i // tiles_per_batch, 0, j))],
        out_specs=pl.BlockSpec((tm, tn), lambda i, j: (i, j)),
        compiler_params=_cparams("parallel", "parallel"),
        name=name,
    )(a, w, x2d, gate)


def _softplus(x):
    return jnp.maximum(x, 0.0) + jnp.log1p(jnp.exp(-jnp.abs(x)))


def _lru_kernel(x_ref, gate_ref, cw_ref, cb_ref, wr_ref, br_ref, wi_ref, bi_ref, lam_ref,
                o_ref, a_scr, u_scr, h_scr):
    S = x_ref.shape[1]
    x = x_ref[0]
    rows = _iota(x.shape, 0)
    xm2 = jnp.where(rows >= 2, pltpu.roll(x, 2, 0), 0.0)
    xm1 = jnp.where(rows >= 1, pltpu.roll(x, 1, 0), 0.0)
    xp1 = jnp.where(rows < S - 1, pltpu.roll(x, S - 1, 0), 0.0)
    xc = (cw_ref[0:1, :] * xm2 + cw_ref[1:2, :] * xm1 + cw_ref[2:3, :] * x
          + cw_ref[3:4, :] * xp1 + cb_ref[...])
    for z in range(2):
        r = _sigmoid(_bdot(xc, wr_ref[z, 0]) + br_ref[z:z + 1, :])
        i = _sigmoid(_bdot(xc, wi_ref[z, 0]) + bi_ref[z:z + 1, :])
        log_a = -LRU_C * r * _softplus(-lam_ref[z:z + 1, :])
        a = jnp.exp(log_a)
        a_scr[z] = a
        u_scr[z] = jnp.sqrt(-jnp.tanh(log_a) * (a * a + 1.0)) * i * xc

    n_tiles = S // SUBLANES
    L = x.shape[1]
    trow = _iota((SUBLANES, L), 0)

    def tile_scan(a, u, reverse):
        for d in (1, 2, 4):
            if reverse:
                keep = trow < SUBLANES - d
                a_s = jnp.where(keep, pltpu.roll(a, SUBLANES - d, 0), 1.0)
                u_s = jnp.where(keep, pltpu.roll(u, SUBLANES - d, 0), 0.0)
            else:
                keep = trow >= d
                a_s = jnp.where(keep, pltpu.roll(a, d, 0), 1.0)
                u_s = jnp.where(keep, pltpu.roll(u, d, 0), 0.0)
            u = u + a * u_s
            a = a * a_s
        return a, u

    def body(k, carry):
        hf, hb = carry
        f0 = pl.multiple_of(k * SUBLANES, SUBLANES)
        b0 = pl.multiple_of((n_tiles - 1 - k) * SUBLANES, SUBLANES)
        af, uf = tile_scan(a_scr[0, pl.ds(f0, SUBLANES), :], u_scr[0, pl.ds(f0, SUBLANES), :], False)
        ab, ub = tile_scan(a_scr[1, pl.ds(b0, SUBLANES), :], u_scr[1, pl.ds(b0, SUBLANES), :], True)
        tf = uf + af * hf
        tb = ub + ab * hb
        h_scr[0, pl.ds(f0, SUBLANES), :] = tf
        h_scr[1, pl.ds(b0, SUBLANES), :] = tb
        return tf[SUBLANES - 1:SUBLANES, :], tb[0:1, :]

    zero = jnp.zeros((1, L), F32)
    lax.fori_loop(0, n_tiles, body, (zero, zero), unroll=8)
    o_ref[0] = ((h_scr[0] + h_scr[1]) * _gelu(gate_ref[0])).astype(o_ref.dtype)


def _lru(proj, B, S, W, x_col, g_col, cw, cb, wr, br, wi, bi, lam):
    nb = W // LANES
    return pl.pallas_call(
        _lru_kernel,
        out_shape=jax.ShapeDtypeStruct((B, S, W), BF16),
        grid=(B, nb),
        in_specs=[pl.BlockSpec((1, S, LANES), lambda b, j: (b, 0, x_col + j)),
                  pl.BlockSpec((1, S, LANES), lambda b, j: (b, 0, g_col + j)),
                  pl.BlockSpec((CONV_WIDTH, LANES), lambda b, j: (0, j)),
                  pl.BlockSpec((1, LANES), lambda b, j: (0, j)),
                  pl.BlockSpec((2, 1, LANES, LANES), lambda b, j: (0, j, 0, 0)),
                  pl.BlockSpec((2, LANES), lambda b, j: (0, j)),
                  pl.BlockSpec((2, 1, LANES, LANES), lambda b, j: (0, j, 0, 0)),
                  pl.BlockSpec((2, LANES), lambda b, j: (0, j)),
                  pl.BlockSpec((2, LANES), lambda b, j: (0, j))],
        out_specs=pl.BlockSpec((1, S, LANES), lambda b, j: (b, 0, j)),
        scratch_shapes=[pltpu.VMEM((2, S, LANES), F32)] * 3,
        compiler_params=_cparams("parallel", "parallel"),
        name="rglru",
    )(proj, proj, cw, cb.reshape(1, W), wr, br, wi, bi, lam)


def _rope_kernel(ang_ref, cos_ref, sin_ref):
    ang = ang_ref[0]
    lane = _iota(ang.shape, 1)
    cos_ref[0] = jnp.cos(ang)
    s = jnp.sin(ang)
    sin_ref[0] = jnp.where((lane % 64) < 32, -s, s)


def _rope_tables(ang):
    B, S, _ = ang.shape
    ts = min(512, S)
    spec = pl.BlockSpec((1, ts, LANES), lambda b, i: (b, i, 0))
    return pl.pallas_call(
        _rope_kernel,
        out_shape=(jax.ShapeDtypeStruct((B, S, LANES), F32),) * 2,
        grid=(B, S // ts),
        in_specs=[spec],
        out_specs=(spec, spec),
        compiler_params=_cparams("parallel", "parallel"),
        name="rope_tables",
    )(ang)


def _ret_kernel(q_ref, k_ref, v_ref, g_ref, cos_ref, sin_ref, lg_ref, o_ref,
                qs_scr, ks_scr, kv_scr):
    S = q_ref.shape[1]
    C = RET_CHUNK
    N = S // C
    dk = 64
    p = pl.program_id(1)
    lane = _iota((S, LANES), 1)
    first_half = (lane % 64) < 32

    def rope(x):
        swapped = jnp.where(first_half, pltpu.roll(x, LANES - 32, 1), pltpu.roll(x, 32, 1))
        return x * cos_ref[0] + swapped * sin_ref[0]

    qs_scr[...] = rope(q_ref[0]) * (dk ** -0.5)
    ks_scr[...] = rope(k_ref[0])

    pos_r = _iota((C, 1), 0).astype(F32)
    pos_l = _iota((1, C), 1).astype(F32)
    ii = _iota((C, C), 0)
    jj = _iota((C, C), 1)
    dist = jnp.abs(ii - jj).astype(F32)
    clane = _iota((C, LANES), 1)

    for hh in range(2):
        lg = lg_ref[pl.ds(2 * p + hh, 1), :][:, 0:1]
        hmask = (clane // 64) == hh
        intra = jnp.exp(dist * lg)
        dec_kf_row = jnp.exp((C - 1.0 - pos_l) * lg)
        dec_kb_row = jnp.exp(pos_l * lg)
        dec_qf = jnp.exp((pos_r + 1.0) * lg)
        dec_qb = jnp.exp((C - pos_r) * lg)
        chunk_decay = jnp.exp(C * lg)

        def kv_body(n, _):
            r0 = pl.multiple_of(n * C, C)
            kt = jnp.where(hmask, ks_scr[pl.ds(r0, C), :], 0.0).T
            vc = v_ref[0, pl.ds(r0, C), hh * LANES:(hh + 1) * LANES]
            kv_scr[n] = _bdot(jnp.concatenate([kt * dec_kf_row, kt * dec_kb_row], axis=0), vc)
            return 0

        lax.fori_loop(0, N, kv_body, 0, unroll=4)

        def fwd_body(n, st):
            cur = kv_scr[n, 0:LANES, :]
            kv_scr[n, 0:LANES, :] = st
            return chunk_decay * st + cur

        lax.fori_loop(0, N, fwd_body, jnp.zeros((LANES, LANES), F32))

        def bwd_body(m, st):
            n = N - 1 - m
            cur = kv_scr[n, LANES:, :]
            kv_scr[n, LANES:, :] = st
            return chunk_decay * st + cur

        lax.fori_loop(0, N, bwd_body, jnp.zeros((LANES, LANES), F32))

        def out_body(n, _):
            r0 = pl.multiple_of(n * C, C)
            qc = qs_scr[pl.ds(r0, C), :]
            kc = jnp.where(hmask, ks_scr[pl.ds(r0, C), :], 0.0)
            vc = v_ref[0, pl.ds(r0, C), hh * LANES:(hh + 1) * LANES]
            scores = _bdot_nt(qc, kc) * intra
            o = _bdot(scores, vc) + _bdot(jnp.concatenate([qc * dec_qf, qc * dec_qb], axis=1), kv_scr[n])
            mu = jnp.mean(o, axis=-1, keepdims=True)
            var = jnp.mean(o * o, axis=-1, keepdims=True) - mu * mu
            o = (o - mu) * lax.rsqrt(var + EPS)
            gc = g_ref[0, pl.ds(r0, C), hh * LANES:(hh + 1) * LANES]
            o_ref[0, pl.ds(r0, C), hh * LANES:(hh + 1) * LANES] = (_silu(gc) * o).astype(o_ref.dtype)
            return 0

        lax.fori_loop(0, N, out_body, 0, unroll=4)


def _retention(proj, B, S, W, q_col, k_col, v_col, g_col, cosf, sinf, lg_tab):
    n_pairs = RET_HEADS // 2
    N = S // RET_CHUNK
    return pl.pallas_call(
        _ret_kernel,
        out_shape=jax.ShapeDtypeStruct((B, S, W), BF16),
        grid=(B, n_pairs),
        in_specs=[pl.BlockSpec((1, S, LANES), lambda b, p: (b, 0, q_col + p)),
                  pl.BlockSpec((1, S, LANES), lambda b, p: (b, 0, k_col + p)),
                  pl.BlockSpec((1, S, 2 * LANES), lambda b, p: (b, 0, v_col + p)),
                  pl.BlockSpec((1, S, 2 * LANES), lambda b, p: (b, 0, g_col + p)),
                  pl.BlockSpec((1, S, LANES), lambda b, p: (b, 0, 0)),
                  pl.BlockSpec((1, S, LANES), lambda b, p: (b, 0, 0)),
                  pl.BlockSpec((RET_HEADS, LANES), lambda b, p: (0, 0))],
        out_specs=pl.BlockSpec((1, S, 2 * LANES), lambda b, p: (b, 0, p)),
        scratch_shapes=[pltpu.VMEM((S, LANES), F32), pltpu.VMEM((S, LANES), F32),
                        pltpu.VMEM((N, 2 * LANES, LANES), F32)],
        compiler_params=_cparams("parallel", "parallel"),
        name="retention",
    )(proj, proj, proj, proj, cosf, sinf, lg_tab)


def _sgu_kernel(u_ref, v_ref, ng_ref, w_ref, bt_ref, o_ref):
    ts = u_ref.shape[1]
    C = SGU_CHUNK
    v = _gelu(v_ref[0])
    mu = jnp.mean(v, axis=-1, keepdims=True)
    var = jnp.mean(jnp.square(v - mu), axis=-1, keepdims=True)
    v = ((v - mu) * lax.rsqrt(var + EPS) * ng_ref[...]).astype(BF16)
    for c in range(ts // C):
        for g in range(SGU_GROUPS):
            vc = v[c * C:(c + 1) * C, g * LANES:(g + 1) * LANES]
            mixed = jnp.dot(w_ref[g].astype(BF16), vc, preferred_element_type=F32) + bt_ref[:, g:g + 1]
            uc = _gelu(u_ref[0, c * C:(c + 1) * C, g * LANES:(g + 1) * LANES])
            o_ref[0, c * C:(c + 1) * C, g * LANES:(g + 1) * LANES] = (uc * mixed).astype(o_ref.dtype)


def _sgu(proj, B, S, W, u_col, v_col, norm_g, w_s, b_s):
    ts = min(512, S)
    return pl.pallas_call(
        _sgu_kernel,
        out_shape=jax.ShapeDtypeStruct((B, S, W), BF16),
        grid=(B, S // ts),
        in_specs=[pl.BlockSpec((1, ts, W), lambda b, i: (b, i, u_col)),
                  pl.BlockSpec((1, ts, W), lambda b, i: (b, i, v_col)),
                  pl.BlockSpec((1, W), lambda b, i: (0, 0)),
                  pl.BlockSpec((SGU_GROUPS, SGU_CHUNK, SGU_CHUNK), lambda b, i: (0, 0, 0)),
                  pl.BlockSpec((SGU_CHUNK, SGU_GROUPS), lambda b, i: (0, 0))],
        out_specs=pl.BlockSpec((1, ts, W), lambda b, i: (b, i, 0)),
        compiler_params=_cparams("parallel", "parallel"),
        name="spatial_gating",
    )(proj, proj, norm_g.reshape(1, W), w_s, b_s.T)


def _head_sum(x, hd):
    seg = (_iota((LANES, LANES), 0) // hd == _iota((LANES, LANES), 1) // hd).astype(BF16)
    hi = x.astype(BF16)
    lo = (x - hi.astype(F32)).astype(BF16)
    parts = [jnp.dot(hi[:, j * LANES:(j + 1) * LANES], seg, preferred_element_type=F32)
             + jnp.dot(lo[:, j * LANES:(j + 1) * LANES], seg, preferred_element_type=F32)
             for j in range(x.shape[-1] // LANES)]
    return jnp.concatenate(parts, axis=1)


def _rwkv_pre_kernel(f_ref, fp_ref, fn_ref, mup_ref, mun_ref, w0_ref, wup_ref, a0_ref, aup_ref,
                     gup_ref, kk_ref, ka_ref, rk_ref,
                     r_out, kk_out, v_out, g_out, bonus_out, lw_out, bb_out, kt_out):
    ts = f_ref.shape[1]
    W = r_out.shape[2]
    i = pl.program_id(1)
    n_i = pl.num_programs(1)
    f = f_ref[0]
    rows = _iota(f.shape, 0)
    prev_row = jnp.where(i > 0, fp_ref[0, SUBLANES - 1:SUBLANES, :], 0.0)
    next_row = jnp.where(i < n_i - 1, fn_ref[0, 0:1, :], 0.0)
    prev = jnp.where(rows >= 1, pltpu.roll(f, 1, 0), prev_row)
    nxt = jnp.where(rows < ts - 1, pltpu.roll(f, ts - 1, 0), next_row)
    f = f + mup_ref[...] * (prev - f) + mun_ref[...] * (nxt - f)

    r = f[:, 0:W]
    k = f[:, W:2 * W]
    v = f[:, 2 * W:3 * W]
    o = 3 * W
    wd = jnp.tanh(f[:, o:o + 2 * DECAY_LORA])
    ad = f[:, o + 2 * DECAY_LORA:o + 2 * DECAY_LORA + 2 * ICLR_LORA]
    gd = f[:, o + 2 * DECAY_LORA + 2 * ICLR_LORA:]

    g_out[0] = _bdot(_sigmoid(gd), gup_ref[...])
    kk = k * kk_ref[...]
    kk = kk * lax.rsqrt(_head_sum(kk * kk, RWKV_HEAD_DIM) + EPS)
    r_out[0] = r.astype(r_out.dtype)
    kk_out[0] = kk.astype(kk_out.dtype)
    v_out[0] = v.astype(v_out.dtype)
    bonus = jnp.zeros_like(r)
    for z in range(2):
        w_raw = _bdot(wd[:, z * DECAY_LORA:(z + 1) * DECAY_LORA], wup_ref[z]) + w0_ref[z:z + 1, :]
        lw_out[z, 0] = -DECAY_SCALE * _sigmoid(w_raw)
        a = _sigmoid(_bdot(ad[:, z * ICLR_LORA:(z + 1) * ICLR_LORA], aup_ref[z]) + a0_ref[z:z + 1, :])
        bb_out[z, 0] = (a * kk).astype(bb_out.dtype)
        kt = k * (1.0 + (a - 1.0) * ka_ref[...])
        kt_out[z, 0] = kt.astype(kt_out.dtype)
        bonus = bonus + _head_sum(r * kt * rk_ref[...], RWKV_HEAD_DIM) * v
    bonus_out[0] = bonus


def _rwkv_pre(feat, mu_prev, mu_next, w0, w_up, a0, a_up, g_up, k_k, k_a, r_k):
    B, S, Fw = feat.shape
    W = w0.shape[1]
    ts = min(256, S)
    hb = ts // SUBLANES
    n_hb = S // SUBLANES
    row = lambda a: a.reshape(1, -1)
    full2 = lambda a: pl.BlockSpec(a.shape, lambda b, i: (0, 0))
    full3 = lambda a: pl.BlockSpec(a.shape, lambda b, i: (0, 0, 0))
    bsw = pl.BlockSpec((1, ts, W), lambda b, i: (b, i, 0))
    zsw = pl.BlockSpec((2, 1, ts, W), lambda b, i: (0, b, i, 0))
    sds = jax.ShapeDtypeStruct((B, S, W), F32)
    zds = jax.ShapeDtypeStruct((2, B, S, W), F32)
    sdh = jax.ShapeDtypeStruct((B, S, W), BF16)
    zdh = jax.ShapeDtypeStruct((2, B, S, W), BF16)
    args = (row(mu_prev), row(mu_next), w0, w_up, a0, a_up, g_up, row(k_k), row(k_a), row(r_k))
    specs = [full2(args[0]), full2(args[1]), full2(w0), full3(w_up), full2(a0), full3(a_up),
             full2(g_up), full2(args[7]), full2(args[8]), full2(args[9])]
    return pl.pallas_call(
        _rwkv_pre_kernel,
        out_shape=(sdh, sdh, sdh, sds, sds, zds, zdh, zdh),
        grid=(B, S // ts),
        in_specs=[pl.BlockSpec((1, ts, Fw), lambda b, i: (b, i, 0)),
                  pl.BlockSpec((1, SUBLANES, Fw), lambda b, i: (b, jnp.maximum(i * hb - 1, 0), 0)),
                  pl.BlockSpec((1, SUBLANES, Fw), lambda b, i: (b, jnp.minimum((i + 1) * hb, n_hb - 1), 0)),
                  ] + specs,
        out_specs=(bsw, bsw, bsw, bsw, bsw, zsw, zsw, zsw),
        compiler_params=_cparams("parallel", "parallel"),
        name="rwkv_prepare",
    )(feat, feat, feat, *args)


def _rwkv_scan_kernel(rf_ref, kkf_ref, vf_ref, rb_ref, kkb_ref, vb_ref,
                      lwf_ref, bbf_ref, ktf_ref, lwb_ref, bbb_ref, ktb_ref, yf_ref, yb_ref, st_scr):
    C = RWKV_CHUNK
    C2 = 2 * C
    n_pairs = rf_ref.shape[2] // LANES

    @pl.when(pl.program_id(1) == 0)
    def _():
        st_scr[...] = jnp.zeros_like(st_scr)

    wide_t = _iota((C, C2), 0)
    wide_s = _iota((C, C2), 1) % C
    eye_w = (wide_t == wide_s).astype(F32)
    emask = (_iota((C2, LANES), 0) // C) == (_iota((C2, LANES), 1) // RWKV_HEAD_DIM)
    bmask = (_iota((C2, C2), 0) // C) == (_iota((C2, C2), 1) // C)
    diag_l = _iota((LANES, LANES), 0) == _iota((LANES, LANES), 1)
    tri = _iota((C, C), 0) - _iota((C, C), 1)
    cum_mat = [(sg * tri >= 0).astype(BF16) for sg in (1, -1)]
    strict = [sg * (wide_t - wide_s) > 0 for sg in (1, -1)]
    incl = [sg * (wide_t - wide_s) >= 0 for sg in (1, -1)]
    refs = [(rf_ref, kkf_ref, vf_ref, lwf_ref, bbf_ref, ktf_ref, yf_ref),
            (rb_ref, kkb_ref, vb_ref, lwb_ref, bbb_ref, ktb_ref, yb_ref)]

    def expand(x):
        return jnp.where(emask, jnp.concatenate([x, x], axis=0), 0.0)

    def blockdiag(a):
        return jnp.where(bmask, jnp.concatenate([a, a], axis=0), 0.0)

    def bf(x):
        return x.astype(BF16)

    def mm(a, b):
        return jnp.dot(a, b, preferred_element_type=F32)

    def cumsum3(m, x):
        hi = bf(x)
        r1 = x - hi.astype(F32)
        mid = bf(r1)
        lo = bf(r1 - mid.astype(F32))
        return mm(m, hi) + (mm(m, mid) + mm(m, lo))

    units = [(z, j) for z in range(2) for j in range(n_pairs)]
    zs = [z for z, _ in units]
    sl = [slice(j * LANES, (j + 1) * LANES) for _, j in units]
    r_in = [refs[z][0][0, :, s] for z, s in zip(zs, sl)]
    kk_in = [refs[z][1][0, :, s] for z, s in zip(zs, sl)]
    v_in = [refs[z][2][0, :, s] for z, s in zip(zs, sl)]
    lw = [refs[z][3][0, 0, :, s] for z, s in zip(zs, sl)]
    bb_in = [refs[z][4][0, 0, :, s] for z, s in zip(zs, sl)]
    kt_in = [refs[z][5][0, 0, :, s] for z, s in zip(zs, sl)]

    cum = [cumsum3(cum_mat[z], x) for z, x in zip(zs, lw)]
    tot = [jnp.sum(x, axis=0, keepdims=True) for x in lw]
    p_inv = [jnp.exp(-c) for c in cum]
    p_rest = [jnp.exp(t - c) for t, c in zip(tot, cum)]
    kh_c = [bf(k * jnp.exp(c - x)) for k, c, x in zip(kk_in, cum, lw)]
    rh_c = [bf(r * jnp.exp(c)) for r, c in zip(r_in, cum)]
    kh_e = [expand(k) for k in kh_c]
    bh_e = [expand(bf(b * p)) for b, p in zip(bb_in, p_inv)]
    ktil_e = [expand(bf(k * p)) for k, p in zip(kt_in, p_inv)]
    b_rest_t = [bf(expand(b * p).T) for b, p in zip(bb_in, p_rest)]
    k_rest_t = [bf(expand(k * p).T) for k, p in zip(kt_in, p_rest)]
    v_e = [expand(v) for v in v_in]

    a_all = [lax.dot_general(jnp.concatenate([k, r], axis=0), jnp.concatenate([b, t], axis=0),
                             (((1,), (1,)), ((), ())), preferred_element_type=F32)
             for k, r, b, t in zip(kh_c, rh_c, bh_e, ktil_e)]
    a_b = [jnp.where(strict[z], a[0:C, 0:C2], 0.0) for z, a in zip(zs, a_all)]
    a_k = [jnp.where(strict[z], a[0:C, C2:], 0.0) for z, a in zip(zs, a_all)]
    a_rb = [jnp.where(incl[z], a[C:, 0:C2], 0.0) for z, a in zip(zs, a_all)]
    a_rk = [jnp.where(incl[z], a[C:, C2:], 0.0) for z, a in zip(zs, a_all)]

    pw = [bf(-a) for a in a_b]
    t_inv = [eye_w] * len(units)
    n_dbl = int(math.log2(C))
    for k in range(n_dbl):
        if k < n_dbl - 1:
            both = [mm(p, jnp.concatenate([blockdiag(p), blockdiag(bf(t))], axis=1))
                    for p, t in zip(pw, t_inv)]
            pw = [bf(x[:, 0:C2]) for x in both]
            t_inv = [t + x[:, C2:] for t, x in zip(t_inv, both)]
        else:
            t_inv = [t + mm(p, blockdiag(bf(t))) for p, t in zip(pw, t_inv)]

    akv = [mm(bf(jnp.concatenate([ak, ark], axis=0)), v)
           for ak, ark, v in zip(a_k, a_rk, v_e)]
    tw = [mm(bf(t), jnp.concatenate([k, expand(bf(x[0:C]))], axis=1))
          for t, k, x in zip(t_inv, kh_e, akv)]
    st = [st_scr[z, j] for z, j in units]
    ws = [mm(jnp.concatenate([bf(x[:, 0:LANES]), r], axis=0), bf(s))
          for x, r, s in zip(tw, rh_c, st)]
    u_c = [x[0:C] + t[:, LANES:] for x, t in zip(ws, tw)]
    y_c = [x[C:] + k[C:] - mm(bf(a), expand(bf(u))) for x, k, a, u in zip(ws, akv, a_rb, u_c)]
    for z, s, y in zip(zs, sl, y_c):
        refs[z][6][0, :, s] = y.astype(refs[z][6].dtype)

    wu_e = [jnp.concatenate([expand(bf(t[:, 0:LANES])), expand(bf(t[:, LANES:]))], axis=1) for t in tw]
    bw = [mm(b, x) for b, x in zip(b_rest_t, wu_e)]
    kv = [mm(k, v) for k, v in zip(k_rest_t, v_e)]
    for n, (z, j) in enumerate(units):
        m_t = jnp.where(diag_l, jnp.exp(tot[n]), 0.0) - bw[n][:, 0:LANES]
        g_t = kv[n] - bw[n][:, LANES:]
        st_scr[z, j] = mm(bf(m_t), bf(st[n])) + g_t


def _rwkv_scan(r, kk, v, lw, bb, kt):
    B, S, W = r.shape
    C = RWKV_CHUNK
    nc = S // C
    fwd = pl.BlockSpec((1, C, W), lambda b, i: (b, i, 0))
    bwd = pl.BlockSpec((1, C, W), lambda b, i: (b, nc - 1 - i, 0))
    fwd_z = pl.BlockSpec((1, 1, C, W), lambda b, i: (0, b, i, 0))
    bwd_z = pl.BlockSpec((1, 1, C, W), lambda b, i: (1, b, nc - 1 - i, 0))
    sds = jax.ShapeDtypeStruct((B, S, W), BF16)
    return pl.pallas_call(
        _rwkv_scan_kernel,
        out_shape=(sds, sds),
        grid=(B, nc),
        in_specs=[fwd, fwd, fwd, bwd, bwd, bwd, fwd_z, fwd_z, fwd_z, bwd_z, bwd_z, bwd_z],
        out_specs=(fwd, bwd),
        scratch_shapes=[pltpu.VMEM((2, W // LANES, LANES, LANES), F32)],
        compiler_params=_cparams("parallel", "arbitrary"),
        name="rwkv_scan",
    )(r, kk, v, r, kk, v, lw, bb, kt, lw, bb, kt)


def _rwkv_post_kernel(yf_ref, yb_ref, bonus_ref, g_ref, o_ref):
    hd = RWKV_HEAD_DIM
    acc = bonus_ref[0]
    for y_ref in (yf_ref, yb_ref):
        y = y_ref[0].astype(F32)
        d = y - _head_sum(y, hd) * (1.0 / hd)
        var = _head_sum(d * d, hd) * (1.0 / hd)
        acc = acc + d * lax.rsqrt(var + EPS)
    o_ref[0] = (acc * g_ref[0]).astype(o_ref.dtype)


def _rwkv_post(y_fwd, y_bwd, bonus, g):
    B, S, W = y_fwd.shape
    ts = min(256, S)
    bsw = pl.BlockSpec((1, ts, W), lambda b, i: (b, i, 0))
    return pl.pallas_call(
        _rwkv_post_kernel,
        out_shape=jax.ShapeDtypeStruct((B, S, W), BF16),
        grid=(B, S // ts),
        in_specs=[bsw, bsw, bsw, bsw],
        out_specs=bsw,
        compiler_params=_cparams("parallel", "parallel"),
        name="rwkv_finish",
    )(y_fwd, y_bwd, bonus, g)


def _merge_kernel(ya_ref, yb_ref, yc_ref, yd_ref, wb_ref, l0_ref, l1_ref, l2_ref, l3_ref, o_ref,
                  wb_bf):
    @pl.when(pl.program_id(1) == 0)
    def _():
        wb_bf[...] = wb_ref[0].astype(BF16)

    acc = None
    for n, (y_ref, l_ref) in enumerate(((ya_ref, l0_ref), (yb_ref, l1_ref), (yc_ref, l2_ref),
                                        (yd_ref, l3_ref))):
        br = jnp.dot(y_ref[...], wb_bf[n], preferred_element_type=F32)
        t = _sigmoid(l_ref[...].astype(F32)) * br
        acc = t if acc is None else acc + t
    o_ref[...] = acc.astype(o_ref.dtype)


def _merge(ys, wb, layer, logits, tm, tn):
    T, W = ys[0].shape
    D = wb.shape[3]
    nj = D // tn
    yspec = pl.BlockSpec((tm, W), lambda j, i: (i, 0))
    lspec = lambda n: pl.BlockSpec((tm, tn), lambda j, i: (i, n * nj + j))
    return pl.pallas_call(
        _merge_kernel,
        out_shape=jax.ShapeDtypeStruct((T, D), BF16),
        grid=(nj, T // tm),
        in_specs=[yspec, yspec, yspec, yspec,
                  pl.BlockSpec((1, N_BRANCH, W, tn), lambda j, i: (layer, 0, 0, j)),
                  lspec(0), lspec(1), lspec(2), lspec(3)],
        out_specs=pl.BlockSpec((tm, tn), lambda j, i: (i, j)),
        scratch_shapes=[pltpu.VMEM((N_BRANCH, W, tn), BF16)],
        compiler_params=_cparams("parallel", "arbitrary"),
        name="branch_merge",
    )(*ys, wb, logits, logits, logits, logits)


def _router_kernel(h_ref, w_ref, b_ref, o_ref):
    logits = jnp.dot(_unpack_bf16_pairs(h_ref[...]), w_ref[...], preferred_element_type=F32) + b_ref[...]
    lane = _iota(logits.shape, 1)
    lane_f = lane.astype(F32)
    neg = -3.0e38
    far = float(LANES)
    is_grp = lane < N_GROUPS
    gl = jnp.where(is_grp, logits, neg)
    gmax = jnp.max(gl, axis=-1, keepdims=True)
    gsum = jnp.sum(jnp.where(is_grp, jnp.exp(gl - gmax), 0.0), axis=-1, keepdims=True)
    grp_p = 1.0 / gsum
    grp_i = jnp.min(jnp.where(gl == gmax, lane_f, far), axis=-1, keepdims=True)
    lo = N_GROUPS + grp_i * EXPERTS_PER_GROUP
    in_grp = jnp.logical_and(lane_f >= lo, lane_f < lo + EXPERTS_PER_GROUP)
    el = jnp.where(in_grp, logits, neg)
    m1 = jnp.max(el, axis=-1, keepdims=True)
    i1 = jnp.min(jnp.where(el == m1, lane_f, far), axis=-1, keepdims=True)
    el2 = jnp.where(lane_f == i1, neg, el)
    m2 = jnp.max(el2, axis=-1, keepdims=True)
    i2 = jnp.min(jnp.where(el2 == m2, lane_f, far), axis=-1, keepdims=True)
    t = jnp.exp(m2 - m1)
    w1 = grp_p / (1.0 + t)
    w2 = grp_p * t / (1.0 + t)
    o_ref[...] = jnp.where(lane == 0, w1, jnp.where(lane == 1, w2, jnp.where(
        lane == 2, i1 - N_GROUPS, jnp.where(lane == 3, i2 - N_GROUPS, 0.0))))


def _router(h, w_r, b_r):
    T = h.shape[0]
    D = w_r.shape[0]
    tm = min(1024, T)
    return pl.pallas_call(
        _router_kernel,
        out_shape=jax.ShapeDtypeStruct((T, LANES), F32),
        grid=(T // tm,),
        in_specs=[pl.BlockSpec((tm, D // 2), lambda i: (i, 0)),
                  pl.BlockSpec((D, LANES), lambda i: (0, 0)),
                  pl.BlockSpec((1, LANES), lambda i: (0, 0))],
        out_specs=pl.BlockSpec((tm, LANES), lambda i: (i, 0)),
        compiler_params=_cparams("parallel"),
        name="router",
    )(h, w_r, b_r)


def _expert_kernel(be_ref, nb_ref, first_ref, slot_ref, nxt_ref, xs_ref, wgu_hbm, wd_hbm, o_ref,
                   wgu_f32, wd_f32, wgu_bf, wd_bf, sem, *, layer):
    i = pl.program_id(0)
    used = i < nb_ref[0]

    def weight_copies(e, s):
        return (pltpu.make_async_copy(wgu_hbm.at[layer, e], wgu_f32.at[s], sem.at[s, 0]),
                pltpu.make_async_copy(wd_hbm.at[layer, e], wd_f32.at[s], sem.at[s, 1]))

    @pl.when(jnp.logical_and(used, first_ref[i] == 1))
    def _():
        s = slot_ref[i]

        @pl.when(i == 0)
        def _():
            for cp in weight_copies(be_ref[i], s):
                cp.start()

        for cp in weight_copies(be_ref[i], s):
            cp.wait()

        @pl.when(nxt_ref[i] >= 0)
        def _():
            for cp in weight_copies(nxt_ref[i], 1 - s):
                cp.start()

        wgu_bf[...] = wgu_f32[s].astype(BF16)
        wd_bf[...] = wd_f32[s].astype(BF16)

    @pl.when(used)
    def _():
        gu = jnp.dot(_unpack_bf16_pairs(xs_ref[...]), wgu_bf[...], preferred_element_type=F32)
        eh = gu.shape[1] // 2
        mid = (_silu(gu[:, :eh]) * gu[:, eh:]).astype(BF16)
        o_ref[...] = jnp.dot(mid, wd_bf[...], preferred_element_type=F32)

    @pl.when(jnp.logical_not(used))
    def _():
        o_ref[...] = jnp.zeros_like(o_ref)


def _expert_runs(block_expert, n_used):
    n_blocks = block_expert.shape[0]
    idx = jnp.arange(n_blocks, dtype=jnp.int32)
    changed = jnp.logical_or(idx == 0, block_expert != jnp.roll(block_expert, 1))
    first = jnp.logical_and(changed, idx < n_used[0]).astype(jnp.int32)
    slot = (jnp.cumsum(first) - 1) % 2
    start_pos = jnp.where(first == 1, idx, n_blocks)
    later = lax.cummin(start_pos, axis=0, reverse=True)
    next_pos = jnp.concatenate([later[1:], jnp.full((1,), n_blocks, jnp.int32)])
    nxt = jnp.where(next_pos < n_blocks, block_expert[jnp.minimum(next_pos, n_blocks - 1)], -1)
    return first, slot.astype(jnp.int32), nxt.astype(jnp.int32)


def _experts(xs, block_expert, n_used, w_gu, w_down, layer):
    cap = xs.shape[0]
    D, H2 = w_gu.shape[2], w_gu.shape[3]
    n_blocks = cap // MOE_BLOCK
    first, slot, nxt = _expert_runs(block_expert, n_used)
    grid_spec = pltpu.PrefetchScalarGridSpec(
        num_scalar_prefetch=5,
        grid=(n_blocks,),
        in_specs=[pl.BlockSpec((MOE_BLOCK, D // 2),
                               lambda i, be, nb, fi, sl, nx: (jnp.minimum(i, nb[0] - 1), 0)),
                  pl.BlockSpec(memory_space=pl.ANY),
                  pl.BlockSpec(memory_space=pl.ANY)],
        out_specs=pl.BlockSpec((MOE_BLOCK, D), lambda i, be, nb, fi, sl, nx: (i, 0)),
        scratch_shapes=[pltpu.VMEM((2, D, H2), F32), pltpu.VMEM((2, H2 // 2, D), F32),
                        pltpu.VMEM((D, H2), BF16), pltpu.VMEM((H2 // 2, D), BF16),
                        pltpu.SemaphoreType.DMA((2, 2))],
    )
    return pl.pallas_call(
        functools.partial(_expert_kernel, layer=layer),
        out_shape=jax.ShapeDtypeStruct((cap, D), F32),
        grid_spec=grid_spec,
        compiler_params=_cparams("arbitrary"),
        name="experts",
    )(block_expert, n_used, first, slot, nxt, xs, w_gu, w_down)


def _row_copy(src_ref, s, dst_ref, d, sem):
    return pltpu.make_async_copy(src_ref.at[pl.ds(s, 1), :], dst_ref.at[pl.ds(d, 1), :], sem)


def _rows_wait(src_ref, dst_ref, n, sem):
    pltpu.make_async_copy(src_ref.at[pl.ds(0, n), :], dst_ref.at[pl.ds(0, n), :], sem).wait()


def _combine_kernel(dest_ref, ys_ref, x_ref, r_ref, g_ref, ng_ref, o_ref, y0_buf, y1_buf, sem, *,
                    final_norm):
    tm = x_ref.shape[0]
    base = pl.program_id(0) * tm

    def issue(k, c):
        t = base + k
        _row_copy(ys_ref, dest_ref[TOP_K * t], y0_buf, k, sem.at[0]).start()
        _row_copy(ys_ref, dest_ref[TOP_K * t + 1], y1_buf, k, sem.at[1]).start()
        return c

    lax.fori_loop(0, tm, issue, 0, unroll=4)
    _rows_wait(ys_ref, y0_buf, tm, sem.at[0])
    _rows_wait(ys_ref, y1_buf, tm, sem.at[1])
    w0 = r_ref[:, 0:1]
    w1 = r_ref[:, 1:2]
    x = x_ref[...] + g_ref[0] * (w0 * y0_buf[...] + w1 * y1_buf[...])
    if final_norm:
        x = x * lax.rsqrt(jnp.mean(x * x, axis=-1, keepdims=True) + EPS) * ng_ref[...]
    o_ref[...] = x


def _combine(x2d, ys, dest, route, gate, rows_per_batch, norm_g, final_norm):
    T, D = x2d.shape
    tm = min(256, T)
    tiles_per_batch = rows_per_batch // tm
    spec = pl.BlockSpec((tm, D), lambda i, d: (i, 0))
    grid_spec = pltpu.PrefetchScalarGridSpec(
        num_scalar_prefetch=1,
        grid=(T // tm,),
        in_specs=[pl.BlockSpec(memory_space=pl.ANY), spec,
                  pl.BlockSpec((tm, LANES), lambda i, d: (i, 0)),
                  pl.BlockSpec((1, 1, D), lambda i, d: (i // tiles_per_batch, 0, 0)),
                  pl.BlockSpec((1, D), lambda i, d: (0, 0))],
        out_specs=spec,
        scratch_shapes=[pltpu.VMEM((tm, D), F32), pltpu.VMEM((tm, D), F32),
                        pltpu.SemaphoreType.DMA((2,))],
    )
    return pl.pallas_call(
        functools.partial(_combine_kernel, final_norm=final_norm),
        out_shape=jax.ShapeDtypeStruct((T, D), F32),
        grid_spec=grid_spec,
        compiler_params=_cparams("arbitrary"),
        name="moe_combine",
    )(dest, ys, x2d, route, gate, norm_g.reshape(1, D))


def _dispatch_kernel(dest_ref, h_ref, xs_in_ref, xs_ref, sem, *, tm):
    del xs_in_ref
    base = pl.program_id(0) * tm

    def issue(k, c):
        t = base + k
        for kk in range(TOP_K):
            _row_copy(h_ref, k, xs_ref, dest_ref[TOP_K * t + kk], sem.at[kk]).start()
        return c

    lax.fori_loop(0, tm, issue, 0, unroll=4)
    for kk in range(TOP_K):
        _rows_wait(h_ref, xs_ref, tm, sem.at[kk])


def _dispatch(h, dest, cap):
    T, Dp = h.shape
    tm = min(512, T)
    grid_spec = pltpu.PrefetchScalarGridSpec(
        num_scalar_prefetch=1,
        grid=(T // tm,),
        in_specs=[pl.BlockSpec((tm, Dp), lambda i, d: (i, 0)), pl.BlockSpec(memory_space=pl.ANY)],
        out_specs=pl.BlockSpec(memory_space=pl.ANY),
        scratch_shapes=[pltpu.SemaphoreType.DMA((TOP_K,))],
    )
    return pl.pallas_call(
        functools.partial(_dispatch_kernel, tm=tm),
        out_shape=jax.ShapeDtypeStruct((cap, Dp), h.dtype),
        grid_spec=grid_spec,
        input_output_aliases={2: 0},
        compiler_params=_cparams("arbitrary"),
        name="moe_dispatch",
    )(dest, h, jnp.zeros((cap, Dp), h.dtype))


def _rank_kernel(route_ref, rank_ref, cnt_ref, carry):
    tm = route_ref.shape[0]
    i = pl.program_id(0)

    @pl.when(i == 0)
    def _():
        carry[...] = jnp.zeros_like(carry)

    r = route_ref[...]
    lane = _iota(r.shape, 1)
    lane_f = lane.astype(F32)
    oh0 = (lane_f == r[:, 2:3]).astype(F32)
    oh1 = (lane_f == r[:, 3:4]).astype(F32)
    both = oh0 + oh1
    earlier = (_iota((tm, tm), 0) > _iota((tm, tm), 1)).astype(BF16)
    prefix = jnp.dot(earlier, both.astype(BF16), preferred_element_type=F32) + carry[0:1, :]
    rank0 = jnp.sum(prefix * oh0, axis=-1, keepdims=True)
    rank1 = jnp.sum((prefix + oh0) * oh1, axis=-1, keepdims=True)
    rank_ref[...] = jnp.where(lane == 0, rank0, jnp.where(lane == 1, rank1, 0.0))
    carry[...] = carry[...] + jnp.sum(both, axis=0, keepdims=True)
    cnt_ref[...] = carry[...]


def _dest_kernel(route_ref, rank_ref, cnt_ref, dest_ref):
    r = route_ref[...]
    lane = _iota(r.shape, 1)
    lane_f = lane.astype(F32)
    padded = jnp.floor((cnt_ref[...] + (MOE_BLOCK - 1.0)) * (1.0 / MOE_BLOCK)) * MOE_BLOCK
    before = (_iota((LANES, LANES), 0) < _iota((LANES, LANES), 1)).astype(F32)
    pad_start = jnp.dot(padded, before, precision=HI, preferred_element_type=F32)[0:1, :]
    oh0 = lane_f == r[:, 2:3]
    oh1 = lane_f == r[:, 3:4]
    d0 = jnp.sum(jnp.where(oh0, pad_start, 0.0), axis=-1, keepdims=True) + rank_ref[:, 0:1]
    d1 = jnp.sum(jnp.where(oh1, pad_start, 0.0), axis=-1, keepdims=True) + rank_ref[:, 1:2]
    dest_ref[...] = jnp.where(lane == 0, d0, jnp.where(lane == 1, d1, 0.0))


def _dispatch_plan(route):
    T = route.shape[0]
    tm = min(512, T)
    slab = pl.BlockSpec((tm, LANES), lambda i: (i, 0))
    cnt_spec = pl.BlockSpec((SUBLANES, LANES), lambda i: (0, 0))
    rank, cnt = pl.pallas_call(
        _rank_kernel,
        out_shape=(jax.ShapeDtypeStruct((T, LANES), F32), jax.ShapeDtypeStruct((SUBLANES, LANES), F32)),
        grid=(T // tm,),
        in_specs=[slab],
        out_specs=(slab, cnt_spec),
        scratch_shapes=[pltpu.VMEM((SUBLANES, LANES), F32)],
        compiler_params=_cparams("arbitrary"),
        name="moe_rank",
    )(route)
    dest = pl.pallas_call(
        _dest_kernel,
        out_shape=jax.ShapeDtypeStruct((T, LANES), F32),
        grid=(T // tm,),
        in_specs=[slab, slab, cnt_spec],
        out_specs=slab,
        compiler_params=_cparams("parallel"),
        name="moe_dest",
    )(route, rank, cnt)
    dest = dest[:, :TOP_K].astype(jnp.int32).reshape(T * TOP_K)
    counts = cnt[0, :N_EXPERTS].astype(jnp.int32)
    padded = (counts + MOE_BLOCK - 1) // MOE_BLOCK * MOE_BLOCK
    pad_end = jnp.cumsum(padded)
    n_blocks = -(-(T * TOP_K) // MOE_BLOCK) + N_EXPERTS
    block_row0 = jnp.arange(n_blocks, dtype=jnp.int32) * MOE_BLOCK
    block_expert = jnp.minimum(jnp.searchsorted(pad_end, block_row0, side="right"),
                               N_EXPERTS - 1).astype(jnp.int32)
    n_used = (pad_end[-1] // MOE_BLOCK).astype(jnp.int32).reshape(1)
    return dest, block_expert, n_used, n_blocks * MOE_BLOCK


def kernel(x, c, positions, norm1_g, norm2_g, ada_w, ada_b, w_in, lru_conv_w, lru_conv_b, lru_w_r, lru_b_r, lru_w_i, lru_b_i, lru_lambda, sgu_norm_g, sgu_w, sgu_b, rwkv_mu_prev, rwkv_mu_next, rwkv_w0, rwkv_w_up, rwkv_a0, rwkv_a_up, rwkv_g_up, rwkv_k_k, rwkv_k_a, rwkv_r_k, w_branch, w_out, router_grp_w, router_grp_b, router_exp_w, router_exp_b, expert_w_gu, expert_w_down, final_norm_g):
    B, S, D = x.shape
    L = ada_w.shape[0]
    W = D // 2
    T = B * S
    feat_w = rwkv_mu_prev.shape[1]

    c_pad = jnp.zeros((SUBLANES, D), F32).at[:B].set(c)
    mod = _ada(c_pad, ada_w, ada_b)

    dk = D // 2 // (2 * RET_HEADS)
    inv_freq = ROPE_THETA ** (-jnp.arange(0, dk, 2, dtype=F32) / dk)
    ang = positions.astype(F32)[..., None] * inv_freq
    cosf, sinf = _rope_tables(jnp.tile(ang, (1, 1, LANES // (dk // 2))))
    log_gamma = np.log1p(-np.exp2(-5.0 - np.arange(RET_HEADS, dtype=np.float64)))
    lg_tab = jnp.asarray(np.broadcast_to(log_gamma[:, None], (RET_HEADS, LANES)), F32)

    x2d = x.reshape(T, D)
    for l in range(L):
        m = mod[l, :B]
        sh1, sc1, g1, sh2, sc2, g2 = [m[:, None, k * D:(k + 1) * D] for k in range(6)]

        h = _normmod(x2d.reshape(B, S, D), norm1_g[l], sc1, sh1, BF16).reshape(T, D)
        n_a = 7 * W
        proj_a = _mm(h, w_in, l, 0, n_a, 2048, 512, name="proj_a").reshape(B, S, n_a)
        feat = _mm(h, w_in, l, n_a, feat_w, 1024, feat_w // 3, name="proj_rwkv").reshape(B, S, feat_w)
        logits = _mm(h, w_in, l, n_a + feat_w, N_BRANCH * D, 2048, 512, out_dtype=BF16,
                     name="proj_gate")

        y_a = _lru(proj_a, B, S, W, 0, W // LANES, lru_conv_w[l], lru_conv_b[l], lru_w_r[l],
                   lru_b_r[l], lru_w_i[l], lru_b_i[l], lru_lambda[l])
        y_b = _retention(proj_a, B, S, W, 2 * W // LANES, 5 * W // (2 * LANES),
                         3 * W // (2 * LANES), 4 * W // (2 * LANES), cosf, sinf, lg_tab)
        y_c = _sgu(proj_a, B, S, W, 5, 6, sgu_norm_g[l], sgu_w[l], sgu_b[l])
        r, kk, v, g, bonus, lw, bb, kt = _rwkv_pre(
            feat, rwkv_mu_prev[l], rwkv_mu_next[l], rwkv_w0[l], rwkv_w_up[l], rwkv_a0[l],
            rwkv_a_up[l], rwkv_g_up[l], rwkv_k_k[l], rwkv_k_a[l], rwkv_r_k[l].reshape(-1))
        y_d = _rwkv_post(*_rwkv_scan(r, kk, v, lw, bb, kt), bonus, g)

        merged = _merge([y.reshape(T, W) for y in (y_a, y_b, y_c, y_d)],
                        w_branch, l, logits, 1024, 512)
        x2d = _mm_residual(merged, w_out, l, x2d, g1, S, 2048, 512)

        h2 = _normmod(x2d.reshape(B, S, D), norm2_g[l], sc2, sh2, jnp.uint32).reshape(T, D // 2)
        n_r = N_GROUPS + N_EXPERTS
        w_r = jnp.zeros((D, LANES), F32).at[:, :N_GROUPS].set(router_grp_w[l]).at[:, N_GROUPS:n_r].set(
            router_exp_w[l]).astype(BF16)
        b_r = jnp.zeros((1, LANES), F32).at[0, :N_GROUPS].set(router_grp_b[l]).at[0, N_GROUPS:n_r].set(
            router_exp_b[l])
        route = _router(h2, w_r, b_r)
        dest, block_expert, n_used, cap = _dispatch_plan(route)
        ys = _experts(_dispatch(h2, dest, cap), block_expert, n_used, expert_w_gu, expert_w_down, l)
        x2d = _combine(x2d, ys, dest, route, g2, S, final_norm_g, l == L - 1)

    return x2d.reshape(B, S, D)
```

```python
import functools
import math

import jax
import jax.numpy as jnp
import numpy as np
from jax import lax
from jax.experimental import pallas as pl
from jax.experimental.pallas import tpu as pltpu

F32 = jnp.float32
BF16 = jnp.bfloat16
EPS = 1e-6

LANES = 128
SUBLANES = 8
VMEM_LIMIT_BYTES = 56 * 1024 * 1024

LRU_BLOCKS = 8
CONV_WIDTH = 4
LRU_C = 8.0
RET_HEADS = 8
RET_CHUNK = 128
ROPE_THETA = 10000.0
SGU_GROUPS = 8
SGU_CHUNK = 128
RWKV_HEAD_DIM = 64
DECAY_LORA = 64
ICLR_LORA = 64
GATE_LORA = 128
DECAY_SCALE = math.exp(-0.5)
N_GROUPS = 4
EXPERTS_PER_GROUP = 8
N_EXPERTS = N_GROUPS * EXPERTS_PER_GROUP
TOP_K = 2
MOE_BLOCK = 256
N_BRANCH = 4

RWKV_CHUNK = 64
HI = lax.Precision.HIGHEST


def _cparams(*sem):
    return pltpu.CompilerParams(dimension_semantics=sem, vmem_limit_bytes=VMEM_LIMIT_BYTES)


def _bdot(a, b):
    return jnp.dot(a.astype(BF16), b.astype(BF16), preferred_element_type=F32)


def _bdot_nt(a, b):
    return lax.dot_general(a.astype(BF16), b.astype(BF16), (((1,), (1,)), ((), ())),
                           preferred_element_type=F32)


def _gelu(x):
    return 0.5 * x * (1.0 + jnp.tanh(math.sqrt(2.0 / math.pi) * (x + 0.044715 * (x * x * x))))


def _sigmoid(x):
    return 0.5 * jnp.tanh(0.5 * x) + 0.5


def _silu(x):
    return x * _sigmoid(x)


def _iota(shape, dim):
    return lax.broadcasted_iota(jnp.int32, shape, dim)


def _ada_kernel(c_ref, w_ref, b_ref, o_ref):
    cond = _silu(c_ref[...])
    o_ref[0] = _bdot(cond, w_ref[0]) + b_ref[0]


def _ada(c_pad, ada_w, ada_b):
    L, D, N = ada_w.shape
    R = c_pad.shape[0]
    tn = 1024
    return pl.pallas_call(
        _ada_kernel,
        out_shape=jax.ShapeDtypeStruct((L, R, N), F32),
        grid=(L, N // tn),
        in_specs=[pl.BlockSpec((R, D), lambda l, j: (0, 0)),
                  pl.BlockSpec((1, D, tn), lambda l, j: (l, 0, j)),
                  pl.BlockSpec((1, 1, tn), lambda l, j: (l, 0, j))],
        out_specs=pl.BlockSpec((1, R, tn), lambda l, j: (l, 0, j)),
        compiler_params=_cparams("parallel", "parallel"),
        name="ada_mod",
    )(c_pad, ada_w, ada_b.reshape(L, 1, N))


def _pack_bf16_pairs(y):
    half = y.shape[1] // 2
    bits = pltpu.bitcast(y.astype(BF16).astype(F32), jnp.uint32)
    return (bits[:, :half] >> 16) | (bits[:, half:] & jnp.uint32(0xFFFF0000))


def _unpack_bf16_pairs(u):
    lo = pltpu.bitcast(u << 16, F32)
    hi = pltpu.bitcast(u & jnp.uint32(0xFFFF0000), F32)
    return jnp.concatenate([lo, hi], axis=1).astype(BF16)


def _normmod_kernel(x_ref, g_ref, sc_ref, sh_ref, o_ref):
    x = x_ref[0]
    y = x * lax.rsqrt(jnp.mean(x * x, axis=-1, keepdims=True) + EPS) * g_ref[...]
    y = y * (1.0 + sc_ref[0]) + sh_ref[0]
    if o_ref.dtype == jnp.uint32:
        o_ref[0] = _pack_bf16_pairs(y)
    else:
        o_ref[0] = y.astype(o_ref.dtype)


def _normmod(x, g, sc, sh, out_dtype):
    B, S, D = x.shape
    ts = min(512, S)
    d_out = D // 2 if out_dtype == jnp.uint32 else D
    return pl.pallas_call(
        _normmod_kernel,
        out_shape=jax.ShapeDtypeStruct((B, S, d_out), out_dtype),
        grid=(B, S // ts),
        in_specs=[pl.BlockSpec((1, ts, D), lambda b, i: (b, i, 0)),
                  pl.BlockSpec((1, D), lambda b, i: (0, 0)),
                  pl.BlockSpec((1, 1, D), lambda b, i: (b, 0, 0)),
                  pl.BlockSpec((1, 1, D), lambda b, i: (b, 0, 0))],
        out_specs=pl.BlockSpec((1, ts, d_out), lambda b, i: (b, i, 0)),
        compiler_params=_cparams("parallel", "parallel"),
        name="norm_mod",
    )(x, g.reshape(1, D), sc, sh)


def _mm_kernel(a_ref, w_ref, o_ref):
    o_ref[...] = jnp.dot(a_ref[...], w_ref[0].astype(BF16),
                         preferred_element_type=F32).astype(o_ref.dtype)


def _mm(a, w, layer, col0, n_cols, tm, tn, out_dtype=F32, name="proj"):
    M, K = a.shape
    return pl.pallas_call(
        _mm_kernel,
        out_shape=jax.ShapeDtypeStruct((M, n_cols), out_dtype),
        grid=(M // tm, n_cols // tn),
        in_specs=[pl.BlockSpec((tm, K), lambda i, j: (i, 0)),
                  pl.BlockSpec((pl.Element(1), pl.Element(K), pl.Element(tn)),
                               lambda i, j: (layer, 0, pl.multiple_of(col0 + j * tn, LANES)))],
        out_specs=pl.BlockSpec((tm, tn), lambda i, j: (i, j)),
        compiler_params=_cparams("parallel", "parallel"),
        name=name,
    )(a, w)


def _mm_res_kernel(a_ref, w_ref, x_ref, g_ref, o_ref):
    y = jnp.dot(a_ref[...], w_ref[0].astype(BF16), preferred_element_type=F32)
    o_ref[...] = x_ref[...] + g_ref[0] * y


def _mm_residual(a, w, layer, x2d, gate, rows_per_batch, tm, tn, name="out_proj"):
    M, K = a.shape
    N = w.shape[2]
    tiles_per_batch = rows_per_batch // tm
    return pl.pallas_call(
        _mm_res_kernel,
        out_shape=jax.ShapeDtypeStruct((M, N), F32),
        grid=(M // tm, N // tn),
        in_specs=[pl.BlockSpec((tm, K), lambda i, j: (i, 0)),
                  pl.BlockSpec((1, K, tn), lambda i, j: (layer, 0, j)),
                  pl.BlockSpec((tm, tn), lambda i, j: (i, j)),
                  pl.BlockSpec((1, 1, tn), lambda i, j: (i // tiles_per_batch, 0, j))],
        out_specs=pl.BlockSpec((tm, tn), lambda i, j: (i, j)),
        compiler_params=_cparams("parallel", "parallel"),
        name=name,
    )(a, w, x2d, gate)


def _softplus(x):
    return jnp.maximum(x, 0.0) + jnp.log1p(jnp.exp(-jnp.abs(x)))


def _lru_kernel(x_ref, gate_ref, cw_ref, cb_ref, wr_ref, br_ref, wi_ref, bi_ref, lam_ref,
                o_ref, a_scr, u_scr, h_scr):
    S = x_ref.shape[1]
    x = x_ref[0]
    rows = _iota(x.shape, 0)
    xm2 = jnp.where(rows >= 2, pltpu.roll(x, 2, 0), 0.0)
    xm1 = jnp.where(rows >= 1, pltpu.roll(x, 1, 0), 0.0)
    xp1 = jnp.where(rows < S - 1, pltpu.roll(x, S - 1, 0), 0.0)
    xc = (cw_ref[0:1, :] * xm2 + cw_ref[1:2, :] * xm1 + cw_ref[2:3, :] * x
          + cw_ref[3:4, :] * xp1 + cb_ref[...])
    for z in range(2):
        r = _sigmoid(_bdot(xc, wr_ref[z, 0]) + br_ref[z:z + 1, :])
        i = _sigmoid(_bdot(xc, wi_ref[z, 0]) + bi_ref[z:z + 1, :])
        log_a = -LRU_C * r * _softplus(-lam_ref[z:z + 1, :])
        a = jnp.exp(log_a)
        a_scr[z] = a
        u_scr[z] = jnp.sqrt(-jnp.tanh(log_a) * (a * a + 1.0)) * i * xc

    n_tiles = S // SUBLANES
    L = x.shape[1]
    trow = _iota((SUBLANES, L), 0)

    def tile_scan(a, u, reverse):
        for d in (1, 2, 4):
            if reverse:
                keep = trow < SUBLANES - d
                a_s = jnp.where(keep, pltpu.roll(a, SUBLANES - d, 0), 1.0)
                u_s = jnp.where(keep, pltpu.roll(u, SUBLANES - d, 0), 0.0)
            else:
                keep = trow >= d
                a_s = jnp.where(keep, pltpu.roll(a, d, 0), 1.0)
                u_s = jnp.where(keep, pltpu.roll(u, d, 0), 0.0)
            u = u + a * u_s
            a = a * a_s
        return a, u

    def body(k, carry):
        hf, hb = carry
        f0 = pl.multiple_of(k * SUBLANES, SUBLANES)
        b0 = pl.multiple_of((n_tiles - 1 - k) * SUBLANES, SUBLANES)
        af, uf = tile_scan(a_scr[0, pl.ds(f0, SUBLANES), :], u_scr[0, pl.ds(f0, SUBLANES), :], False)
        ab, ub = tile_scan(a_scr[1, pl.ds(b0, SUBLANES), :], u_scr[1, pl.ds(b0, SUBLANES), :], True)
        tf = uf + af * hf
        tb = ub + ab * hb
        h_scr[0, pl.ds(f0, SUBLANES), :] = tf
        h_scr[1, pl.ds(b0, SUBLANES), :] = tb
        return tf[SUBLANES - 1:SUBLANES, :], tb[0:1, :]

    zero = jnp.zeros((1, L), F32)
    lax.fori_loop(0, n_tiles, body, (zero, zero), unroll=8)
    o_ref[0] = ((h_scr[0] + h_scr[1]) * _gelu(gate_ref[0])).astype(o_ref.dtype)


def _lru(proj, B, S, W, x_col, g_col, cw, cb, wr, br, wi, bi, lam):
    nb = W // LANES
    return pl.pallas_call(
        _lru_kernel,
        out_shape=jax.ShapeDtypeStruct((B, S, W), BF16),
        grid=(B, nb),
        in_specs=[pl.BlockSpec((1, S, LANES), lambda b, j: (b, 0, x_col + j)),
                  pl.BlockSpec((1, S, LANES), lambda b, j: (b, 0, g_col + j)),
                  pl.BlockSpec((CONV_WIDTH, LANES), lambda b, j: (0, j)),
                  pl.BlockSpec((1, LANES), lambda b, j: (0, j)),
                  pl.BlockSpec((2, 1, LANES, LANES), lambda b, j: (0, j, 0, 0)),
                  pl.BlockSpec((2, LANES), lambda b, j: (0, j)),
                  pl.BlockSpec((2, 1, LANES, LANES), lambda b, j: (0, j, 0, 0)),
                  pl.BlockSpec((2, LANES), lambda b, j: (0, j)),
                  pl.BlockSpec((2, LANES), lambda b, j: (0, j))],
        out_specs=pl.BlockSpec((1, S, LANES), lambda b, j: (b, 0, j)),
        scratch_shapes=[pltpu.VMEM((2, S, LANES), F32)] * 3,
        compiler_params=_cparams("parallel", "parallel"),
        name="rglru",
    )(proj, proj, cw, cb.reshape(1, W), wr, br, wi, bi, lam)


def _rope_kernel(ang_ref, cos_ref, sin_ref):
    ang = ang_ref[0]
    lane = _iota(ang.shape, 1)
    cos_ref[0] = jnp.cos(ang)
    s = jnp.sin(ang)
    sin_ref[0] = jnp.where((lane % 64) < 32, -s, s)


def _rope_tables(ang):
    B, S, _ = ang.shape
    ts = min(512, S)
    spec = pl.BlockSpec((1, ts, LANES), lambda b, i: (b, i, 0))
    return pl.pallas_call(
        _rope_kernel,
        out_shape=(jax.ShapeDtypeStruct((B, S, LANES), F32),) * 2,
        grid=(B, S // ts),
        in_specs=[spec],
        out_specs=(spec, spec),
        compiler_params=_cparams("parallel", "parallel"),
        name="rope_tables",
    )(ang)


def _ret_kernel(q_ref, k_ref, v_ref, g_ref, cos_ref, sin_ref, lg_ref, o_ref,
                qs_scr, ks_scr, kv_scr):
    S = q_ref.shape[1]
    C = RET_CHUNK
    N = S // C
    dk = 64
    p = pl.program_id(1)
    lane = _iota((S, LANES), 1)
    first_half = (lane % 64) < 32

    def rope(x):
        swapped = jnp.where(first_half, pltpu.roll(x, LANES - 32, 1), pltpu.roll(x, 32, 1))
        return x * cos_ref[0] + swapped * sin_ref[0]

    qs_scr[...] = rope(q_ref[0]) * (dk ** -0.5)
    ks_scr[...] = rope(k_ref[0])

    pos_r = _iota((C, 1), 0).astype(F32)
    pos_l = _iota((1, C), 1).astype(F32)
    ii = _iota((C, C), 0)
    jj = _iota((C, C), 1)
    dist = jnp.abs(ii - jj).astype(F32)
    clane = _iota((C, LANES), 1)

    for hh in range(2):
        lg = lg_ref[pl.ds(2 * p + hh, 1), :][:, 0:1]
        hmask = (clane // 64) == hh
        intra = jnp.exp(dist * lg)
        dec_kf_row = jnp.exp((C - 1.0 - pos_l) * lg)
        dec_kb_row = jnp.exp(pos_l * lg)
        dec_qf = jnp.exp((pos_r + 1.0) * lg)
        dec_qb = jnp.exp((C - pos_r) * lg)
        chunk_decay = jnp.exp(C * lg)

        def kv_body(n, _):
            r0 = pl.multiple_of(n * C, C)
            kt = jnp.where(hmask, ks_scr[pl.ds(r0, C), :], 0.0).T
            vc = v_ref[0, pl.ds(r0, C), hh * LANES:(hh + 1) * LANES]
            kv_scr[n] = _bdot(jnp.concatenate([kt * dec_kf_row, kt * dec_kb_row], axis=0), vc)
            return 0

        lax.fori_loop(0, N, kv_body, 0, unroll=4)

        def fwd_body(n, st):
            cur = kv_scr[n, 0:LANES, :]
            kv_scr[n, 0:LANES, :] = st
            return chunk_decay * st + cur

        lax.fori_loop(0, N, fwd_body, jnp.zeros((LANES, LANES), F32))

        def bwd_body(m, st):
            n = N - 1 - m
            cur = kv_scr[n, LANES:, :]
            kv_scr[n, LANES:, :] = st
            return chunk_decay * st + cur

        lax.fori_loop(0, N, bwd_body, jnp.zeros((LANES, LANES), F32))

        def out_body(n, _):
            r0 = pl.multiple_of(n * C, C)
            qc = qs_scr[pl.ds(r0, C), :]
            kc = jnp.where(hmask, ks_scr[pl.ds(r0, C), :], 0.0)
            vc = v_ref[0, pl.ds(r0, C), hh * LANES:(hh + 1) * LANES]
            scores = _bdot_nt(qc, kc) * intra
            o = _bdot(scores, vc) + _bdot(jnp.concatenate([qc * dec_qf, qc * dec_qb], axis=1), kv_scr[n])
            mu = jnp.mean(o, axis=-1, keepdims=True)
            var = jnp.mean(o * o, axis=-1, keepdims=True) - mu * mu
            o = (o - mu) * lax.rsqrt(var + EPS)
            gc = g_ref[0, pl.ds(r0, C), hh * LANES:(hh + 1) * LANES]
            o_ref[0, pl.ds(r0, C), hh * LANES:(hh + 1) * LANES] = (_silu(gc) * o).astype(o_ref.dtype)
            return 0

        lax.fori_loop(0, N, out_body, 0, unroll=4)


def _retention(proj, B, S, W, q_col, k_col, v_col, g_col, cosf, sinf, lg_tab):
    n_pairs = RET_HEADS // 2
    N = S // RET_CHUNK
    return pl.pallas_call(
        _ret_kernel,
        out_shape=jax.ShapeDtypeStruct((B, S, W), BF16),
        grid=(B, n_pairs),
        in_specs=[pl.BlockSpec((1, S, LANES), lambda b, p: (b, 0, q_col + p)),
                  pl.BlockSpec((1, S, LANES), lambda b, p: (b, 0, k_col + p)),
                  pl.BlockSpec((1, S, 2 * LANES), lambda b, p: (b, 0, v_col + p)),
                  pl.BlockSpec((1, S, 2 * LANES), lambda b, p: (b, 0, g_col + p)),
                  pl.BlockSpec((1, S, LANES), lambda b, p: (b, 0, 0)),
                  pl.BlockSpec((1, S, LANES), lambda b, p: (b, 0, 0)),
                  pl.BlockSpec((RET_HEADS, LANES), lambda b, p: (0, 0))],
        out_specs=pl.BlockSpec((1, S, 2 * LANES), lambda b, p: (b, 0, p)),
        scratch_shapes=[pltpu.VMEM((S, LANES), F32), pltpu.VMEM((S, LANES), F32),
                        pltpu.VMEM((N, 2 * LANES, LANES), F32)],
        compiler_params=_cparams("parallel", "parallel"),
        name="retention",
    )(proj, proj, proj, proj, cosf, sinf, lg_tab)


def _sgu_kernel(u_ref, v_ref, ng_ref, w_ref, bt_ref, o_ref):
    ts = u_ref.shape[1]
    C = SGU_CHUNK
    v = _gelu(v_ref[0])
    mu = jnp.mean(v, axis=-1, keepdims=True)
    var = jnp.mean(jnp.square(v - mu), axis=-1, keepdims=True)
    v = ((v - mu) * lax.rsqrt(var + EPS) * ng_ref[...]).astype(BF16)
    for c in range(ts // C):
        for g in range(SGU_GROUPS):
            vc = v[c * C:(c + 1) * C, g * LANES:(g + 1) * LANES]
            mixed = jnp.dot(w_ref[g].astype(BF16), vc, preferred_element_type=F32) + bt_ref[:, g:g + 1]
            uc = _gelu(u_ref[0, c * C:(c + 1) * C, g * LANES:(g + 1) * LANES])
            o_ref[0, c * C:(c + 1) * C, g * LANES:(g + 1) * LANES] = (uc * mixed).astype(o_ref.dtype)


def _sgu(proj, B, S, W, u_col, v_col, norm_g, w_s, b_s):
    ts = min(512, S)
    return pl.pallas_call(
        _sgu_kernel,
        out_shape=jax.ShapeDtypeStruct((B, S, W), BF16),
        grid=(B, S // ts),
        in_specs=[pl.BlockSpec((1, ts, W), lambda b, i: (b, i, u_col)),
                  pl.BlockSpec((1, ts, W), lambda b, i: (b, i, v_col)),
                  pl.BlockSpec((1, W), lambda b, i: (0, 0)),
                  pl.BlockSpec((SGU_GROUPS, SGU_CHUNK, SGU_CHUNK), lambda b, i: (0, 0, 0)),
                  pl.BlockSpec((SGU_CHUNK, SGU_GROUPS), lambda b, i: (0, 0))],
        out_specs=pl.BlockSpec((1, ts, W), lambda b, i: (b, i, 0)),
        compiler_params=_cparams("parallel", "parallel"),
        name="spatial_gating",
    )(proj, proj, norm_g.reshape(1, W), w_s, b_s.T)


def _head_sum(x, hd):
    seg = (_iota((LANES, LANES), 0) // hd == _iota((LANES, LANES), 1) // hd).astype(BF16)
    hi = x.astype(BF16)
    lo = (x - hi.astype(F32)).astype(BF16)
    parts = [jnp.dot(hi[:, j * LANES:(j + 1) * LANES], seg, preferred_element_type=F32)
             + jnp.dot(lo[:, j * LANES:(j + 1) * LANES], seg, preferred_element_type=F32)
             for j in range(x.shape[-1] // LANES)]
    return jnp.concatenate(parts, axis=1)


def _rwkv_pre_kernel(f_ref, fp_ref, fn_ref, mup_ref, mun_ref, w0_ref, wup_ref, a0_ref, aup_ref,
                     gup_ref, kk_ref, ka_ref, rk_ref,
                     r_out, kk_out, v_out, g_out, bonus_out, lw_out, bb_out, kt_out):
    ts = f_ref.shape[1]
    W = r_out.shape[2]
    i = pl.program_id(1)
    n_i = pl.num_programs(1)
    f = f_ref[0]
    rows = _iota(f.shape, 0)
    prev_row = jnp.where(i > 0, fp_ref[0, SUBLANES - 1:SUBLANES, :], 0.0)
    next_row = jnp.where(i < n_i - 1, fn_ref[0, 0:1, :], 0.0)
    prev = jnp.where(rows >= 1, pltpu.roll(f, 1, 0), prev_row)
    nxt = jnp.where(rows < ts - 1, pltpu.roll(f, ts - 1, 0), next_row)
    f = f + mup_ref[...] * (prev - f) + mun_ref[...] * (nxt - f)

    r = f[:, 0:W]
    k = f[:, W:2 * W]
    v = f[:, 2 * W:3 * W]
    o = 3 * W
    wd = jnp.tanh(f[:, o:o + 2 * DECAY_LORA])
    ad = f[:, o + 2 * DECAY_LORA:o + 2 * DECAY_LORA + 2 * ICLR_LORA]
    gd = f[:, o + 2 * DECAY_LORA + 2 * ICLR_LORA:]

    g_out[0] = _bdot(_sigmoid(gd), gup_ref[...]).astype(g_out.dtype)
    kk = k * kk_ref[...]
    kk = kk * lax.rsqrt(_head_sum(kk * kk, RWKV_HEAD_DIM) + EPS)
    r_out[0] = r.astype(r_out.dtype)
    kk_out[0] = kk.astype(kk_out.dtype)
    v_out[0] = v.astype(v_out.dtype)
    bonus = jnp.zeros_like(r)
    for z in range(2):
        w_raw = _bdot(wd[:, z * DECAY_LORA:(z + 1) * DECAY_LORA], wup_ref[z]) + w0_ref[z:z + 1, :]
        lw_out[z, 0] = -DECAY_SCALE * _sigmoid(w_raw)
        a = _sigmoid(_bdot(ad[:, z * ICLR_LORA:(z + 1) * ICLR_LORA], aup_ref[z]) + a0_ref[z:z + 1, :])
        bb_out[z, 0] = (a * kk).astype(bb_out.dtype)
        kt = k * (1.0 + (a - 1.0) * ka_ref[...])
        kt_out[z, 0] = kt.astype(kt_out.dtype)
        bonus = bonus + _head_sum(r * kt * rk_ref[...], RWKV_HEAD_DIM) * v
    bonus_out[0] = bonus.astype(bonus_out.dtype)


def _rwkv_pre(feat, mu_prev, mu_next, w0, w_up, a0, a_up, g_up, k_k, k_a, r_k):
    B, S, Fw = feat.shape
    W = w0.shape[1]
    ts = min(256, S)
    hb = ts // SUBLANES
    n_hb = S // SUBLANES
    row = lambda a: a.reshape(1, -1)
    full2 = lambda a: pl.BlockSpec(a.shape, lambda b, i: (0, 0))
    full3 = lambda a: pl.BlockSpec(a.shape, lambda b, i: (0, 0, 0))
    bsw = pl.BlockSpec((1, ts, W), lambda b, i: (b, i, 0))
    zsw = pl.BlockSpec((2, 1, ts, W), lambda b, i: (0, b, i, 0))
    sds = jax.ShapeDtypeStruct((B, S, W), F32)
    zds = jax.ShapeDtypeStruct((2, B, S, W), F32)
    sdh = jax.ShapeDtypeStruct((B, S, W), BF16)
    zdh = jax.ShapeDtypeStruct((2, B, S, W), BF16)
    args = (row(mu_prev), row(mu_next), w0, w_up, a0, a_up, g_up, row(k_k), row(k_a), row(r_k))
    specs = [full2(args[0]), full2(args[1]), full2(w0), full3(w_up), full2(a0), full3(a_up),
             full2(g_up), full2(args[7]), full2(args[8]), full2(args[9])]
    return pl.pallas_call(
        _rwkv_pre_kernel,
        out_shape=(sdh, sdh, sdh, sdh, sdh, zds, zdh, zdh),
        grid=(B, S // ts),
        in_specs=[pl.BlockSpec((1, ts, Fw), lambda b, i: (b, i, 0)),
                  pl.BlockSpec((1, SUBLANES, Fw), lambda b, i: (b, jnp.maximum(i * hb - 1, 0), 0)),
                  pl.BlockSpec((1, SUBLANES, Fw), lambda b, i: (b, jnp.minimum((i + 1) * hb, n_hb - 1), 0)),
                  ] + specs,
        out_specs=(bsw, bsw, bsw, bsw, bsw, zsw, zsw, zsw),
        compiler_params=_cparams("parallel", "parallel"),
        name="rwkv_prepare",
    )(feat, feat, feat, *args)


def _rwkv_scan_kernel(rf_ref, kkf_ref, vf_ref, rb_ref, kkb_ref, vb_ref,
                      lwf_ref, bbf_ref, ktf_ref, lwb_ref, bbb_ref, ktb_ref, yf_ref, yb_ref, st_scr):
    C = RWKV_CHUNK
    C2 = 2 * C
    n_pairs = rf_ref.shape[2] // LANES

    @pl.when(pl.program_id(1) == 0)
    def _():
        st_scr[...] = jnp.zeros_like(st_scr)

    wide_t = _iota((C, C2), 0)
    wide_s = _iota((C, C2), 1) % C
    eye_w = (wide_t == wide_s).astype(F32)
    emask = (_iota((C2, LANES), 0) // C) == (_iota((C2, LANES), 1) // RWKV_HEAD_DIM)
    bmask = (_iota((C2, C2), 0) // C) == (_iota((C2, C2), 1) // C)
    diag_l = _iota((LANES, LANES), 0) == _iota((LANES, LANES), 1)
    tri = _iota((C, C), 0) - _iota((C, C), 1)
    cum_mat = [(sg * tri >= 0).astype(BF16) for sg in (1, -1)]
    strict = [sg * (wide_t - wide_s) > 0 for sg in (1, -1)]
    incl = [sg * (wide_t - wide_s) >= 0 for sg in (1, -1)]
    refs = [(rf_ref, kkf_ref, vf_ref, lwf_ref, bbf_ref, ktf_ref, yf_ref),
            (rb_ref, kkb_ref, vb_ref, lwb_ref, bbb_ref, ktb_ref, yb_ref)]

    def expand(x):
        return jnp.where(emask, jnp.concatenate([x, x], axis=0), 0.0)

    def blockdiag(a):
        return jnp.where(bmask, jnp.concatenate([a, a], axis=0), 0.0)

    def bf(x):
        return x.astype(BF16)

    def mm(a, b):
        return jnp.dot(a, b, preferred_element_type=F32)

    def cumsum3(m, x):
        hi = bf(x)
        r1 = x - hi.astype(F32)
        mid = bf(r1)
        lo = bf(r1 - mid.astype(F32))
        return mm(m, hi) + (mm(m, mid) + mm(m, lo))

    units = [(z, j) for z in range(2) for j in range(n_pairs)]
    zs = [z for z, _ in units]
    sl = [slice(j * LANES, (j + 1) * LANES) for _, j in units]
    r_in = [refs[z][0][0, :, s] for z, s in zip(zs, sl)]
    kk_in = [refs[z][1][0, :, s] for z, s in zip(zs, sl)]
    v_in = [refs[z][2][0, :, s] for z, s in zip(zs, sl)]
    lw = [refs[z][3][0, 0, :, s] for z, s in zip(zs, sl)]
    bb_in = [refs[z][4][0, 0, :, s] for z, s in zip(zs, sl)]
    kt_in = [refs[z][5][0, 0, :, s] for z, s in zip(zs, sl)]

    cum = [cumsum3(cum_mat[z], x) for z, x in zip(zs, lw)]
    tot = [jnp.sum(x, axis=0, keepdims=True) for x in lw]
    p_inv = [jnp.exp(-c) for c in cum]
    p_rest = [jnp.exp(t - c) for t, c in zip(tot, cum)]
    kh_c = [bf(k * jnp.exp(c - x)) for k, c, x in zip(kk_in, cum, lw)]
    rh_c = [bf(r * jnp.exp(c)) for r, c in zip(r_in, cum)]
    kh_e = [expand(k) for k in kh_c]
    bh_e = [expand(bf(b * p)) for b, p in zip(bb_in, p_inv)]
    ktil_e = [expand(bf(k * p)) for k, p in zip(kt_in, p_inv)]
    b_rest_t = [bf(expand(b * p).T) for b, p in zip(bb_in, p_rest)]
    k_rest_t = [bf(expand(k * p).T) for k, p in zip(kt_in, p_rest)]
    v_e = [expand(v) for v in v_in]

    a_all = [lax.dot_general(jnp.concatenate([k, r], axis=0), jnp.concatenate([b, t], axis=0),
                             (((1,), (1,)), ((), ())), preferred_element_type=F32)
             for k, r, b, t in zip(kh_c, rh_c, bh_e, ktil_e)]
    a_b = [jnp.where(strict[z], a[0:C, 0:C2], 0.0) for z, a in zip(zs, a_all)]
    a_k = [jnp.where(strict[z], a[0:C, C2:], 0.0) for z, a in zip(zs, a_all)]
    a_rb = [jnp.where(incl[z], a[C:, 0:C2], 0.0) for z, a in zip(zs, a_all)]
    a_rk = [jnp.where(incl[z], a[C:, C2:], 0.0) for z, a in zip(zs, a_all)]

    pw = [bf(-a) for a in a_b]
    t_inv = [eye_w] * len(units)
    n_dbl = int(math.log2(C))
    for k in range(n_dbl):
        if k < n_dbl - 1:
            both = [mm(p, jnp.concatenate([blockdiag(p), blockdiag(bf(t))], axis=1))
                    for p, t in zip(pw, t_inv)]
            pw = [bf(x[:, 0:C2]) for x in both]
            t_inv = [t + x[:, C2:] for t, x in zip(t_inv, both)]
        else:
            t_inv = [t + mm(p, blockdiag(bf(t))) for p, t in zip(pw, t_inv)]

    akv = [mm(bf(jnp.concatenate([ak, ark], axis=0)), v)
           for ak, ark, v in zip(a_k, a_rk, v_e)]
    tw = [mm(bf(t), jnp.concatenate([k, expand(bf(x[0:C]))], axis=1))
          for t, k, x in zip(t_inv, kh_e, akv)]
    st = [st_scr[z, j] for z, j in units]
    ws = [mm(jnp.concatenate([bf(x[:, 0:LANES]), r], axis=0), bf(s))
          for x, r, s in zip(tw, rh_c, st)]
    u_c = [x[0:C] + t[:, LANES:] for x, t in zip(ws, tw)]
    y_c = [x[C:] + k[C:] - mm(bf(a), expand(bf(u))) for x, k, a, u in zip(ws, akv, a_rb, u_c)]
    for z, s, y in zip(zs, sl, y_c):
        refs[z][6][0, :, s] = y.astype(refs[z][6].dtype)

    wu_e = [jnp.concatenate([expand(bf(t[:, 0:LANES])), expand(bf(t[:, LANES:]))], axis=1) for t in tw]
    bw = [mm(b, x) for b, x in zip(b_rest_t, wu_e)]
    kv = [mm(k, v) for k, v in zip(k_rest_t, v_e)]
    for n, (z, j) in enumerate(units):
        m_t = jnp.where(diag_l, jnp.exp(tot[n]), 0.0) - bw[n][:, 0:LANES]
        g_t = kv[n] - bw[n][:, LANES:]
        st_scr[z, j] = mm(bf(m_t), bf(st[n])) + g_t


def _rwkv_scan(r, kk, v, lw, bb, kt):
    B, S, W = r.shape
    C = RWKV_CHUNK
    nc = S // C
    fwd = pl.BlockSpec((1, C, W), lambda b, i: (b, i, 0))
    bwd = pl.BlockSpec((1, C, W), lambda b, i: (b, nc - 1 - i, 0))
    fwd_z = pl.BlockSpec((1, 1, C, W), lambda b, i: (0, b, i, 0))
    bwd_z = pl.BlockSpec((1, 1, C, W), lambda b, i: (1, b, nc - 1 - i, 0))
    sds = jax.ShapeDtypeStruct((B, S, W), BF16)
    return pl.pallas_call(
        _rwkv_scan_kernel,
        out_shape=(sds, sds),
        grid=(B, nc),
        in_specs=[fwd, fwd, fwd, bwd, bwd, bwd, fwd_z, fwd_z, fwd_z, bwd_z, bwd_z, bwd_z],
        out_specs=(fwd, bwd),
        scratch_shapes=[pltpu.VMEM((2, W // LANES, LANES, LANES), F32)],
        compiler_params=_cparams("parallel", "arbitrary"),
        name="rwkv_scan",
    )(r, kk, v, r, kk, v, lw, bb, kt, lw, bb, kt)


def _rwkv_post_kernel(yf_ref, yb_ref, bonus_ref, g_ref, o_ref):
    hd = RWKV_HEAD_DIM
    acc = bonus_ref[0].astype(F32)
    for y_ref in (yf_ref, yb_ref):
        y = y_ref[0].astype(F32)
        d = y - _head_sum(y, hd) * (1.0 / hd)
        var = _head_sum(d * d, hd) * (1.0 / hd)
        acc = acc + d * lax.rsqrt(var + EPS)
    o_ref[0] = (acc * g_ref[0].astype(F32)).astype(o_ref.dtype)


def _rwkv_post(y_fwd, y_bwd, bonus, g):
    B, S, W = y_fwd.shape
    ts = min(256, S)
    bsw = pl.BlockSpec((1, ts, W), lambda b, i: (b, i, 0))
    return pl.pallas_call(
        _rwkv_post_kernel,
        out_shape=jax.ShapeDtypeStruct((B, S, W), BF16),
        grid=(B, S // ts),
        in_specs=[bsw, bsw, bsw, bsw],
        out_specs=bsw,
        compiler_params=_cparams("parallel", "parallel"),
        name="rwkv_finish",
    )(y_fwd, y_bwd, bonus, g)


def _merge_kernel(ya_ref, yb_ref, yc_ref, yd_ref, wb_ref, l0_ref, l1_ref, l2_ref, l3_ref, o_ref,
                  wb_bf):
    @pl.when(pl.program_id(1) == 0)
    def _():
        wb_bf[...] = wb_ref[0].astype(BF16)

    acc = None
    for n, (y_ref, l_ref) in enumerate(((ya_ref, l0_ref), (yb_ref, l1_ref), (yc_ref, l2_ref),
                                        (yd_ref, l3_ref))):
        br = jnp.dot(y_ref[...], wb_bf[n], preferred_element_type=F32)
        t = _sigmoid(l_ref[...].astype(F32)) * br
        acc = t if acc is None else acc + t
    o_ref[...] = acc.astype(o_ref.dtype)


def _merge(ys, wb, layer, logits, tm, tn):
    T, W = ys[0].shape
    D = wb.shape[3]
    nj = D // tn
    yspec = pl.BlockSpec((tm, W), lambda j, i: (i, 0))
    lspec = lambda n: pl.BlockSpec((tm, tn), lambda j, i: (i, n * nj + j))
    return pl.pallas_call(
        _merge_kernel,
        out_shape=jax.ShapeDtypeStruct((T, D), BF16),
        grid=(nj, T // tm),
        in_specs=[yspec, yspec, yspec, yspec,
                  pl.BlockSpec((1, N_BRANCH, W, tn), lambda j, i: (layer, 0, 0, j)),
                  lspec(0), lspec(1), lspec(2), lspec(3)],
        out_specs=pl.BlockSpec((tm, tn), lambda j, i: (i, j)),
        scratch_shapes=[pltpu.VMEM((N_BRANCH, W, tn), BF16)],
        compiler_params=_cparams("parallel", "arbitrary"),
        name="branch_merge",
    )(*ys, wb, logits, logits, logits, logits)


def _router_kernel(h_ref, w_ref, b_ref, o_ref):
    logits = jnp.dot(_unpack_bf16_pairs(h_ref[...]), w_ref[...], preferred_element_type=F32) + b_ref[...]
    lane = _iota(logits.shape, 1)
    lane_f = lane.astype(F32)
    neg = -3.0e38
    far = float(LANES)
    is_grp = lane < N_GROUPS
    gl = jnp.where(is_grp, logits, neg)
    gmax = jnp.max(gl, axis=-1, keepdims=True)
    gsum = jnp.sum(jnp.where(is_grp, jnp.exp(gl - gmax), 0.0), axis=-1, keepdims=True)
    grp_p = 1.0 / gsum
    grp_i = jnp.min(jnp.where(gl == gmax, lane_f, far), axis=-1, keepdims=True)
    lo = N_GROUPS + grp_i * EXPERTS_PER_GROUP
    in_grp = jnp.logical_and(lane_f >= lo, lane_f < lo + EXPERTS_PER_GROUP)
    el = jnp.where(in_grp, logits, neg)
    m1 = jnp.max(el, axis=-1, keepdims=True)
    i1 = jnp.min(jnp.where(el == m1, lane_f, far), axis=-1, keepdims=True)
    el2 = jnp.where(lane_f == i1, neg, el)
    m2 = jnp.max(el2, axis=-1, keepdims=True)
    i2 = jnp.min(jnp.where(el2 == m2, lane_f, far), axis=-1, keepdims=True)
    t = jnp.exp(m2 - m1)
    w1 = grp_p / (1.0 + t)
    w2 = grp_p * t / (1.0 + t)
    o_ref[...] = jnp.where(lane == 0, w1, jnp.where(lane == 1, w2, jnp.where(
        lane == 2, i1 - N_GROUPS, jnp.where(lane == 3, i2 - N_GROUPS, 0.0))))


def _router(h, w_r, b_r):
    T = h.shape[0]
    D = w_r.shape[0]
    tm = min(1024, T)
    return pl.pallas_call(
        _router_kernel,
        out_shape=jax.ShapeDtypeStruct((T, LANES), F32),
        grid=(T // tm,),
        in_specs=[pl.BlockSpec((tm, D // 2), lambda i: (i, 0)),
                  pl.BlockSpec((D, LANES), lambda i: (0, 0)),
                  pl.BlockSpec((1, LANES), lambda i: (0, 0))],
        out_specs=pl.BlockSpec((tm, LANES), lambda i: (i, 0)),
        compiler_params=_cparams("parallel"),
        name="router",
    )(h, w_r, b_r)


def _expert_kernel(be_ref, nb_ref, first_ref, slot_ref, nxt_ref, xs_ref, wgu_hbm, wd_hbm, o_ref,
                   wgu_f32, wd_f32, wgu_bf, wd_bf, sem, *, layer):
    i = pl.program_id(0)
    used = i < nb_ref[0]

    def weight_copies(e, s):
        return (pltpu.make_async_copy(wgu_hbm.at[layer, e], wgu_f32.at[s], sem.at[s, 0]),
                pltpu.make_async_copy(wd_hbm.at[layer, e], wd_f32.at[s], sem.at[s, 1]))

    @pl.when(jnp.logical_and(used, first_ref[i] == 1))
    def _():
        s = slot_ref[i]

        @pl.when(i == 0)
        def _():
            for cp in weight_copies(be_ref[i], s):
                cp.start()

        for cp in weight_copies(be_ref[i], s):
            cp.wait()

        @pl.when(nxt_ref[i] >= 0)
        def _():
            for cp in weight_copies(nxt_ref[i], 1 - s):
                cp.start()

        wgu_bf[...] = wgu_f32[s].astype(BF16)
        wd_bf[...] = wd_f32[s].astype(BF16)

    @pl.when(used)
    def _():
        gu = jnp.dot(_unpack_bf16_pairs(xs_ref[...]), wgu_bf[...], preferred_element_type=F32)
        eh = gu.shape[1] // 2
        mid = (_silu(gu[:, :eh]) * gu[:, eh:]).astype(BF16)
        o_ref[...] = jnp.dot(mid, wd_bf[...], preferred_element_type=F32)

    @pl.when(jnp.logical_not(used))
    def _():
        o_ref[...] = jnp.zeros_like(o_ref)


def _expert_runs(block_expert, n_used):
    n_blocks = block_expert.shape[0]
    idx = jnp.arange(n_blocks, dtype=jnp.int32)
    changed = jnp.logical_or(idx == 0, block_expert != jnp.roll(block_expert, 1))
    first = jnp.logical_and(changed, idx < n_used[0]).astype(jnp.int32)
    slot = (jnp.cumsum(first) - 1) % 2
    start_pos = jnp.where(first == 1, idx, n_blocks)
    later = lax.cummin(start_pos, axis=0, reverse=True)
    next_pos = jnp.concatenate([later[1:], jnp.full((1,), n_blocks, jnp.int32)])
    nxt = jnp.where(next_pos < n_blocks, block_expert[jnp.minimum(next_pos, n_blocks - 1)], -1)
    return first, slot.astype(jnp.int32), nxt.astype(jnp.int32)


def _experts(xs, block_expert, n_used, w_gu, w_down, layer):
    cap = xs.shape[0]
    D, H2 = w_gu.shape[2], w_gu.shape[3]
    n_blocks = cap // MOE_BLOCK
    first, slot, nxt = _expert_runs(block_expert, n_used)
    grid_spec = pltpu.PrefetchScalarGridSpec(
        num_scalar_prefetch=5,
        grid=(n_blocks,),
        in_specs=[pl.BlockSpec((MOE_BLOCK, D // 2),
                               lambda i, be, nb, fi, sl, nx: (jnp.minimum(i, nb[0] - 1), 0)),
                  pl.BlockSpec(memory_space=pl.ANY),
                  pl.BlockSpec(memory_space=pl.ANY)],
        out_specs=pl.BlockSpec((MOE_BLOCK, D), lambda i, be, nb, fi, sl, nx: (i, 0)),
        scratch_shapes=[pltpu.VMEM((2, D, H2), F32), pltpu.VMEM((2, H2 // 2, D), F32),
                        pltpu.VMEM((D, H2), BF16), pltpu.VMEM((H2 // 2, D), BF16),
                        pltpu.SemaphoreType.DMA((2, 2))],
    )
    return pl.pallas_call(
        functools.partial(_expert_kernel, layer=layer),
        out_shape=jax.ShapeDtypeStruct((cap, D), F32),
        grid_spec=grid_spec,
        compiler_params=_cparams("arbitrary"),
        name="experts",
    )(block_expert, n_used, first, slot, nxt, xs, w_gu, w_down)


def _row_copy(src_ref, s, dst_ref, d, sem):
    return pltpu.make_async_copy(src_ref.at[pl.ds(s, 1), :], dst_ref.at[pl.ds(d, 1), :], sem)


def _rows_wait(src_ref, dst_ref, n, sem):
    pltpu.make_async_copy(src_ref.at[pl.ds(0, n), :], dst_ref.at[pl.ds(0, n), :], sem).wait()


def _combine_kernel(dest_ref, ys_ref, x_ref, r_ref, g_ref, ng_ref, o_ref, y0_buf, y1_buf, sem, *,
                    final_norm):
    tm = x_ref.shape[0]
    base = pl.program_id(0) * tm

    def issue(k, c):
        t = base + k
        _row_copy(ys_ref, dest_ref[TOP_K * t], y0_buf, k, sem.at[0]).start()
        _row_copy(ys_ref, dest_ref[TOP_K * t + 1], y1_buf, k, sem.at[1]).start()
        return c

    lax.fori_loop(0, tm, issue, 0, unroll=4)
    _rows_wait(ys_ref, y0_buf, tm, sem.at[0])
    _rows_wait(ys_ref, y1_buf, tm, sem.at[1])
    w0 = r_ref[:, 0:1]
    w1 = r_ref[:, 1:2]
    x = x_ref[...] + g_ref[0] * (w0 * y0_buf[...] + w1 * y1_buf[...])
    if final_norm:
        x = x * lax.rsqrt(jnp.mean(x * x, axis=-1, keepdims=True) + EPS) * ng_ref[...]
    o_ref[...] = x


def _combine(x2d, ys, dest, route, gate, rows_per_batch, norm_g, final_norm):
    T, D = x2d.shape
    tm = min(512, T)
    tiles_per_batch = rows_per_batch // tm
    spec = pl.BlockSpec((tm, D), lambda i, d: (i, 0))
    grid_spec = pltpu.PrefetchScalarGridSpec(
        num_scalar_prefetch=1,
        grid=(T // tm,),
        in_specs=[pl.BlockSpec(memory_space=pl.ANY), spec,
                  pl.BlockSpec((tm, LANES), lambda i, d: (i, 0)),
                  pl.BlockSpec((1, 1, D), lambda i, d: (i // tiles_per_batch, 0, 0)),
                  pl.BlockSpec((1, D), lambda i, d: (0, 0))],
        out_specs=spec,
        scratch_shapes=[pltpu.VMEM((tm, D), F32), pltpu.VMEM((tm, D), F32),
                        pltpu.SemaphoreType.DMA((2,))],
    )
    return pl.pallas_call(
        functools.partial(_combine_kernel, final_norm=final_norm),
        out_shape=jax.ShapeDtypeStruct((T, D), F32),
        grid_spec=grid_spec,
        compiler_params=_cparams("arbitrary"),
        name="moe_combine",
    )(dest, ys, x2d, route, gate, norm_g.reshape(1, D))


def _dispatch_kernel(dest_ref, h_ref, xs_in_ref, xs_ref, sem, *, tm):
    del xs_in_ref
    base = pl.program_id(0) * tm

    def issue(k, c):
        t = base + k
        for kk in range(TOP_K):
            _row_copy(h_ref, k, xs_ref, dest_ref[TOP_K * t + kk], sem.at[kk]).start()
        return c

    lax.fori_loop(0, tm, issue, 0, unroll=4)
    for kk in range(TOP_K):
        _rows_wait(h_ref, xs_ref, tm, sem.at[kk])


def _dispatch(h, dest, cap):
    T, Dp = h.shape
    tm = min(512, T)
    grid_spec = pltpu.PrefetchScalarGridSpec(
        num_scalar_prefetch=1,
        grid=(T // tm,),
        in_specs=[pl.BlockSpec((tm, Dp), lambda i, d: (i, 0)), pl.BlockSpec(memory_space=pl.ANY)],
        out_specs=pl.BlockSpec(memory_space=pl.ANY),
        scratch_shapes=[pltpu.SemaphoreType.DMA((TOP_K,))],
    )
    return pl.pallas_call(
        functools.partial(_dispatch_kernel, tm=tm),
        out_shape=jax.ShapeDtypeStruct((cap, Dp), h.dtype),
        grid_spec=grid_spec,
        input_output_aliases={2: 0},
        compiler_params=_cparams("arbitrary"),
        name="moe_dispatch",
    )(dest, h, jnp.zeros((cap, Dp), h.dtype))


def _rank_kernel(route_ref, rank_ref, cnt_ref, carry):
    tm = route_ref.shape[0]
    i = pl.program_id(0)

    @pl.when(i == 0)
    def _():
        carry[...] = jnp.zeros_like(carry)

    r = route_ref[...]
    lane = _iota(r.shape, 1)
    lane_f = lane.astype(F32)
    oh0 = (lane_f == r[:, 2:3]).astype(F32)
    oh1 = (lane_f == r[:, 3:4]).astype(F32)
    both = oh0 + oh1
    earlier = (_iota((tm, tm), 0) > _iota((tm, tm), 1)).astype(BF16)
    prefix = jnp.dot(earlier, both.astype(BF16), preferred_element_type=F32) + carry[0:1, :]
    rank0 = jnp.sum(prefix * oh0, axis=-1, keepdims=True)
    rank1 = jnp.sum((prefix + oh0) * oh1, axis=-1, keepdims=True)
    rank_ref[...] = jnp.where(lane == 0, rank0, jnp.where(lane == 1, rank1, 0.0))
    carry[...] = carry[...] + jnp.sum(both, axis=0, keepdims=True)
    cnt_ref[...] = carry[...]


def _dest_kernel(route_ref, rank_ref, cnt_ref, dest_ref):
    r = route_ref[...]
    lane = _iota(r.shape, 1)
    lane_f = lane.astype(F32)
    padded = jnp.floor((cnt_ref[...] + (MOE_BLOCK - 1.0)) * (1.0 / MOE_BLOCK)) * MOE_BLOCK
    before = (_iota((LANES, LANES), 0) < _iota((LANES, LANES), 1)).astype(F32)
    pad_start = jnp.dot(padded, before, precision=HI, preferred_element_type=F32)[0:1, :]
    oh0 = lane_f == r[:, 2:3]
    oh1 = lane_f == r[:, 3:4]
    d0 = jnp.sum(jnp.where(oh0, pad_start, 0.0), axis=-1, keepdims=True) + rank_ref[:, 0:1]
    d1 = jnp.sum(jnp.where(oh1, pad_start, 0.0), axis=-1, keepdims=True) + rank_ref[:, 1:2]
    dest_ref[...] = jnp.where(lane == 0, d0, jnp.where(lane == 1, d1, 0.0))


def _dispatch_plan(route):
    T = route.shape[0]
    tm = min(512, T)
    slab = pl.BlockSpec((tm, LANES), lambda i: (i, 0))
    cnt_spec = pl.BlockSpec((SUBLANES, LANES), lambda i: (0, 0))
    rank, cnt = pl.pallas_call(
        _rank_kernel,
        out_shape=(jax.ShapeDtypeStruct((T, LANES), F32), jax.ShapeDtypeStruct((SUBLANES, LANES), F32)),
        grid=(T // tm,),
        in_specs=[slab],
        out_specs=(slab, cnt_spec),
        scratch_shapes=[pltpu.VMEM((SUBLANES, LANES), F32)],
        compiler_params=_cparams("arbitrary"),
        name="moe_rank",
    )(route)
    dest = pl.pallas_call(
        _dest_kernel,
        out_shape=jax.ShapeDtypeStruct((T, LANES), F32),
        grid=(T // tm,),
        in_specs=[slab, slab, cnt_spec],
        out_specs=slab,
        compiler_params=_cparams("parallel"),
        name="moe_dest",
    )(route, rank, cnt)
    dest = dest[:, :TOP_K].astype(jnp.int32).reshape(T * TOP_K)
    counts = cnt[0, :N_EXPERTS].astype(jnp.int32)
    padded = (counts + MOE_BLOCK - 1) // MOE_BLOCK * MOE_BLOCK
    pad_end = jnp.cumsum(padded)
    n_blocks = -(-(T * TOP_K) // MOE_BLOCK) + N_EXPERTS
    block_row0 = jnp.arange(n_blocks, dtype=jnp.int32) * MOE_BLOCK
    block_expert = jnp.minimum(jnp.searchsorted(pad_end, block_row0, side="right"),
                               N_EXPERTS - 1).astype(jnp.int32)
    n_used = (pad_end[-1] // MOE_BLOCK).astype(jnp.int32).reshape(1)
    return dest, block_expert, n_used, n_blocks * MOE_BLOCK


def kernel(x, c, positions, norm1_g, norm2_g, ada_w, ada_b, w_in, lru_conv_w, lru_conv_b, lru_w_r, lru_b_r, lru_w_i, lru_b_i, lru_lambda, sgu_norm_g, sgu_w, sgu_b, rwkv_mu_prev, rwkv_mu_next, rwkv_w0, rwkv_w_up, rwkv_a0, rwkv_a_up, rwkv_g_up, rwkv_k_k, rwkv_k_a, rwkv_r_k, w_branch, w_out, router_grp_w, router_grp_b, router_exp_w, router_exp_b, expert_w_gu, expert_w_down, final_norm_g):
    B, S, D = x.shape
    L = ada_w.shape[0]
    W = D // 2
    T = B * S
    feat_w = rwkv_mu_prev.shape[1]

    c_pad = jnp.zeros((SUBLANES, D), F32).at[:B].set(c)
    mod = _ada(c_pad, ada_w, ada_b)

    dk = D // 2 // (2 * RET_HEADS)
    inv_freq = ROPE_THETA ** (-jnp.arange(0, dk, 2, dtype=F32) / dk)
    ang = positions.astype(F32)[..., None] * inv_freq
    cosf, sinf = _rope_tables(jnp.tile(ang, (1, 1, LANES // (dk // 2))))
    log_gamma = np.log1p(-np.exp2(-5.0 - np.arange(RET_HEADS, dtype=np.float64)))
    lg_tab = jnp.asarray(np.broadcast_to(log_gamma[:, None], (RET_HEADS, LANES)), F32)

    x2d = x.reshape(T, D)
    for l in range(L):
        m = mod[l, :B]
        sh1, sc1, g1, sh2, sc2, g2 = [m[:, None, k * D:(k + 1) * D] for k in range(6)]

        h = _normmod(x2d.reshape(B, S, D), norm1_g[l], sc1, sh1, BF16).reshape(T, D)
        n_a = 7 * W
        proj_a = _mm(h, w_in, l, 0, n_a, 2048, 512, name="proj_a").reshape(B, S, n_a)
        feat = _mm(h, w_in, l, n_a, feat_w, 1024, feat_w // 3, name="proj_rwkv").reshape(B, S, feat_w)
        logits = _mm(h, w_in, l, n_a + feat_w, N_BRANCH * D, 2048, 512, out_dtype=BF16,
                     name="proj_gate")

        y_a = _lru(proj_a, B, S, W, 0, W // LANES, lru_conv_w[l], lru_conv_b[l], lru_w_r[l],
                   lru_b_r[l], lru_w_i[l], lru_b_i[l], lru_lambda[l])
        y_b = _retention(proj_a, B, S, W, 2 * W // LANES, 5 * W // (2 * LANES),
                         3 * W // (2 * LANES), 4 * W // (2 * LANES), cosf, sinf, lg_tab)
        y_c = _sgu(proj_a, B, S, W, 5, 6, sgu_norm_g[l], sgu_w[l], sgu_b[l])
        r, kk, v, g, bonus, lw, bb, kt = _rwkv_pre(
            feat, rwkv_mu_prev[l], rwkv_mu_next[l], rwkv_w0[l], rwkv_w_up[l], rwkv_a0[l],
            rwkv_a_up[l], rwkv_g_up[l], rwkv_k_k[l], rwkv_k_a[l], rwkv_r_k[l].reshape(-1))
        y_d = _rwkv_post(*_rwkv_scan(r, kk, v, lw, bb, kt), bonus, g)

        merged = _merge([y.reshape(T, W) for y in (y_a, y_b, y_c, y_d)],
                        w_branch, l, logits, 1024, 512)
        x2d = _mm_residual(merged, w_out, l, x2d, g1, S, 2048, 512)

        h2 = _normmod(x2d.reshape(B, S, D), norm2_g[l], sc2, sh2, jnp.uint32).reshape(T, D // 2)
        n_r = N_GROUPS + N_EXPERTS
        w_r = jnp.zeros((D, LANES), F32).at[:, :N_GROUPS].set(router_grp_w[l]).at[:, N_GROUPS:n_r].set(
            router_exp_w[l]).astype(BF16)
        b_r = jnp.zeros((1, LANES), F32).at[0, :N_GROUPS].set(router_grp_b[l]).at[0, N_GROUPS:n_r].set(
            router_exp_b[l])
        route = _router(h2, w_r, b_r)
        dest, block_expert, n_used, cap = _dispatch_plan(route)
        ys = _experts(_dispatch(h2, dest, cap), block_expert, n_used, expert_w_gu, expert_w_down, l)
        x2d = _combine(x2d, ys, dest, route, g2, S, final_norm_g, l == L - 1)

    return x2d.reshape(B, S, D)
```

```python
import functools
import math

import jax
import jax.numpy as jnp
import numpy as np
from jax import lax
from jax.experimental import pallas as pl
from jax.experimental.pallas import tpu as pltpu

F32 = jnp.float32
BF16 = jnp.bfloat16
EPS = 1e-6

LANES = 128
SUBLANES = 8
VMEM_LIMIT_BYTES = 56 * 1024 * 1024

LRU_BLOCKS = 8
CONV_WIDTH = 4
LRU_C = 8.0
RET_HEADS = 8
RET_CHUNK = 128
ROPE_THETA = 10000.0
SGU_GROUPS = 8
SGU_CHUNK = 128
RWKV_HEAD_DIM = 64
DECAY_LORA = 64
ICLR_LORA = 64
GATE_LORA = 128
DECAY_SCALE = math.exp(-0.5)
N_GROUPS = 4
EXPERTS_PER_GROUP = 8
N_EXPERTS = N_GROUPS * EXPERTS_PER_GROUP
TOP_K = 2
MOE_BLOCK = 256
N_BRANCH = 4

RWKV_CHUNK = 64
HI = lax.Precision.HIGHEST


def _cparams(*sem):
    return pltpu.CompilerParams(dimension_semantics=sem, vmem_limit_bytes=VMEM_LIMIT_BYTES)


def _bdot(a, b):
    return jnp.dot(a.astype(BF16), b.astype(BF16), preferred_element_type=F32)


def _bdot_nt(a, b):
    return lax.dot_general(a.astype(BF16), b.astype(BF16), (((1,), (1,)), ((), ())),
                           preferred_element_type=F32)


def _gelu(x):
    return 0.5 * x * (1.0 + jnp.tanh(math.sqrt(2.0 / math.pi) * (x + 0.044715 * (x * x * x))))


def _sigmoid(x):
    return 0.5 * jnp.tanh(0.5 * x) + 0.5


def _silu(x):
    return x * _sigmoid(x)


def _iota(shape, dim):
    return lax.broadcasted_iota(jnp.int32, shape, dim)


def _ada_kernel(c_ref, w_ref, b_ref, o_ref):
    cond = _silu(c_ref[...])
    o_ref[0] = _bdot(cond, w_ref[0]) + b_ref[0]


def _ada(c_pad, ada_w, ada_b):
    L, D, N = ada_w.shape
    R = c_pad.shape[0]
    tn = 1024
    return pl.pallas_call(
        _ada_kernel,
        out_shape=jax.ShapeDtypeStruct((L, R, N), F32),
        grid=(L, N // tn),
        in_specs=[pl.BlockSpec((R, D), lambda l, j: (0, 0)),
                  pl.BlockSpec((1, D, tn), lambda l, j: (l, 0, j)),
                  pl.BlockSpec((1, 1, tn), lambda l, j: (l, 0, j))],
        out_specs=pl.BlockSpec((1, R, tn), lambda l, j: (l, 0, j)),
        compiler_params=_cparams("parallel", "parallel"),
        name="ada_mod",
    )(c_pad, ada_w, ada_b.reshape(L, 1, N))


def _pack_bf16_pairs(y):
    half = y.shape[1] // 2
    bits = pltpu.bitcast(y.astype(BF16).astype(F32), jnp.uint32)
    return (bits[:, :half] >> 16) | (bits[:, half:] & jnp.uint32(0xFFFF0000))


def _unpack_bf16_pairs(u):
    lo = pltpu.bitcast(u << 16, F32)
    hi = pltpu.bitcast(u & jnp.uint32(0xFFFF0000), F32)
    return jnp.concatenate([lo, hi], axis=1).astype(BF16)


def _normmod_kernel(x_ref, g_ref, sc_ref, sh_ref, o_ref):
    x = x_ref[0]
    y = x * lax.rsqrt(jnp.mean(x * x, axis=-1, keepdims=True) + EPS) * g_ref[...]
    y = y * (1.0 + sc_ref[0]) + sh_ref[0]
    if o_ref.dtype == jnp.uint32:
        o_ref[0] = _pack_bf16_pairs(y)
    else:
        o_ref[0] = y.astype(o_ref.dtype)


def _normmod(x, g, sc, sh, out_dtype):
    B, S, D = x.shape
    ts = min(512, S)
    d_out = D // 2 if out_dtype == jnp.uint32 else D
    return pl.pallas_call(
        _normmod_kernel,
        out_shape=jax.ShapeDtypeStruct((B, S, d_out), out_dtype),
        grid=(B, S // ts),
        in_specs=[pl.BlockSpec((1, ts, D), lambda b, i: (b, i, 0)),
                  pl.BlockSpec((1, D), lambda b, i: (0, 0)),
                  pl.BlockSpec((1, 1, D), lambda b, i: (b, 0, 0)),
                  pl.BlockSpec((1, 1, D), lambda b, i: (b, 0, 0))],
        out_specs=pl.BlockSpec((1, ts, d_out), lambda b, i: (b, i, 0)),
        compiler_params=_cparams("parallel", "parallel"),
        name="norm_mod",
    )(x, g.reshape(1, D), sc, sh)


def _mm_kernel(a_ref, w_ref, o_ref):
    o_ref[...] = jnp.dot(a_ref[...], w_ref[0].astype(BF16),
                         preferred_element_type=F32).astype(o_ref.dtype)


def _mm(a, w, layer, col0, n_cols, tm, tn, out_dtype=F32, name="proj"):
    M, K = a.shape
    return pl.pallas_call(
        _mm_kernel,
        out_shape=jax.ShapeDtypeStruct((M, n_cols), out_dtype),
        grid=(M // tm, n_cols // tn),
        in_specs=[pl.BlockSpec((tm, K), lambda i, j: (i, 0)),
                  pl.BlockSpec((pl.Element(1), pl.Element(K), pl.Element(tn)),
                               lambda i, j: (layer, 0, pl.multiple_of(col0 + j * tn, LANES)))],
        out_specs=pl.BlockSpec((tm, tn), lambda i, j: (i, j)),
        compiler_params=_cparams("parallel", "parallel"),
        name=name,
    )(a, w)


def _mm_res_kernel(a_ref, w_ref, x_ref, g_ref, o_ref):
    y = jnp.dot(a_ref[...], w_ref[0].astype(BF16), preferred_element_type=F32)
    o_ref[...] = x_ref[...] + g_ref[0] * y


def _mm_residual(a, w, layer, x2d, gate, rows_per_batch, tm, tn, name="out_proj"):
    M, K = a.shape
    N = w.shape[2]
    tiles_per_batch = rows_per_batch // tm
    return pl.pallas_call(
        _mm_res_kernel,
        out_shape=jax.ShapeDtypeStruct((M, N), F32),
        grid=(M // tm, N // tn),
        in_specs=[pl.BlockSpec((tm, K), lambda i, j: (i, 0)),
                  pl.BlockSpec((1, K, tn), lambda i, j: (layer, 0, j)),
                  pl.BlockSpec((tm, tn), lambda i, j: (i, j)),
                  pl.BlockSpec((1, 1, tn), lambda i, j: (i // tiles_per_batch, 0, j))],
        out_specs=pl.BlockSpec((tm, tn), lambda i, j: (i, j)),
        compiler_params=_cparams("parallel", "parallel"),
        name=name,
    )(a, w, x2d, gate)


def _softplus(x):
    return jnp.maximum(x, 0.0) + jnp.log1p(jnp.exp(-jnp.abs(x)))


def _lru_kernel(x_ref, gate_ref, cw_ref, cb_ref, wr_ref, br_ref, wi_ref, bi_ref, lam_ref,
                o_ref, a_scr, u_scr, h_scr):
    S = x_ref.shape[1]
    x = x_ref[0]
    rows = _iota(x.shape, 0)
    xm2 = jnp.where(rows >= 2, pltpu.roll(x, 2, 0), 0.0)
    xm1 = jnp.where(rows >= 1, pltpu.roll(x, 1, 0), 0.0)
    xp1 = jnp.where(rows < S - 1, pltpu.roll(x, S - 1, 0), 0.0)
    xc = (cw_ref[0:1, :] * xm2 + cw_ref[1:2, :] * xm1 + cw_ref[2:3, :] * x
          + cw_ref[3:4, :] * xp1 + cb_ref[...])
    for z in range(2):
        r = _sigmoid(_bdot(xc, wr_ref[z, 0]) + br_ref[z:z + 1, :])
        i = _sigmoid(_bdot(xc, wi_ref[z, 0]) + bi_ref[z:z + 1, :])
        log_a = -LRU_C * r * _softplus(-lam_ref[z:z + 1, :])
        a = jnp.exp(log_a)
        a_scr[z] = a
        u_scr[z] = jnp.sqrt(-jnp.tanh(log_a) * (a * a + 1.0)) * i * xc

    n_tiles = S // SUBLANES
    L = x.shape[1]
    trow = _iota((SUBLANES, L), 0)

    def tile_scan(a, u, reverse):
        for d in (1, 2, 4):
            if reverse:
                keep = trow < SUBLANES - d
                a_s = jnp.where(keep, pltpu.roll(a, SUBLANES - d, 0), 1.0)
                u_s = jnp.where(keep, pltpu.roll(u, SUBLANES - d, 0), 0.0)
            else:
                keep = trow >= d
                a_s = jnp.where(keep, pltpu.roll(a, d, 0), 1.0)
                u_s = jnp.where(keep, pltpu.roll(u, d, 0), 0.0)
            u = u + a * u_s
            a = a * a_s
        return a, u

    def body(k, carry):
        hf, hb = carry
        f0 = pl.multiple_of(k * SUBLANES, SUBLANES)
        b0 = pl.multiple_of((n_tiles - 1 - k) * SUBLANES, SUBLANES)
        af, uf = tile_scan(a_scr[0, pl.ds(f0, SUBLANES), :], u_scr[0, pl.ds(f0, SUBLANES), :], False)
        ab, ub = tile_scan(a_scr[1, pl.ds(b0, SUBLANES), :], u_scr[1, pl.ds(b0, SUBLANES), :], True)
        tf = uf + af * hf
        tb = ub + ab * hb
        h_scr[0, pl.ds(f0, SUBLANES), :] = tf
        h_scr[1, pl.ds(b0, SUBLANES), :] = tb
        return tf[SUBLANES - 1:SUBLANES, :], tb[0:1, :]

    zero = jnp.zeros((1, L), F32)
    lax.fori_loop(0, n_tiles, body, (zero, zero), unroll=8)
    o_ref[0] = ((h_scr[0] + h_scr[1]) * _gelu(gate_ref[0])).astype(o_ref.dtype)


def _lru(proj, B, S, W, x_col, g_col, cw, cb, wr, br, wi, bi, lam):
    nb = W // LANES
    return pl.pallas_call(
        _lru_kernel,
        out_shape=jax.ShapeDtypeStruct((B, S, W), BF16),
        grid=(B, nb),
        in_specs=[pl.BlockSpec((1, S, LANES), lambda b, j: (b, 0, x_col + j)),
                  pl.BlockSpec((1, S, LANES), lambda b, j: (b, 0, g_col + j)),
                  pl.BlockSpec((CONV_WIDTH, LANES), lambda b, j: (0, j)),
                  pl.BlockSpec((1, LANES), lambda b, j: (0, j)),
                  pl.BlockSpec((2, 1, LANES, LANES), lambda b, j: (0, j, 0, 0)),
                  pl.BlockSpec((2, LANES), lambda b, j: (0, j)),
                  pl.BlockSpec((2, 1, LANES, LANES), lambda b, j: (0, j, 0, 0)),
                  pl.BlockSpec((2, LANES), lambda b, j: (0, j)),
                  pl.BlockSpec((2, LANES), lambda b, j: (0, j))],
        out_specs=pl.BlockSpec((1, S, LANES), lambda b, j: (b, 0, j)),
        scratch_shapes=[pltpu.VMEM((2, S, LANES), F32)] * 3,
        compiler_params=_cparams("parallel", "parallel"),
        name="rglru",
    )(proj, proj, cw, cb.reshape(1, W), wr, br, wi, bi, lam)


def _rope_kernel(ang_ref, cos_ref, sin_ref):
    ang = ang_ref[0]
    lane = _iota(ang.shape, 1)
    cos_ref[0] = jnp.cos(ang)
    s = jnp.sin(ang)
    sin_ref[0] = jnp.where((lane % 64) < 32, -s, s)


def _rope_tables(ang):
    B, S, _ = ang.shape
    ts = min(512, S)
    spec = pl.BlockSpec((1, ts, LANES), lambda b, i: (b, i, 0))
    return pl.pallas_call(
        _rope_kernel,
        out_shape=(jax.ShapeDtypeStruct((B, S, LANES), F32),) * 2,
        grid=(B, S // ts),
        in_specs=[spec],
        out_specs=(spec, spec),
        compiler_params=_cparams("parallel", "parallel"),
        name="rope_tables",
    )(ang)


def _ret_kernel(q_ref, k_ref, v_ref, g_ref, cos_ref, sin_ref, lg_ref, o_ref,
                qs_scr, ks_scr, kv_scr):
    S = q_ref.shape[1]
    C = RET_CHUNK
    N = S // C
    dk = 64
    p = pl.program_id(1)
    lane = _iota((S, LANES), 1)
    first_half = (lane % 64) < 32

    def rope(x):
        swapped = jnp.where(first_half, pltpu.roll(x, LANES - 32, 1), pltpu.roll(x, 32, 1))
        return x * cos_ref[0] + swapped * sin_ref[0]

    qs_scr[...] = rope(q_ref[0]) * (dk ** -0.5)
    ks_scr[...] = rope(k_ref[0])

    pos_r = _iota((C, 1), 0).astype(F32)
    pos_l = _iota((1, C), 1).astype(F32)
    ii = _iota((C, C), 0)
    jj = _iota((C, C), 1)
    dist = jnp.abs(ii - jj).astype(F32)
    clane = _iota((C, LANES), 1)

    heads = range(2)
    lg = [lg_ref[pl.ds(2 * p + hh, 1), :][:, 0:1] for hh in heads]
    hmask = [(clane // 64) == hh for hh in heads]
    hlanes = [slice(hh * LANES, (hh + 1) * LANES) for hh in heads]
    intra = [jnp.exp(dist * x) for x in lg]
    dec_kf_row = [jnp.exp((C - 1.0 - pos_l) * x) for x in lg]
    dec_kb_row = [jnp.exp(pos_l * x) for x in lg]
    dec_qf = [jnp.exp((pos_r + 1.0) * x) for x in lg]
    dec_qb = [jnp.exp((C - pos_r) * x) for x in lg]
    chunk_decay = [jnp.exp(C * x) for x in lg]

    def kv_body(n, _):
        r0 = pl.multiple_of(n * C, C)
        ks = ks_scr[pl.ds(r0, C), :]
        for hh in heads:
            kt = jnp.where(hmask[hh], ks, 0.0).T
            vc = v_ref[0, pl.ds(r0, C), hlanes[hh]]
            kv_scr[hh, n] = _bdot(
                jnp.concatenate([kt * dec_kf_row[hh], kt * dec_kb_row[hh]], axis=0), vc)
        return 0

    lax.fori_loop(0, N, kv_body, 0, unroll=4)

    def state_body(half):
        def body(m, sts):
            n = m if half == 0 else N - 1 - m
            new = []
            for hh in heads:
                cur = kv_scr[hh, n, half * LANES:(half + 1) * LANES, :]
                kv_scr[hh, n, half * LANES:(half + 1) * LANES, :] = sts[hh]
                new.append(chunk_decay[hh] * sts[hh] + cur)
            return tuple(new)
        return body

    zero_state = (jnp.zeros((LANES, LANES), F32),) * 2
    lax.fori_loop(0, N, state_body(0), zero_state)
    lax.fori_loop(0, N, state_body(1), zero_state)

    def out_body(n, _):
        r0 = pl.multiple_of(n * C, C)
        qc = qs_scr[pl.ds(r0, C), :]
        ks = ks_scr[pl.ds(r0, C), :]
        for hh in heads:
            kc = jnp.where(hmask[hh], ks, 0.0)
            vc = v_ref[0, pl.ds(r0, C), hlanes[hh]]
            scores = _bdot_nt(qc, kc) * intra[hh]
            o = _bdot(scores, vc) + _bdot(
                jnp.concatenate([qc * dec_qf[hh], qc * dec_qb[hh]], axis=1), kv_scr[hh, n])
            mu = jnp.mean(o, axis=-1, keepdims=True)
            var = jnp.mean(o * o, axis=-1, keepdims=True) - mu * mu
            o = (o - mu) * lax.rsqrt(var + EPS)
            gc = g_ref[0, pl.ds(r0, C), hlanes[hh]]
            o_ref[0, pl.ds(r0, C), hlanes[hh]] = (_silu(gc) * o).astype(o_ref.dtype)
        return 0

    lax.fori_loop(0, N, out_body, 0, unroll=4)


def _retention(proj, B, S, W, q_col, k_col, v_col, g_col, cosf, sinf, lg_tab):
    n_pairs = RET_HEADS // 2
    N = S // RET_CHUNK
    return pl.pallas_call(
        _ret_kernel,
        out_shape=jax.ShapeDtypeStruct((B, S, W), BF16),
        grid=(B, n_pairs),
        in_specs=[pl.BlockSpec((1, S, LANES), lambda b, p: (b, 0, q_col + p)),
                  pl.BlockSpec((1, S, LANES), lambda b, p: (b, 0, k_col + p)),
                  pl.BlockSpec((1, S, 2 * LANES), lambda b, p: (b, 0, v_col + p)),
                  pl.BlockSpec((1, S, 2 * LANES), lambda b, p: (b, 0, g_col + p)),
                  pl.BlockSpec((1, S, LANES), lambda b, p: (b, 0, 0)),
                  pl.BlockSpec((1, S, LANES), lambda b, p: (b, 0, 0)),
                  pl.BlockSpec((RET_HEADS, LANES), lambda b, p: (0, 0))],
        out_specs=pl.BlockSpec((1, S, 2 * LANES), lambda b, p: (b, 0, p)),
        scratch_shapes=[pltpu.VMEM((S, LANES), F32), pltpu.VMEM((S, LANES), F32),
                        pltpu.VMEM((2, N, 2 * LANES, LANES), F32)],
        compiler_params=_cparams("parallel", "parallel"),
        name="retention",
    )(proj, proj, proj, proj, cosf, sinf, lg_tab)


def _sgu_kernel(u_ref, v_ref, ng_ref, w_ref, bt_ref, o_ref):
    ts = u_ref.shape[1]
    C = SGU_CHUNK
    v = _gelu(v_ref[0])
    mu = jnp.mean(v, axis=-1, keepdims=True)
    var = jnp.mean(jnp.square(v - mu), axis=-1, keepdims=True)
    v = ((v - mu) * lax.rsqrt(var + EPS) * ng_ref[...]).astype(BF16)
    for c in range(ts // C):
        for g in range(SGU_GROUPS):
            vc = v[c * C:(c + 1) * C, g * LANES:(g + 1) * LANES]
            mixed = jnp.dot(w_ref[g].astype(BF16), vc, preferred_element_type=F32) + bt_ref[:, g:g + 1]
            uc = _gelu(u_ref[0, c * C:(c + 1) * C, g * LANES:(g + 1) * LANES])
            o_ref[0, c * C:(c + 1) * C, g * LANES:(g + 1) * LANES] = (uc * mixed).astype(o_ref.dtype)


def _sgu(proj, B, S, W, u_col, v_col, norm_g, w_s, b_s):
    ts = min(512, S)
    return pl.pallas_call(
        _sgu_kernel,
        out_shape=jax.ShapeDtypeStruct((B, S, W), BF16),
        grid=(B, S // ts),
        in_specs=[pl.BlockSpec((1, ts, W), lambda b, i: (b, i, u_col)),
                  pl.BlockSpec((1, ts, W), lambda b, i: (b, i, v_col)),
                  pl.BlockSpec((1, W), lambda b, i: (0, 0)),
                  pl.BlockSpec((SGU_GROUPS, SGU_CHUNK, SGU_CHUNK), lambda b, i: (0, 0, 0)),
                  pl.BlockSpec((SGU_CHUNK, SGU_GROUPS), lambda b, i: (0, 0))],
        out_specs=pl.BlockSpec((1, ts, W), lambda b, i: (b, i, 0)),
        compiler_params=_cparams("parallel", "parallel"),
        name="spatial_gating",
    )(proj, proj, norm_g.reshape(1, W), w_s, b_s.T)


def _head_sum(x, hd):
    seg = (_iota((LANES, LANES), 0) // hd == _iota((LANES, LANES), 1) // hd).astype(BF16)
    hi = x.astype(BF16)
    lo = (x - hi.astype(F32)).astype(BF16)
    parts = [jnp.dot(hi[:, j * LANES:(j + 1) * LANES], seg, preferred_element_type=F32)
             + jnp.dot(lo[:, j * LANES:(j + 1) * LANES], seg, preferred_element_type=F32)
             for j in range(x.shape[-1] // LANES)]
    return jnp.concatenate(parts, axis=1)


def _rwkv_pre_kernel(f_ref, fp_ref, fn_ref, mup_ref, mun_ref, w0_ref, wup_ref, a0_ref, aup_ref,
                     gup_ref, kk_ref, ka_ref, rk_ref,
                     r_out, kk_out, v_out, g_out, bonus_out, lw_out, bb_out, kt_out):
    ts = f_ref.shape[1]
    W = r_out.shape[2]
    i = pl.program_id(1)
    n_i = pl.num_programs(1)
    f = f_ref[0]
    rows = _iota(f.shape, 0)
    prev_row = jnp.where(i > 0, fp_ref[0, SUBLANES - 1:SUBLANES, :], 0.0)
    next_row = jnp.where(i < n_i - 1, fn_ref[0, 0:1, :], 0.0)
    prev = jnp.where(rows >= 1, pltpu.roll(f, 1, 0), prev_row)
    nxt = jnp.where(rows < ts - 1, pltpu.roll(f, ts - 1, 0), next_row)
    f = f + mup_ref[...] * (prev - f) + mun_ref[...] * (nxt - f)

    r = f[:, 0:W]
    k = f[:, W:2 * W]
    v = f[:, 2 * W:3 * W]
    o = 3 * W
    wd = jnp.tanh(f[:, o:o + 2 * DECAY_LORA])
    ad = f[:, o + 2 * DECAY_LORA:o + 2 * DECAY_LORA + 2 * ICLR_LORA]
    gd = f[:, o + 2 * DECAY_LORA + 2 * ICLR_LORA:]

    g_out[0] = _bdot(_sigmoid(gd), gup_ref[...]).astype(g_out.dtype)
    kk = k * kk_ref[...]
    kk = kk * lax.rsqrt(_head_sum(kk * kk, RWKV_HEAD_DIM) + EPS)
    r_out[0] = r.astype(r_out.dtype)
    kk_out[0] = kk.astype(kk_out.dtype)
    v_out[0] = v.astype(v_out.dtype)
    bonus = jnp.zeros_like(r)
    for z in range(2):
        w_raw = _bdot(wd[:, z * DECAY_LORA:(z + 1) * DECAY_LORA], wup_ref[z]) + w0_ref[z:z + 1, :]
        lw_out[z, 0] = -DECAY_SCALE * _sigmoid(w_raw)
        a = _sigmoid(_bdot(ad[:, z * ICLR_LORA:(z + 1) * ICLR_LORA], aup_ref[z]) + a0_ref[z:z + 1, :])
        bb_out[z, 0] = (a * kk).astype(bb_out.dtype)
        kt = k * (1.0 + (a - 1.0) * ka_ref[...])
        kt_out[z, 0] = kt.astype(kt_out.dtype)
        bonus = bonus + _head_sum(r * kt * rk_ref[...], RWKV_HEAD_DIM) * v
    bonus_out[0] = bonus.astype(bonus_out.dtype)


def _rwkv_pre(feat, mu_prev, mu_next, w0, w_up, a0, a_up, g_up, k_k, k_a, r_k):
    B, S, Fw = feat.shape
    W = w0.shape[1]
    ts = min(256, S)
    hb = ts // SUBLANES
    n_hb = S // SUBLANES
    row = lambda a: a.reshape(1, -1)
    full2 = lambda a: pl.BlockSpec(a.shape, lambda b, i: (0, 0))
    full3 = lambda a: pl.BlockSpec(a.shape, lambda b, i: (0, 0, 0))
    bsw = pl.BlockSpec((1, ts, W), lambda b, i: (b, i, 0))
    zsw = pl.BlockSpec((2, 1, ts, W), lambda b, i: (0, b, i, 0))
    sds = jax.ShapeDtypeStruct((B, S, W), F32)
    zds = jax.ShapeDtypeStruct((2, B, S, W), F32)
    sdh = jax.ShapeDtypeStruct((B, S, W), BF16)
    zdh = jax.ShapeDtypeStruct((2, B, S, W), BF16)
    args = (row(mu_prev), row(mu_next), w0, w_up, a0, a_up, g_up, row(k_k), row(k_a), row(r_k))
    specs = [full2(args[0]), full2(args[1]), full2(w0), full3(w_up), full2(a0), full3(a_up),
             full2(g_up), full2(args[7]), full2(args[8]), full2(args[9])]
    return pl.pallas_call(
        _rwkv_pre_kernel,
        out_shape=(sdh, sdh, sdh, sdh, sdh, zds, zdh, zdh),
        grid=(B, S // ts),
        in_specs=[pl.BlockSpec((1, ts, Fw), lambda b, i: (b, i, 0)),
                  pl.BlockSpec((1, SUBLANES, Fw), lambda b, i: (b, jnp.maximum(i * hb - 1, 0), 0)),
                  pl.BlockSpec((1, SUBLANES, Fw), lambda b, i: (b, jnp.minimum((i + 1) * hb, n_hb - 1), 0)),
                  ] + specs,
        out_specs=(bsw, bsw, bsw, bsw, bsw, zsw, zsw, zsw),
        compiler_params=_cparams("parallel", "parallel"),
        name="rwkv_prepare",
    )(feat, feat, feat, *args)


def _rwkv_scan_kernel(rf_ref, kkf_ref, vf_ref, rb_ref, kkb_ref, vb_ref,
                      lwf_ref, bbf_ref, ktf_ref, lwb_ref, bbb_ref, ktb_ref, yf_ref, yb_ref, st_scr):
    C = RWKV_CHUNK
    C2 = 2 * C
    n_pairs = rf_ref.shape[2] // LANES

    @pl.when(pl.program_id(1) == 0)
    def _():
        st_scr[...] = jnp.zeros_like(st_scr)

    wide_t = _iota((C, C2), 0)
    wide_s = _iota((C, C2), 1) % C
    eye_w = (wide_t == wide_s).astype(F32)
    emask = (_iota((C2, LANES), 0) // C) == (_iota((C2, LANES), 1) // RWKV_HEAD_DIM)
    bmask = (_iota((C2, C2), 0) // C) == (_iota((C2, C2), 1) // C)
    diag_l = _iota((LANES, LANES), 0) == _iota((LANES, LANES), 1)
    tri = _iota((C, C), 0) - _iota((C, C), 1)
    cum_mat = [(sg * tri >= 0).astype(BF16) for sg in (1, -1)]
    strict = [sg * (wide_t - wide_s) > 0 for sg in (1, -1)]
    incl = [sg * (wide_t - wide_s) >= 0 for sg in (1, -1)]
    refs = [(rf_ref, kkf_ref, vf_ref, lwf_ref, bbf_ref, ktf_ref, yf_ref),
            (rb_ref, kkb_ref, vb_ref, lwb_ref, bbb_ref, ktb_ref, yb_ref)]

    def expand(x):
        return jnp.where(emask, jnp.concatenate([x, x], axis=0), 0.0)

    def blockdiag(a):
        return jnp.where(bmask, jnp.concatenate([a, a], axis=0), 0.0)

    def bf(x):
        return x.astype(BF16)

    def mm(a, b):
        return jnp.dot(a, b, preferred_element_type=F32)

    def cumsum3(m, x):
        hi = bf(x)
        r1 = x - hi.astype(F32)
        mid = bf(r1)
        lo = bf(r1 - mid.astype(F32))
        return mm(m, hi) + (mm(m, mid) + mm(m, lo))

    units = [(z, j) for z in range(2) for j in range(n_pairs)]
    zs = [z for z, _ in units]
    sl = [slice(j * LANES, (j + 1) * LANES) for _, j in units]
    r_in = [refs[z][0][0, :, s] for z, s in zip(zs, sl)]
    kk_in = [refs[z][1][0, :, s] for z, s in zip(zs, sl)]
    v_in = [refs[z][2][0, :, s] for z, s in zip(zs, sl)]
    lw = [refs[z][3][0, 0, :, s] for z, s in zip(zs, sl)]
    bb_in = [refs[z][4][0, 0, :, s] for z, s in zip(zs, sl)]
    kt_in = [refs[z][5][0, 0, :, s] for z, s in zip(zs, sl)]

    cum = [cumsum3(cum_mat[z], x) for z, x in zip(zs, lw)]
    tot = [jnp.sum(x, axis=0, keepdims=True) for x in lw]
    p_inv = [jnp.exp(-c) for c in cum]
    p_rest = [jnp.exp(t - c) for t, c in zip(tot, cum)]
    kh_c = [bf(k * jnp.exp(c - x)) for k, c, x in zip(kk_in, cum, lw)]
    rh_c = [bf(r * jnp.exp(c)) for r, c in zip(r_in, cum)]
    kh_e = [expand(k) for k in kh_c]
    bh_e = [expand(bf(b * p)) for b, p in zip(bb_in, p_inv)]
    ktil_e = [expand(bf(k * p)) for k, p in zip(kt_in, p_inv)]
    b_rest_t = [bf(expand(b * p).T) for b, p in zip(bb_in, p_rest)]
    k_rest_t = [bf(expand(k * p).T) for k, p in zip(kt_in, p_rest)]
    v_e = [expand(v) for v in v_in]

    a_all = [lax.dot_general(jnp.concatenate([k, r], axis=0), jnp.concatenate([b, t], axis=0),
                             (((1,), (1,)), ((), ())), preferred_element_type=F32)
             for k, r, b, t in zip(kh_c, rh_c, bh_e, ktil_e)]
    a_b = [jnp.where(strict[z], a[0:C, 0:C2], 0.0) for z, a in zip(zs, a_all)]
    a_k = [jnp.where(strict[z], a[0:C, C2:], 0.0) for z, a in zip(zs, a_all)]
    a_rb = [jnp.where(incl[z], a[C:, 0:C2], 0.0) for z, a in zip(zs, a_all)]
    a_rk = [jnp.where(incl[z], a[C:, C2:], 0.0) for z, a in zip(zs, a_all)]

    pw = [bf(-a) for a in a_b]
    t_inv = [eye_w] * len(units)
    n_dbl = int(math.log2(C))
    for k in range(n_dbl):
        if k < n_dbl - 1:
            both = [mm(p, jnp.concatenate([blockdiag(p), blockdiag(bf(t))], axis=1))
                    for p, t in zip(pw, t_inv)]
            pw = [bf(x[:, 0:C2]) for x in both]
            t_inv = [t + x[:, C2:] for t, x in zip(t_inv, both)]
        else:
            t_inv = [t + mm(p, blockdiag(bf(t))) for p, t in zip(pw, t_inv)]

    akv = [mm(bf(jnp.concatenate([ak, ark], axis=0)), v)
           for ak, ark, v in zip(a_k, a_rk, v_e)]
    tw = [mm(bf(t), jnp.concatenate([k, expand(bf(x[0:C]))], axis=1))
          for t, k, x in zip(t_inv, kh_e, akv)]
    st = [st_scr[z, j] for z, j in units]
    ws = [mm(jnp.concatenate([bf(x[:, 0:LANES]), r], axis=0), bf(s))
          for x, r, s in zip(tw, rh_c, st)]
    u_c = [x[0:C] + t[:, LANES:] for x, t in zip(ws, tw)]
    y_c = [x[C:] + k[C:] - mm(bf(a), expand(bf(u))) for x, k, a, u in zip(ws, akv, a_rb, u_c)]
    for z, s, y in zip(zs, sl, y_c):
        refs[z][6][0, :, s] = y.astype(refs[z][6].dtype)

    wu_e = [jnp.concatenate([expand(bf(t[:, 0:LANES])), expand(bf(t[:, LANES:]))], axis=1) for t in tw]
    bw = [mm(b, x) for b, x in zip(b_rest_t, wu_e)]
    kv = [mm(k, v) for k, v in zip(k_rest_t, v_e)]
    for n, (z, j) in enumerate(units):
        m_t = jnp.where(diag_l, jnp.exp(tot[n]), 0.0) - bw[n][:, 0:LANES]
        g_t = kv[n] - bw[n][:, LANES:]
        st_scr[z, j] = mm(bf(m_t), bf(st[n])) + g_t


def _rwkv_scan(r, kk, v, lw, bb, kt):
    B, S, W = r.shape
    C = RWKV_CHUNK
    nc = S // C
    fwd = pl.BlockSpec((1, C, W), lambda b, i: (b, i, 0))
    bwd = pl.BlockSpec((1, C, W), lambda b, i: (b, nc - 1 - i, 0))
    fwd_z = pl.BlockSpec((1, 1, C, W), lambda b, i: (0, b, i, 0))
    bwd_z = pl.BlockSpec((1, 1, C, W), lambda b, i: (1, b, nc - 1 - i, 0))
    sds = jax.ShapeDtypeStruct((B, S, W), BF16)
    return pl.pallas_call(
        _rwkv_scan_kernel,
        out_shape=(sds, sds),
        grid=(B, nc),
        in_specs=[fwd, fwd, fwd, bwd, bwd, bwd, fwd_z, fwd_z, fwd_z, bwd_z, bwd_z, bwd_z],
        out_specs=(fwd, bwd),
        scratch_shapes=[pltpu.VMEM((2, W // LANES, LANES, LANES), F32)],
        compiler_params=_cparams("parallel", "arbitrary"),
        name="rwkv_scan",
    )(r, kk, v, r, kk, v, lw, bb, kt, lw, bb, kt)


def _rwkv_post_kernel(yf_ref, yb_ref, bonus_ref, g_ref, o_ref):
    hd = RWKV_HEAD_DIM
    acc = bonus_ref[0].astype(F32)
    for y_ref in (yf_ref, yb_ref):
        y = y_ref[0].astype(F32)
        d = y - _head_sum(y, hd) * (1.0 / hd)
        var = _head_sum(d * d, hd) * (1.0 / hd)
        acc = acc + d * lax.rsqrt(var + EPS)
    o_ref[0] = (acc * g_ref[0].astype(F32)).astype(o_ref.dtype)


def _rwkv_post(y_fwd, y_bwd, bonus, g):
    B, S, W = y_fwd.shape
    ts = min(256, S)
    bsw = pl.BlockSpec((1, ts, W), lambda b, i: (b, i, 0))
    return pl.pallas_call(
        _rwkv_post_kernel,
        out_shape=jax.ShapeDtypeStruct((B, S, W), BF16),
        grid=(B, S // ts),
        in_specs=[bsw, bsw, bsw, bsw],
        out_specs=bsw,
        compiler_params=_cparams("parallel", "parallel"),
        name="rwkv_finish",
    )(y_fwd, y_bwd, bonus, g)


def _merge_kernel(ya_ref, yb_ref, yc_ref, yd_ref, wb_ref, l0_ref, l1_ref, l2_ref, l3_ref, o_ref,
                  wb_bf):
    @pl.when(pl.program_id(1) == 0)
    def _():
        wb_bf[...] = wb_ref[0].astype(BF16)

    acc = None
    for n, (y_ref, l_ref) in enumerate(((ya_ref, l0_ref), (yb_ref, l1_ref), (yc_ref, l2_ref),
                                        (yd_ref, l3_ref))):
        br = jnp.dot(y_ref[...], wb_bf[n], preferred_element_type=F32)
        t = _sigmoid(l_ref[...].astype(F32)) * br
        acc = t if acc is None else acc + t
    o_ref[...] = acc.astype(o_ref.dtype)


def _merge(ys, wb, layer, logits, tm, tn):
    T, W = ys[0].shape
    D = wb.shape[3]
    nj = D // tn
    yspec = pl.BlockSpec((tm, W), lambda j, i: (i, 0))
    lspec = lambda n: pl.BlockSpec((tm, tn), lambda j, i: (i, n * nj + j))
    return pl.pallas_call(
        _merge_kernel,
        out_shape=jax.ShapeDtypeStruct((T, D), BF16),
        grid=(nj, T // tm),
        in_specs=[yspec, yspec, yspec, yspec,
                  pl.BlockSpec((1, N_BRANCH, W, tn), lambda j, i: (layer, 0, 0, j)),
                  lspec(0), lspec(1), lspec(2), lspec(3)],
        out_specs=pl.BlockSpec((tm, tn), lambda j, i: (i, j)),
        scratch_shapes=[pltpu.VMEM((N_BRANCH, W, tn), BF16)],
        compiler_params=_cparams("parallel", "arbitrary"),
        name="branch_merge",
    )(*ys, wb, logits, logits, logits, logits)


def _router_kernel(h_ref, w_ref, b_ref, o_ref):
    logits = jnp.dot(_unpack_bf16_pairs(h_ref[...]), w_ref[...], preferred_element_type=F32) + b_ref[...]
    lane = _iota(logits.shape, 1)
    lane_f = lane.astype(F32)
    neg = -3.0e38
    far = float(LANES)
    is_grp = lane < N_GROUPS
    gl = jnp.where(is_grp, logits, neg)
    gmax = jnp.max(gl, axis=-1, keepdims=True)
    gsum = jnp.sum(jnp.where(is_grp, jnp.exp(gl - gmax), 0.0), axis=-1, keepdims=True)
    grp_p = 1.0 / gsum
    grp_i = jnp.min(jnp.where(gl == gmax, lane_f, far), axis=-1, keepdims=True)
    lo = N_GROUPS + grp_i * EXPERTS_PER_GROUP
    in_grp = jnp.logical_and(lane_f >= lo, lane_f < lo + EXPERTS_PER_GROUP)
    el = jnp.where(in_grp, logits, neg)
    m1 = jnp.max(el, axis=-1, keepdims=True)
    i1 = jnp.min(jnp.where(el == m1, lane_f, far), axis=-1, keepdims=True)
    el2 = jnp.where(lane_f == i1, neg, el)
    m2 = jnp.max(el2, axis=-1, keepdims=True)
    i2 = jnp.min(jnp.where(el2 == m2, lane_f, far), axis=-1, keepdims=True)
    t = jnp.exp(m2 - m1)
    w1 = grp_p / (1.0 + t)
    w2 = grp_p * t / (1.0 + t)
    o_ref[...] = jnp.where(lane == 0, w1, jnp.where(lane == 1, w2, jnp.where(
        lane == 2, i1 - N_GROUPS, jnp.where(lane == 3, i2 - N_GROUPS, 0.0))))


def _router(h, w_r, b_r):
    T = h.shape[0]
    D = w_r.shape[0]
    tm = min(1024, T)
    return pl.pallas_call(
        _router_kernel,
        out_shape=jax.ShapeDtypeStruct((T, LANES), F32),
        grid=(T // tm,),
        in_specs=[pl.BlockSpec((tm, D // 2), lambda i: (i, 0)),
                  pl.BlockSpec((D, LANES), lambda i: (0, 0)),
                  pl.BlockSpec((1, LANES), lambda i: (0, 0))],
        out_specs=pl.BlockSpec((tm, LANES), lambda i: (i, 0)),
        compiler_params=_cparams("parallel"),
        name="router",
    )(h, w_r, b_r)


def _expert_kernel(be_ref, nb_ref, first_ref, slot_ref, nxt_ref, xs_ref, wgu_hbm, wd_hbm, o_ref,
                   wgu_f32, wd_f32, wgu_bf, wd_bf, sem, *, layer):
    i = pl.program_id(0)
    used = i < nb_ref[0]

    def weight_copies(e, s):
        return (pltpu.make_async_copy(wgu_hbm.at[layer, e], wgu_f32.at[s], sem.at[s, 0]),
                pltpu.make_async_copy(wd_hbm.at[layer, e], wd_f32.at[s], sem.at[s, 1]))

    @pl.when(jnp.logical_and(used, first_ref[i] == 1))
    def _():
        s = slot_ref[i]

        @pl.when(i == 0)
        def _():
            for cp in weight_copies(be_ref[i], s):
                cp.start()

        for cp in weight_copies(be_ref[i], s):
            cp.wait()

        @pl.when(nxt_ref[i] >= 0)
        def _():
            for cp in weight_copies(nxt_ref[i], 1 - s):
                cp.start()

        wgu_bf[...] = wgu_f32[s].astype(BF16)
        wd_bf[...] = wd_f32[s].astype(BF16)

    @pl.when(used)
    def _():
        gu = jnp.dot(_unpack_bf16_pairs(xs_ref[...]), wgu_bf[...], preferred_element_type=F32)
        eh = gu.shape[1] // 2
        mid = (_silu(gu[:, :eh]) * gu[:, eh:]).astype(BF16)
        o_ref[...] = jnp.dot(mid, wd_bf[...], preferred_element_type=F32)

    @pl.when(jnp.logical_not(used))
    def _():
        o_ref[...] = jnp.zeros_like(o_ref)


def _expert_runs(block_expert, n_used):
    n_blocks = block_expert.shape[0]
    idx = jnp.arange(n_blocks, dtype=jnp.int32)
    changed = jnp.logical_or(idx == 0, block_expert != jnp.roll(block_expert, 1))
    first = jnp.logical_and(changed, idx < n_used[0]).astype(jnp.int32)
    slot = (jnp.cumsum(first) - 1) % 2
    start_pos = jnp.where(first == 1, idx, n_blocks)
    later = lax.cummin(start_pos, axis=0, reverse=True)
    next_pos = jnp.concatenate([later[1:], jnp.full((1,), n_blocks, jnp.int32)])
    nxt = jnp.where(next_pos < n_blocks, block_expert[jnp.minimum(next_pos, n_blocks - 1)], -1)
    return first, slot.astype(jnp.int32), nxt.astype(jnp.int32)


def _experts(xs, block_expert, n_used, w_gu, w_down, layer):
    cap = xs.shape[0]
    D, H2 = w_gu.shape[2], w_gu.shape[3]
    n_blocks = cap // MOE_BLOCK
    first, slot, nxt = _expert_runs(block_expert, n_used)
    grid_spec = pltpu.PrefetchScalarGridSpec(
        num_scalar_prefetch=5,
        grid=(n_blocks,),
        in_specs=[pl.BlockSpec((MOE_BLOCK, D // 2),
                               lambda i, be, nb, fi, sl, nx: (jnp.minimum(i, nb[0] - 1), 0)),
                  pl.BlockSpec(memory_space=pl.ANY),
                  pl.BlockSpec(memory_space=pl.ANY)],
        out_specs=pl.BlockSpec((MOE_BLOCK, D), lambda i, be, nb, fi, sl, nx: (i, 0)),
        scratch_shapes=[pltpu.VMEM((2, D, H2), F32), pltpu.VMEM((2, H2 // 2, D), F32),
                        pltpu.VMEM((D, H2), BF16), pltpu.VMEM((H2 // 2, D), BF16),
                        pltpu.SemaphoreType.DMA((2, 2))],
    )
    return pl.pallas_call(
        functools.partial(_expert_kernel, layer=layer),
        out_shape=jax.ShapeDtypeStruct((cap, D), F32),
        grid_spec=grid_spec,
        compiler_params=_cparams("arbitrary"),
        name="experts",
    )(block_expert, n_used, first, slot, nxt, xs, w_gu, w_down)


def _row_copy(src_ref, s, dst_ref, d, sem):
    return pltpu.make_async_copy(src_ref.at[pl.ds(s, 1), :], dst_ref.at[pl.ds(d, 1), :], sem)


def _rows_wait(src_ref, dst_ref, n, sem):
    pltpu.make_async_copy(src_ref.at[pl.ds(0, n), :], dst_ref.at[pl.ds(0, n), :], sem).wait()


def _combine_kernel(dest_ref, ys_ref, x_ref, r_ref, g_ref, ng_ref, o_ref, y0_buf, y1_buf, sem, *,
                    final_norm):
    tm = x_ref.shape[0]
    base = pl.program_id(0) * tm

    def issue(k, c):
        t = base + k
        _row_copy(ys_ref, dest_ref[TOP_K * t], y0_buf, k, sem.at[0]).start()
        _row_copy(ys_ref, dest_ref[TOP_K * t + 1], y1_buf, k, sem.at[1]).start()
        return c

    lax.fori_loop(0, tm, issue, 0, unroll=4)
    _rows_wait(ys_ref, y0_buf, tm, sem.at[0])
    _rows_wait(ys_ref, y1_buf, tm, sem.at[1])
    w0 = r_ref[:, 0:1]
    w1 = r_ref[:, 1:2]
    x = x_ref[...] + g_ref[0] * (w0 * y0_buf[...] + w1 * y1_buf[...])
    if final_norm:
        x = x * lax.rsqrt(jnp.mean(x * x, axis=-1, keepdims=True) + EPS) * ng_ref[...]
    o_ref[...] = x


def _combine(x2d, ys, dest, route, gate, rows_per_batch, norm_g, final_norm):
    T, D = x2d.shape
    tm = min(512, T)
    tiles_per_batch = rows_per_batch // tm
    spec = pl.BlockSpec((tm, D), lambda i, d: (i, 0))
    grid_spec = pltpu.PrefetchScalarGridSpec(
        num_scalar_prefetch=1,
        grid=(T // tm,),
        in_specs=[pl.BlockSpec(memory_space=pl.ANY), spec,
                  pl.BlockSpec((tm, LANES), lambda i, d: (i, 0)),
                  pl.BlockSpec((1, 1, D), lambda i, d: (i // tiles_per_batch, 0, 0)),
                  pl.BlockSpec((1, D), lambda i, d: (0, 0))],
        out_specs=spec,
        scratch_shapes=[pltpu.VMEM((tm, D), F32), pltpu.VMEM((tm, D), F32),
                        pltpu.SemaphoreType.DMA((2,))],
    )
    return pl.pallas_call(
        functools.partial(_combine_kernel, final_norm=final_norm),
        out_shape=jax.ShapeDtypeStruct((T, D), F32),
        grid_spec=grid_spec,
        compiler_params=_cparams("arbitrary"),
        name="moe_combine",
    )(dest, ys, x2d, route, gate, norm_g.reshape(1, D))


def _dispatch_kernel(dest_ref, h_ref, xs_in_ref, xs_ref, sem, *, tm):
    del xs_in_ref
    base = pl.program_id(0) * tm

    def issue(k, c):
        t = base + k
        for kk in range(TOP_K):
            _row_copy(h_ref, k, xs_ref, dest_ref[TOP_K * t + kk], sem.at[kk]).start()
        return c

    lax.fori_loop(0, tm, issue, 0, unroll=4)
    for kk in range(TOP_K):
        _rows_wait(h_ref, xs_ref, tm, sem.at[kk])


def _dispatch(h, dest, cap):
    T, Dp = h.shape
    tm = min(512, T)
    grid_spec = pltpu.PrefetchScalarGridSpec(
        num_scalar_prefetch=1,
        grid=(T // tm,),
        in_specs=[pl.BlockSpec((tm, Dp), lambda i, d: (i, 0)), pl.BlockSpec(memory_space=pl.ANY)],
        out_specs=pl.BlockSpec(memory_space=pl.ANY),
        scratch_shapes=[pltpu.SemaphoreType.DMA((TOP_K,))],
    )
    return pl.pallas_call(
        functools.partial(_dispatch_kernel, tm=tm),
        out_shape=jax.ShapeDtypeStruct((cap, Dp), h.dtype),
        grid_spec=grid_spec,
        input_output_aliases={2: 0},
        compiler_params=_cparams("arbitrary"),
        name="moe_dispatch",
    )(dest, h, jnp.zeros((cap, Dp), h.dtype))


def _rank_kernel(route_ref, rank_ref, cnt_ref, carry):
    tm = route_ref.shape[0]
    i = pl.program_id(0)

    @pl.when(i == 0)
    def _():
        carry[...] = jnp.zeros_like(carry)

    r = route_ref[...]
    lane = _iota(r.shape, 1)
    lane_f = lane.astype(F32)
    oh0 = (lane_f == r[:, 2:3]).astype(F32)
    oh1 = (lane_f == r[:, 3:4]).astype(F32)
    both = oh0 + oh1
    earlier = (_iota((tm, tm), 0) > _iota((tm, tm), 1)).astype(BF16)
    prefix = jnp.dot(earlier, both.astype(BF16), preferred_element_type=F32) + carry[0:1, :]
    rank0 = jnp.sum(prefix * oh0, axis=-1, keepdims=True)
    rank1 = jnp.sum((prefix + oh0) * oh1, axis=-1, keepdims=True)
    rank_ref[...] = jnp.where(lane == 0, rank0, jnp.where(lane == 1, rank1, 0.0))
    carry[...] = carry[...] + jnp.sum(both, axis=0, keepdims=True)
    cnt_ref[...] = carry[...]


def _dest_kernel(route_ref, rank_ref, cnt_ref, dest_ref):
    r = route_ref[...]
    lane = _iota(r.shape, 1)
    lane_f = lane.astype(F32)
    padded = jnp.floor((cnt_ref[...] + (MOE_BLOCK - 1.0)) * (1.0 / MOE_BLOCK)) * MOE_BLOCK
    before = (_iota((LANES, LANES), 0) < _iota((LANES, LANES), 1)).astype(F32)
    pad_start = jnp.dot(padded, before, precision=HI, preferred_element_type=F32)[0:1, :]
    oh0 = lane_f == r[:, 2:3]
    oh1 = lane_f == r[:, 3:4]
    d0 = jnp.sum(jnp.where(oh0, pad_start, 0.0), axis=-1, keepdims=True) + rank_ref[:, 0:1]
    d1 = jnp.sum(jnp.where(oh1, pad_start, 0.0), axis=-1, keepdims=True) + rank_ref[:, 1:2]
    dest_ref[...] = jnp.where(lane == 0, d0, jnp.where(lane == 1, d1, 0.0))


def _dispatch_plan(route):
    T = route.shape[0]
    tm = min(512, T)
    slab = pl.BlockSpec((tm, LANES), lambda i: (i, 0))
    cnt_spec = pl.BlockSpec((SUBLANES, LANES), lambda i: (0, 0))
    rank, cnt = pl.pallas_call(
        _rank_kernel,
        out_shape=(jax.ShapeDtypeStruct((T, LANES), F32), jax.ShapeDtypeStruct((SUBLANES, LANES), F32)),
        grid=(T // tm,),
        in_specs=[slab],
        out_specs=(slab, cnt_spec),
        scratch_shapes=[pltpu.VMEM((SUBLANES, LANES), F32)],
        compiler_params=_cparams("arbitrary"),
        name="moe_rank",
    )(route)
    dest = pl.pallas_call(
        _dest_kernel,
        out_shape=jax.ShapeDtypeStruct((T, LANES), F32),
        grid=(T // tm,),
        in_specs=[slab, slab, cnt_spec],
        out_specs=slab,
        compiler_params=_cparams("parallel"),
        name="moe_dest",
    )(route, rank, cnt)
    dest = dest[:, :TOP_K].astype(jnp.int32).reshape(T * TOP_K)
    counts = cnt[0, :N_EXPERTS].astype(jnp.int32)
    padded = (counts + MOE_BLOCK - 1) // MOE_BLOCK * MOE_BLOCK
    pad_end = jnp.cumsum(padded)
    n_blocks = -(-(T * TOP_K) // MOE_BLOCK) + N_EXPERTS
    block_row0 = jnp.arange(n_blocks, dtype=jnp.int32) * MOE_BLOCK
    block_expert = jnp.minimum(jnp.searchsorted(pad_end, block_row0, side="right"),
                               N_EXPERTS - 1).astype(jnp.int32)
    n_used = (pad_end[-1] // MOE_BLOCK).astype(jnp.int32).reshape(1)
    return dest, block_expert, n_used, n_blocks * MOE_BLOCK


def kernel(x, c, positions, norm1_g, norm2_g, ada_w, ada_b, w_in, lru_conv_w, lru_conv_b, lru_w_r, lru_b_r, lru_w_i, lru_b_i, lru_lambda, sgu_norm_g, sgu_w, sgu_b, rwkv_mu_prev, rwkv_mu_next, rwkv_w0, rwkv_w_up, rwkv_a0, rwkv_a_up, rwkv_g_up, rwkv_k_k, rwkv_k_a, rwkv_r_k, w_branch, w_out, router_grp_w, router_grp_b, router_exp_w, router_exp_b, expert_w_gu, expert_w_down, final_norm_g):
    B, S, D = x.shape
    L = ada_w.shape[0]
    W = D // 2
    T = B * S
    feat_w = rwkv_mu_prev.shape[1]

    c_pad = jnp.zeros((SUBLANES, D), F32).at[:B].set(c)
    mod = _ada(c_pad, ada_w, ada_b)

    dk = D // 2 // (2 * RET_HEADS)
    inv_freq = ROPE_THETA ** (-jnp.arange(0, dk, 2, dtype=F32) / dk)
    ang = positions.astype(F32)[..., None] * inv_freq
    cosf, sinf = _rope_tables(jnp.tile(ang, (1, 1, LANES // (dk // 2))))
    log_gamma = np.log1p(-np.exp2(-5.0 - np.arange(RET_HEADS, dtype=np.float64)))
    lg_tab = jnp.asarray(np.broadcast_to(log_gamma[:, None], (RET_HEADS, LANES)), F32)

    x2d = x.reshape(T, D)
    for l in range(L):
        m = mod[l, :B]
        sh1, sc1, g1, sh2, sc2, g2 = [m[:, None, k * D:(k + 1) * D] for k in range(6)]

        h = _normmod(x2d.reshape(B, S, D), norm1_g[l], sc1, sh1, BF16).reshape(T, D)
        n_a = 7 * W
        proj_a = _mm(h, w_in, l, 0, n_a, 2048, 512, name="proj_a").reshape(B, S, n_a)
        feat = _mm(h, w_in, l, n_a, feat_w, 1024, feat_w // 3, name="proj_rwkv").reshape(B, S, feat_w)
        logits = _mm(h, w_in, l, n_a + feat_w, N_BRANCH * D, 2048, 512, out_dtype=BF16,
                     name="proj_gate")

        y_a = _lru(proj_a, B, S, W, 0, W // LANES, lru_conv_w[l], lru_conv_b[l], lru_w_r[l],
                   lru_b_r[l], lru_w_i[l], lru_b_i[l], lru_lambda[l])
        y_b = _retention(proj_a, B, S, W, 2 * W // LANES, 5 * W // (2 * LANES),
                         3 * W // (2 * LANES), 4 * W // (2 * LANES), cosf, sinf, lg_tab)
        y_c = _sgu(proj_a, B, S, W, 5, 6, sgu_norm_g[l], sgu_w[l], sgu_b[l])
        r, kk, v, g, bonus, lw, bb, kt = _rwkv_pre(
            feat, rwkv_mu_prev[l], rwkv_mu_next[l], rwkv_w0[l], rwkv_w_up[l], rwkv_a0[l],
            rwkv_a_up[l], rwkv_g_up[l], rwkv_k_k[l], rwkv_k_a[l], rwkv_r_k[l].reshape(-1))
        y_d = _rwkv_post(*_rwkv_scan(r, kk, v, lw, bb, kt), bonus, g)

        merged = _merge([y.reshape(T, W) for y in (y_a, y_b, y_c, y_d)],
                        w_branch, l, logits, 1024, 512)
        x2d = _mm_residual(merged, w_out, l, x2d, g1, S, 2048, 512)

        h2 = _normmod(x2d.reshape(B, S, D), norm2_g[l], sc2, sh2, jnp.uint32).reshape(T, D // 2)
        n_r = N_GROUPS + N_EXPERTS
        w_r = jnp.zeros((D, LANES), F32).at[:, :N_GROUPS].set(router_grp_w[l]).at[:, N_GROUPS:n_r].set(
            router_exp_w[l]).astype(BF16)
        b_r = jnp.zeros((1, LANES), F32).at[0, :N_GROUPS].set(router_grp_b[l]).at[0, N_GROUPS:n_r].set(
            router_exp_b[l])
        route = _router(h2, w_r, b_r)
        dest, block_expert, n_used, cap = _dispatch_plan(route)
        ys = _experts(_dispatch(h2, dest, cap), block_expert, n_used, expert_w_gu, expert_w_down, l)
        x2d = _combine(x2d, ys, dest, route, g2, S, final_norm_g, l == L - 1)

    return x2d.reshape(B, S, D)
```

```python
import functools
import math

import jax
import jax.numpy as jnp
import numpy as np
from jax import lax
from jax.experimental import pallas as pl
from jax.experimental.pallas import tpu as pltpu

F32 = jnp.float32
BF16 = jnp.bfloat16
EPS = 1e-6

LANES = 128
SUBLANES = 8
VMEM_LIMIT_BYTES = 56 * 1024 * 1024

CONV_WIDTH = 4
LRU_C = 8.0
RET_HEADS = 8
RET_CHUNK = 128
ROPE_THETA = 10000.0
SGU_GROUPS = 8
SGU_CHUNK = 128
RWKV_HEAD_DIM = 64
DECAY_LORA = 64
ICLR_LORA = 64
DECAY_SCALE = math.exp(-0.5)
N_GROUPS = 4
EXPERTS_PER_GROUP = 8
N_EXPERTS = N_GROUPS * EXPERTS_PER_GROUP
TOP_K = 2
MOE_BLOCK = 256
N_BRANCH = 4

RWKV_CHUNK = 64
HI = lax.Precision.HIGHEST


def _cparams(*sem):
    return pltpu.CompilerParams(dimension_semantics=sem, vmem_limit_bytes=VMEM_LIMIT_BYTES)


def _bdot(a, b):
    return jnp.dot(a.astype(BF16), b.astype(BF16), preferred_element_type=F32)


def _bdot_nt(a, b):
    return lax.dot_general(a.astype(BF16), b.astype(BF16), (((1,), (1,)), ((), ())),
                           preferred_element_type=F32)


def _gelu(x):
    return 0.5 * x * (1.0 + jnp.tanh(math.sqrt(2.0 / math.pi) * (x + 0.044715 * (x * x * x))))


def _sigmoid(x):
    return 0.5 * jnp.tanh(0.5 * x) + 0.5


def _silu(x):
    return x * _sigmoid(x)


def _iota(shape, dim):
    return lax.broadcasted_iota(jnp.int32, shape, dim)


def _ada_kernel(c_ref, w_ref, b_ref, o_ref):
    cond = _silu(c_ref[...])
    o_ref[0] = _bdot(cond, w_ref[0]) + b_ref[0]


def _ada(c_pad, ada_w, ada_b):
    L, D, N = ada_w.shape
    R = c_pad.shape[0]
    tn = 1024
    return pl.pallas_call(
        _ada_kernel,
        out_shape=jax.ShapeDtypeStruct((L, R, N), F32),
        grid=(L, N // tn),
        in_specs=[pl.BlockSpec((R, D), lambda l, j: (0, 0)),
                  pl.BlockSpec((1, D, tn), lambda l, j: (l, 0, j)),
                  pl.BlockSpec((1, 1, tn), lambda l, j: (l, 0, j))],
        out_specs=pl.BlockSpec((1, R, tn), lambda l, j: (l, 0, j)),
        compiler_params=_cparams("parallel", "parallel"),
        name="ada_mod",
    )(c_pad, ada_w, ada_b.reshape(L, 1, N))


def _pack_bf16_pairs(y):
    half = y.shape[1] // 2
    bits = pltpu.bitcast(y.astype(BF16).astype(F32), jnp.uint32)
    return (bits[:, :half] >> 16) | (bits[:, half:] & jnp.uint32(0xFFFF0000))


def _unpack_bf16_pairs(u):
    lo = pltpu.bitcast(u << 16, F32)
    hi = pltpu.bitcast(u & jnp.uint32(0xFFFF0000), F32)
    return jnp.concatenate([lo, hi], axis=1).astype(BF16)


def _normmod_kernel(x_ref, g_ref, sc_ref, sh_ref, o_ref):
    x = x_ref[0]
    y = x * lax.rsqrt(jnp.mean(x * x, axis=-1, keepdims=True) + EPS) * g_ref[...]
    y = y * (1.0 + sc_ref[0]) + sh_ref[0]
    if o_ref.dtype == jnp.uint32:
        o_ref[0] = _pack_bf16_pairs(y)
    else:
        o_ref[0] = y.astype(o_ref.dtype)


def _normmod(x, g, sc, sh, out_dtype):
    B, S, D = x.shape
    ts = min(512, S)
    d_out = D // 2 if out_dtype == jnp.uint32 else D
    return pl.pallas_call(
        _normmod_kernel,
        out_shape=jax.ShapeDtypeStruct((B, S, d_out), out_dtype),
        grid=(B, S // ts),
        in_specs=[pl.BlockSpec((1, ts, D), lambda b, i: (b, i, 0)),
                  pl.BlockSpec((1, D), lambda b, i: (0, 0)),
                  pl.BlockSpec((1, 1, D), lambda b, i: (b, 0, 0)),
                  pl.BlockSpec((1, 1, D), lambda b, i: (b, 0, 0))],
        out_specs=pl.BlockSpec((1, ts, d_out), lambda b, i: (b, i, 0)),
        compiler_params=_cparams("parallel", "parallel"),
        name="norm_mod",
    )(x, g.reshape(1, D), sc, sh)


def _mm_kernel(a_ref, w_ref, o_ref):
    o_ref[...] = jnp.dot(a_ref[...], w_ref[0].astype(BF16),
                         preferred_element_type=F32).astype(o_ref.dtype)


def _mm(a, w, layer, col0, n_cols, tm, tn, out_dtype=F32, name="proj"):
    M, K = a.shape
    return pl.pallas_call(
        _mm_kernel,
        out_shape=jax.ShapeDtypeStruct((M, n_cols), out_dtype),
        grid=(M // tm, n_cols // tn),
        in_specs=[pl.BlockSpec((tm, K), lambda i, j: (i, 0)),
                  pl.BlockSpec((pl.Element(1), pl.Element(K), pl.Element(tn)),
                               lambda i, j: (layer, 0, pl.multiple_of(col0 + j * tn, LANES)))],
        out_specs=pl.BlockSpec((tm, tn), lambda i, j: (i, j)),
        compiler_params=_cparams("parallel", "parallel"),
        name=name,
    )(a, w)


def _mm_res_kernel(a_ref, w_ref, x_ref, g_ref, o_ref):
    y = jnp.dot(a_ref[...], w_ref[0].astype(BF16), preferred_element_type=F32)
    o_ref[...] = x_ref[...] + g_ref[0] * y


def _mm_residual(a, w, layer, x2d, gate, rows_per_batch, tm, tn, name="out_proj"):
    M, K = a.shape
    N = w.shape[2]
    tiles_per_batch = rows_per_batch // tm
    return pl.pallas_call(
        _mm_res_kernel,
        out_shape=jax.ShapeDtypeStruct((M, N), F32),
        grid=(M // tm, N // tn),
        in_specs=[pl.BlockSpec((tm, K), lambda i, j: (i, 0)),
                  pl.BlockSpec((1, K, tn), lambda i, j: (layer, 0, j)),
                  pl.BlockSpec((tm, tn), lambda i, j: (i, j)),
                  pl.BlockSpec((1, 1, tn), lambda i, j: (i // tiles_per_batch, 0, j))],
        out_specs=pl.BlockSpec((tm, tn), lambda i, j: (i, j)),
        compiler_params=_cparams("parallel", "parallel"),
        name=name,
    )(a, w, x2d, gate)


def _softplus(x):
    return jnp.maximum(x, 0.0) + jnp.log1p(jnp.exp(-jnp.abs(x)))


def _lru_kernel(x_ref, gate_ref, cw_ref, cb_ref, wr_ref, br_ref, wi_ref, bi_ref, lam_ref,
                o_ref, a_scr, u_scr, h_scr):
    S = x_ref.shape[1]
    x = x_ref[0]
    rows = _iota(x.shape, 0)
    xm2 = jnp.where(rows >= 2, pltpu.roll(x, 2, 0), 0.0)
    xm1 = jnp.where(rows >= 1, pltpu.roll(x, 1, 0), 0.0)
    xp1 = jnp.where(rows < S - 1, pltpu.roll(x, S - 1, 0), 0.0)
    xc = (cw_ref[0:1, :] * xm2 + cw_ref[1:2, :] * xm1 + cw_ref[2:3, :] * x
          + cw_ref[3:4, :] * xp1 + cb_ref[...])
    for z in range(2):
        r = _sigmoid(_bdot(xc, wr_ref[z, 0]) + br_ref[z:z + 1, :])
        i = _sigmoid(_bdot(xc, wi_ref[z, 0]) + bi_ref[z:z + 1, :])
        log_a = -LRU_C * r * _softplus(-lam_ref[z:z + 1, :])
        a = jnp.exp(log_a)
        a_scr[z] = a
        u_scr[z] = jnp.sqrt(-jnp.tanh(log_a) * (a * a + 1.0)) * i * xc

    n_tiles = S // SUBLANES
    L = x.shape[1]
    trow = _iota((SUBLANES, L), 0)

    def tile_scan(a, u, reverse):
        for d in (1, 2, 4):
            if reverse:
                keep = trow < SUBLANES - d
                a_s = jnp.where(keep, pltpu.roll(a, SUBLANES - d, 0), 1.0)
                u_s = jnp.where(keep, pltpu.roll(u, SUBLANES - d, 0), 0.0)
            else:
                keep = trow >= d
                a_s = jnp.where(keep, pltpu.roll(a, d, 0), 1.0)
                u_s = jnp.where(keep, pltpu.roll(u, d, 0), 0.0)
            u = u + a * u_s
            a = a * a_s
        return a, u

    def body(k, carry):
        hf, hb = carry
        f0 = pl.multiple_of(k * SUBLANES, SUBLANES)
        b0 = pl.multiple_of((n_tiles - 1 - k) * SUBLANES, SUBLANES)
        af, uf = tile_scan(a_scr[0, pl.ds(f0, SUBLANES), :], u_scr[0, pl.ds(f0, SUBLANES), :], False)
        ab, ub = tile_scan(a_scr[1, pl.ds(b0, SUBLANES), :], u_scr[1, pl.ds(b0, SUBLANES), :], True)
        tf = uf + af * hf
        tb = ub + ab * hb
        h_scr[0, pl.ds(f0, SUBLANES), :] = tf
        h_scr[1, pl.ds(b0, SUBLANES), :] = tb
        return tf[SUBLANES - 1:SUBLANES, :], tb[0:1, :]

    zero = jnp.zeros((1, L), F32)
    lax.fori_loop(0, n_tiles, body, (zero, zero), unroll=8)
    o_ref[0] = ((h_scr[0] + h_scr[1]) * _gelu(gate_ref[0])).astype(o_ref.dtype)


def _lru(proj, B, S, W, x_col, g_col, cw, cb, wr, br, wi, bi, lam):
    nb = W // LANES
    return pl.pallas_call(
        _lru_kernel,
        out_shape=jax.ShapeDtypeStruct((B, S, W), BF16),
        grid=(B, nb),
        in_specs=[pl.BlockSpec((1, S, LANES), lambda b, j: (b, 0, x_col + j)),
                  pl.BlockSpec((1, S, LANES), lambda b, j: (b, 0, g_col + j)),
                  pl.BlockSpec((CONV_WIDTH, LANES), lambda b, j: (0, j)),
                  pl.BlockSpec((1, LANES), lambda b, j: (0, j)),
                  pl.BlockSpec((2, 1, LANES, LANES), lambda b, j: (0, j, 0, 0)),
                  pl.BlockSpec((2, LANES), lambda b, j: (0, j)),
                  pl.BlockSpec((2, 1, LANES, LANES), lambda b, j: (0, j, 0, 0)),
                  pl.BlockSpec((2, LANES), lambda b, j: (0, j)),
                  pl.BlockSpec((2, LANES), lambda b, j: (0, j))],
        out_specs=pl.BlockSpec((1, S, LANES), lambda b, j: (b, 0, j)),
        scratch_shapes=[pltpu.VMEM((2, S, LANES), F32)] * 3,
        compiler_params=_cparams("parallel", "parallel"),
        name="rglru",
    )(proj, proj, cw, cb.reshape(1, W), wr, br, wi, bi, lam)


def _rope_kernel(ang_ref, cos_ref, sin_ref):
    ang = ang_ref[0]
    lane = _iota(ang.shape, 1)
    cos_ref[0] = jnp.cos(ang)
    s = jnp.sin(ang)
    sin_ref[0] = jnp.where((lane % 64) < 32, -s, s)


def _rope_tables(ang):
    B, S, _ = ang.shape
    ts = min(512, S)
    spec = pl.BlockSpec((1, ts, LANES), lambda b, i: (b, i, 0))
    return pl.pallas_call(
        _rope_kernel,
        out_shape=(jax.ShapeDtypeStruct((B, S, LANES), F32),) * 2,
        grid=(B, S // ts),
        in_specs=[spec],
        out_specs=(spec, spec),
        compiler_params=_cparams("parallel", "parallel"),
        name="rope_tables",
    )(ang)


def _ret_kernel(q_ref, k_ref, v_ref, g_ref, cos_ref, sin_ref, lg_ref, o_ref,
                qs_scr, ks_scr, kv_scr):
    S = q_ref.shape[1]
    C = RET_CHUNK
    N = S // C
    dk = 64
    p = pl.program_id(1)
    lane = _iota((S, LANES), 1)
    first_half = (lane % 64) < 32

    def rope(x):
        swapped = jnp.where(first_half, pltpu.roll(x, LANES - 32, 1), pltpu.roll(x, 32, 1))
        return x * cos_ref[0] + swapped * sin_ref[0]

    qs_scr[...] = rope(q_ref[0]) * (dk ** -0.5)
    ks_scr[...] = rope(k_ref[0])

    pos_r = _iota((C, 1), 0).astype(F32)
    pos_l = _iota((1, C), 1).astype(F32)
    ii = _iota((C, C), 0)
    jj = _iota((C, C), 1)
    dist = jnp.abs(ii - jj).astype(F32)
    clane = _iota((C, LANES), 1)

    heads = range(2)
    lg = [lg_ref[pl.ds(2 * p + hh, 1), :][:, 0:1] for hh in heads]
    hmask = [(clane // 64) == hh for hh in heads]
    hlanes = [slice(hh * LANES, (hh + 1) * LANES) for hh in heads]
    intra = [jnp.exp(dist * x) for x in lg]
    dec_kf_row = [jnp.exp((C - 1.0 - pos_l) * x) for x in lg]
    dec_kb_row = [jnp.exp(pos_l * x) for x in lg]
    dec_qf = [jnp.exp((pos_r + 1.0) * x) for x in lg]
    dec_qb = [jnp.exp((C - pos_r) * x) for x in lg]
    chunk_decay = [jnp.exp(C * x) for x in lg]

    def kv_body(n, _):
        r0 = pl.multiple_of(n * C, C)
        ks = ks_scr[pl.ds(r0, C), :]
        for hh in heads:
            kt = jnp.where(hmask[hh], ks, 0.0).T
            vc = v_ref[0, pl.ds(r0, C), hlanes[hh]]
            kv_scr[hh, n] = _bdot(
                jnp.concatenate([kt * dec_kf_row[hh], kt * dec_kb_row[hh]], axis=0), vc)
        return 0

    lax.fori_loop(0, N, kv_body, 0, unroll=4)

    def state_body(half):
        def body(m, sts):
            n = m if half == 0 else N - 1 - m
            new = []
            for hh in heads:
                cur = kv_scr[hh, n, half * LANES:(half + 1) * LANES, :]
                kv_scr[hh, n, half * LANES:(half + 1) * LANES, :] = sts[hh]
                new.append(chunk_decay[hh] * sts[hh] + cur)
            return tuple(new)
        return body

    zero_state = (jnp.zeros((LANES, LANES), F32),) * 2
    lax.fori_loop(0, N, state_body(0), zero_state)
    lax.fori_loop(0, N, state_body(1), zero_state)

    def out_body(n, _):
        r0 = pl.multiple_of(n * C, C)
        qc = qs_scr[pl.ds(r0, C), :]
        ks = ks_scr[pl.ds(r0, C), :]
        for hh in heads:
            kc = jnp.where(hmask[hh], ks, 0.0)
            vc = v_ref[0, pl.ds(r0, C), hlanes[hh]]
            scores = _bdot_nt(qc, kc) * intra[hh]
            o = _bdot(scores, vc) + _bdot(
                jnp.concatenate([qc * dec_qf[hh], qc * dec_qb[hh]], axis=1), kv_scr[hh, n])
            mu = jnp.mean(o, axis=-1, keepdims=True)
            var = jnp.mean(o * o, axis=-1, keepdims=True) - mu * mu
            o = (o - mu) * lax.rsqrt(var + EPS)
            gc = g_ref[0, pl.ds(r0, C), hlanes[hh]]
            o_ref[0, pl.ds(r0, C), hlanes[hh]] = (_silu(gc) * o).astype(o_ref.dtype)
        return 0

    lax.fori_loop(0, N, out_body, 0, unroll=8)


def _retention(proj, B, S, W, q_col, k_col, v_col, g_col, cosf, sinf, lg_tab):
    n_pairs = RET_HEADS // 2
    N = S // RET_CHUNK
    return pl.pallas_call(
        _ret_kernel,
        out_shape=jax.ShapeDtypeStruct((B, S, W), BF16),
        grid=(B, n_pairs),
        in_specs=[pl.BlockSpec((1, S, LANES), lambda b, p: (b, 0, q_col + p)),
                  pl.BlockSpec((1, S, LANES), lambda b, p: (b, 0, k_col + p)),
                  pl.BlockSpec((1, S, 2 * LANES), lambda b, p: (b, 0, v_col + p)),
                  pl.BlockSpec((1, S, 2 * LANES), lambda b, p: (b, 0, g_col + p)),
                  pl.BlockSpec((1, S, LANES), lambda b, p: (b, 0, 0)),
                  pl.BlockSpec((1, S, LANES), lambda b, p: (b, 0, 0)),
                  pl.BlockSpec((RET_HEADS, LANES), lambda b, p: (0, 0))],
        out_specs=pl.BlockSpec((1, S, 2 * LANES), lambda b, p: (b, 0, p)),
        scratch_shapes=[pltpu.VMEM((S, LANES), F32), pltpu.VMEM((S, LANES), F32),
                        pltpu.VMEM((2, N, 2 * LANES, LANES), F32)],
        compiler_params=_cparams("parallel", "parallel"),
        name="retention",
    )(proj, proj, proj, proj, cosf, sinf, lg_tab)


def _sgu_kernel(u_ref, v_ref, ng_ref, w_ref, bt_ref, o_ref):
    ts = u_ref.shape[1]
    C = SGU_CHUNK
    v = _gelu(v_ref[0])
    mu = jnp.mean(v, axis=-1, keepdims=True)
    var = jnp.mean(jnp.square(v - mu), axis=-1, keepdims=True)
    v = ((v - mu) * lax.rsqrt(var + EPS) * ng_ref[...]).astype(BF16)
    for c in range(ts // C):
        for g in range(SGU_GROUPS):
            vc = v[c * C:(c + 1) * C, g * LANES:(g + 1) * LANES]
            mixed = jnp.dot(w_ref[g].astype(BF16), vc, preferred_element_type=F32) + bt_ref[:, g:g + 1]
            uc = _gelu(u_ref[0, c * C:(c + 1) * C, g * LANES:(g + 1) * LANES])
            o_ref[0, c * C:(c + 1) * C, g * LANES:(g + 1) * LANES] = (uc * mixed).astype(o_ref.dtype)


def _sgu(proj, B, S, W, u_col, v_col, norm_g, w_s, b_s):
    ts = min(512, S)
    return pl.pallas_call(
        _sgu_kernel,
        out_shape=jax.ShapeDtypeStruct((B, S, W), BF16),
        grid=(B, S // ts),
        in_specs=[pl.BlockSpec((1, ts, W), lambda b, i: (b, i, u_col)),
                  pl.BlockSpec((1, ts, W), lambda b, i: (b, i, v_col)),
                  pl.BlockSpec((1, W), lambda b, i: (0, 0)),
                  pl.BlockSpec((SGU_GROUPS, SGU_CHUNK, SGU_CHUNK), lambda b, i: (0, 0, 0)),
                  pl.BlockSpec((SGU_CHUNK, SGU_GROUPS), lambda b, i: (0, 0))],
        out_specs=pl.BlockSpec((1, ts, W), lambda b, i: (b, i, 0)),
        compiler_params=_cparams("parallel", "parallel"),
        name="spatial_gating",
    )(proj, proj, norm_g.reshape(1, W), w_s, b_s.T)


def _head_sum(x, hd):
    seg = (_iota((LANES, LANES), 0) // hd == _iota((LANES, LANES), 1) // hd).astype(BF16)
    hi = x.astype(BF16)
    lo = (x - hi.astype(F32)).astype(BF16)
    parts = [jnp.dot(hi[:, j * LANES:(j + 1) * LANES], seg, preferred_element_type=F32)
             + jnp.dot(lo[:, j * LANES:(j + 1) * LANES], seg, preferred_element_type=F32)
             for j in range(x.shape[-1] // LANES)]
    return jnp.concatenate(parts, axis=1)


def _rwkv_pre_kernel(f_ref, fp_ref, fn_ref, mup_ref, mun_ref, w0_ref, wup_ref, a0_ref, aup_ref,
                     gup_ref, kk_ref, ka_ref, rk_ref,
                     r_out, kk_out, v_out, g_out, bonus_out, lw_out, bb_out, kt_out):
    ts = f_ref.shape[1]
    W = r_out.shape[2]
    i = pl.program_id(1)
    n_i = pl.num_programs(1)
    f = f_ref[0]
    rows = _iota(f.shape, 0)
    prev_row = jnp.where(i > 0, fp_ref[0, SUBLANES - 1:SUBLANES, :], 0.0)
    next_row = jnp.where(i < n_i - 1, fn_ref[0, 0:1, :], 0.0)
    prev = jnp.where(rows >= 1, pltpu.roll(f, 1, 0), prev_row)
    nxt = jnp.where(rows < ts - 1, pltpu.roll(f, ts - 1, 0), next_row)
    f = f + mup_ref[...] * (prev - f) + mun_ref[...] * (nxt - f)

    r = f[:, 0:W]
    k = f[:, W:2 * W]
    v = f[:, 2 * W:3 * W]
    o = 3 * W
    wd = jnp.tanh(f[:, o:o + 2 * DECAY_LORA])
    ad = f[:, o + 2 * DECAY_LORA:o + 2 * DECAY_LORA + 2 * ICLR_LORA]
    gd = f[:, o + 2 * DECAY_LORA + 2 * ICLR_LORA:]

    g_out[0] = _bdot(_sigmoid(gd), gup_ref[...]).astype(g_out.dtype)
    kk = k * kk_ref[...]
    kk = kk * lax.rsqrt(_head_sum(kk * kk, RWKV_HEAD_DIM) + EPS)
    r_out[0] = r.astype(r_out.dtype)
    kk_out[0] = kk.astype(kk_out.dtype)
    v_out[0] = v.astype(v_out.dtype)
    bonus = jnp.zeros_like(r)
    for z in range(2):
        w_raw = _bdot(wd[:, z * DECAY_LORA:(z + 1) * DECAY_LORA], wup_ref[z]) + w0_ref[z:z + 1, :]
        lw_out[z, 0] = -DECAY_SCALE * _sigmoid(w_raw)
        a = _sigmoid(_bdot(ad[:, z * ICLR_LORA:(z + 1) * ICLR_LORA], aup_ref[z]) + a0_ref[z:z + 1, :])
        bb_out[z, 0] = (a * kk).astype(bb_out.dtype)
        kt = k * (1.0 + (a - 1.0) * ka_ref[...])
        kt_out[z, 0] = kt.astype(kt_out.dtype)
        bonus = bonus + _head_sum(r * kt * rk_ref[...], RWKV_HEAD_DIM) * v
    bonus_out[0] = bonus.astype(bonus_out.dtype)


def _rwkv_pre(feat, mu_prev, mu_next, w0, w_up, a0, a_up, g_up, k_k, k_a, r_k):
    B, S, Fw = feat.shape
    W = w0.shape[1]
    ts = min(256, S)
    hb = ts // SUBLANES
    n_hb = S // SUBLANES
    row = lambda a: a.reshape(1, -1)
    full2 = lambda a: pl.BlockSpec(a.shape, lambda b, i: (0, 0))
    full3 = lambda a: pl.BlockSpec(a.shape, lambda b, i: (0, 0, 0))
    bsw = pl.BlockSpec((1, ts, W), lambda b, i: (b, i, 0))
    zsw = pl.BlockSpec((2, 1, ts, W), lambda b, i: (0, b, i, 0))
    sds = jax.ShapeDtypeStruct((B, S, W), F32)
    zds = jax.ShapeDtypeStruct((2, B, S, W), F32)
    sdh = jax.ShapeDtypeStruct((B, S, W), BF16)
    zdh = jax.ShapeDtypeStruct((2, B, S, W), BF16)
    args = (row(mu_prev), row(mu_next), w0, w_up, a0, a_up, g_up, row(k_k), row(k_a), row(r_k))
    specs = [full2(args[0]), full2(args[1]), full2(w0), full3(w_up), full2(a0), full3(a_up),
             full2(g_up), full2(args[7]), full2(args[8]), full2(args[9])]
    return pl.pallas_call(
        _rwkv_pre_kernel,
        out_shape=(sdh, sdh, sdh, sdh, sdh, zds, zdh, zdh),
        grid=(B, S // ts),
        in_specs=[pl.BlockSpec((1, ts, Fw), lambda b, i: (b, i, 0)),
                  pl.BlockSpec((1, SUBLANES, Fw), lambda b, i: (b, jnp.maximum(i * hb - 1, 0), 0)),
                  pl.BlockSpec((1, SUBLANES, Fw), lambda b, i: (b, jnp.minimum((i + 1) * hb, n_hb - 1), 0)),
                  ] + specs,
        out_specs=(bsw, bsw, bsw, bsw, bsw, zsw, zsw, zsw),
        compiler_params=_cparams("parallel", "parallel"),
        name="rwkv_prepare",
    )(feat, feat, feat, *args)


def _rwkv_scan_kernel(rf_ref, kkf_ref, vf_ref, rb_ref, kkb_ref, vb_ref,
                      lwf_ref, bbf_ref, ktf_ref, lwb_ref, bbb_ref, ktb_ref, yf_ref, yb_ref, st_scr):
    C = RWKV_CHUNK
    C2 = 2 * C
    n_pairs = rf_ref.shape[2] // LANES

    @pl.when(pl.program_id(1) == 0)
    def _():
        st_scr[...] = jnp.zeros_like(st_scr)

    wide_t = _iota((C, C2), 0)
    wide_s = _iota((C, C2), 1) % C
    eye_w = (wide_t == wide_s).astype(F32)
    emask = (_iota((C2, LANES), 0) // C) == (_iota((C2, LANES), 1) // RWKV_HEAD_DIM)
    bmask = (_iota((C2, C2), 0) // C) == (_iota((C2, C2), 1) // C)
    diag_l = _iota((LANES, LANES), 0) == _iota((LANES, LANES), 1)
    tri = _iota((C, C), 0) - _iota((C, C), 1)
    cum_mat = [(sg * tri >= 0).astype(BF16) for sg in (1, -1)]
    strict = [sg * (wide_t - wide_s) > 0 for sg in (1, -1)]
    incl = [sg * (wide_t - wide_s) >= 0 for sg in (1, -1)]
    refs = [(rf_ref, kkf_ref, vf_ref, lwf_ref, bbf_ref, ktf_ref, yf_ref),
            (rb_ref, kkb_ref, vb_ref, lwb_ref, bbb_ref, ktb_ref, yb_ref)]

    def expand(x):
        return jnp.where(emask, jnp.concatenate([x, x], axis=0), 0.0)

    def blockdiag(a):
        return jnp.where(bmask, jnp.concatenate([a, a], axis=0), 0.0)

    def bf(x):
        return x.astype(BF16)

    def mm(a, b):
        return jnp.dot(a, b, preferred_element_type=F32)

    def cumsum3(m, x):
        hi = bf(x)
        r1 = x - hi.astype(F32)
        mid = bf(r1)
        lo = bf(r1 - mid.astype(F32))
        return mm(m, hi) + (mm(m, mid) + mm(m, lo))

    units = [(z, j) for z in range(2) for j in range(n_pairs)]
    zs = [z for z, _ in units]
    sl = [slice(j * LANES, (j + 1) * LANES) for _, j in units]
    r_in = [refs[z][0][0, :, s] for z, s in zip(zs, sl)]
    kk_in = [refs[z][1][0, :, s] for z, s in zip(zs, sl)]
    v_in = [refs[z][2][0, :, s] for z, s in zip(zs, sl)]
    lw = [refs[z][3][0, 0, :, s] for z, s in zip(zs, sl)]
    bb_in = [refs[z][4][0, 0, :, s] for z, s in zip(zs, sl)]
    kt_in = [refs[z][5][0, 0, :, s] for z, s in zip(zs, sl)]

    cum = [cumsum3(cum_mat[z], x) for z, x in zip(zs, lw)]
    tot = [jnp.sum(x, axis=0, keepdims=True) for x in lw]
    p_inv = [jnp.exp(-c) for c in cum]
    p_rest = [jnp.exp(t - c) for t, c in zip(tot, cum)]
    kh_c = [bf(k * jnp.exp(c - x)) for k, c, x in zip(kk_in, cum, lw)]
    rh_c = [bf(r * jnp.exp(c)) for r, c in zip(r_in, cum)]
    kh_e = [expand(k) for k in kh_c]
    bh_e = [expand(bf(b * p)) for b, p in zip(bb_in, p_inv)]
    ktil_e = [expand(bf(k * p)) for k, p in zip(kt_in, p_inv)]
    b_rest_t = [bf(expand(b * p).T) for b, p in zip(bb_in, p_rest)]
    k_rest_t = [bf(expand(k * p).T) for k, p in zip(kt_in, p_rest)]
    v_e = [expand(v) for v in v_in]

    a_all = [lax.dot_general(jnp.concatenate([k, r], axis=0), jnp.concatenate([b, t], axis=0),
                             (((1,), (1,)), ((), ())), preferred_element_type=F32)
             for k, r, b, t in zip(kh_c, rh_c, bh_e, ktil_e)]
    a_b = [jnp.where(strict[z], a[0:C, 0:C2], 0.0) for z, a in zip(zs, a_all)]
    a_k = [jnp.where(strict[z], a[0:C, C2:], 0.0) for z, a in zip(zs, a_all)]
    a_rb = [jnp.where(incl[z], a[C:, 0:C2], 0.0) for z, a in zip(zs, a_all)]
    a_rk = [jnp.where(incl[z], a[C:, C2:], 0.0) for z, a in zip(zs, a_all)]

    pw = [bf(-a) for a in a_b]
    t_inv = [eye_w] * len(units)
    n_dbl = int(math.log2(C))
    for k in range(n_dbl):
        if k < n_dbl - 1:
            both = [mm(p, jnp.concatenate([blockdiag(p), blockdiag(bf(t))], axis=1))
                    for p, t in zip(pw, t_inv)]
            pw = [bf(x[:, 0:C2]) for x in both]
            t_inv = [t + x[:, C2:] for t, x in zip(t_inv, both)]
        else:
            t_inv = [t + mm(p, blockdiag(bf(t))) for p, t in zip(pw, t_inv)]

    akv = [mm(bf(jnp.concatenate([ak, ark], axis=0)), v)
           for ak, ark, v in zip(a_k, a_rk, v_e)]
    tw = [mm(bf(t), jnp.concatenate([k, expand(bf(x[0:C]))], axis=1))
          for t, k, x in zip(t_inv, kh_e, akv)]
    st = [st_scr[z, j] for z, j in units]
    ws = [mm(jnp.concatenate([bf(x[:, 0:LANES]), r], axis=0), bf(s))
          for x, r, s in zip(tw, rh_c, st)]
    u_c = [x[0:C] + t[:, LANES:] for x, t in zip(ws, tw)]
    y_c = [x[C:] + k[C:] - mm(bf(a), expand(bf(u))) for x, k, a, u in zip(ws, akv, a_rb, u_c)]
    for z, s, y in zip(zs, sl, y_c):
        refs[z][6][0, :, s] = y.astype(refs[z][6].dtype)

    wu_e = [jnp.concatenate([expand(bf(t[:, 0:LANES])), expand(bf(t[:, LANES:]))], axis=1) for t in tw]
    bw = [mm(b, x) for b, x in zip(b_rest_t, wu_e)]
    kv = [mm(k, v) for k, v in zip(k_rest_t, v_e)]
    for n, (z, j) in enumerate(units):
        m_t = jnp.where(diag_l, jnp.exp(tot[n]), 0.0) - bw[n][:, 0:LANES]
        g_t = kv[n] - bw[n][:, LANES:]
        st_scr[z, j] = mm(bf(m_t), bf(st[n])) + g_t


def _rwkv_scan(r, kk, v, lw, bb, kt):
    B, S, W = r.shape
    C = RWKV_CHUNK
    nc = S // C
    fwd = pl.BlockSpec((1, C, W), lambda b, i: (b, i, 0))
    bwd = pl.BlockSpec((1, C, W), lambda b, i: (b, nc - 1 - i, 0))
    fwd_z = pl.BlockSpec((1, 1, C, W), lambda b, i: (0, b, i, 0))
    bwd_z = pl.BlockSpec((1, 1, C, W), lambda b, i: (1, b, nc - 1 - i, 0))
    sds = jax.ShapeDtypeStruct((B, S, W), BF16)
    return pl.pallas_call(
        _rwkv_scan_kernel,
        out_shape=(sds, sds),
        grid=(B, nc),
        in_specs=[fwd, fwd, fwd, bwd, bwd, bwd, fwd_z, fwd_z, fwd_z, bwd_z, bwd_z, bwd_z],
        out_specs=(fwd, bwd),
        scratch_shapes=[pltpu.VMEM((2, W // LANES, LANES, LANES), F32)],
        compiler_params=_cparams("parallel", "arbitrary"),
        name="rwkv_scan",
    )(r, kk, v, r, kk, v, lw, bb, kt, lw, bb, kt)


def _rwkv_post_kernel(yf_ref, yb_ref, bonus_ref, g_ref, o_ref):
    hd = RWKV_HEAD_DIM
    acc = bonus_ref[0].astype(F32)
    for y_ref in (yf_ref, yb_ref):
        y = y_ref[0].astype(F32)
        d = y - _head_sum(y, hd) * (1.0 / hd)
        var = _head_sum(d * d, hd) * (1.0 / hd)
        acc = acc + d * lax.rsqrt(var + EPS)
    o_ref[0] = (acc * g_ref[0].astype(F32)).astype(o_ref.dtype)


def _rwkv_post(y_fwd, y_bwd, bonus, g):
    B, S, W = y_fwd.shape
    ts = min(256, S)
    bsw = pl.BlockSpec((1, ts, W), lambda b, i: (b, i, 0))
    return pl.pallas_call(
        _rwkv_post_kernel,
        out_shape=jax.ShapeDtypeStruct((B, S, W), BF16),
        grid=(B, S // ts),
        in_specs=[bsw, bsw, bsw, bsw],
        out_specs=bsw,
        compiler_params=_cparams("parallel", "parallel"),
        name="rwkv_finish",
    )(y_fwd, y_bwd, bonus, g)


def _merge_kernel(ya_ref, yb_ref, yc_ref, yd_ref, wb_ref, l0_ref, l1_ref, l2_ref, l3_ref, o_ref,
                  wb_bf):
    @pl.when(pl.program_id(1) == 0)
    def _():
        wb_bf[...] = wb_ref[0].astype(BF16)

    acc = None
    for n, (y_ref, l_ref) in enumerate(((ya_ref, l0_ref), (yb_ref, l1_ref), (yc_ref, l2_ref),
                                        (yd_ref, l3_ref))):
        br = jnp.dot(y_ref[...], wb_bf[n], preferred_element_type=F32)
        t = _sigmoid(l_ref[...].astype(F32)) * br
        acc = t if acc is None else acc + t
    o_ref[...] = acc.astype(o_ref.dtype)


def _merge(ys, wb, layer, logits, tm, tn):
    T, W = ys[0].shape
    D = wb.shape[3]
    nj = D // tn
    yspec = pl.BlockSpec((tm, W), lambda j, i: (i, 0))
    lspec = lambda n: pl.BlockSpec((tm, tn), lambda j, i: (i, n * nj + j))
    return pl.pallas_call(
        _merge_kernel,
        out_shape=jax.ShapeDtypeStruct((T, D), BF16),
        grid=(nj, T // tm),
        in_specs=[yspec, yspec, yspec, yspec,
                  pl.BlockSpec((1, N_BRANCH, W, tn), lambda j, i: (layer, 0, 0, j)),
                  lspec(0), lspec(1), lspec(2), lspec(3)],
        out_specs=pl.BlockSpec((tm, tn), lambda j, i: (i, j)),
        scratch_shapes=[pltpu.VMEM((N_BRANCH, W, tn), BF16)],
        compiler_params=_cparams("parallel", "arbitrary"),
        name="branch_merge",
    )(*ys, wb, logits, logits, logits, logits)


def _router_kernel(h_ref, w_ref, b_ref, o_ref):
    logits = jnp.dot(_unpack_bf16_pairs(h_ref[...]), w_ref[...], preferred_element_type=F32) + b_ref[...]
    lane = _iota(logits.shape, 1)
    lane_f = lane.astype(F32)
    neg = -3.0e38
    far = float(LANES)
    is_grp = lane < N_GROUPS
    gl = jnp.where(is_grp, logits, neg)
    gmax = jnp.max(gl, axis=-1, keepdims=True)
    gsum = jnp.sum(jnp.where(is_grp, jnp.exp(gl - gmax), 0.0), axis=-1, keepdims=True)
    grp_p = 1.0 / gsum
    grp_i = jnp.min(jnp.where(gl == gmax, lane_f, far), axis=-1, keepdims=True)
    lo = N_GROUPS + grp_i * EXPERTS_PER_GROUP
    in_grp = jnp.logical_and(lane_f >= lo, lane_f < lo + EXPERTS_PER_GROUP)
    el = jnp.where(in_grp, logits, neg)
    m1 = jnp.max(el, axis=-1, keepdims=True)
    i1 = jnp.min(jnp.where(el == m1, lane_f, far), axis=-1, keepdims=True)
    el2 = jnp.where(lane_f == i1, neg, el)
    m2 = jnp.max(el2, axis=-1, keepdims=True)
    i2 = jnp.min(jnp.where(el2 == m2, lane_f, far), axis=-1, keepdims=True)
    t = jnp.exp(m2 - m1)
    w1 = grp_p / (1.0 + t)
    w2 = grp_p * t / (1.0 + t)
    o_ref[...] = jnp.where(lane == 0, w1, jnp.where(lane == 1, w2, jnp.where(
        lane == 2, i1 - N_GROUPS, jnp.where(lane == 3, i2 - N_GROUPS, 0.0))))


def _router(h, w_r, b_r):
    T = h.shape[0]
    D = w_r.shape[0]
    tm = min(1024, T)
    return pl.pallas_call(
        _router_kernel,
        out_shape=jax.ShapeDtypeStruct((T, LANES), F32),
        grid=(T // tm,),
        in_specs=[pl.BlockSpec((tm, D // 2), lambda i: (i, 0)),
                  pl.BlockSpec((D, LANES), lambda i: (0, 0)),
                  pl.BlockSpec((1, LANES), lambda i: (0, 0))],
        out_specs=pl.BlockSpec((tm, LANES), lambda i: (i, 0)),
        compiler_params=_cparams("parallel"),
        name="router",
    )(h, w_r, b_r)


def _expert_kernel(be_ref, nb_ref, first_ref, slot_ref, nxt_ref, xs_ref, wgu_hbm, wd_hbm, o_ref,
                   wgu_f32, wd_f32, wgu_bf, wd_bf, sem, *, layer):
    i = pl.program_id(0)
    used = i < nb_ref[0]

    def weight_copies(e, s):
        return (pltpu.make_async_copy(wgu_hbm.at[layer, e], wgu_f32.at[s], sem.at[s, 0]),
                pltpu.make_async_copy(wd_hbm.at[layer, e], wd_f32.at[s], sem.at[s, 1]))

    @pl.when(jnp.logical_and(used, first_ref[i] == 1))
    def _():
        s = slot_ref[i]

        @pl.when(i == 0)
        def _():
            for cp in weight_copies(be_ref[i], s):
                cp.start()

        for cp in weight_copies(be_ref[i], s):
            cp.wait()

        @pl.when(nxt_ref[i] >= 0)
        def _():
            for cp in weight_copies(nxt_ref[i], 1 - s):
                cp.start()

        wgu_bf[...] = wgu_f32[s].astype(BF16)
        wd_bf[...] = wd_f32[s].astype(BF16)

    @pl.when(used)
    def _():
        gu = jnp.dot(_unpack_bf16_pairs(xs_ref[...]), wgu_bf[...], preferred_element_type=F32)
        eh = gu.shape[1] // 2
        mid = (_silu(gu[:, :eh]) * gu[:, eh:]).astype(BF16)
        o_ref[...] = jnp.dot(mid, wd_bf[...], preferred_element_type=F32)

    @pl.when(jnp.logical_not(used))
    def _():
        o_ref[...] = jnp.zeros_like(o_ref)


def _expert_runs(block_expert, n_used):
    n_blocks = block_expert.shape[0]
    idx = jnp.arange(n_blocks, dtype=jnp.int32)
    changed = jnp.logical_or(idx == 0, block_expert != jnp.roll(block_expert, 1))
    first = jnp.logical_and(changed, idx < n_used[0]).astype(jnp.int32)
    slot = (jnp.cumsum(first) - 1) % 2
    start_pos = jnp.where(first == 1, idx, n_blocks)
    later = lax.cummin(start_pos, axis=0, reverse=True)
    next_pos = jnp.concatenate([later[1:], jnp.full((1,), n_blocks, jnp.int32)])
    nxt = jnp.where(next_pos < n_blocks, block_expert[jnp.minimum(next_pos, n_blocks - 1)], -1)
    return first, slot.astype(jnp.int32), nxt.astype(jnp.int32)


def _experts(xs, block_expert, n_used, w_gu, w_down, layer):
    cap = xs.shape[0]
    D, H2 = w_gu.shape[2], w_gu.shape[3]
    n_blocks = cap // MOE_BLOCK
    first, slot, nxt = _expert_runs(block_expert, n_used)
    grid_spec = pltpu.PrefetchScalarGridSpec(
        num_scalar_prefetch=5,
        grid=(n_blocks,),
        in_specs=[pl.BlockSpec((MOE_BLOCK, D // 2),
                               lambda i, be, nb, fi, sl, nx: (jnp.minimum(i, nb[0] - 1), 0)),
                  pl.BlockSpec(memory_space=pl.ANY),
                  pl.BlockSpec(memory_space=pl.ANY)],
        out_specs=pl.BlockSpec((MOE_BLOCK, D), lambda i, be, nb, fi, sl, nx: (i, 0)),
        scratch_shapes=[pltpu.VMEM((2, D, H2), F32), pltpu.VMEM((2, H2 // 2, D), F32),
                        pltpu.VMEM((D, H2), BF16), pltpu.VMEM((H2 // 2, D), BF16),
                        pltpu.SemaphoreType.DMA((2, 2))],
    )
    return pl.pallas_call(
        functools.partial(_expert_kernel, layer=layer),
        out_shape=jax.ShapeDtypeStruct((cap, D), F32),
        grid_spec=grid_spec,
        compiler_params=_cparams("arbitrary"),
        name="experts",
    )(block_expert, n_used, first, slot, nxt, xs, w_gu, w_down)


def _row_copy(src_ref, s, dst_ref, d, sem):
    return pltpu.make_async_copy(src_ref.at[pl.ds(s, 1), :], dst_ref.at[pl.ds(d, 1), :], sem)


def _rows_wait(src_ref, dst_ref, n, sem):
    pltpu.make_async_copy(src_ref.at[pl.ds(0, n), :], dst_ref.at[pl.ds(0, n), :], sem).wait()


def _combine_kernel(dest_ref, ys_ref, x_ref, r_ref, g_ref, ng_ref, o_ref, y0_buf, y1_buf, sem, *,
                    final_norm):
    tm = x_ref.shape[0]
    base = pl.program_id(0) * tm

    def issue(k, c):
        t = base + k
        _row_copy(ys_ref, dest_ref[TOP_K * t], y0_buf, k, sem.at[0]).start()
        _row_copy(ys_ref, dest_ref[TOP_K * t + 1], y1_buf, k, sem.at[1]).start()
        return c

    lax.fori_loop(0, tm, issue, 0, unroll=4)
    _rows_wait(ys_ref, y0_buf, tm, sem.at[0])
    _rows_wait(ys_ref, y1_buf, tm, sem.at[1])
    w0 = r_ref[:, 0:1]
    w1 = r_ref[:, 1:2]
    x = x_ref[...] + g_ref[0] * (w0 * y0_buf[...] + w1 * y1_buf[...])
    if final_norm:
        x = x * lax.rsqrt(jnp.mean(x * x, axis=-1, keepdims=True) + EPS) * ng_ref[...]
    o_ref[...] = x


def _combine(x2d, ys, dest, route, gate, rows_per_batch, norm_g, final_norm):
    T, D = x2d.shape
    tm = min(512, T)
    tiles_per_batch = rows_per_batch // tm
    spec = pl.BlockSpec((tm, D), lambda i, d: (i, 0))
    grid_spec = pltpu.PrefetchScalarGridSpec(
        num_scalar_prefetch=1,
        grid=(T // tm,),
        in_specs=[pl.BlockSpec(memory_space=pl.ANY), spec,
                  pl.BlockSpec((tm, LANES), lambda i, d: (i, 0)),
                  pl.BlockSpec((1, 1, D), lambda i, d: (i // tiles_per_batch, 0, 0)),
                  pl.BlockSpec((1, D), lambda i, d: (0, 0))],
        out_specs=spec,
        scratch_shapes=[pltpu.VMEM((tm, D), F32), pltpu.VMEM((tm, D), F32),
                        pltpu.SemaphoreType.DMA((2,))],
    )
    return pl.pallas_call(
        functools.partial(_combine_kernel, final_norm=final_norm),
        out_shape=jax.ShapeDtypeStruct((T, D), F32),
        grid_spec=grid_spec,
        compiler_params=_cparams("arbitrary"),
        name="moe_combine",
    )(dest, ys, x2d, route, gate, norm_g.reshape(1, D))


def _dispatch_kernel(dest_ref, h_ref, xs_in_ref, xs_ref, sem, *, tm):
    del xs_in_ref
    base = pl.program_id(0) * tm

    def issue(k, c):
        t = base + k
        for kk in range(TOP_K):
            _row_copy(h_ref, k, xs_ref, dest_ref[TOP_K * t + kk], sem.at[kk]).start()
        return c

    lax.fori_loop(0, tm, issue, 0, unroll=4)
    for kk in range(TOP_K):
        _rows_wait(h_ref, xs_ref, tm, sem.at[kk])


def _dispatch(h, dest, cap):
    T, Dp = h.shape
    tm = min(1024, T)
    grid_spec = pltpu.PrefetchScalarGridSpec(
        num_scalar_prefetch=1,
        grid=(T // tm,),
        in_specs=[pl.BlockSpec((tm, Dp), lambda i, d: (i, 0)), pl.BlockSpec(memory_space=pl.ANY)],
        out_specs=pl.BlockSpec(memory_space=pl.ANY),
        scratch_shapes=[pltpu.SemaphoreType.DMA((TOP_K,))],
    )
    return pl.pallas_call(
        functools.partial(_dispatch_kernel, tm=tm),
        out_shape=jax.ShapeDtypeStruct((cap, Dp), h.dtype),
        grid_spec=grid_spec,
        input_output_aliases={2: 0},
        compiler_params=_cparams("arbitrary"),
        name="moe_dispatch",
    )(dest, h, jnp.zeros((cap, Dp), h.dtype))


def _rank_kernel(route_ref, rank_ref, cnt_ref, carry):
    tm = route_ref.shape[0]
    i = pl.program_id(0)

    @pl.when(i == 0)
    def _():
        carry[...] = jnp.zeros_like(carry)

    r = route_ref[...]
    lane = _iota(r.shape, 1)
    lane_f = lane.astype(F32)
    oh0 = (lane_f == r[:, 2:3]).astype(F32)
    oh1 = (lane_f == r[:, 3:4]).astype(F32)
    both = oh0 + oh1
    earlier = (_iota((tm, tm), 0) > _iota((tm, tm), 1)).astype(BF16)
    prefix = jnp.dot(earlier, both.astype(BF16), preferred_element_type=F32) + carry[0:1, :]
    rank0 = jnp.sum(prefix * oh0, axis=-1, keepdims=True)
    rank1 = jnp.sum((prefix + oh0) * oh1, axis=-1, keepdims=True)
    rank_ref[...] = jnp.where(lane == 0, rank0, jnp.where(lane == 1, rank1, 0.0))
    carry[...] = carry[...] + jnp.sum(both, axis=0, keepdims=True)
    cnt_ref[...] = carry[...]


def _dest_kernel(route_ref, rank_ref, cnt_ref, dest_ref):
    r = route_ref[...]
    lane = _iota(r.shape, 1)
    lane_f = lane.astype(F32)
    padded = jnp.floor((cnt_ref[...] + (MOE_BLOCK - 1.0)) * (1.0 / MOE_BLOCK)) * MOE_BLOCK
    before = (_iota((LANES, LANES), 0) < _iota((LANES, LANES), 1)).astype(F32)
    pad_start = jnp.dot(padded, before, precision=HI, preferred_element_type=F32)[0:1, :]
    oh0 = lane_f == r[:, 2:3]
    oh1 = lane_f == r[:, 3:4]
    d0 = jnp.sum(jnp.where(oh0, pad_start, 0.0), axis=-1, keepdims=True) + rank_ref[:, 0:1]
    d1 = jnp.sum(jnp.where(oh1, pad_start, 0.0), axis=-1, keepdims=True) + rank_ref[:, 1:2]
    dest_ref[...] = jnp.where(lane == 0, d0, jnp.where(lane == 1, d1, 0.0))


def _dispatch_plan(route):
    T = route.shape[0]
    tm = min(512, T)
    slab = pl.BlockSpec((tm, LANES), lambda i: (i, 0))
    cnt_spec = pl.BlockSpec((SUBLANES, LANES), lambda i: (0, 0))
    rank, cnt = pl.pallas_call(
        _rank_kernel,
        out_shape=(jax.ShapeDtypeStruct((T, LANES), F32), jax.ShapeDtypeStruct((SUBLANES, LANES), F32)),
        grid=(T // tm,),
        in_specs=[slab],
        out_specs=(slab, cnt_spec),
        scratch_shapes=[pltpu.VMEM((SUBLANES, LANES), F32)],
        compiler_params=_cparams("arbitrary"),
        name="moe_rank",
    )(route)
    dest = pl.pallas_call(
        _dest_kernel,
        out_shape=jax.ShapeDtypeStruct((T, LANES), F32),
        grid=(T // tm,),
        in_specs=[slab, slab, cnt_spec],
        out_specs=slab,
        compiler_params=_cparams("parallel"),
        name="moe_dest",
    )(route, rank, cnt)
    dest = dest[:, :TOP_K].astype(jnp.int32).reshape(T * TOP_K)
    counts = cnt[0, :N_EXPERTS].astype(jnp.int32)
    padded = (counts + MOE_BLOCK - 1) // MOE_BLOCK * MOE_BLOCK
    pad_end = jnp.cumsum(padded)
    n_blocks = -(-(T * TOP_K) // MOE_BLOCK) + N_EXPERTS
    block_row0 = jnp.arange(n_blocks, dtype=jnp.int32) * MOE_BLOCK
    block_expert = jnp.minimum(jnp.searchsorted(pad_end, block_row0, side="right"),
                               N_EXPERTS - 1).astype(jnp.int32)
    n_used = (pad_end[-1] // MOE_BLOCK).astype(jnp.int32).reshape(1)
    return dest, block_expert, n_used, n_blocks * MOE_BLOCK


def kernel(x, c, positions, norm1_g, norm2_g, ada_w, ada_b, w_in, lru_conv_w, lru_conv_b, lru_w_r, lru_b_r, lru_w_i, lru_b_i, lru_lambda, sgu_norm_g, sgu_w, sgu_b, rwkv_mu_prev, rwkv_mu_next, rwkv_w0, rwkv_w_up, rwkv_a0, rwkv_a_up, rwkv_g_up, rwkv_k_k, rwkv_k_a, rwkv_r_k, w_branch, w_out, router_grp_w, router_grp_b, router_exp_w, router_exp_b, expert_w_gu, expert_w_down, final_norm_g):
    B, S, D = x.shape
    L = ada_w.shape[0]
    W = D // 2
    T = B * S
    feat_w = rwkv_mu_prev.shape[1]

    c_pad = jnp.zeros((SUBLANES, D), F32).at[:B].set(c)
    mod = _ada(c_pad, ada_w, ada_b)

    dk = D // 2 // (2 * RET_HEADS)
    inv_freq = ROPE_THETA ** (-jnp.arange(0, dk, 2, dtype=F32) / dk)
    ang = positions.astype(F32)[..., None] * inv_freq
    cosf, sinf = _rope_tables(jnp.tile(ang, (1, 1, LANES // (dk // 2))))
    log_gamma = np.log1p(-np.exp2(-5.0 - np.arange(RET_HEADS, dtype=np.float64)))
    lg_tab = jnp.asarray(np.broadcast_to(log_gamma[:, None], (RET_HEADS, LANES)), F32)

    x2d = x.reshape(T, D)
    for l in range(L):
        m = mod[l, :B]
        sh1, sc1, g1, sh2, sc2, g2 = [m[:, None, k * D:(k + 1) * D] for k in range(6)]

        h = _normmod(x2d.reshape(B, S, D), norm1_g[l], sc1, sh1, BF16).reshape(T, D)
        n_a = 7 * W
        proj_a = _mm(h, w_in, l, 0, n_a, 2048, 512, name="proj_a").reshape(B, S, n_a)
        feat = _mm(h, w_in, l, n_a, feat_w, 1024, feat_w // 3, name="proj_rwkv").reshape(B, S, feat_w)
        logits = _mm(h, w_in, l, n_a + feat_w, N_BRANCH * D, 2048, 512, out_dtype=BF16,
                     name="proj_gate")

        y_a = _lru(proj_a, B, S, W, 0, W // LANES, lru_conv_w[l], lru_conv_b[l], lru_w_r[l],
                   lru_b_r[l], lru_w_i[l], lru_b_i[l], lru_lambda[l])
        y_b = _retention(proj_a, B, S, W, 2 * W // LANES, 5 * W // (2 * LANES),
                         3 * W // (2 * LANES), 4 * W // (2 * LANES), cosf, sinf, lg_tab)
        y_c = _sgu(proj_a, B, S, W, 5, 6, sgu_norm_g[l], sgu_w[l], sgu_b[l])
        r, kk, v, g, bonus, lw, bb, kt = _rwkv_pre(
            feat, rwkv_mu_prev[l], rwkv_mu_next[l], rwkv_w0[l], rwkv_w_up[l], rwkv_a0[l],
            rwkv_a_up[l], rwkv_g_up[l], rwkv_k_k[l], rwkv_k_a[l], rwkv_r_k[l].reshape(-1))
        y_d = _rwkv_post(*_rwkv_scan(r, kk, v, lw, bb, kt), bonus, g)

        merged = _merge([y.reshape(T, W) for y in (y_a, y_b, y_c, y_d)],
                        w_branch, l, logits, 1024, 512)
        x2d = _mm_residual(merged, w_out, l, x2d, g1, S, 2048, 512)

        h2 = _normmod(x2d.reshape(B, S, D), norm2_g[l], sc2, sh2, jnp.uint32).reshape(T, D // 2)
        n_r = N_GROUPS + N_EXPERTS
        w_r = jnp.zeros((D, LANES), F32).at[:, :N_GROUPS].set(router_grp_w[l]).at[:, N_GROUPS:n_r].set(
            router_exp_w[l]).astype(BF16)
        b_r = jnp.zeros((1, LANES), F32).at[0, :N_GROUPS].set(router_grp_b[l]).at[0, N_GROUPS:n_r].set(
            router_exp_b[l])
        route = _router(h2, w_r, b_r)
        dest, block_expert, n_used, cap = _dispatch_plan(route)
        ys = _experts(_dispatch(h2, dest, cap), block_expert, n_used, expert_w_gu, expert_w_down, l)
        x2d = _combine(x2d, ys, dest, route, g2, S, final_norm_g, l == L - 1)

    return x2d.reshape(B, S, D)
```

```python
import functools
import math

import jax
import jax.numpy as jnp
import numpy as np
from jax import lax
from jax.experimental import pallas as pl
from jax.experimental.pallas import tpu as pltpu

F32 = jnp.float32
BF16 = jnp.bfloat16
EPS = 1e-6

LANES = 128
SUBLANES = 8
VMEM_LIMIT_BYTES = 56 * 1024 * 1024

CONV_WIDTH = 4
LRU_C = 8.0
RET_HEADS = 8
RET_CHUNK = 128
ROPE_THETA = 10000.0
SGU_GROUPS = 8
SGU_CHUNK = 128
RWKV_HEAD_DIM = 64
DECAY_LORA = 64
ICLR_LORA = 64
DECAY_SCALE = math.exp(-0.5)
N_GROUPS = 4
EXPERTS_PER_GROUP = 8
N_EXPERTS = N_GROUPS * EXPERTS_PER_GROUP
TOP_K = 2
MOE_BLOCK = 256
N_BRANCH = 4

RWKV_CHUNK = 64
HI = lax.Precision.HIGHEST


def _cparams(*sem):
    return pltpu.CompilerParams(dimension_semantics=sem, vmem_limit_bytes=VMEM_LIMIT_BYTES)


def _bdot(a, b):
    return jnp.dot(a.astype(BF16), b.astype(BF16), preferred_element_type=F32)


def _bdot_nt(a, b):
    return lax.dot_general(a.astype(BF16), b.astype(BF16), (((1,), (1,)), ((), ())),
                           preferred_element_type=F32)


def _gelu(x):
    return 0.5 * x * (1.0 + jnp.tanh(math.sqrt(2.0 / math.pi) * (x + 0.044715 * (x * x * x))))


def _sigmoid(x):
    return 0.5 * jnp.tanh(0.5 * x) + 0.5


def _silu(x):
    return x * _sigmoid(x)


def _iota(shape, dim):
    return lax.broadcasted_iota(jnp.int32, shape, dim)


def _ada_kernel(c_ref, w_ref, b_ref, o_ref):
    cond = _silu(c_ref[...])
    o_ref[0] = _bdot(cond, w_ref[0]) + b_ref[0]


def _ada(c_pad, ada_w, ada_b):
    L, D, N = ada_w.shape
    R = c_pad.shape[0]
    tn = 1024
    return pl.pallas_call(
        _ada_kernel,
        out_shape=jax.ShapeDtypeStruct((L, R, N), F32),
        grid=(L, N // tn),
        in_specs=[pl.BlockSpec((R, D), lambda l, j: (0, 0)),
                  pl.BlockSpec((1, D, tn), lambda l, j: (l, 0, j)),
                  pl.BlockSpec((1, 1, tn), lambda l, j: (l, 0, j))],
        out_specs=pl.BlockSpec((1, R, tn), lambda l, j: (l, 0, j)),
        compiler_params=_cparams("parallel", "parallel"),
        name="ada_mod",
    )(c_pad, ada_w, ada_b.reshape(L, 1, N))


def _pack_bf16_pairs(y):
    half = y.shape[1] // 2
    bits = pltpu.bitcast(y.astype(BF16).astype(F32), jnp.uint32)
    return (bits[:, :half] >> 16) | (bits[:, half:] & jnp.uint32(0xFFFF0000))


def _unpack_bf16_pairs(u):
    lo = pltpu.bitcast(u << 16, F32)
    hi = pltpu.bitcast(u & jnp.uint32(0xFFFF0000), F32)
    return jnp.concatenate([lo, hi], axis=1).astype(BF16)


def _normmod_kernel(x_ref, g_ref, sc_ref, sh_ref, o_ref):
    x = x_ref[0]
    y = x * lax.rsqrt(jnp.mean(x * x, axis=-1, keepdims=True) + EPS) * g_ref[...]
    y = y * (1.0 + sc_ref[0]) + sh_ref[0]
    if o_ref.dtype == jnp.uint32:
        o_ref[0] = _pack_bf16_pairs(y)
    else:
        o_ref[0] = y.astype(o_ref.dtype)


def _normmod(x, g, sc, sh, out_dtype):
    B, S, D = x.shape
    ts = min(512, S)
    d_out = D // 2 if out_dtype == jnp.uint32 else D
    return pl.pallas_call(
        _normmod_kernel,
        out_shape=jax.ShapeDtypeStruct((B, S, d_out), out_dtype),
        grid=(B, S // ts),
        in_specs=[pl.BlockSpec((1, ts, D), lambda b, i: (b, i, 0)),
                  pl.BlockSpec((1, D), lambda b, i: (0, 0)),
                  pl.BlockSpec((1, 1, D), lambda b, i: (b, 0, 0)),
                  pl.BlockSpec((1, 1, D), lambda b, i: (b, 0, 0))],
        out_specs=pl.BlockSpec((1, ts, d_out), lambda b, i: (b, i, 0)),
        compiler_params=_cparams("parallel", "parallel"),
        name="norm_mod",
    )(x, g.reshape(1, D), sc, sh)


def _mm_kernel(a_ref, w_ref, o_ref):
    o_ref[...] = jnp.dot(a_ref[...], w_ref[0].astype(BF16),
                         preferred_element_type=F32).astype(o_ref.dtype)


def _mm(a, w, layer, col0, n_cols, tm, tn, out_dtype=F32, name="proj"):
    M, K = a.shape
    return pl.pallas_call(
        _mm_kernel,
        out_shape=jax.ShapeDtypeStruct((M, n_cols), out_dtype),
        grid=(M // tm, n_cols // tn),
        in_specs=[pl.BlockSpec((tm, K), lambda i, j: (i, 0)),
                  pl.BlockSpec((pl.Element(1), pl.Element(K), pl.Element(tn)),
                               lambda i, j: (layer, 0, pl.multiple_of(col0 + j * tn, LANES)))],
        out_specs=pl.BlockSpec((tm, tn), lambda i, j: (i, j)),
        compiler_params=_cparams("parallel", "parallel"),
        name=name,
    )(a, w)


def _mm_res_kernel(a_ref, w_ref, x_ref, g_ref, o_ref):
    y = jnp.dot(a_ref[...], w_ref[0].astype(BF16), preferred_element_type=F32)
    o_ref[...] = x_ref[...] + g_ref[0] * y


def _mm_residual(a, w, layer, x2d, gate, rows_per_batch, tm, tn, name="out_proj"):
    M, K = a.shape
    N = w.shape[2]
    tiles_per_batch = rows_per_batch // tm
    return pl.pallas_call(
        _mm_res_kernel,
        out_shape=jax.ShapeDtypeStruct((M, N), F32),
        grid=(M // tm, N // tn),
        in_specs=[pl.BlockSpec((tm, K), lambda i, j: (i, 0)),
                  pl.BlockSpec((1, K, tn), lambda i, j: (layer, 0, j)),
                  pl.BlockSpec((tm, tn), lambda i, j: (i, j)),
                  pl.BlockSpec((1, 1, tn), lambda i, j: (i // tiles_per_batch, 0, j))],
        out_specs=pl.BlockSpec((tm, tn), lambda i, j: (i, j)),
        compiler_params=_cparams("parallel", "parallel"),
        name=name,
    )(a, w, x2d, gate)


def _softplus(x):
    return jnp.maximum(x, 0.0) + jnp.log1p(jnp.exp(-jnp.abs(x)))


def _lru_kernel(x_ref, gate_ref, cw_ref, cb_ref, wr_ref, br_ref, wi_ref, bi_ref, lam_ref,
                o_ref, a_scr, u_scr, h_scr):
    S = x_ref.shape[1]
    x = x_ref[0]
    rows = _iota(x.shape, 0)
    xm2 = jnp.where(rows >= 2, pltpu.roll(x, 2, 0), 0.0)
    xm1 = jnp.where(rows >= 1, pltpu.roll(x, 1, 0), 0.0)
    xp1 = jnp.where(rows < S - 1, pltpu.roll(x, S - 1, 0), 0.0)
    xc = (cw_ref[0:1, :] * xm2 + cw_ref[1:2, :] * xm1 + cw_ref[2:3, :] * x
          + cw_ref[3:4, :] * xp1 + cb_ref[...])
    for z in range(2):
        r = _sigmoid(_bdot(xc, wr_ref[z, 0]) + br_ref[z:z + 1, :])
        i = _sigmoid(_bdot(xc, wi_ref[z, 0]) + bi_ref[z:z + 1, :])
        log_a = -LRU_C * r * _softplus(-lam_ref[z:z + 1, :])
        a = jnp.exp(log_a)
        a_scr[z] = a
        u_scr[z] = jnp.sqrt(-jnp.tanh(log_a) * (a * a + 1.0)) * i * xc

    n_tiles = S // SUBLANES
    L = x.shape[1]
    trow = _iota((SUBLANES, L), 0)

    def tile_scan(a, u, reverse):
        for d in (1, 2, 4):
            if reverse:
                keep = trow < SUBLANES - d
                a_s = jnp.where(keep, pltpu.roll(a, SUBLANES - d, 0), 1.0)
                u_s = jnp.where(keep, pltpu.roll(u, SUBLANES - d, 0), 0.0)
            else:
                keep = trow >= d
                a_s = jnp.where(keep, pltpu.roll(a, d, 0), 1.0)
                u_s = jnp.where(keep, pltpu.roll(u, d, 0), 0.0)
            u = u + a * u_s
            a = a * a_s
        return a, u

    def body(k, carry):
        hf, hb = carry
        f0 = pl.multiple_of(k * SUBLANES, SUBLANES)
        b0 = pl.multiple_of((n_tiles - 1 - k) * SUBLANES, SUBLANES)
        af, uf = tile_scan(a_scr[0, pl.ds(f0, SUBLANES), :], u_scr[0, pl.ds(f0, SUBLANES), :], False)
        ab, ub = tile_scan(a_scr[1, pl.ds(b0, SUBLANES), :], u_scr[1, pl.ds(b0, SUBLANES), :], True)
        tf = uf + af * hf
        tb = ub + ab * hb
        h_scr[0, pl.ds(f0, SUBLANES), :] = tf
        h_scr[1, pl.ds(b0, SUBLANES), :] = tb
        return tf[SUBLANES - 1:SUBLANES, :], tb[0:1, :]

    zero = jnp.zeros((1, L), F32)
    lax.fori_loop(0, n_tiles, body, (zero, zero), unroll=8)
    o_ref[0] = ((h_scr[0] + h_scr[1]) * _gelu(gate_ref[0])).astype(o_ref.dtype)


def _lru(proj, B, S, W, x_col, g_col, cw, cb, wr, br, wi, bi, lam):
    nb = W // LANES
    return pl.pallas_call(
        _lru_kernel,
        out_shape=jax.ShapeDtypeStruct((B, S, W), BF16),
        grid=(B, nb),
        in_specs=[pl.BlockSpec((1, S, LANES), lambda b, j: (b, 0, x_col + j)),
                  pl.BlockSpec((1, S, LANES), lambda b, j: (b, 0, g_col + j)),
                  pl.BlockSpec((CONV_WIDTH, LANES), lambda b, j: (0, j)),
                  pl.BlockSpec((1, LANES), lambda b, j: (0, j)),
                  pl.BlockSpec((2, 1, LANES, LANES), lambda b, j: (0, j, 0, 0)),
                  pl.BlockSpec((2, LANES), lambda b, j: (0, j)),
                  pl.BlockSpec((2, 1, LANES, LANES), lambda b, j: (0, j, 0, 0)),
                  pl.BlockSpec((2, LANES), lambda b, j: (0, j)),
                  pl.BlockSpec((2, LANES), lambda b, j: (0, j))],
        out_specs=pl.BlockSpec((1, S, LANES), lambda b, j: (b, 0, j)),
        scratch_shapes=[pltpu.VMEM((2, S, LANES), F32)] * 3,
        compiler_params=_cparams("parallel", "parallel"),
        name="rglru",
    )(proj, proj, cw, cb.reshape(1, W), wr, br, wi, bi, lam)


def _rope_kernel(ang_ref, cos_ref, sin_ref):
    ang = ang_ref[0]
    lane = _iota(ang.shape, 1)
    cos_ref[0] = jnp.cos(ang)
    s = jnp.sin(ang)
    sin_ref[0] = jnp.where((lane % 64) < 32, -s, s)


def _rope_tables(ang):
    B, S, _ = ang.shape
    ts = min(512, S)
    spec = pl.BlockSpec((1, ts, LANES), lambda b, i: (b, i, 0))
    return pl.pallas_call(
        _rope_kernel,
        out_shape=(jax.ShapeDtypeStruct((B, S, LANES), F32),) * 2,
        grid=(B, S // ts),
        in_specs=[spec],
        out_specs=(spec, spec),
        compiler_params=_cparams("parallel", "parallel"),
        name="rope_tables",
    )(ang)


def _ret_kernel(q_ref, k_ref, v_ref, g_ref, cos_ref, sin_ref, lg_ref, o_ref,
                qs_scr, ks_scr, kv_scr):
    S = q_ref.shape[1]
    C = RET_CHUNK
    N = S // C
    dk = 64
    p = pl.program_id(1)
    lane = _iota((S, LANES), 1)
    first_half = (lane % 64) < 32

    def rope(x):
        swapped = jnp.where(first_half, pltpu.roll(x, LANES - 32, 1), pltpu.roll(x, 32, 1))
        return x * cos_ref[0] + swapped * sin_ref[0]

    qs_scr[...] = rope(q_ref[0]) * (dk ** -0.5)
    ks_scr[...] = rope(k_ref[0])

    pos_r = _iota((C, 1), 0).astype(F32)
    pos_l = _iota((1, C), 1).astype(F32)
    ii = _iota((C, C), 0)
    jj = _iota((C, C), 1)
    dist = jnp.abs(ii - jj).astype(F32)
    clane = _iota((C, LANES), 1)

    heads = range(2)
    lg = [lg_ref[pl.ds(2 * p + hh, 1), :][:, 0:1] for hh in heads]
    hmask = [(clane // 64) == hh for hh in heads]
    hlanes = [slice(hh * LANES, (hh + 1) * LANES) for hh in heads]
    intra = [jnp.exp(dist * x) for x in lg]
    dec_kf_row = [jnp.exp((C - 1.0 - pos_l) * x) for x in lg]
    dec_kb_row = [jnp.exp(pos_l * x) for x in lg]
    dec_qf = [jnp.exp((pos_r + 1.0) * x) for x in lg]
    dec_qb = [jnp.exp((C - pos_r) * x) for x in lg]
    chunk_decay = [jnp.exp(C * x) for x in lg]

    def kv_body(n, _):
        r0 = pl.multiple_of(n * C, C)
        ks = ks_scr[pl.ds(r0, C), :]
        for hh in heads:
            kt = jnp.where(hmask[hh], ks, 0.0).T
            vc = v_ref[0, pl.ds(r0, C), hlanes[hh]]
            kv_scr[hh, n] = _bdot(
                jnp.concatenate([kt * dec_kf_row[hh], kt * dec_kb_row[hh]], axis=0), vc)
        return 0

    lax.fori_loop(0, N, kv_body, 0, unroll=4)

    def state_body(half):
        def body(m, sts):
            n = m if half == 0 else N - 1 - m
            new = []
            for hh in heads:
                cur = kv_scr[hh, n, half * LANES:(half + 1) * LANES, :]
                kv_scr[hh, n, half * LANES:(half + 1) * LANES, :] = sts[hh]
                new.append(chunk_decay[hh] * sts[hh] + cur)
            return tuple(new)
        return body

    zero_state = (jnp.zeros((LANES, LANES), F32),) * 2
    lax.fori_loop(0, N, state_body(0), zero_state)
    lax.fori_loop(0, N, state_body(1), zero_state)

    def out_body(n, _):
        r0 = pl.multiple_of(n * C, C)
        qc = qs_scr[pl.ds(r0, C), :]
        ks = ks_scr[pl.ds(r0, C), :]
        for hh in heads:
            kc = jnp.where(hmask[hh], ks, 0.0)
            vc = v_ref[0, pl.ds(r0, C), hlanes[hh]]
            scores = _bdot_nt(qc, kc) * intra[hh]
            o = _bdot(scores, vc) + _bdot(
                jnp.concatenate([qc * dec_qf[hh], qc * dec_qb[hh]], axis=1), kv_scr[hh, n])
            mu = jnp.mean(o, axis=-1, keepdims=True)
            var = jnp.mean(o * o, axis=-1, keepdims=True) - mu * mu
            o = (o - mu) * lax.rsqrt(var + EPS)
            gc = g_ref[0, pl.ds(r0, C), hlanes[hh]]
            o_ref[0, pl.ds(r0, C), hlanes[hh]] = (_silu(gc) * o).astype(o_ref.dtype)
        return 0

    lax.fori_loop(0, N, out_body, 0, unroll=8)


def _retention(proj, B, S, W, q_col, k_col, v_col, g_col, cosf, sinf, lg_tab):
    n_pairs = RET_HEADS // 2
    N = S // RET_CHUNK
    return pl.pallas_call(
        _ret_kernel,
        out_shape=jax.ShapeDtypeStruct((B, S, W), BF16),
        grid=(B, n_pairs),
        in_specs=[pl.BlockSpec((1, S, LANES), lambda b, p: (b, 0, q_col + p)),
                  pl.BlockSpec((1, S, LANES), lambda b, p: (b, 0, k_col + p)),
                  pl.BlockSpec((1, S, 2 * LANES), lambda b, p: (b, 0, v_col + p)),
                  pl.BlockSpec((1, S, 2 * LANES), lambda b, p: (b, 0, g_col + p)),
                  pl.BlockSpec((1, S, LANES), lambda b, p: (b, 0, 0)),
                  pl.BlockSpec((1, S, LANES), lambda b, p: (b, 0, 0)),
                  pl.BlockSpec((RET_HEADS, LANES), lambda b, p: (0, 0))],
        out_specs=pl.BlockSpec((1, S, 2 * LANES), lambda b, p: (b, 0, p)),
        scratch_shapes=[pltpu.VMEM((S, LANES), F32), pltpu.VMEM((S, LANES), F32),
                        pltpu.VMEM((2, N, 2 * LANES, LANES), F32)],
        compiler_params=_cparams("parallel", "parallel"),
        name="retention",
    )(proj, proj, proj, proj, cosf, sinf, lg_tab)


def _sgu_kernel(u_ref, v_ref, ng_ref, w_ref, bt_ref, o_ref):
    ts = u_ref.shape[1]
    C = SGU_CHUNK
    v = _gelu(v_ref[0])
    mu = jnp.mean(v, axis=-1, keepdims=True)
    var = jnp.mean(jnp.square(v - mu), axis=-1, keepdims=True)
    v = ((v - mu) * lax.rsqrt(var + EPS) * ng_ref[...]).astype(BF16)
    for c in range(ts // C):
        for g in range(SGU_GROUPS):
            vc = v[c * C:(c + 1) * C, g * LANES:(g + 1) * LANES]
            mixed = jnp.dot(w_ref[g].astype(BF16), vc, preferred_element_type=F32) + bt_ref[:, g:g + 1]
            uc = _gelu(u_ref[0, c * C:(c + 1) * C, g * LANES:(g + 1) * LANES])
            o_ref[0, c * C:(c + 1) * C, g * LANES:(g + 1) * LANES] = (uc * mixed).astype(o_ref.dtype)


def _sgu(proj, B, S, W, u_col, v_col, norm_g, w_s, b_s):
    ts = min(512, S)
    return pl.pallas_call(
        _sgu_kernel,
        out_shape=jax.ShapeDtypeStruct((B, S, W), BF16),
        grid=(B, S // ts),
        in_specs=[pl.BlockSpec((1, ts, W), lambda b, i: (b, i, u_col)),
                  pl.BlockSpec((1, ts, W), lambda b, i: (b, i, v_col)),
                  pl.BlockSpec((1, W), lambda b, i: (0, 0)),
                  pl.BlockSpec((SGU_GROUPS, SGU_CHUNK, SGU_CHUNK), lambda b, i: (0, 0, 0)),
                  pl.BlockSpec((SGU_CHUNK, SGU_GROUPS), lambda b, i: (0, 0))],
        out_specs=pl.BlockSpec((1, ts, W), lambda b, i: (b, i, 0)),
        compiler_params=_cparams("parallel", "parallel"),
        name="spatial_gating",
    )(proj, proj, norm_g.reshape(1, W), w_s, b_s.T)


def _head_sum(x, hd):
    seg = (_iota((LANES, LANES), 0) // hd == _iota((LANES, LANES), 1) // hd).astype(BF16)
    hi = x.astype(BF16)
    lo = (x - hi.astype(F32)).astype(BF16)
    parts = [jnp.dot(hi[:, j * LANES:(j + 1) * LANES], seg, preferred_element_type=F32)
             + jnp.dot(lo[:, j * LANES:(j + 1) * LANES], seg, preferred_element_type=F32)
             for j in range(x.shape[-1] // LANES)]
    return jnp.concatenate(parts, axis=1)


def _rwkv_pre_kernel(f_ref, fp_ref, fn_ref, mup_ref, mun_ref, w0_ref, wup_ref, a0_ref, aup_ref,
                     gup_ref, kk_ref, ka_ref, rk_ref,
                     r_out, kk_out, v_out, g_out, bonus_out, lw_out, bb_out, kt_out):
    ts = f_ref.shape[1]
    W = r_out.shape[2]
    i = pl.program_id(1)
    n_i = pl.num_programs(1)
    f = f_ref[0]
    rows = _iota(f.shape, 0)
    prev_row = jnp.where(i > 0, fp_ref[0, SUBLANES - 1:SUBLANES, :], 0.0)
    next_row = jnp.where(i < n_i - 1, fn_ref[0, 0:1, :], 0.0)
    prev = jnp.where(rows >= 1, pltpu.roll(f, 1, 0), prev_row)
    nxt = jnp.where(rows < ts - 1, pltpu.roll(f, ts - 1, 0), next_row)
    f = f + mup_ref[...] * (prev - f) + mun_ref[...] * (nxt - f)

    r = f[:, 0:W]
    k = f[:, W:2 * W]
    v = f[:, 2 * W:3 * W]
    o = 3 * W
    wd = jnp.tanh(f[:, o:o + 2 * DECAY_LORA])
    ad = f[:, o + 2 * DECAY_LORA:o + 2 * DECAY_LORA + 2 * ICLR_LORA]
    gd = f[:, o + 2 * DECAY_LORA + 2 * ICLR_LORA:]

    g_out[0] = _bdot(_sigmoid(gd), gup_ref[...]).astype(g_out.dtype)
    kk = k * kk_ref[...]
    kk = kk * lax.rsqrt(_head_sum(kk * kk, RWKV_HEAD_DIM) + EPS)
    r_out[0] = r.astype(r_out.dtype)
    kk_out[0] = kk.astype(kk_out.dtype)
    v_out[0] = v.astype(v_out.dtype)
    bonus = jnp.zeros_like(r)
    for z in range(2):
        w_raw = _bdot(wd[:, z * DECAY_LORA:(z + 1) * DECAY_LORA], wup_ref[z]) + w0_ref[z:z + 1, :]
        lw_out[z, 0] = -DECAY_SCALE * _sigmoid(w_raw)
        a = _sigmoid(_bdot(ad[:, z * ICLR_LORA:(z + 1) * ICLR_LORA], aup_ref[z]) + a0_ref[z:z + 1, :])
        bb_out[z, 0] = (a * kk).astype(bb_out.dtype)
        kt = k * (1.0 + (a - 1.0) * ka_ref[...])
        kt_out[z, 0] = kt.astype(kt_out.dtype)
        bonus = bonus + _head_sum(r * kt * rk_ref[...], RWKV_HEAD_DIM) * v
    bonus_out[0] = bonus.astype(bonus_out.dtype)


def _rwkv_pre(feat, mu_prev, mu_next, w0, w_up, a0, a_up, g_up, k_k, k_a, r_k):
    B, S, Fw = feat.shape
    W = w0.shape[1]
    ts = min(256, S)
    hb = ts // SUBLANES
    n_hb = S // SUBLANES
    row = lambda a: a.reshape(1, -1)
    full2 = lambda a: pl.BlockSpec(a.shape, lambda b, i: (0, 0))
    full3 = lambda a: pl.BlockSpec(a.shape, lambda b, i: (0, 0, 0))
    bsw = pl.BlockSpec((1, ts, W), lambda b, i: (b, i, 0))
    zsw = pl.BlockSpec((2, 1, ts, W), lambda b, i: (0, b, i, 0))
    sds = jax.ShapeDtypeStruct((B, S, W), F32)
    zds = jax.ShapeDtypeStruct((2, B, S, W), F32)
    sdh = jax.ShapeDtypeStruct((B, S, W), BF16)
    zdh = jax.ShapeDtypeStruct((2, B, S, W), BF16)
    args = (row(mu_prev), row(mu_next), w0, w_up, a0, a_up, g_up, row(k_k), row(k_a), row(r_k))
    specs = [full2(args[0]), full2(args[1]), full2(w0), full3(w_up), full2(a0), full3(a_up),
             full2(g_up), full2(args[7]), full2(args[8]), full2(args[9])]
    return pl.pallas_call(
        _rwkv_pre_kernel,
        out_shape=(sdh, sdh, sdh, sdh, sdh, zds, zdh, zdh),
        grid=(B, S // ts),
        in_specs=[pl.BlockSpec((1, ts, Fw), lambda b, i: (b, i, 0)),
                  pl.BlockSpec((1, SUBLANES, Fw), lambda b, i: (b, jnp.maximum(i * hb - 1, 0), 0)),
                  pl.BlockSpec((1, SUBLANES, Fw), lambda b, i: (b, jnp.minimum((i + 1) * hb, n_hb - 1), 0)),
                  ] + specs,
        out_specs=(bsw, bsw, bsw, bsw, bsw, zsw, zsw, zsw),
        compiler_params=_cparams("parallel", "parallel"),
        name="rwkv_prepare",
    )(feat, feat, feat, *args)


def _rwkv_scan_kernel(rf_ref, kkf_ref, vf_ref, rb_ref, kkb_ref, vb_ref,
                      lwf_ref, bbf_ref, ktf_ref, lwb_ref, bbb_ref, ktb_ref, yf_ref, yb_ref, st_scr):
    C = RWKV_CHUNK
    C2 = 2 * C
    n_pairs = rf_ref.shape[2] // LANES

    @pl.when(pl.program_id(1) == 0)
    def _():
        st_scr[...] = jnp.zeros_like(st_scr)

    wide_t = _iota((C, C2), 0)
    wide_s = _iota((C, C2), 1) % C
    eye_w = (wide_t == wide_s).astype(F32)
    emask = (_iota((C2, LANES), 0) // C) == (_iota((C2, LANES), 1) // RWKV_HEAD_DIM)
    bmask = (_iota((C2, C2), 0) // C) == (_iota((C2, C2), 1) // C)
    diag_l = _iota((LANES, LANES), 0) == _iota((LANES, LANES), 1)
    tri = _iota((C, C), 0) - _iota((C, C), 1)
    cum_mat = [(sg * tri >= 0).astype(BF16) for sg in (1, -1)]
    strict = [sg * (wide_t - wide_s) > 0 for sg in (1, -1)]
    incl = [sg * (wide_t - wide_s) >= 0 for sg in (1, -1)]
    refs = [(rf_ref, kkf_ref, vf_ref, lwf_ref, bbf_ref, ktf_ref, yf_ref),
            (rb_ref, kkb_ref, vb_ref, lwb_ref, bbb_ref, ktb_ref, yb_ref)]

    def expand(x):
        return jnp.where(emask, jnp.concatenate([x, x], axis=0), 0.0)

    def blockdiag(a):
        return jnp.where(bmask, jnp.concatenate([a, a], axis=0), 0.0)

    def bf(x):
        return x.astype(BF16)

    def mm(a, b):
        return jnp.dot(a, b, preferred_element_type=F32)

    def cumsum3(m, x):
        hi = bf(x)
        r1 = x - hi.astype(F32)
        mid = bf(r1)
        lo = bf(r1 - mid.astype(F32))
        return mm(m, hi) + (mm(m, mid) + mm(m, lo))

    units = [(z, j) for z in range(2) for j in range(n_pairs)]
    zs = [z for z, _ in units]
    sl = [slice(j * LANES, (j + 1) * LANES) for _, j in units]
    r_in = [refs[z][0][0, :, s] for z, s in zip(zs, sl)]
    kk_in = [refs[z][1][0, :, s] for z, s in zip(zs, sl)]
    v_in = [refs[z][2][0, :, s] for z, s in zip(zs, sl)]
    lw = [refs[z][3][0, 0, :, s] for z, s in zip(zs, sl)]
    bb_in = [refs[z][4][0, 0, :, s] for z, s in zip(zs, sl)]
    kt_in = [refs[z][5][0, 0, :, s] for z, s in zip(zs, sl)]

    cum = [cumsum3(cum_mat[z], x) for z, x in zip(zs, lw)]
    tot = [jnp.sum(x, axis=0, keepdims=True) for x in lw]
    p_inv = [jnp.exp(-c) for c in cum]
    p_rest = [jnp.exp(t - c) for t, c in zip(tot, cum)]
    kh_c = [bf(k * jnp.exp(c - x)) for k, c, x in zip(kk_in, cum, lw)]
    rh_c = [bf(r * jnp.exp(c)) for r, c in zip(r_in, cum)]
    kh_e = [expand(k) for k in kh_c]
    bh_e = [expand(bf(b * p)) for b, p in zip(bb_in, p_inv)]
    ktil_e = [expand(bf(k * p)) for k, p in zip(kt_in, p_inv)]
    b_rest_t = [bf(expand(b * p).T) for b, p in zip(bb_in, p_rest)]
    k_rest_t = [bf(expand(k * p).T) for k, p in zip(kt_in, p_rest)]
    v_e = [expand(v) for v in v_in]

    a_all = [lax.dot_general(jnp.concatenate([k, r], axis=0), jnp.concatenate([b, t], axis=0),
                             (((1,), (1,)), ((), ())), preferred_element_type=F32)
             for k, r, b, t in zip(kh_c, rh_c, bh_e, ktil_e)]
    a_b = [jnp.where(strict[z], a[0:C, 0:C2], 0.0) for z, a in zip(zs, a_all)]
    a_k = [jnp.where(strict[z], a[0:C, C2:], 0.0) for z, a in zip(zs, a_all)]
    a_rb = [jnp.where(incl[z], a[C:, 0:C2], 0.0) for z, a in zip(zs, a_all)]
    a_rk = [jnp.where(incl[z], a[C:, C2:], 0.0) for z, a in zip(zs, a_all)]

    pw = [bf(-a) for a in a_b]
    t_inv = [eye_w] * len(units)
    n_dbl = int(math.log2(C))
    for k in range(n_dbl):
        if k < n_dbl - 1:
            both = [mm(p, jnp.concatenate([blockdiag(p), blockdiag(bf(t))], axis=1))
                    for p, t in zip(pw, t_inv)]
            pw = [bf(x[:, 0:C2]) for x in both]
            t_inv = [t + x[:, C2:] for t, x in zip(t_inv, both)]
        else:
            t_inv = [t + mm(p, blockdiag(bf(t))) for p, t in zip(pw, t_inv)]

    akv = [mm(bf(jnp.concatenate([ak, ark], axis=0)), v)
           for ak, ark, v in zip(a_k, a_rk, v_e)]
    tw = [mm(bf(t), jnp.concatenate([k, expand(bf(x[0:C]))], axis=1))
          for t, k, x in zip(t_inv, kh_e, akv)]
    st = [st_scr[z, j] for z, j in units]
    ws = [mm(jnp.concatenate([bf(x[:, 0:LANES]), r], axis=0), bf(s))
          for x, r, s in zip(tw, rh_c, st)]
    u_c = [x[0:C] + t[:, LANES:] for x, t in zip(ws, tw)]
    y_c = [x[C:] + k[C:] - mm(bf(a), expand(bf(u))) for x, k, a, u in zip(ws, akv, a_rb, u_c)]
    for z, s, y in zip(zs, sl, y_c):
        refs[z][6][0, :, s] = y.astype(refs[z][6].dtype)

    wu_e = [jnp.concatenate([expand(bf(t[:, 0:LANES])), expand(bf(t[:, LANES:]))], axis=1) for t in tw]
    bw = [mm(b, x) for b, x in zip(b_rest_t, wu_e)]
    kv = [mm(k, v) for k, v in zip(k_rest_t, v_e)]
    for n, (z, j) in enumerate(units):
        m_t = jnp.where(diag_l, jnp.exp(tot[n]), 0.0) - bw[n][:, 0:LANES]
        g_t = kv[n] - bw[n][:, LANES:]
        st_scr[z, j] = mm(bf(m_t), bf(st[n])) + g_t


def _rwkv_scan(r, kk, v, lw, bb, kt):
    B, S, W = r.shape
    C = RWKV_CHUNK
    nc = S // C
    fwd = pl.BlockSpec((1, C, W), lambda b, i: (b, i, 0))
    bwd = pl.BlockSpec((1, C, W), lambda b, i: (b, nc - 1 - i, 0))
    fwd_z = pl.BlockSpec((1, 1, C, W), lambda b, i: (0, b, i, 0))
    bwd_z = pl.BlockSpec((1, 1, C, W), lambda b, i: (1, b, nc - 1 - i, 0))
    sds = jax.ShapeDtypeStruct((B, S, W), BF16)
    return pl.pallas_call(
        _rwkv_scan_kernel,
        out_shape=(sds, sds),
        grid=(B, nc),
        in_specs=[fwd, fwd, fwd, bwd, bwd, bwd, fwd_z, fwd_z, fwd_z, bwd_z, bwd_z, bwd_z],
        out_specs=(fwd, bwd),
        scratch_shapes=[pltpu.VMEM((2, W // LANES, LANES, LANES), F32)],
        compiler_params=_cparams("parallel", "arbitrary"),
        name="rwkv_scan",
    )(r, kk, v, r, kk, v, lw, bb, kt, lw, bb, kt)


def _rwkv_post_kernel(yf_ref, yb_ref, bonus_ref, g_ref, o_ref):
    hd = RWKV_HEAD_DIM
    acc = bonus_ref[0].astype(F32)
    for y_ref in (yf_ref, yb_ref):
        y = y_ref[0].astype(F32)
        d = y - _head_sum(y, hd) * (1.0 / hd)
        var = _head_sum(d * d, hd) * (1.0 / hd)
        acc = acc + d * lax.rsqrt(var + EPS)
    o_ref[0] = (acc * g_ref[0].astype(F32)).astype(o_ref.dtype)


def _rwkv_post(y_fwd, y_bwd, bonus, g):
    B, S, W = y_fwd.shape
    ts = min(256, S)
    bsw = pl.BlockSpec((1, ts, W), lambda b, i: (b, i, 0))
    return pl.pallas_call(
        _rwkv_post_kernel,
        out_shape=jax.ShapeDtypeStruct((B, S, W), BF16),
        grid=(B, S // ts),
        in_specs=[bsw, bsw, bsw, bsw],
        out_specs=bsw,
        compiler_params=_cparams("parallel", "parallel"),
        name="rwkv_finish",
    )(y_fwd, y_bwd, bonus, g)


def _merge_kernel(ya_ref, yb_ref, yc_ref, yd_ref, wb_ref, l0_ref, l1_ref, l2_ref, l3_ref, o_ref,
                  wb_bf):
    @pl.when(pl.program_id(1) == 0)
    def _():
        wb_bf[...] = wb_ref[0].astype(BF16)

    acc = None
    for n, (y_ref, l_ref) in enumerate(((ya_ref, l0_ref), (yb_ref, l1_ref), (yc_ref, l2_ref),
                                        (yd_ref, l3_ref))):
        br = jnp.dot(y_ref[...], wb_bf[n], preferred_element_type=F32)
        t = _sigmoid(l_ref[...].astype(F32)) * br
        acc = t if acc is None else acc + t
    o_ref[...] = acc.astype(o_ref.dtype)


def _merge(ys, wb, layer, logits, tm, tn):
    T, W = ys[0].shape
    D = wb.shape[3]
    nj = D // tn
    yspec = pl.BlockSpec((tm, W), lambda j, i: (i, 0))
    lspec = lambda n: pl.BlockSpec((tm, tn), lambda j, i: (i, n * nj + j))
    return pl.pallas_call(
        _merge_kernel,
        out_shape=jax.ShapeDtypeStruct((T, D), BF16),
        grid=(nj, T // tm),
        in_specs=[yspec, yspec, yspec, yspec,
                  pl.BlockSpec((1, N_BRANCH, W, tn), lambda j, i: (layer, 0, 0, j)),
                  lspec(0), lspec(1), lspec(2), lspec(3)],
        out_specs=pl.BlockSpec((tm, tn), lambda j, i: (i, j)),
        scratch_shapes=[pltpu.VMEM((N_BRANCH, W, tn), BF16)],
        compiler_params=_cparams("parallel", "arbitrary"),
        name="branch_merge",
    )(*ys, wb, logits, logits, logits, logits)


def _router_kernel(h_ref, w_ref, b_ref, o_ref):
    logits = jnp.dot(_unpack_bf16_pairs(h_ref[...]), w_ref[...], preferred_element_type=F32) + b_ref[...]
    lane = _iota(logits.shape, 1)
    lane_f = lane.astype(F32)
    neg = -3.0e38
    far = float(LANES)
    is_grp = lane < N_GROUPS
    gl = jnp.where(is_grp, logits, neg)
    gmax = jnp.max(gl, axis=-1, keepdims=True)
    gsum = jnp.sum(jnp.where(is_grp, jnp.exp(gl - gmax), 0.0), axis=-1, keepdims=True)
    grp_p = 1.0 / gsum
    grp_i = jnp.min(jnp.where(gl == gmax, lane_f, far), axis=-1, keepdims=True)
    lo = N_GROUPS + grp_i * EXPERTS_PER_GROUP
    in_grp = jnp.logical_and(lane_f >= lo, lane_f < lo + EXPERTS_PER_GROUP)
    el = jnp.where(in_grp, logits, neg)
    m1 = jnp.max(el, axis=-1, keepdims=True)
    i1 = jnp.min(jnp.where(el == m1, lane_f, far), axis=-1, keepdims=True)
    el2 = jnp.where(lane_f == i1, neg, el)
    m2 = jnp.max(el2, axis=-1, keepdims=True)
    i2 = jnp.min(jnp.where(el2 == m2, lane_f, far), axis=-1, keepdims=True)
    t = jnp.exp(m2 - m1)
    w1 = grp_p / (1.0 + t)
    w2 = grp_p * t / (1.0 + t)
    o_ref[...] = jnp.where(lane == 0, w1, jnp.where(lane == 1, w2, jnp.where(
        lane == 2, i1 - N_GROUPS, jnp.where(lane == 3, i2 - N_GROUPS, 0.0))))


def _router(h, w_r, b_r):
    T = h.shape[0]
    D = w_r.shape[0]
    tm = min(1024, T)
    return pl.pallas_call(
        _router_kernel,
        out_shape=jax.ShapeDtypeStruct((T, LANES), F32),
        grid=(T // tm,),
        in_specs=[pl.BlockSpec((tm, D // 2), lambda i: (i, 0)),
                  pl.BlockSpec((D, LANES), lambda i: (0, 0)),
                  pl.BlockSpec((1, LANES), lambda i: (0, 0))],
        out_specs=pl.BlockSpec((tm, LANES), lambda i: (i, 0)),
        compiler_params=_cparams("parallel"),
        name="router",
    )(h, w_r, b_r)


def _expert_kernel(be_ref, nb_ref, first_ref, slot_ref, nxt_ref, xs_ref, wgu_hbm, wd_hbm, o_ref,
                   wgu_f32, wd_f32, wgu_bf, wd_bf, sem, *, layer):
    i = pl.program_id(0)
    used = i < nb_ref[0]

    def weight_copies(e, s):
        return (pltpu.make_async_copy(wgu_hbm.at[layer, e], wgu_f32.at[s], sem.at[s, 0]),
                pltpu.make_async_copy(wd_hbm.at[layer, e], wd_f32.at[s], sem.at[s, 1]))

    @pl.when(jnp.logical_and(used, first_ref[i] == 1))
    def _():
        s = slot_ref[i]

        @pl.when(i == 0)
        def _():
            for cp in weight_copies(be_ref[i], s):
                cp.start()

        for cp in weight_copies(be_ref[i], s):
            cp.wait()

        @pl.when(nxt_ref[i] >= 0)
        def _():
            for cp in weight_copies(nxt_ref[i], 1 - s):
                cp.start()

        wgu_bf[...] = wgu_f32[s].astype(BF16)
        wd_bf[...] = wd_f32[s].astype(BF16)

    @pl.when(used)
    def _():
        gu = jnp.dot(_unpack_bf16_pairs(xs_ref[...]), wgu_bf[...], preferred_element_type=F32)
        eh = gu.shape[1] // 2
        mid = (_silu(gu[:, :eh]) * gu[:, eh:]).astype(BF16)
        o_ref[...] = jnp.dot(mid, wd_bf[...], preferred_element_type=F32)

    @pl.when(jnp.logical_not(used))
    def _():
        o_ref[...] = jnp.zeros_like(o_ref)


def _expert_runs(block_expert, n_used):
    n_blocks = block_expert.shape[0]
    idx = jnp.arange(n_blocks, dtype=jnp.int32)
    changed = jnp.logical_or(idx == 0, block_expert != jnp.roll(block_expert, 1))
    first = jnp.logical_and(changed, idx < n_used[0]).astype(jnp.int32)
    slot = (jnp.cumsum(first) - 1) % 2
    start_pos = jnp.where(first == 1, idx, n_blocks)
    later = lax.cummin(start_pos, axis=0, reverse=True)
    next_pos = jnp.concatenate([later[1:], jnp.full((1,), n_blocks, jnp.int32)])
    nxt = jnp.where(next_pos < n_blocks, block_expert[jnp.minimum(next_pos, n_blocks - 1)], -1)
    return first, slot.astype(jnp.int32), nxt.astype(jnp.int32)


def _experts(xs, block_expert, n_used, w_gu, w_down, layer):
    cap = xs.shape[0]
    D, H2 = w_gu.shape[2], w_gu.shape[3]
    n_blocks = cap // MOE_BLOCK
    first, slot, nxt = _expert_runs(block_expert, n_used)
    grid_spec = pltpu.PrefetchScalarGridSpec(
        num_scalar_prefetch=5,
        grid=(n_blocks,),
        in_specs=[pl.BlockSpec((MOE_BLOCK, D // 2),
                               lambda i, be, nb, fi, sl, nx: (jnp.minimum(i, nb[0] - 1), 0)),
                  pl.BlockSpec(memory_space=pl.ANY),
                  pl.BlockSpec(memory_space=pl.ANY)],
        out_specs=pl.BlockSpec((MOE_BLOCK, D), lambda i, be, nb, fi, sl, nx: (i, 0)),
        scratch_shapes=[pltpu.VMEM((2, D, H2), F32), pltpu.VMEM((2, H2 // 2, D), F32),
                        pltpu.VMEM((D, H2), BF16), pltpu.VMEM((H2 // 2, D), BF16),
                        pltpu.SemaphoreType.DMA((2, 2))],
    )
    return pl.pallas_call(
        functools.partial(_expert_kernel, layer=layer),
        out_shape=jax.ShapeDtypeStruct((cap, D), F32),
        grid_spec=grid_spec,
        compiler_params=_cparams("arbitrary"),
        name="experts",
    )(block_expert, n_used, first, slot, nxt, xs, w_gu, w_down)


def _row_copy(src_ref, s, dst_ref, d, sem):
    return pltpu.make_async_copy(src_ref.at[pl.ds(s, 1), :], dst_ref.at[pl.ds(d, 1), :], sem)


def _rows_wait(src_ref, dst_ref, n, sem):
    pltpu.make_async_copy(src_ref.at[pl.ds(0, n), :], dst_ref.at[pl.ds(0, n), :], sem).wait()


def _combine_kernel(dest_ref, ys_ref, x_ref, r_ref, g_ref, ng_ref, o_ref, y0_buf, y1_buf, sem, *,
                    final_norm):
    tm = x_ref.shape[0]
    base = pl.program_id(0) * tm

    def issue(k, c):
        t = base + k
        _row_copy(ys_ref, dest_ref[TOP_K * t], y0_buf, k, sem.at[0]).start(priority=0)
        _row_copy(ys_ref, dest_ref[TOP_K * t + 1], y1_buf, k, sem.at[1]).start(priority=1)
        return c

    lax.fori_loop(0, tm, issue, 0, unroll=4)
    _rows_wait(ys_ref, y0_buf, tm, sem.at[0])
    _rows_wait(ys_ref, y1_buf, tm, sem.at[1])
    w0 = r_ref[:, 0:1]
    w1 = r_ref[:, 1:2]
    x = x_ref[...] + g_ref[0] * (w0 * y0_buf[...] + w1 * y1_buf[...])
    if final_norm:
        x = x * lax.rsqrt(jnp.mean(x * x, axis=-1, keepdims=True) + EPS) * ng_ref[...]
    o_ref[...] = x


def _combine(x2d, ys, dest, route, gate, rows_per_batch, norm_g, final_norm):
    T, D = x2d.shape
    tm = min(512, T)
    tiles_per_batch = rows_per_batch // tm
    spec = pl.BlockSpec((tm, D), lambda i, d: (i, 0))
    grid_spec = pltpu.PrefetchScalarGridSpec(
        num_scalar_prefetch=1,
        grid=(T // tm,),
        in_specs=[pl.BlockSpec(memory_space=pl.ANY), spec,
                  pl.BlockSpec((tm, LANES), lambda i, d: (i, 0)),
                  pl.BlockSpec((1, 1, D), lambda i, d: (i // tiles_per_batch, 0, 0)),
                  pl.BlockSpec((1, D), lambda i, d: (0, 0))],
        out_specs=spec,
        scratch_shapes=[pltpu.VMEM((tm, D), F32), pltpu.VMEM((tm, D), F32),
                        pltpu.SemaphoreType.DMA((2,))],
    )
    return pl.pallas_call(
        functools.partial(_combine_kernel, final_norm=final_norm),
        out_shape=jax.ShapeDtypeStruct((T, D), F32),
        grid_spec=grid_spec,
        compiler_params=_cparams("arbitrary"),
        name="moe_combine",
    )(dest, ys, x2d, route, gate, norm_g.reshape(1, D))


def _dispatch_kernel(dest_ref, h_ref, xs_in_ref, xs_ref, sem, *, tm):
    del xs_in_ref
    base = pl.program_id(0) * tm

    def issue(k, c):
        t = base + k
        for kk in range(TOP_K):
            _row_copy(h_ref, k, xs_ref, dest_ref[TOP_K * t + kk], sem.at[kk]).start(priority=kk)
        return c

    lax.fori_loop(0, tm, issue, 0, unroll=4)
    for kk in range(TOP_K):
        _rows_wait(h_ref, xs_ref, tm, sem.at[kk])


def _dispatch(h, dest, cap):
    T, Dp = h.shape
    tm = min(1024, T)
    grid_spec = pltpu.PrefetchScalarGridSpec(
        num_scalar_prefetch=1,
        grid=(T // tm,),
        in_specs=[pl.BlockSpec((tm, Dp), lambda i, d: (i, 0)), pl.BlockSpec(memory_space=pl.ANY)],
        out_specs=pl.BlockSpec(memory_space=pl.ANY),
        scratch_shapes=[pltpu.SemaphoreType.DMA((TOP_K,))],
    )
    return pl.pallas_call(
        functools.partial(_dispatch_kernel, tm=tm),
        out_shape=jax.ShapeDtypeStruct((cap, Dp), h.dtype),
        grid_spec=grid_spec,
        input_output_aliases={2: 0},
        compiler_params=_cparams("arbitrary"),
        name="moe_dispatch",
    )(dest, h, jnp.zeros((cap, Dp), h.dtype))


def _rank_kernel(route_ref, rank_ref, cnt_ref, carry):
    tm = route_ref.shape[0]
    i = pl.program_id(0)

    @pl.when(i == 0)
    def _():
        carry[...] = jnp.zeros_like(carry)

    r = route_ref[...]
    lane = _iota(r.shape, 1)
    lane_f = lane.astype(F32)
    oh0 = (lane_f == r[:, 2:3]).astype(F32)
    oh1 = (lane_f == r[:, 3:4]).astype(F32)
    both = oh0 + oh1
    earlier = (_iota((tm, tm), 0) > _iota((tm, tm), 1)).astype(BF16)
    prefix = jnp.dot(earlier, both.astype(BF16), preferred_element_type=F32) + carry[0:1, :]
    rank0 = jnp.sum(prefix * oh0, axis=-1, keepdims=True)
    rank1 = jnp.sum((prefix + oh0) * oh1, axis=-1, keepdims=True)
    rank_ref[...] = jnp.where(lane == 0, rank0, jnp.where(lane == 1, rank1, 0.0))
    carry[...] = carry[...] + jnp.sum(both, axis=0, keepdims=True)
    cnt_ref[...] = carry[...]


def _dest_kernel(route_ref, rank_ref, cnt_ref, dest_ref):
    r = route_ref[...]
    lane = _iota(r.shape, 1)
    lane_f = lane.astype(F32)
    padded = jnp.floor((cnt_ref[...] + (MOE_BLOCK - 1.0)) * (1.0 / MOE_BLOCK)) * MOE_BLOCK
    before = (_iota((LANES, LANES), 0) < _iota((LANES, LANES), 1)).astype(F32)
    pad_start = jnp.dot(padded, before, precision=HI, preferred_element_type=F32)[0:1, :]
    oh0 = lane_f == r[:, 2:3]
    oh1 = lane_f == r[:, 3:4]
    d0 = jnp.sum(jnp.where(oh0, pad_start, 0.0), axis=-1, keepdims=True) + rank_ref[:, 0:1]
    d1 = jnp.sum(jnp.where(oh1, pad_start, 0.0), axis=-1, keepdims=True) + rank_ref[:, 1:2]
    dest_ref[...] = jnp.where(lane == 0, d0, jnp.where(lane == 1, d1, 0.0))


def _dispatch_plan(route):
    T = route.shape[0]
    tm = min(512, T)
    slab = pl.BlockSpec((tm, LANES), lambda i: (i, 0))
    cnt_spec = pl.BlockSpec((SUBLANES, LANES), lambda i: (0, 0))
    rank, cnt = pl.pallas_call(
        _rank_kernel,
        out_shape=(jax.ShapeDtypeStruct((T, LANES), F32), jax.ShapeDtypeStruct((SUBLANES, LANES), F32)),
        grid=(T // tm,),
        in_specs=[slab],
        out_specs=(slab, cnt_spec),
        scratch_shapes=[pltpu.VMEM((SUBLANES, LANES), F32)],
        compiler_params=_cparams("arbitrary"),
        name="moe_rank",
    )(route)
    dest = pl.pallas_call(
        _dest_kernel,
        out_shape=jax.ShapeDtypeStruct((T, LANES), F32),
        grid=(T // tm,),
        in_specs=[slab, slab, cnt_spec],
        out_specs=slab,
        compiler_params=_cparams("parallel"),
        name="moe_dest",
    )(route, rank, cnt)
    dest = dest[:, :TOP_K].astype(jnp.int32).reshape(T * TOP_K)
    counts = cnt[0, :N_EXPERTS].astype(jnp.int32)
    padded = (counts + MOE_BLOCK - 1) // MOE_BLOCK * MOE_BLOCK
    pad_end = jnp.cumsum(padded)
    n_blocks = -(-(T * TOP_K) // MOE_BLOCK) + N_EXPERTS
    block_row0 = jnp.arange(n_blocks, dtype=jnp.int32) * MOE_BLOCK
    block_expert = jnp.minimum(jnp.searchsorted(pad_end, block_row0, side="right"),
                               N_EXPERTS - 1).astype(jnp.int32)
    n_used = (pad_end[-1] // MOE_BLOCK).astype(jnp.int32).reshape(1)
    return dest, block_expert, n_used, n_blocks * MOE_BLOCK


def kernel(x, c, positions, norm1_g, norm2_g, ada_w, ada_b, w_in, lru_conv_w, lru_conv_b, lru_w_r, lru_b_r, lru_w_i, lru_b_i, lru_lambda, sgu_norm_g, sgu_w, sgu_b, rwkv_mu_prev, rwkv_mu_next, rwkv_w0, rwkv_w_up, rwkv_a0, rwkv_a_up, rwkv_g_up, rwkv_k_k, rwkv_k_a, rwkv_r_k, w_branch, w_out, router_grp_w, router_grp_b, router_exp_w, router_exp_b, expert_w_gu, expert_w_down, final_norm_g):
    B, S, D = x.shape
    L = ada_w.shape[0]
    W = D // 2
    T = B * S
    feat_w = rwkv_mu_prev.shape[1]

    c_pad = jnp.zeros((SUBLANES, D), F32).at[:B].set(c)
    mod = _ada(c_pad, ada_w, ada_b)

    dk = D // 2 // (2 * RET_HEADS)
    inv_freq = ROPE_THETA ** (-jnp.arange(0, dk, 2, dtype=F32) / dk)
    ang = positions.astype(F32)[..., None] * inv_freq
    cosf, sinf = _rope_tables(jnp.tile(ang, (1, 1, LANES // (dk // 2))))
    log_gamma = np.log1p(-np.exp2(-5.0 - np.arange(RET_HEADS, dtype=np.float64)))
    lg_tab = jnp.asarray(np.broadcast_to(log_gamma[:, None], (RET_HEADS, LANES)), F32)

    x2d = x.reshape(T, D)
    for l in range(L):
        m = mod[l, :B]
        sh1, sc1, g1, sh2, sc2, g2 = [m[:, None, k * D:(k + 1) * D] for k in range(6)]

        h = _normmod(x2d.reshape(B, S, D), norm1_g[l], sc1, sh1, BF16).reshape(T, D)
        n_a = 7 * W
        proj_a = _mm(h, w_in, l, 0, n_a, 2048, 512, name="proj_a").reshape(B, S, n_a)
        feat = _mm(h, w_in, l, n_a, feat_w, 1024, feat_w // 3, name="proj_rwkv").reshape(B, S, feat_w)
        logits = _mm(h, w_in, l, n_a + feat_w, N_BRANCH * D, 2048, 512, out_dtype=BF16,
                     name="proj_gate")

        y_a = _lru(proj_a, B, S, W, 0, W // LANES, lru_conv_w[l], lru_conv_b[l], lru_w_r[l],
                   lru_b_r[l], lru_w_i[l], lru_b_i[l], lru_lambda[l])
        y_b = _retention(proj_a, B, S, W, 2 * W // LANES, 5 * W // (2 * LANES),
                         3 * W // (2 * LANES), 4 * W // (2 * LANES), cosf, sinf, lg_tab)
        y_c = _sgu(proj_a, B, S, W, 5, 6, sgu_norm_g[l], sgu_w[l], sgu_b[l])
        r, kk, v, g, bonus, lw, bb, kt = _rwkv_pre(
            feat, rwkv_mu_prev[l], rwkv_mu_next[l], rwkv_w0[l], rwkv_w_up[l], rwkv_a0[l],
            rwkv_a_up[l], rwkv_g_up[l], rwkv_k_k[l], rwkv_k_a[l], rwkv_r_k[l].reshape(-1))
        y_d = _rwkv_post(*_rwkv_scan(r, kk, v, lw, bb, kt), bonus, g)

        merged = _merge([y.reshape(T, W) for y in (y_a, y_b, y_c, y_d)],
                        w_branch, l, logits, 1024, 512)
        x2d = _mm_residual(merged, w_out, l, x2d, g1, S, 2048, 512)

        h2 = _normmod(x2d.reshape(B, S, D), norm2_g[l], sc2, sh2, jnp.uint32).reshape(T, D // 2)
        n_r = N_GROUPS + N_EXPERTS
        w_r = jnp.zeros((D, LANES), F32).at[:, :N_GROUPS].set(router_grp_w[l]).at[:, N_GROUPS:n_r].set(
            router_exp_w[l]).astype(BF16)
        b_r = jnp.zeros((1, LANES), F32).at[0, :N_GROUPS].set(router_grp_b[l]).at[0, N_GROUPS:n_r].set(
            router_exp_b[l])
        route = _router(h2, w_r, b_r)
        dest, block_expert, n_used, cap = _dispatch_plan(route)
        ys = _experts(_dispatch(h2, dest, cap), block_expert, n_used, expert_w_gu, expert_w_down, l)
        x2d = _combine(x2d, ys, dest, route, g2, S, final_norm_g, l == L - 1)

    return x2d.reshape(B, S, D)
```
